```python
import math
import jax, jax.numpy as jnp
from jax import lax
import numpy as np

D_MODEL = 1024
BATCH = 8
SEQ = 4096
DEPTH = 1

CHUNK = 64
EPS = 1e-6
H_A = 8
DK_A = 128
DV_A = 128
CONV_K = 4
H_B = 8
D_HEAD_B = 128
H_IDX = 8
D_IDX = 64
TOPK_MAX = 256
Q_BLOCK = 128
N_BUCKETS = 32
MAX_DISTANCE = 128
D_FF = 4 * D_MODEL
N_BRANCH = 2

SPLIT_SIZES = (H_A * DK_A, H_A * DK_A, H_A * DV_A, H_A * DV_A, H_A, H_A,
               H_B * D_HEAD_B, H_B * D_HEAD_B, H_B * D_HEAD_B, H_IDX * D_IDX, D_IDX, H_IDX)
D_IN = sum(SPLIT_SIZES)

kernel_name = 'hybrid_gdn_dsa_streaming_block'


def _rmsnorm(x, w):
    xf = x.astype(jnp.float32)
    y = xf * lax.rsqrt(jnp.mean(xf * xf, axis=-1, keepdims=True) + EPS)
    return (y * w.astype(jnp.float32)).astype(x.dtype)


def _l2norm(x):
    xf = x.astype(jnp.float32)
    return xf * lax.rsqrt(jnp.sum(xf * xf, axis=-1, keepdims=True) + EPS)


def _causal_depthwise_conv(x, w):
    return lax.conv_general_dilated(
        x, w[:, None, :].astype(x.dtype), window_strides=(1,), padding=[(CONV_K - 1, 0)],
        dimension_numbers=('NWC', 'WIO', 'NWC'), feature_group_count=x.shape[-1])


def _t5_bucket(rel):
    half = N_BUCKETS // 2
    max_exact = half // 2
    base = jnp.where(rel > 0, half, 0)
    n = jnp.abs(rel)
    n_f = jnp.maximum(n, 1).astype(jnp.float32)
    large = max_exact + (jnp.log(n_f / max_exact) / math.log(MAX_DISTANCE / max_exact)
                         * (half - max_exact)).astype(jnp.int32)
    large = jnp.minimum(large, half - 1)
    return base + jnp.where(n < max_exact, n, large)


def _gated_delta_rule(q, k, v, g, beta):
    b_, t_, h_, dk = q.shape
    dv = v.shape[-1]
    n = t_ // CHUNK

    def chunks(a):
        return a.astype(jnp.float32).reshape(b_, n, CHUNK, h_, -1).transpose(1, 0, 3, 2, 4)

    q, k, v = chunks(q), chunks(k), chunks(v)
    g = jnp.cumsum(chunks(g[..., None])[..., 0], axis=-1)
    beta = chunks(beta[..., None])[..., 0]
    causal_incl = jnp.tril(jnp.ones((CHUNK, CHUNK), dtype=bool))
    strict = jnp.tril(jnp.ones((CHUNK, CHUNK), dtype=bool), k=-1)
    diff = g[..., :, None] - g[..., None, :]
    decay = jnp.where(causal_incl, jnp.exp(jnp.where(causal_incl, diff, 0.0)), 0.0)
    k_beta = k * beta[..., None]
    lower = jnp.where(strict, jnp.einsum('nbhcd,nbhsd->nbhcs', k_beta, k) * decay, 0.0)
    eye = jnp.eye(CHUNK, dtype=jnp.float32)
    rhs = jnp.concatenate([v * beta[..., None], k_beta * jnp.exp(g)[..., None]], axis=-1)
    sol = lax.linalg.triangular_solve(lower + eye, rhs, left_side=True, lower=True, unit_diagonal=True)
    u, w = sol[..., :dv], sol[..., dv:]
    qk = jnp.where(causal_incl, jnp.einsum('nbhcd,nbhsd->nbhcs', q, k) * decay, 0.0)
    q_dec = q * jnp.exp(g)[..., None]
    g_last = g[..., -1]
    k_tail = k * jnp.exp(g_last[..., None] - g)[..., None]

    def step(S, xs):
        u_i, w_i, qk_i, q_dec_i, k_tail_i, g_last_i = xs
        v_new = u_i - jnp.einsum('bhcd,bhde->bhce', w_i, S)
        o_i = jnp.einsum('bhcd,bhde->bhce', q_dec_i, S) + jnp.einsum('bhcs,bhse->bhce', qk_i, v_new)
        S = S * jnp.exp(g_last_i)[..., None, None] + jnp.einsum('bhcd,bhce->bhde', k_tail_i, v_new)
        return S, o_i

    S0 = jnp.zeros((b_, h_, dk, dv), jnp.float32)
    _, o = lax.scan(step, S0, (u, w, qk, q_dec, k_tail, g_last))
    return o.transpose(1, 0, 3, 2, 4).reshape(b_, t_, h_, dv)


def _dsa_attention(q, k, v, iq, ik, iw, rel_table):
    t_ = q.shape[1]
    topk = min(TOPK_MAX, t_ // 4)
    n_blocks = t_ // Q_BLOCK
    key_chunk = jnp.arange(t_, dtype=jnp.int32) // CHUNK
    scale = D_HEAD_B ** -0.5

    def per_sequence(args):
        q_s, k_s, v_s, iq_s, ik_s, iw_s = args
        ik_f = ik_s.astype(jnp.float32)

        def per_block(blk):
            start = blk * Q_BLOCK
            qb = lax.dynamic_slice_in_dim(q_s, start, Q_BLOCK, axis=0)
            iqb = lax.dynamic_slice_in_dim(iq_s, start, Q_BLOCK, axis=0).astype(jnp.float32)
            iwb = lax.dynamic_slice_in_dim(iw_s, start, Q_BLOCK, axis=0).astype(jnp.float32)
            q_pos = start + jnp.arange(Q_BLOCK, dtype=jnp.int32)
            score = jnp.einsum('qh,qhs->qs', iwb, jax.nn.relu(jnp.einsum('qhd,sd->qhs', iqb, ik_f)))
            visible = key_chunk[None, :] <= (q_pos // CHUNK)[:, None]
            score = jnp.where(visible, score, -jnp.inf)
            top_val, top_idx = lax.top_k(score, topk)
            valid = jnp.isfinite(top_val)
            kg = k_s[top_idx]
            vg = v_s[top_idx]
            logits = jnp.einsum('qhd,qkhd->qhk', qb, kg).astype(jnp.float32) * scale
            bias = rel_table[_t5_bucket(top_idx - q_pos[:, None])].astype(jnp.float32)
            logits = jnp.where(valid[:, None, :], logits + bias.transpose(0, 2, 1), -jnp.inf)
            p = jax.nn.softmax(logits, axis=-1)
            return jnp.einsum('qhk,qkhd->qhd', p.astype(vg.dtype), vg)

        out = lax.map(per_block, jnp.arange(n_blocks, dtype=jnp.int32))
        return out.reshape(t_, q_s.shape[1], q_s.shape[2])

    return lax.map(per_sequence, (q, k, v, iq, ik, iw))


def setup_inputs(seed: int = 0) -> dict:
    key = jax.random.key(seed)
    ks = jax.random.split(key, 17)
    f32 = jnp.float32
    nrm = lambda k_, shape, s: jax.random.normal(k_, shape, f32) * s
    return {
        'x': jax.random.normal(ks[0], (BATCH, SEQ, D_MODEL), f32),
        'norm1_w': 1.0 + nrm(ks[1], (DEPTH, D_MODEL), 0.05),
        'w_in': nrm(ks[2], (DEPTH, D_MODEL, D_IN), D_MODEL ** -0.5),
        'conv_a_w': nrm(ks[3], (DEPTH, CONV_K, 2 * H_A * DK_A + H_A * DV_A), CONV_K ** -0.5),
        'a_log': jnp.log(jax.random.uniform(ks[4], (DEPTH, H_A), f32, 1.0, 16.0)),
        'dt_bias': jnp.log(jnp.expm1(jax.random.uniform(ks[5], (DEPTH, H_A), f32, 1e-3, 0.1))),
        'norm_a_w': 1.0 + nrm(ks[6], (DEPTH, DV_A), 0.05),
        'rel_bias_table': nrm(ks[7], (N_BUCKETS, H_B), 0.5),
        'w_gate': nrm(ks[8], (DEPTH, D_MODEL, N_BRANCH * D_MODEL), D_MODEL ** -0.5),
        'b_gate': nrm(ks[9], (DEPTH, N_BRANCH * D_MODEL), 0.01),
        'w_proj_a': nrm(ks[10], (DEPTH, H_A * DV_A, D_MODEL), (H_A * DV_A) ** -0.5),
        'w_proj_b': nrm(ks[11], (DEPTH, H_B * D_HEAD_B, D_MODEL), (H_B * D_HEAD_B) ** -0.5),
        'w_out': nrm(ks[12], (DEPTH, D_MODEL, D_MODEL), D_MODEL ** -0.5),
        'norm2_w': 1.0 + nrm(ks[13], (DEPTH, D_MODEL), 0.05),
        'w_ff1': nrm(ks[14], (DEPTH, D_MODEL, D_FF), D_MODEL ** -0.5),
        'w_ff2': nrm(ks[15], (DEPTH, D_FF, D_MODEL), D_FF ** -0.5),
        'norm_final_w': 1.0 + nrm(ks[16], (D_MODEL,), 0.05),
    }


def reference(x, norm1_w, w_in, conv_a_w, a_log, dt_bias, norm_a_w, rel_bias_table, w_gate, b_gate,
              w_proj_a, w_proj_b, w_out, norm2_w, w_ff1, w_ff2, norm_final_w):
    b_, t_, _ = x.shape
    f32 = jnp.float32
    split_points = np.cumsum(SPLIT_SIZES)[:-1].tolist()
    for layer in range(DEPTH):
        h = _rmsnorm(x, norm1_w[layer])
        qa, ka, va, za, ba, aa, qb, kb, vb, iq, ik, iw = jnp.split(h @ w_in[layer], split_points, axis=-1)

        qkv_a = jax.nn.silu(_causal_depthwise_conv(jnp.concatenate([qa, ka, va], axis=-1), conv_a_w[layer]))
        qa, ka, va = jnp.split(qkv_a, [H_A * DK_A, 2 * H_A * DK_A], axis=-1)
        qa = _l2norm(qa.reshape(b_, t_, H_A, DK_A)) * (DK_A ** -0.5)
        ka = _l2norm(ka.reshape(b_, t_, H_A, DK_A))
        beta = jax.nn.sigmoid(ba.astype(f32))
        g = -jnp.exp(a_log[layer].astype(f32)) * jax.nn.softplus(aa.astype(f32) + dt_bias[layer].astype(f32))
        oa = _gated_delta_rule(qa, ka, va.reshape(b_, t_, H_A, DV_A), g, beta).astype(x.dtype)
        oa = (_rmsnorm(oa, norm_a_w[layer]) * jax.nn.silu(za.reshape(b_, t_, H_A, DV_A))).reshape(b_, t_, H_A * DV_A)

        ob = _dsa_attention(qb.reshape(b_, t_, H_B, D_HEAD_B), kb.reshape(b_, t_, H_B, D_HEAD_B),
                            vb.reshape(b_, t_, H_B, D_HEAD_B), iq.reshape(b_, t_, H_IDX, D_IDX), ik, iw,
                            rel_bias_table).reshape(b_, t_, H_B * D_HEAD_B)

        gates = jax.nn.sigmoid(h @ w_gate[layer] + b_gate[layer]).reshape(b_, t_, N_BRANCH, D_MODEL)
        merged = gates[:, :, 0, :] * (oa @ w_proj_a[layer]) + gates[:, :, 1, :] * (ob @ w_proj_b[layer])
        x = x + merged @ w_out[layer]

        h2 = _rmsnorm(x, norm2_w[layer])
        x = x + jnp.square(jax.nn.relu(h2 @ w_ff1[layer])) @ w_ff2[layer]
    return _rmsnorm(x, norm_final_w)
```

```python
import functools
import math

import jax
import jax.numpy as jnp
import numpy as np
from jax import lax
from jax.experimental import pallas as pl
from jax.experimental.pallas import tpu as pltpu

EPS = 1e-6
H_A = 8
DK_A = 128
DV_A = 128
CONV_K = 4
GDN_CHUNK = 128
H_B = 8
D_HEAD_B = 128
H_IDX = 8
D_IDX = 64
TOPK_MAX = 256
Q_BLOCK = 128
CHUNK = 64
KEY_TILE = 512
N_BUCKETS = 32
MAX_DISTANCE = 128

LANE = 128
VMEM_LIMIT = 56 * 1024 * 1024

INT_MIN = -(2**31)
INT_MAX = 2**31 - 1
KEY_NEG_INF = -2139095041
MASK_NEG = -1e30

_F32 = jnp.float32
_BF16 = jnp.bfloat16


def _dot(a, b):
    return jnp.dot(a, b, preferred_element_type=_F32)


def _dot_nt(a, b):
    return lax.dot_general(a, b, (((1,), (1,)), ((), ())), preferred_element_type=_F32)


def _sigmoid(x):
    return 1.0 / (1.0 + jnp.exp(-x))


def _rms(x, w):
    return x * lax.rsqrt(jnp.mean(x * x, axis=-1, keepdims=True) + EPS) * w


def _norm_proj_kernel(x_ref, nw_ref, w_ref, o_ref, h_ref):
    @pl.when(pl.program_id(1) == 0)
    def _():
        h_ref[...] = _rms(x_ref[...], nw_ref[...]).astype(_BF16)

    o_ref[...] = _dot(h_ref[...], w_ref[...]).astype(o_ref.dtype)


def _norm_proj(x2d, nw, w, out_dtype, tn, tc, name):
    n, d = x2d.shape
    c = w.shape[1]
    return pl.pallas_call(
        _norm_proj_kernel,
        grid=(n // tn, c // tc),
        in_specs=[
            pl.BlockSpec((tn, d), lambda i, j: (i, 0)),
            pl.BlockSpec((1, d), lambda i, j: (0, 0)),
            pl.BlockSpec((d, tc), lambda i, j: (0, j)),
        ],
        out_specs=pl.BlockSpec((tn, tc), lambda i, j: (i, j)),
        out_shape=jax.ShapeDtypeStruct((n, c), out_dtype),
        scratch_shapes=[pltpu.VMEM((tn, d), _BF16)],
        compiler_params=pltpu.CompilerParams(
            dimension_semantics=("arbitrary", "arbitrary"), vmem_limit_bytes=VMEM_LIMIT),
        name=name,
    )(x2d, nw, w)


def _gdn_kernel(q_ref, k_ref, v_ref, z_ref, sm_ref, cw_ref, arow_ref, dtrow_ref, naw_ref, o_ref,
                xbuf, s_ref, kn_s, kb_s, qn_s, rhs_s, dec_s, qdec_s, ktt_s, eg_s):
    tt = q_ref.shape[0]
    hd = H_A * DK_A

    @pl.when(pl.program_id(1) == 0)
    def _():
        xbuf[:, 0:8, :] = jnp.zeros((3, 8, hd), _F32)
        s_ref[...] = jnp.zeros_like(s_ref)

    for idx, ref in enumerate((q_ref, k_ref, v_ref)):
        xbuf[idx, 8:8 + tt, :] = ref[...]

    sm = sm_ref[...]
    beta_full = _sigmoid(sm)
    xg = sm + dtrow_ref[...]
    softplus = jnp.maximum(xg, 0.0) + jnp.log(1.0 + jnp.exp(-jnp.abs(xg)))
    g_full = -jnp.exp(arow_ref[...]) * softplus
    row = lax.broadcasted_iota(jnp.int32, (tt, tt), 0)
    col = lax.broadcasted_iota(jnp.int32, (tt, tt), 1)
    tri = (col <= row).astype(_F32)
    gcum = jnp.dot(tri, g_full, preferred_element_type=_F32, precision=lax.Precision.HIGHEST)
    gcum_t = gcum.T
    strict = col < row
    eye = (col == row).astype(_F32)

    for h in range(H_A):
        sl = slice(h * DK_A, (h + 1) * DK_A)
        conv = []
        for idx in range(3):
            acc = None
            for j in range(CONV_K):
                term = cw_ref[j:j + 1, idx * hd + h * DK_A: idx * hd + (h + 1) * DK_A] * \
                    xbuf[idx, 8 - (CONV_K - 1) + j: 8 - (CONV_K - 1) + j + tt, sl]
                acc = term if acc is None else acc + term
            conv.append(acc * _sigmoid(acc))
        qh, kh, vh = conv
        qn = qh * lax.rsqrt(jnp.sum(qh * qh, axis=-1, keepdims=True) + EPS) * (DK_A ** -0.5)
        kn = kh * lax.rsqrt(jnp.sum(kh * kh, axis=-1, keepdims=True) + EPS)
        bcol = beta_full[:, h:h + 1]
        gcol = gcum[:, H_A + h:H_A + h + 1]
        grow = gcum_t[H_A + h:H_A + h + 1, :]
        glast = gcum[tt - 1:tt, H_A + h:H_A + h + 1]
        kb = kn * bcol
        eg = jnp.exp(gcol)
        rhs_s[h, :, 0:DV_A] = vh * bcol
        rhs_s[h, :, DV_A:DV_A + DK_A] = kb * eg
        dec_s[h] = jnp.where(strict, jnp.exp(jnp.where(strict, gcol - grow, 0.0)), 0.0)
        qdec_s[h] = (qn * eg).astype(_BF16)
        ktt_s[h] = (kn * jnp.exp(glast - gcol)).T.astype(_BF16)
        kn_s[h] = kn.astype(_BF16)
        kb_s[h] = kb.astype(_BF16)
        qn_s[h] = qn.astype(_BF16)
        eg_s[h] = jnp.broadcast_to(jnp.exp(glast), (DK_A, DV_A))

    for idx in range(3):
        xbuf[idx, 0:8, :] = xbuf[idx, tt:tt + 8, :]

    n_sq = int(math.log2(tt))

    def head_body(h, carry):
        kn = kn_s[h]
        dec = dec_s[h]
        a = _dot_nt(kb_s[h], kn)
        m = -(a * dec)
        x = m
        for _ in range(n_sq - 1):
            mb = m.astype(_BF16)
            m = _dot(mb, mb)
            x = x + m + _dot(x.astype(_BF16), m.astype(_BF16))
        rhs = rhs_s[h]
        sol = rhs + _dot(x.astype(_BF16), rhs.astype(_BF16))
        u = sol[:, 0:DV_A]
        w = sol[:, DV_A:DV_A + DK_A].astype(_BF16)
        qk = (_dot_nt(qn_s[h], kn) * (dec + eye)).astype(_BF16)
        s = s_ref[h]
        sb = s.astype(_BF16)
        vnew = u - _dot(w, sb)
        vb = vnew.astype(_BF16)
        o = _dot(qdec_s[h], sb) + _dot(qk, vb)
        s_ref[h] = s * eg_s[h] + _dot(ktt_s[h], vb)
        rhs_s[h, :, 0:DV_A] = o
        return carry

    lax.fori_loop(0, H_A, head_body, 0, unroll=2)

    naw = naw_ref[...]
    for h in range(H_A):
        sl = slice(h * DV_A, (h + 1) * DV_A)
        o = rhs_s[h, :, 0:DV_A]
        z = z_ref[:, sl]
        on = o * lax.rsqrt(jnp.mean(o * o, axis=-1, keepdims=True) + EPS) * naw
        o_ref[:, sl] = (on * (z * _sigmoid(z))).astype(o_ref.dtype)


def _gdn(p32, cw, arow, dtrow, naw, b_, t_, col0):
    tt = GDN_CHUNK
    hd = H_A * DK_A
    nt = t_ // tt
    cb = col0 // hd

    def rowblk(j):
        return pl.BlockSpec((tt, hd), lambda b, t: (b * nt + t, cb + j))

    sm_blk = (col0 + 4 * hd) // LANE
    const = lambda shape: pl.BlockSpec(shape, lambda b, t: (0, 0))
    return pl.pallas_call(
        _gdn_kernel,
        grid=(b_, nt),
        in_specs=[rowblk(0), rowblk(1), rowblk(2), rowblk(3),
                  pl.BlockSpec((tt, LANE), lambda b, t: (b * nt + t, sm_blk)),
                  const((CONV_K, 3 * hd)), const((1, LANE)), const((1, LANE)), const((1, DV_A))],
        out_specs=pl.BlockSpec((tt, hd), lambda b, t: (b * nt + t, 0)),
        out_shape=jax.ShapeDtypeStruct((b_ * t_, hd), _BF16),
        scratch_shapes=[
            pltpu.VMEM((3, tt + 8, hd), _F32),
            pltpu.VMEM((H_A, DK_A, DV_A), _F32),
            pltpu.VMEM((H_A, tt, DK_A), _BF16),
            pltpu.VMEM((H_A, tt, DK_A), _BF16),
            pltpu.VMEM((H_A, tt, DK_A), _BF16),
            pltpu.VMEM((H_A, tt, DV_A + DK_A), _F32),
            pltpu.VMEM((H_A, tt, tt), _F32),
            pltpu.VMEM((H_A, tt, DK_A), _BF16),
            pltpu.VMEM((H_A, DK_A, tt), _BF16),
            pltpu.VMEM((H_A, DK_A, DV_A), _F32),
        ],
        compiler_params=pltpu.CompilerParams(
            dimension_semantics=("arbitrary", "arbitrary"), vmem_limit_bytes=VMEM_LIMIT),
        name="gdn",
    )(p32, p32, p32, p32, p32, cw, arow, dtrow, naw)


def _bias_kernel(tab_ref, bucket_ref, o_ref):
    bucket = bucket_ref[...]
    nq, w = bucket.shape
    for h in range(H_B):
        acc = jnp.zeros((nq, w), _F32)
        for b in range(N_BUCKETS):
            acc = acc + jnp.where(bucket == b, tab_ref[b, h], 0.0)
        far = acc[:, 2 * LANE:3 * LANE]
        o_ref[h] = acc[:, 0:2 * LANE] - jnp.concatenate([far, far], axis=1)


def _bias_tiles(rel_table, bucket):
    return pl.pallas_call(
        _bias_kernel,
        in_specs=[pl.BlockSpec(memory_space=pltpu.SMEM),
                  pl.BlockSpec(bucket.shape, lambda: (0, 0))],
        out_specs=pl.BlockSpec((H_B, Q_BLOCK, 2 * LANE), lambda: (0, 0, 0)),
        out_shape=jax.ShapeDtypeStruct((H_B, Q_BLOCK, 2 * LANE), _F32),
        name="rel_bias_tiles",
    )(rel_table, bucket)


def _dsa_kernel(q_ref, k_ref, v_ref, iq_ref, ik_ref, sm_ref, bias_ref, o_ref,
                key_s, mb_s, s_s, tm_s, *, topk):
    nq = Q_BLOCK
    st_w = KEY_TILE
    qi = pl.program_id(1)
    n_st = qi // (st_w // nq) + 1
    hd = D_HEAD_B
    scale = D_HEAD_B ** -0.5

    lane_j = lax.broadcasted_iota(jnp.int32, (nq, st_w), 1)
    row_i = lax.broadcasted_iota(jnp.int32, (nq, st_w), 0)
    q_chunk = qi * (nq // CHUNK) + row_i // CHUNK

    iw = sm_ref[...]
    iq = iq_ref[...]

    def p1(st, carry):
        ikt = ik_ref[pl.ds(pl.multiple_of(st * st_w, st_w), st_w), :]
        acc = jnp.zeros((nq, st_w), _F32)
        for p in range(H_IDX // 2):
            xq = iq[:, p * LANE:(p + 1) * LANE]
            for half in range(2):
                h = 2 * p + half
                s = _dot_nt(xq, ikt[:, half * LANE:(half + 1) * LANE])
                acc = acc + iw[:, 2 * H_A + h:2 * H_A + h + 1] * jnp.maximum(s, 0.0)
        visible = (st * st_w + lane_j) // CHUNK <= q_chunk
        acc = jnp.where(visible, acc, -jnp.inf)
        bits = pltpu.bitcast(acc, jnp.int32)
        key_s[st] = bits ^ ((bits >> 31) & INT_MAX)
        return carry

    lax.fori_loop(0, n_st, p1, 0)

    def count(pred_fn):
        def body(st, acc):
            c = pred_fn(key_s[st], st).astype(jnp.int32)
            for g in range(st_w // LANE):
                acc = acc + c[:, g * LANE:(g + 1) * LANE]
            return acc
        acc = lax.fori_loop(0, n_st, body, jnp.zeros((nq, LANE), jnp.int32))
        return jnp.sum(acc, axis=1, keepdims=True)

    tm_s[0] = jnp.full((nq, LANE), KEY_NEG_INF, jnp.int32)
    tm_s[1] = jnp.zeros((nq, LANE), jnp.int32)

    @pl.when((qi + 1) * nq > topk)
    def _():
        c0 = count(lambda k, st: k >= 0)
        ok0 = c0 >= topk
        t0 = jnp.where(ok0, 0, INT_MIN)
        cnt0 = jnp.where(ok0, c0, n_st * st_w)

        def bit_body(i, carry):
            t, cnt_t = carry
            cand = t + jnp.left_shift(jnp.int32(1), 30 - i)
            c = count(lambda k, st: k >= cand)
            ok = c >= topk
            return jnp.where(ok, cand, t), jnp.where(ok, c, cnt_t)

        t, cnt_t = lax.fori_loop(0, 31, bit_body, (t0, cnt0))
        tm_s[0] = jnp.broadcast_to(t, (nq, LANE))
        tm_s[1] = jnp.full((nq, LANE), INT_MAX, jnp.int32)

        @pl.when(jnp.max(cnt_t) > topk)
        def _():
            r = topk - count(lambda k, st: k > t)
            n_bits = max(1, int(math.ceil(math.log2(key_s.shape[0] * st_w))))

            def idx_body(i, mp):
                cand = mp + jnp.left_shift(jnp.int32(1), n_bits - 1 - i)
                c = count(lambda k, st: jnp.where(k == t, st * st_w + lane_j, INT_MAX) < cand)
                return jnp.where(c < r, cand, mp)

            mp = lax.fori_loop(0, n_bits, idx_body, jnp.zeros((nq, 1), jnp.int32))
            tm_s[1] = jnp.broadcast_to(mp + 1, (nq, LANE))

    t = tm_s[0][:, 0:1]
    m_eff = jnp.where(t == KEY_NEG_INF, 0, tm_s[1][:, 0:1])

    def p2(st, carry):
        k = key_s[st]
        jj = jnp.where(k == t, st * st_w + lane_j, INT_MAX)
        mb_s[st] = jnp.where(k > t, 0.0, jnp.where(jj < m_eff, 0.0, MASK_NEG))
        return carry

    lax.fori_loop(0, n_st, p2, 0)

    for h in range(H_B):
        sl = slice(h * hd, (h + 1) * hd)
        qh = (q_ref[:, sl].astype(_F32) * scale).astype(_BF16)

        def body(st, carry, h=h, sl=sl, qh=qh):
            m_run, l_run, acc = carry
            off = pl.multiple_of(st * st_w, st_w)
            kt = k_ref[pl.ds(off, st_w), sl]
            s_s[...] = _dot_nt(qh, kt) + mb_s[st]

            @pl.when(st >= n_st - 2)
            def _():
                for c in range(st_w // nq):
                    g = st * (st_w // nq) + c

                    @pl.when(g == qi)
                    def _():
                        s_s[:, c * nq:(c + 1) * nq] += bias_ref[h, :, LANE:2 * LANE]

                    @pl.when(g == qi - 1)
                    def _():
                        s_s[:, c * nq:(c + 1) * nq] += bias_ref[h, :, 0:LANE]

            s = s_s[...]
            m_new = jnp.maximum(m_run, jnp.max(s, axis=1, keepdims=True))
            alpha = jnp.exp(m_run - m_new)
            p = jnp.exp(s - m_new)
            l_new = alpha * l_run + jnp.sum(p, axis=1, keepdims=True)
            acc = alpha * acc + _dot(p.astype(_BF16), v_ref[pl.ds(off, st_w), sl])
            return m_new, l_new, acc

        init = (jnp.full((nq, 1), MASK_NEG, _F32), jnp.zeros((nq, 1), _F32), jnp.zeros((nq, hd), _F32))
        _, l_fin, acc = lax.fori_loop(0, n_st, body, init)
        o_ref[:, sl] = (acc / l_fin).astype(o_ref.dtype)


def _dsa(p16, p32, bias, b_, t_, sm_col):
    nq = Q_BLOCK
    nb = t_ // nq
    hd = H_B * D_HEAD_B
    n_st = t_ // KEY_TILE
    topk = min(TOPK_MAX, t_ // 4)
    iq_w = H_IDX * D_IDX
    ik_w = 2 * LANE
    return pl.pallas_call(
        functools.partial(_dsa_kernel, topk=topk),
        grid=(b_, nb),
        in_specs=[
            pl.BlockSpec((nq, hd), lambda b, i: (b * nb + i, 0)),
            pl.BlockSpec((t_, hd), lambda b, i: (b, 1)),
            pl.BlockSpec((t_, hd), lambda b, i: (b, 2)),
            pl.BlockSpec((nq, iq_w), lambda b, i: (b * nb + i, 3 * hd // iq_w)),
            pl.BlockSpec((t_, ik_w), lambda b, i: (b, (3 * hd + iq_w) // ik_w)),
            pl.BlockSpec((nq, LANE), lambda b, i: (b * nb + i, sm_col // LANE)),
            pl.BlockSpec((H_B, nq, 2 * LANE), lambda b, i: (0, 0, 0)),
        ],
        out_specs=pl.BlockSpec((nq, hd), lambda b, i: (b * nb + i, 0)),
        out_shape=jax.ShapeDtypeStruct((b_ * t_, hd), _BF16),
        scratch_shapes=[
            pltpu.VMEM((n_st, nq, KEY_TILE), jnp.int32),
            pltpu.VMEM((n_st, nq, KEY_TILE), _F32),
            pltpu.VMEM((nq, KEY_TILE), _F32),
            pltpu.VMEM((2, nq, LANE), jnp.int32),
        ],
        compiler_params=pltpu.CompilerParams(
            dimension_semantics=("arbitrary", "arbitrary"), vmem_limit_bytes=VMEM_LIMIT),
        name="dsa",
    )(p16, p16, p16, p16, p16, p32, bias)


def _merge_kernel(oa_ref, ob_ref, g_ref, x_ref, bg_ref, wa_ref, wb_ref, wo_ref, o_ref):
    d = x_ref.shape[1]
    a = _dot(oa_ref[...], wa_ref[...])
    b = _dot(ob_ref[...], wb_ref[...])
    g = _sigmoid(g_ref[...] + bg_ref[...])
    merged = g[:, 0:d] * a + g[:, d:2 * d] * b
    o_ref[...] = x_ref[...] + _dot(merged.astype(_BF16), wo_ref[...])


def _merge(oa, ob, p32, x2d, bg, wa, wb, wo, tn):
    n, d = x2d.shape
    row = lambda w: pl.BlockSpec((tn, w), lambda i: (i, 0))
    const = lambda a: pl.BlockSpec(a.shape, lambda i: (0, 0))
    return pl.pallas_call(
        _merge_kernel,
        grid=(n // tn,),
        in_specs=[row(oa.shape[1]), row(ob.shape[1]), row(2 * d), row(d),
                  const(bg), const(wa), const(wb), const(wo)],
        out_specs=row(d),
        out_shape=jax.ShapeDtypeStruct((n, d), _F32),
        compiler_params=pltpu.CompilerParams(
            dimension_semantics=("arbitrary",), vmem_limit_bytes=VMEM_LIMIT),
        name="merge",
    )(oa, ob, p32, x2d, bg, wa, wb, wo)


def _mlp_kernel(x_ref, n2_ref, w1_ref, w2_ref, nf_ref, o_ref, *, final_norm):
    x = x_ref[...]
    h2 = _rms(x, n2_ref[...]).astype(_BF16)
    hid = jnp.maximum(_dot(h2, w1_ref[...]), 0.0)
    y = x + _dot((hid * hid).astype(_BF16), w2_ref[...])
    o_ref[...] = _rms(y, nf_ref[...]) if final_norm else y


def _mlp(x1, n2, w1, w2, nf, tn, final_norm):
    n, d = x1.shape
    row = pl.BlockSpec((tn, d), lambda i: (i, 0))
    const = lambda a: pl.BlockSpec(a.shape, lambda i: (0, 0))
    return pl.pallas_call(
        functools.partial(_mlp_kernel, final_norm=final_norm),
        grid=(n // tn,),
        in_specs=[row, const(n2), const(w1), const(w2), const(nf)],
        out_specs=row,
        out_shape=jax.ShapeDtypeStruct((n, d), _F32),
        compiler_params=pltpu.CompilerParams(
            dimension_semantics=("arbitrary",), vmem_limit_bytes=VMEM_LIMIT),
        name="mlp",
    )(x1, n2, w1, w2, nf)


def _t5_bucket(rel):
    half = N_BUCKETS // 2
    max_exact = half // 2
    base = jnp.where(rel > 0, half, 0)
    n = jnp.abs(rel)
    n_f = jnp.maximum(n, 1).astype(jnp.float32)
    large = max_exact + (jnp.log(n_f / max_exact) / math.log(MAX_DISTANCE / max_exact)
                         * (half - max_exact)).astype(jnp.int32)
    large = jnp.minimum(large, half - 1)
    return base + jnp.where(n < max_exact, n, large)


def _pick_tile(n, prefs):
    for t in prefs:
        if n % t == 0:
            return t
    raise ValueError(f"no tile in {prefs} divides {n}")


def kernel(x, norm1_w, w_in, conv_a_w, a_log, dt_bias, norm_a_w, rel_bias_table, w_gate, b_gate,
           w_proj_a, w_proj_b, w_out, norm2_w, w_ff1, w_ff2, norm_final_w):
    b_, t_, d = x.shape
    depth = norm1_w.shape[0]
    n = b_ * t_
    ha, hb = H_A * DK_A, H_B * D_HEAD_B
    assert t_ % KEY_TILE == 0 and t_ % GDN_CHUNK == 0 and d % LANE == 0
    assert DK_A == DV_A == D_HEAD_B == LANE and 2 * D_IDX == LANE
    assert Q_BLOCK >= MAX_DISTANCE

    o_za = 3 * ha
    o_ba = 4 * ha
    o_aa = o_ba + H_A
    o_qb = o_aa + H_A
    o_iq = o_qb + 3 * hb
    o_ik = o_iq + H_IDX * D_IDX
    o_iw = o_ik + D_IDX

    rel = (jnp.arange(2 * Q_BLOCK, dtype=jnp.int32)[None, :] - Q_BLOCK) \
        - jnp.arange(Q_BLOCK, dtype=jnp.int32)[:, None]
    far = jnp.full((Q_BLOCK, LANE), -(Q_BLOCK + 1), jnp.int32)
    bucket = _t5_bucket(jnp.concatenate([rel, far], axis=1))
    bias = _bias_tiles(rel_bias_table.astype(_F32), bucket)

    x2d = x.reshape(n, d)
    tn = _pick_tile(n, (512, 256, 128))
    for layer in range(depth):
        wi = w_in[layer]
        zpad = lambda w: jnp.zeros((d, w), wi.dtype)
        small = jnp.concatenate([wi[:, o_ba:o_qb], wi[:, o_iw:o_iw + H_IDX], zpad(LANE - 3 * H_A)], axis=1)
        w32 = jnp.concatenate([w_gate[layer], wi[:, 0:o_ba], small, zpad(LANE)], axis=1).astype(_BF16)
        ik = wi[:, o_ik:o_iw]
        w16 = jnp.concatenate([wi[:, o_qb:o_ik], ik, zpad(D_IDX), zpad(D_IDX), ik], axis=1).astype(_BF16)
        nw = norm1_w[layer].reshape(1, d).astype(_F32)
        p32 = _norm_proj(x2d, nw, w32, _F32, tn, _pick_tile(w32.shape[1], (1280, 640, 128)), "proj32")
        p16 = _norm_proj(x2d, nw, w16, _BF16, tn, _pick_tile(w16.shape[1], (1280, 640, 128)), "proj16")

        col_a = 2 * d
        assert col_a % ha == 0
        sm_col = col_a + 4 * ha
        lanes = lambda v, off: jnp.zeros((1, LANE), _F32).at[0, off:off + H_A].set(v.astype(_F32))
        oa = _gdn(p32, conv_a_w[layer].astype(_F32), lanes(a_log[layer], H_A), lanes(dt_bias[layer], H_A),
                  norm_a_w[layer].reshape(1, DV_A).astype(_F32), b_, t_, col_a)
        ob = _dsa(p16, p32, bias, b_, t_, sm_col)

        x2d = _merge(oa, ob, p32, x2d, b_gate[layer].reshape(1, 2 * d).astype(_F32),
                     w_proj_a[layer].astype(_BF16), w_proj_b[layer].astype(_BF16),
                     w_out[layer].astype(_BF16), tn)
        x2d = _mlp(x2d, norm2_w[layer].reshape(1, d).astype(_F32), w_ff1[layer].astype(_BF16),
                   w_ff2[layer].astype(_BF16), norm_final_w.reshape(1, d).astype(_F32),
                   _pick_tile(n, (256, 128)), final_norm=layer == depth - 1)
    return x2d.reshape(b_, t_, d)
```

```python
import functools
import math

import jax
import jax.numpy as jnp
import numpy as np
from jax import lax
from jax.experimental import pallas as pl
from jax.experimental.pallas import tpu as pltpu

EPS = 1e-6
H_A = 8
DK_A = 128
DV_A = 128
CONV_K = 4
GDN_CHUNK = 128
H_B = 8
D_HEAD_B = 128
H_IDX = 8
D_IDX = 64
TOPK_MAX = 256
Q_BLOCK = 128
CHUNK = 64
KEY_TILE = 512
N_BUCKETS = 32
MAX_DISTANCE = 128

LANE = 128
VMEM_LIMIT = 56 * 1024 * 1024

INT_MIN = -(2**31)
INT_MAX = 2**31 - 1
KEY_NEG_INF = -2139095041
MASK_NEG = -1e30

_F32 = jnp.float32
_BF16 = jnp.bfloat16


def _dot(a, b):
    return jnp.dot(a, b, preferred_element_type=_F32)


def _dot_nt(a, b):
    return lax.dot_general(a, b, (((1,), (1,)), ((), ())), preferred_element_type=_F32)


def _sigmoid(x):
    return 1.0 / (1.0 + jnp.exp(-x))


def _rms(x, w):
    return x * lax.rsqrt(jnp.mean(x * x, axis=-1, keepdims=True) + EPS) * w


def _norm_proj_kernel(x_ref, nw_ref, w_ref, o_ref, h_ref):
    @pl.when(pl.program_id(1) == 0)
    def _():
        h_ref[...] = _rms(x_ref[...], nw_ref[...]).astype(_BF16)

    o_ref[...] = _dot(h_ref[...], w_ref[...]).astype(o_ref.dtype)


def _norm_proj(x2d, nw, w, out_dtype, tn, tc, name):
    n, d = x2d.shape
    c = w.shape[1]
    return pl.pallas_call(
        _norm_proj_kernel,
        grid=(n // tn, c // tc),
        in_specs=[
            pl.BlockSpec((tn, d), lambda i, j: (i, 0)),
            pl.BlockSpec((1, d), lambda i, j: (0, 0)),
            pl.BlockSpec((d, tc), lambda i, j: (0, j)),
        ],
        out_specs=pl.BlockSpec((tn, tc), lambda i, j: (i, j)),
        out_shape=jax.ShapeDtypeStruct((n, c), out_dtype),
        scratch_shapes=[pltpu.VMEM((tn, d), _BF16)],
        compiler_params=pltpu.CompilerParams(
            dimension_semantics=("arbitrary", "arbitrary"), vmem_limit_bytes=VMEM_LIMIT),
        name=name,
    )(x2d, nw, w)


def _gdn_kernel(q_ref, k_ref, v_ref, z_ref, sm_ref, cw_ref, arow_ref, dtrow_ref, naw_ref, o_ref,
                xbuf, s_ref, kn_s, kb_s, qn_s, rhs_s, dec_s, qdec_s, ktt_s, eg_s, m_s, x_s):
    tt = q_ref.shape[0]
    hd = H_A * DK_A

    @pl.when(pl.program_id(1) == 0)
    def _():
        xbuf[:, 0:8, :] = jnp.zeros((3, 8, hd), _F32)
        s_ref[...] = jnp.zeros_like(s_ref)

    for idx, ref in enumerate((q_ref, k_ref, v_ref)):
        xbuf[idx, 8:8 + tt, :] = ref[...]

    sm = sm_ref[...]
    beta_full = _sigmoid(sm)
    xg = sm + dtrow_ref[...]
    softplus = jnp.maximum(xg, 0.0) + jnp.log(1.0 + jnp.exp(-jnp.abs(xg)))
    g_full = -jnp.exp(arow_ref[...]) * softplus
    row = lax.broadcasted_iota(jnp.int32, (tt, tt), 0)
    col = lax.broadcasted_iota(jnp.int32, (tt, tt), 1)
    tri = (col <= row).astype(_F32)
    gcum = jnp.dot(tri, g_full, preferred_element_type=_F32, precision=lax.Precision.HIGHEST)
    gcum_t = gcum.T
    strict = col < row
    eye = (col == row).astype(_F32)

    for h in range(H_A):
        sl = slice(h * DK_A, (h + 1) * DK_A)
        conv = []
        for idx in range(3):
            acc = None
            for j in range(CONV_K):
                term = cw_ref[j:j + 1, idx * hd + h * DK_A: idx * hd + (h + 1) * DK_A] * \
                    xbuf[idx, 8 - (CONV_K - 1) + j: 8 - (CONV_K - 1) + j + tt, sl]
                acc = term if acc is None else acc + term
            conv.append(acc * _sigmoid(acc))
        qh, kh, vh = conv
        qn = qh * lax.rsqrt(jnp.sum(qh * qh, axis=-1, keepdims=True) + EPS) * (DK_A ** -0.5)
        kn = kh * lax.rsqrt(jnp.sum(kh * kh, axis=-1, keepdims=True) + EPS)
        bcol = beta_full[:, h:h + 1]
        gcol = gcum[:, H_A + h:H_A + h + 1]
        grow = gcum_t[H_A + h:H_A + h + 1, :]
        glast = gcum[tt - 1:tt, H_A + h:H_A + h + 1]
        kb = kn * bcol
        eg = jnp.exp(gcol)
        rhs_s[h, :, 0:DV_A] = vh * bcol
        rhs_s[h, :, DV_A:DV_A + DK_A] = kb * eg
        dec_s[h] = jnp.where(strict, jnp.exp(jnp.where(strict, gcol - grow, 0.0)), 0.0)
        qdec_s[h] = (qn * eg).astype(_BF16)
        ktt_s[h] = (kn * jnp.exp(glast - gcol)).T.astype(_BF16)
        kn_s[h] = kn.astype(_BF16)
        kb_s[h] = kb.astype(_BF16)
        qn_s[h] = qn.astype(_BF16)
        eg_s[h] = jnp.broadcast_to(jnp.exp(glast), (DK_A, DV_A))

    for idx in range(3):
        xbuf[idx, 0:8, :] = xbuf[idx, tt:tt + 8, :]

    n_sq = int(math.log2(tt))

    heads = range(H_A)
    for h in heads:
        m = -(_dot_nt(kb_s[h], kn_s[h]) * dec_s[h])
        m_s[h] = m
        x_s[h] = m
    for _ in range(n_sq - 1):
        for h in heads:
            mb = m_s[h].astype(_BF16)
            m = _dot(mb, mb)
            m_s[h] = m
            x_s[h] = x_s[h] + m + _dot(x_s[h].astype(_BF16), m.astype(_BF16))
    for h in heads:
        rhs = rhs_s[h]
        rhs_s[h] = rhs + _dot(x_s[h].astype(_BF16), rhs.astype(_BF16))
    for h in heads:
        m_s[h] = _dot_nt(qn_s[h], kn_s[h]) * (dec_s[h] + eye)
    for h in heads:
        sb = s_ref[h].astype(_BF16)
        vnew = rhs_s[h, :, 0:DV_A] - _dot(rhs_s[h, :, DV_A:DV_A + DK_A].astype(_BF16), sb)
        vb = vnew.astype(_BF16)
        o = _dot(qdec_s[h], sb) + _dot(m_s[h].astype(_BF16), vb)
        s_ref[h] = s_ref[h] * eg_s[h] + _dot(ktt_s[h], vb)
        rhs_s[h, :, 0:DV_A] = o

    naw = naw_ref[...]
    for h in range(H_A):
        sl = slice(h * DV_A, (h + 1) * DV_A)
        o = rhs_s[h, :, 0:DV_A]
        z = z_ref[:, sl]
        on = o * lax.rsqrt(jnp.mean(o * o, axis=-1, keepdims=True) + EPS) * naw
        o_ref[:, sl] = (on * (z * _sigmoid(z))).astype(o_ref.dtype)


def _gdn(p32, cw, arow, dtrow, naw, b_, t_, col0):
    tt = GDN_CHUNK
    hd = H_A * DK_A
    nt = t_ // tt
    cb = col0 // hd

    def rowblk(j):
        return pl.BlockSpec((tt, hd), lambda b, t: (b * nt + t, cb + j))

    sm_blk = (col0 + 4 * hd) // LANE
    const = lambda shape: pl.BlockSpec(shape, lambda b, t: (0, 0))
    return pl.pallas_call(
        _gdn_kernel,
        grid=(b_, nt),
        in_specs=[rowblk(0), rowblk(1), rowblk(2), rowblk(3),
                  pl.BlockSpec((tt, LANE), lambda b, t: (b * nt + t, sm_blk)),
                  const((CONV_K, 3 * hd)), const((1, LANE)), const((1, LANE)), const((1, DV_A))],
        out_specs=pl.BlockSpec((tt, hd), lambda b, t: (b * nt + t, 0)),
        out_shape=jax.ShapeDtypeStruct((b_ * t_, hd), _BF16),
        scratch_shapes=[
            pltpu.VMEM((3, tt + 8, hd), _F32),
            pltpu.VMEM((H_A, DK_A, DV_A), _F32),
            pltpu.VMEM((H_A, tt, DK_A), _BF16),
            pltpu.VMEM((H_A, tt, DK_A), _BF16),
            pltpu.VMEM((H_A, tt, DK_A), _BF16),
            pltpu.VMEM((H_A, tt, DV_A + DK_A), _F32),
            pltpu.VMEM((H_A, tt, tt), _F32),
            pltpu.VMEM((H_A, tt, DK_A), _BF16),
            pltpu.VMEM((H_A, DK_A, tt), _BF16),
            pltpu.VMEM((H_A, DK_A, DV_A), _F32),
            pltpu.VMEM((H_A, tt, tt), _F32),
            pltpu.VMEM((H_A, tt, tt), _F32),
        ],
        compiler_params=pltpu.CompilerParams(
            dimension_semantics=("arbitrary", "arbitrary"), vmem_limit_bytes=VMEM_LIMIT),
        name="gdn",
    )(p32, p32, p32, p32, p32, cw, arow, dtrow, naw)


def _bias_kernel(tab_ref, bucket_ref, o_ref):
    bucket = bucket_ref[...]
    nq, w = bucket.shape
    for h in range(H_B):
        acc = jnp.zeros((nq, w), _F32)
        for b in range(N_BUCKETS):
            acc = acc + jnp.where(bucket == b, tab_ref[b, h], 0.0)
        far = acc[:, 2 * LANE:3 * LANE]
        o_ref[h] = acc[:, 0:2 * LANE] - jnp.concatenate([far, far], axis=1)


def _bias_tiles(rel_table, bucket):
    return pl.pallas_call(
        _bias_kernel,
        in_specs=[pl.BlockSpec(memory_space=pltpu.SMEM),
                  pl.BlockSpec(bucket.shape, lambda: (0, 0))],
        out_specs=pl.BlockSpec((H_B, Q_BLOCK, 2 * LANE), lambda: (0, 0, 0)),
        out_shape=jax.ShapeDtypeStruct((H_B, Q_BLOCK, 2 * LANE), _F32),
        name="rel_bias_tiles",
    )(rel_table, bucket)


def _dsa_kernel(q_ref, k_ref, v_ref, iq_ref, ik_ref, sm_ref, bias_ref, o_ref,
                key_s, qs_s, iwb_s, m_s, l_s, acc_s, s_s, si_s, sf_s, *, topk):
    nq = q_ref.shape[0]
    st_w = KEY_TILE
    assert nq == st_w
    ng = st_w // LANE
    qb = pl.program_id(1)
    n_st = qb + 1
    hd = D_HEAD_B
    scale = D_HEAD_B ** -0.5
    f_topk = float(topk)

    lane_j = lax.broadcasted_iota(jnp.int32, (nq, LANE), 1)
    row_i = lax.broadcasted_iota(jnp.int32, (nq, LANE), 0)

    def grp(g):
        return slice(g * LANE, (g + 1) * LANE)

    iw = sm_ref[...]
    for h in range(H_B):
        sl = slice(h * hd, (h + 1) * hd)
        qs_s[h] = (q_ref[:, sl].astype(_F32) * scale).astype(_BF16)
        iwb_s[h] = jnp.broadcast_to(iw[:, 2 * H_A + h:2 * H_A + h + 1], (nq, LANE))
        m_s[h] = jnp.full((nq, LANE), MASK_NEG, _F32)
        l_s[h] = jnp.zeros((nq, LANE), _F32)
        acc_s[h] = jnp.zeros((nq, hd), _F32)

    def scores(st, diag):
        ikt = ik_ref[pl.ds(pl.multiple_of(st * st_w, st_w), st_w), :]
        acc = [jnp.zeros((nq, LANE), _F32) for _ in range(ng)]
        for p in range(H_IDX // 2):
            xq = iq_ref[:, p * LANE:(p + 1) * LANE]
            for half in range(2):
                h = 2 * p + half
                s = jnp.maximum(_dot_nt(xq, ikt[:, half * LANE:(half + 1) * LANE]), 0.0)
                w = iwb_s[h]
                for g in range(ng):
                    acc[g] = acc[g] + w * s[:, grp(g)]
        for g in range(ng):
            a = acc[g]
            if diag:
                a = jnp.where((g * LANE + lane_j) // CHUNK <= row_i // CHUNK, a, -jnp.inf)
            bits = pltpu.bitcast(a, jnp.int32)
            key_s[st, :, grp(g)] = bits ^ ((bits >> 31) & INT_MAX)

    def p1(st, carry):
        scores(st, False)
        return carry

    lax.fori_loop(0, qb, p1, 0)
    scores(qb, True)

    ones_b = jnp.ones((LANE, LANE), _BF16)
    assert key_s.shape[0] * ng <= 256

    T_, CAND_, MIDX_ = 0, 1, 2
    ACC_, CNT_ = 0, 1
    rb = 64
    lane_rb = lax.broadcasted_iota(jnp.int32, (rb, LANE), 1)

    def count(pred_fn):
        sf_s[ACC_] = jnp.zeros((nq, LANE), _F32)

        def body(st, carry):
            for r0 in range(0, nq, rb):
                rs = slice(r0, r0 + rb)
                acc = sf_s[ACC_, rs, :]
                for g in range(ng):
                    acc = acc + jnp.where(
                        pred_fn(key_s[st, rs, grp(g)], st * st_w + g * LANE + lane_rb, rs), 1.0, 0.0)
                sf_s[ACC_, rs, :] = acc
            return carry

        lax.fori_loop(0, n_st, body, 0)
        return _dot(sf_s[ACC_].astype(_BF16), ones_b)

    c0 = count(lambda k, j, rs: k >= 0)
    ok0 = c0 >= f_topk
    si_s[T_] = jnp.where(ok0, 0, INT_MIN)
    sf_s[CNT_] = jnp.where(ok0, c0, (n_st * st_w).astype(_F32))

    def bit_body(i, carry):
        t_old = si_s[T_]
        cand = t_old + jnp.left_shift(jnp.int32(1), 30 - i)
        si_s[CAND_] = cand
        c = count(lambda k, j, rs: k >= si_s[CAND_, rs, :])
        ok = c >= f_topk
        si_s[T_] = jnp.where(ok, cand, t_old)
        sf_s[CNT_] = jnp.where(ok, c, sf_s[CNT_])
        return carry

    lax.fori_loop(0, 31, bit_body, 0)
    finite_thr = si_s[T_] != KEY_NEG_INF
    si_s[MIDX_] = jnp.where(finite_thr, INT_MAX, 0)

    @pl.when(jnp.max(jnp.where(finite_thr, sf_s[CNT_], 0.0)) > f_topk)
    def _():
        sf_s[CNT_] = f_topk - count(lambda k, j, rs: k > si_s[T_, rs, :])
        n_bits = max(1, int(math.ceil(math.log2(key_s.shape[0] * st_w))))
        si_s[MIDX_] = jnp.zeros((nq, LANE), jnp.int32)

        def idx_body(i, carry):
            mp = si_s[MIDX_]
            cand = mp + jnp.left_shift(jnp.int32(1), n_bits - 1 - i)
            si_s[CAND_] = cand
            c = count(lambda k, j, rs: jnp.where(k == si_s[T_, rs, :], j, INT_MAX) < si_s[CAND_, rs, :])
            si_s[MIDX_] = jnp.where(c < sf_s[CNT_], cand, mp)
            return carry

        lax.fori_loop(0, n_bits, idx_body, 0)
        si_s[MIDX_] = jnp.where(si_s[T_] != KEY_NEG_INF, si_s[MIDX_] + 1, 0)

    def p2(st, carry):
        for r0 in range(0, nq, rb):
            rs = slice(r0, r0 + rb)
            t = si_s[T_, rs, :]
            m_idx = si_s[MIDX_, rs, :]
            for g in range(ng):
                k = key_s[st, rs, grp(g)]
                jj = jnp.where(k == t, st * st_w + g * LANE + lane_rb, INT_MAX)
                mb = jnp.where(k > t, 0.0, jnp.where(jj < m_idx, 0.0, MASK_NEG))
                key_s[st, rs, grp(g)] = pltpu.bitcast(mb, jnp.int32)
        return carry

    lax.fori_loop(0, n_st, p2, 0)

    def attend(st, mode):
        off = pl.multiple_of(st * st_w, st_w)
        for h in range(H_B):
            sl = slice(h * hd, (h + 1) * hd)
            s = _dot_nt(qs_s[h], k_ref[pl.ds(off, st_w), sl]) + pltpu.bitcast(key_s[st], _F32)
            if mode:
                s_s[...] = s
                b_prev = bias_ref[h, :, 0:LANE]
                if mode == 1:
                    s_s[0:LANE, (ng - 1) * LANE:ng * LANE] += b_prev
                else:
                    b_diag = bias_ref[h, :, LANE:2 * LANE]
                    for rt in range(ng):
                        s_s[grp(rt), grp(rt)] += b_diag
                        if rt:
                            s_s[grp(rt), grp(rt - 1)] += b_prev
                s = s_s[...]
            m_old = m_s[h]
            gmax = s[:, grp(0)]
            for g in range(1, ng):
                gmax = jnp.maximum(gmax, s[:, grp(g)])
            m_new = jnp.maximum(m_old, jnp.max(gmax, axis=1, keepdims=True))
            alpha = jnp.exp(m_old - m_new)
            pieces = [jnp.exp(s[:, grp(g)] - m_new) for g in range(ng)]
            psum = pieces[0]
            for g in range(1, ng):
                psum = psum + pieces[g]
            p = jnp.concatenate(pieces, axis=1).astype(_BF16)
            m_s[h] = m_new
            l_s[h] = alpha * l_s[h] + psum
            acc_s[h] = alpha * acc_s[h] + _dot(p, v_ref[pl.ds(off, st_w), sl])

    def p3(st, carry):
        attend(st, 0)
        return carry

    lax.fori_loop(0, qb - 1, p3, 0)

    @pl.when(qb > 0)
    def _():
        attend(qb - 1, 1)

    attend(qb, 2)

    for h in range(H_B):
        sl = slice(h * hd, (h + 1) * hd)
        l_fin = jnp.sum(l_s[h], axis=1, keepdims=True)
        o_ref[:, sl] = (acc_s[h] / l_fin).astype(o_ref.dtype)


def _dsa(p16, p32, bias, b_, t_, sm_col):
    nq = KEY_TILE
    nb = t_ // nq
    hd = H_B * D_HEAD_B
    topk = min(TOPK_MAX, t_ // 4)
    iq_w = H_IDX * D_IDX
    ik_w = 2 * LANE
    resident = functools.partial(pl.BlockSpec, pipeline_mode=pl.Buffered(1))
    return pl.pallas_call(
        functools.partial(_dsa_kernel, topk=topk),
        grid=(b_, nb),
        in_specs=[
            pl.BlockSpec((nq, hd), lambda b, i: (b * nb + i, 0)),
            resident((t_, hd), lambda b, i: (b, 1)),
            resident((t_, hd), lambda b, i: (b, 2)),
            pl.BlockSpec((nq, iq_w), lambda b, i: (b * nb + i, 3 * hd // iq_w)),
            resident((t_, ik_w), lambda b, i: (b, (3 * hd + iq_w) // ik_w)),
            pl.BlockSpec((nq, LANE), lambda b, i: (b * nb + i, sm_col // LANE)),
            resident((H_B, Q_BLOCK, 2 * LANE), lambda b, i: (0, 0, 0)),
        ],
        out_specs=pl.BlockSpec((nq, hd), lambda b, i: (b * nb + i, 0)),
        out_shape=jax.ShapeDtypeStruct((b_ * t_, hd), _BF16),
        scratch_shapes=[
            pltpu.VMEM((nb, nq, KEY_TILE), jnp.int32),
            pltpu.VMEM((H_B, nq, D_HEAD_B), _BF16),
            pltpu.VMEM((H_B, nq, LANE), _F32),
            pltpu.VMEM((H_B, nq, LANE), _F32),
            pltpu.VMEM((H_B, nq, LANE), _F32),
            pltpu.VMEM((H_B, nq, D_HEAD_B), _F32),
            pltpu.VMEM((nq, KEY_TILE), _F32),
            pltpu.VMEM((3, nq, LANE), jnp.int32),
            pltpu.VMEM((2, nq, LANE), _F32),
        ],
        compiler_params=pltpu.CompilerParams(
            dimension_semantics=("arbitrary", "arbitrary"), vmem_limit_bytes=VMEM_LIMIT),
        name="dsa",
    )(p16, p16, p16, p16, p16, p32, bias)


def _merge_kernel(oa_ref, ob_ref, g_ref, x_ref, bg_ref, wa_ref, wb_ref, wo_ref, o_ref):
    d = x_ref.shape[1]
    a = _dot(oa_ref[...], wa_ref[...])
    b = _dot(ob_ref[...], wb_ref[...])
    g = _sigmoid(g_ref[...] + bg_ref[...])
    merged = g[:, 0:d] * a + g[:, d:2 * d] * b
    o_ref[...] = x_ref[...] + _dot(merged.astype(_BF16), wo_ref[...])


def _merge(oa, ob, p32, x2d, bg, wa, wb, wo, tn):
    n, d = x2d.shape
    row = lambda w: pl.BlockSpec((tn, w), lambda i: (i, 0))
    const = lambda a: pl.BlockSpec(a.shape, lambda i: (0, 0))
    return pl.pallas_call(
        _merge_kernel,
        grid=(n // tn,),
        in_specs=[row(oa.shape[1]), row(ob.shape[1]), row(2 * d), row(d),
                  const(bg), const(wa), const(wb), const(wo)],
        out_specs=row(d),
        out_shape=jax.ShapeDtypeStruct((n, d), _F32),
        compiler_params=pltpu.CompilerParams(
            dimension_semantics=("arbitrary",), vmem_limit_bytes=VMEM_LIMIT),
        name="merge",
    )(oa, ob, p32, x2d, bg, wa, wb, wo)


def _mlp_kernel(x_ref, n2_ref, w1_ref, w2_ref, nf_ref, o_ref, *, final_norm):
    x = x_ref[...]
    h2 = _rms(x, n2_ref[...]).astype(_BF16)
    hid = jnp.maximum(_dot(h2, w1_ref[...]), 0.0)
    y = x + _dot((hid * hid).astype(_BF16), w2_ref[...])
    o_ref[...] = _rms(y, nf_ref[...]) if final_norm else y


def _mlp(x1, n2, w1, w2, nf, tn, final_norm):
    n, d = x1.shape
    row = pl.BlockSpec((tn, d), lambda i: (i, 0))
    const = lambda a: pl.BlockSpec(a.shape, lambda i: (0, 0))
    return pl.pallas_call(
        functools.partial(_mlp_kernel, final_norm=final_norm),
        grid=(n // tn,),
        in_specs=[row, const(n2), const(w1), const(w2), const(nf)],
        out_specs=row,
        out_shape=jax.ShapeDtypeStruct((n, d), _F32),
        compiler_params=pltpu.CompilerParams(
            dimension_semantics=("arbitrary",), vmem_limit_bytes=VMEM_LIMIT),
        name="mlp",
    )(x1, n2, w1, w2, nf)


def _t5_bucket(rel):
    half = N_BUCKETS // 2
    max_exact = half // 2
    base = jnp.where(rel > 0, half, 0)
    n = jnp.abs(rel)
    n_f = jnp.maximum(n, 1).astype(jnp.float32)
    large = max_exact + (jnp.log(n_f / max_exact) / math.log(MAX_DISTANCE / max_exact)
                         * (half - max_exact)).astype(jnp.int32)
    large = jnp.minimum(large, half - 1)
    return base + jnp.where(n < max_exact, n, large)


def _pick_tile(n, prefs):
    for t in prefs:
        if n % t == 0:
            return t
    raise ValueError(f"no tile in {prefs} divides {n}")


def kernel(x, norm1_w, w_in, conv_a_w, a_log, dt_bias, norm_a_w, rel_bias_table, w_gate, b_gate,
           w_proj_a, w_proj_b, w_out, norm2_w, w_ff1, w_ff2, norm_final_w):
    b_, t_, d = x.shape
    depth = norm1_w.shape[0]
    n = b_ * t_
    ha, hb = H_A * DK_A, H_B * D_HEAD_B
    assert t_ % KEY_TILE == 0 and t_ % GDN_CHUNK == 0 and d % LANE == 0
    assert DK_A == DV_A == D_HEAD_B == LANE and 2 * D_IDX == LANE
    assert Q_BLOCK >= MAX_DISTANCE

    o_za = 3 * ha
    o_ba = 4 * ha
    o_aa = o_ba + H_A
    o_qb = o_aa + H_A
    o_iq = o_qb + 3 * hb
    o_ik = o_iq + H_IDX * D_IDX
    o_iw = o_ik + D_IDX

    rel = (jnp.arange(2 * Q_BLOCK, dtype=jnp.int32)[None, :] - Q_BLOCK) \
        - jnp.arange(Q_BLOCK, dtype=jnp.int32)[:, None]
    far = jnp.full((Q_BLOCK, LANE), -(Q_BLOCK + 1), jnp.int32)
    bucket = _t5_bucket(jnp.concatenate([rel, far], axis=1))
    bias = _bias_tiles(rel_bias_table.astype(_F32), bucket)

    x2d = x.reshape(n, d)
    tn = _pick_tile(n, (512, 256, 128))
    for layer in range(depth):
        wi = w_in[layer]
        zpad = lambda w: jnp.zeros((d, w), wi.dtype)
        small = jnp.concatenate([wi[:, o_ba:o_qb], wi[:, o_iw:o_iw + H_IDX], zpad(LANE - 3 * H_A)], axis=1)
        w32 = jnp.concatenate([w_gate[layer], wi[:, 0:o_ba], small, zpad(LANE)], axis=1).astype(_BF16)
        ik = wi[:, o_ik:o_iw]
        w16 = jnp.concatenate([wi[:, o_qb:o_ik], ik, zpad(D_IDX), zpad(D_IDX), ik], axis=1).astype(_BF16)
        nw = norm1_w[layer].reshape(1, d).astype(_F32)
        p32 = _norm_proj(x2d, nw, w32, _F32, tn, _pick_tile(w32.shape[1], (1280, 640, 128)), "proj32")
        p16 = _norm_proj(x2d, nw, w16, _BF16, tn, _pick_tile(w16.shape[1], (1280, 640, 128)), "proj16")

        col_a = 2 * d
        assert col_a % ha == 0
        sm_col = col_a + 4 * ha
        lanes = lambda v, off: jnp.zeros((1, LANE), _F32).at[0, off:off + H_A].set(v.astype(_F32))
        oa = _gdn(p32, conv_a_w[layer].astype(_F32), lanes(a_log[layer], H_A), lanes(dt_bias[layer], H_A),
                  norm_a_w[layer].reshape(1, DV_A).astype(_F32), b_, t_, col_a)
        ob = _dsa(p16, p32, bias, b_, t_, sm_col)

        x2d = _merge(oa, ob, p32, x2d, b_gate[layer].reshape(1, 2 * d).astype(_F32),
                     w_proj_a[layer].astype(_BF16), w_proj_b[layer].astype(_BF16),
                     w_out[layer].astype(_BF16), tn)
        x2d = _mlp(x2d, norm2_w[layer].reshape(1, d).astype(_F32), w_ff1[layer].astype(_BF16),
                   w_ff2[layer].astype(_BF16), norm_final_w.reshape(1, d).astype(_F32),
                   _pick_tile(n, (256, 128)), final_norm=layer == depth - 1)
    return x2d.reshape(b_, t_, d)
```

```python
import functools
import math

import jax
import jax.numpy as jnp
import numpy as np
from jax import lax
from jax.experimental import pallas as pl
from jax.experimental.pallas import tpu as pltpu

EPS = 1e-6
H_A = 8
DK_A = 128
DV_A = 128
CONV_K = 4
GDN_CHUNK = 128
H_B = 8
D_HEAD_B = 128
H_IDX = 8
D_IDX = 64
TOPK_MAX = 256
Q_BLOCK = 128
CHUNK = 64
KEY_TILE = 512
N_BUCKETS = 32
MAX_DISTANCE = 128

LANE = 128
VMEM_LIMIT = 56 * 1024 * 1024

INT_MIN = -(2**31)
INT_MAX = 2**31 - 1
KEY_NEG_INF = -2139095041
MASK_NEG = -1e30
LOG2E = math.log2(math.e)

_F32 = jnp.float32
_BF16 = jnp.bfloat16


def _dot(a, b):
    return jnp.dot(a, b, preferred_element_type=_F32)


def _dot_nt(a, b):
    return lax.dot_general(a, b, (((1,), (1,)), ((), ())), preferred_element_type=_F32)


def _sigmoid(x):
    return 1.0 / (1.0 + jnp.exp(-x))


def _rms(x, w):
    return x * lax.rsqrt(jnp.mean(x * x, axis=-1, keepdims=True) + EPS) * w


def _norm_proj_kernel(x_ref, nw_ref, w_ref, ws_ref, o_ref, os_ref, *, chunk):
    h = _rms(x_ref[...], nw_ref[...]).astype(_BF16)
    os_ref[...] = _dot(h, ws_ref[...])
    for c0 in range(0, w_ref.shape[1], chunk):
        o_ref[:, c0:c0 + chunk] = _dot(h, w_ref[:, c0:c0 + chunk]).astype(o_ref.dtype)


def _norm_proj(x2d, nw, w, ws, tn):
    n, d = x2d.shape
    c = w.shape[1]
    const = functools.partial(pl.BlockSpec, pipeline_mode=pl.Buffered(1))
    return pl.pallas_call(
        functools.partial(_norm_proj_kernel, chunk=_pick_tile(c, (768, 512, 256, 128))),
        grid=(n // tn,),
        in_specs=[
            pl.BlockSpec((tn, d), lambda i: (i, 0)),
            const((1, d), lambda i: (0, 0)),
            const((d, c), lambda i: (0, 0)),
            const((d, LANE), lambda i: (0, 0)),
        ],
        out_specs=[pl.BlockSpec((tn, c), lambda i: (i, 0)), pl.BlockSpec((tn, LANE), lambda i: (i, 0))],
        out_shape=[jax.ShapeDtypeStruct((n, c), _BF16), jax.ShapeDtypeStruct((n, LANE), _F32)],
        compiler_params=pltpu.CompilerParams(
            dimension_semantics=("arbitrary",), vmem_limit_bytes=VMEM_LIMIT),
        name="proj",
    )(x2d, nw, w, ws)


def _gdn_kernel(q_ref, k_ref, v_ref, z_ref, sm_ref, cw_ref, arow_ref, dtrow_ref, naw_ref, o_ref,
                xbuf, s_ref, kn_s, kb_s, qn_s, rhs_s, dec_s, qdec_s, ktt_s, eg_s, m_s, x_s):
    tt = q_ref.shape[0]
    hd = H_A * DK_A

    @pl.when(pl.program_id(1) == 0)
    def _():
        xbuf[:, 0:8, :] = jnp.zeros((3, 8, hd), _F32)
        s_ref[...] = jnp.zeros_like(s_ref)

    for idx, ref in enumerate((q_ref, k_ref, v_ref)):
        xbuf[idx, 8:8 + tt, :] = ref[...].astype(_F32)

    sm = sm_ref[...]
    beta_full = _sigmoid(sm)
    xg = sm + dtrow_ref[...]
    softplus = jnp.maximum(xg, 0.0) + jnp.log(1.0 + jnp.exp(-jnp.abs(xg)))
    g_full = -jnp.exp(arow_ref[...]) * softplus
    row = lax.broadcasted_iota(jnp.int32, (tt, tt), 0)
    col = lax.broadcasted_iota(jnp.int32, (tt, tt), 1)
    tri = (col <= row).astype(_F32)
    gcum = jnp.dot(tri, g_full, preferred_element_type=_F32, precision=lax.Precision.HIGHEST)
    gcum_t = gcum.T
    strict = col < row
    eye = (col == row).astype(_F32)

    for h in range(H_A):
        sl = slice(h * DK_A, (h + 1) * DK_A)
        conv = []
        for idx in range(3):
            acc = None
            for j in range(CONV_K):
                term = cw_ref[j:j + 1, idx * hd + h * DK_A: idx * hd + (h + 1) * DK_A] * \
                    xbuf[idx, 8 - (CONV_K - 1) + j: 8 - (CONV_K - 1) + j + tt, sl]
                acc = term if acc is None else acc + term
            conv.append(acc * _sigmoid(acc))
        qh, kh, vh = conv
        qn = qh * lax.rsqrt(jnp.sum(qh * qh, axis=-1, keepdims=True) + EPS) * (DK_A ** -0.5)
        kn = kh * lax.rsqrt(jnp.sum(kh * kh, axis=-1, keepdims=True) + EPS)
        bcol = beta_full[:, h:h + 1]
        gcol = gcum[:, H_A + h:H_A + h + 1]
        grow = gcum_t[H_A + h:H_A + h + 1, :]
        glast = gcum[tt - 1:tt, H_A + h:H_A + h + 1]
        kb = kn * bcol
        eg = jnp.exp(gcol)
        rhs_s[h, :, 0:DV_A] = vh * bcol
        rhs_s[h, :, DV_A:DV_A + DK_A] = kb * eg
        dec_s[h] = jnp.where(strict, jnp.exp(jnp.where(strict, gcol - grow, 0.0)), 0.0)
        qdec_s[h] = (qn * eg).astype(_BF16)
        ktt_s[h] = (kn * jnp.exp(glast - gcol)).T.astype(_BF16)
        kn_s[h] = kn.astype(_BF16)
        kb_s[h] = kb.astype(_BF16)
        qn_s[h] = qn.astype(_BF16)
        eg_s[h] = jnp.broadcast_to(jnp.exp(glast), (DK_A, DV_A))

    for idx in range(3):
        xbuf[idx, 0:8, :] = xbuf[idx, tt:tt + 8, :]

    n_sq = int(math.log2(tt))

    heads = range(H_A)
    for h in heads:
        m = -(_dot_nt(kb_s[h], kn_s[h]) * dec_s[h])
        m_s[h] = m
        x_s[h] = m
    for _ in range(n_sq - 1):
        for h in heads:
            mb = m_s[h].astype(_BF16)
            m = _dot(mb, mb)
            m_s[h] = m
            x_s[h] = x_s[h] + m + _dot(x_s[h].astype(_BF16), m.astype(_BF16))
    for h in heads:
        rhs = rhs_s[h]
        rhs_s[h] = rhs + _dot(x_s[h].astype(_BF16), rhs.astype(_BF16))
    for h in heads:
        m_s[h] = _dot_nt(qn_s[h], kn_s[h]) * (dec_s[h] + eye)
    for h in heads:
        sb = s_ref[h].astype(_BF16)
        vnew = rhs_s[h, :, 0:DV_A] - _dot(rhs_s[h, :, DV_A:DV_A + DK_A].astype(_BF16), sb)
        vb = vnew.astype(_BF16)
        o = _dot(qdec_s[h], sb) + _dot(m_s[h].astype(_BF16), vb)
        s_ref[h] = s_ref[h] * eg_s[h] + _dot(ktt_s[h], vb)
        rhs_s[h, :, 0:DV_A] = o

    naw = naw_ref[...]
    for h in range(H_A):
        sl = slice(h * DV_A, (h + 1) * DV_A)
        o = rhs_s[h, :, 0:DV_A]
        z = z_ref[:, sl].astype(_F32)
        on = o * lax.rsqrt(jnp.mean(o * o, axis=-1, keepdims=True) + EPS) * naw
        o_ref[:, sl] = (on * (z * _sigmoid(z))).astype(o_ref.dtype)


def _gdn(pb, ps, cw, arow, dtrow, naw, b_, t_, col0):
    tt = GDN_CHUNK
    hd = H_A * DK_A
    nt = t_ // tt
    cb = col0 // hd

    def rowblk(j):
        return pl.BlockSpec((tt, hd), lambda b, t: (b * nt + t, cb + j))

    const = lambda shape: pl.BlockSpec(shape, lambda b, t: (0, 0))
    return pl.pallas_call(
        _gdn_kernel,
        grid=(b_, nt),
        in_specs=[rowblk(0), rowblk(1), rowblk(2), rowblk(3),
                  pl.BlockSpec((tt, LANE), lambda b, t: (b * nt + t, 0)),
                  const((CONV_K, 3 * hd)), const((1, LANE)), const((1, LANE)), const((1, DV_A))],
        out_specs=pl.BlockSpec((tt, hd), lambda b, t: (b * nt + t, 0)),
        out_shape=jax.ShapeDtypeStruct((b_ * t_, hd), _BF16),
        scratch_shapes=[
            pltpu.VMEM((3, tt + 8, hd), _F32),
            pltpu.VMEM((H_A, DK_A, DV_A), _F32),
            pltpu.VMEM((H_A, tt, DK_A), _BF16),
            pltpu.VMEM((H_A, tt, DK_A), _BF16),
            pltpu.VMEM((H_A, tt, DK_A), _BF16),
            pltpu.VMEM((H_A, tt, DV_A + DK_A), _F32),
            pltpu.VMEM((H_A, tt, tt), _F32),
            pltpu.VMEM((H_A, tt, DK_A), _BF16),
            pltpu.VMEM((H_A, DK_A, tt), _BF16),
            pltpu.VMEM((H_A, DK_A, DV_A), _F32),
            pltpu.VMEM((H_A, tt, tt), _F32),
            pltpu.VMEM((H_A, tt, tt), _F32),
        ],
        compiler_params=pltpu.CompilerParams(
            dimension_semantics=("arbitrary", "arbitrary"), vmem_limit_bytes=VMEM_LIMIT),
        name="gdn",
    )(pb, pb, pb, pb, ps, cw, arow, dtrow, naw)


def _bias_kernel(tab_ref, bucket_ref, o_ref):
    bucket = bucket_ref[...]
    nq, w = bucket.shape
    for h in range(H_B):
        acc = jnp.zeros((nq, w), _F32)
        for b in range(N_BUCKETS):
            acc = acc + jnp.where(bucket == b, tab_ref[b, h], 0.0)
        far = acc[:, 2 * LANE:3 * LANE]
        o_ref[h] = (acc[:, 0:2 * LANE] - jnp.concatenate([far, far], axis=1)) * LOG2E


def _bias_tiles(rel_table, bucket):
    return pl.pallas_call(
        _bias_kernel,
        in_specs=[pl.BlockSpec(memory_space=pltpu.SMEM),
                  pl.BlockSpec(bucket.shape, lambda: (0, 0))],
        out_specs=pl.BlockSpec((H_B, Q_BLOCK, 2 * LANE), lambda: (0, 0, 0)),
        out_shape=jax.ShapeDtypeStruct((H_B, Q_BLOCK, 2 * LANE), _F32),
        name="rel_bias_tiles",
    )(rel_table, bucket)


def _dsa_kernel(q_ref, k_ref, v_ref, iq_ref, ik_ref, sm_ref, bias_ref, o_ref,
                key_s, qs_s, iwb_s, m_s, l_s, acc_s, s_s, si_s, sf_s, *, topk):
    nq = q_ref.shape[0]
    st_w = KEY_TILE
    assert nq == st_w
    ng = st_w // LANE
    qb = pl.program_id(1)
    n_st = qb + 1
    hd = D_HEAD_B
    scale = D_HEAD_B ** -0.5 * LOG2E
    f_topk = float(topk)

    lane_j = lax.broadcasted_iota(jnp.int32, (nq, LANE), 1)
    row_i = lax.broadcasted_iota(jnp.int32, (nq, LANE), 0)

    def grp(g):
        return slice(g * LANE, (g + 1) * LANE)

    iw = sm_ref[...]
    for h in range(H_B):
        sl = slice(h * hd, (h + 1) * hd)
        qs_s[h] = (q_ref[:, sl].astype(_F32) * scale).astype(_BF16)
        iwb_s[h] = jnp.broadcast_to(iw[:, 2 * H_A + h:2 * H_A + h + 1], (nq, LANE))
        m_s[h] = jnp.full((nq, LANE), MASK_NEG, _F32)
        l_s[h] = jnp.zeros((nq, LANE), _F32)
        acc_s[h] = jnp.zeros((nq, hd), _F32)

    def scores(st, diag):
        ikt = ik_ref[pl.ds(pl.multiple_of(st * st_w, st_w), st_w), :]
        acc = [jnp.zeros((nq, LANE), _F32) for _ in range(ng)]
        for p in range(H_IDX // 2):
            xq = iq_ref[:, p * LANE:(p + 1) * LANE]
            for half in range(2):
                h = 2 * p + half
                s = jnp.maximum(_dot_nt(xq, ikt[:, half * LANE:(half + 1) * LANE]), 0.0)
                w = iwb_s[h]
                for g in range(ng):
                    acc[g] = acc[g] + w * s[:, grp(g)]
        for g in range(ng):
            a = acc[g]
            if diag:
                a = jnp.where((g * LANE + lane_j) // CHUNK <= row_i // CHUNK, a, -jnp.inf)
            bits = pltpu.bitcast(a, jnp.int32)
            key_s[st, :, grp(g)] = bits ^ ((bits >> 31) & INT_MAX)

    def p1(st, carry):
        scores(st, False)
        return carry

    lax.fori_loop(0, qb, p1, 0)
    scores(qb, True)

    ones_b = jnp.ones((LANE, LANE), _BF16)
    assert key_s.shape[0] * ng <= 256

    T_, CAND_, MIDX_ = 0, 1, 2
    ACC_, CNT_ = 0, 1
    rb = 64
    lane_rb = lax.broadcasted_iota(jnp.int32, (rb, LANE), 1)

    def count(pred_fn):
        sf_s[ACC_] = jnp.zeros((nq, LANE), _F32)

        def body(st, carry):
            for r0 in range(0, nq, rb):
                rs = slice(r0, r0 + rb)
                acc = sf_s[ACC_, rs, :]
                for g in range(ng):
                    acc = acc + jnp.where(
                        pred_fn(key_s[st, rs, grp(g)], st * st_w + g * LANE + lane_rb, rs), 1.0, 0.0)
                sf_s[ACC_, rs, :] = acc
            return carry

        lax.fori_loop(0, n_st, body, 0)
        return _dot(sf_s[ACC_].astype(_BF16), ones_b)

    c0 = count(lambda k, j, rs: k >= 0)
    ok0 = c0 >= f_topk
    si_s[T_] = jnp.where(ok0, 0, INT_MIN)
    sf_s[CNT_] = jnp.where(ok0, c0, (n_st * st_w).astype(_F32))

    def bit_body(i, carry):
        t_old = si_s[T_]
        cand = t_old + jnp.left_shift(jnp.int32(1), 30 - i)
        si_s[CAND_] = cand
        c = count(lambda k, j, rs: k >= si_s[CAND_, rs, :])
        ok = c >= f_topk
        si_s[T_] = jnp.where(ok, cand, t_old)
        sf_s[CNT_] = jnp.where(ok, c, sf_s[CNT_])
        return carry

    lax.fori_loop(0, 31, bit_body, 0)
    finite_thr = si_s[T_] != KEY_NEG_INF
    si_s[MIDX_] = jnp.where(finite_thr, INT_MAX, 0)

    @pl.when(jnp.max(jnp.where(finite_thr, sf_s[CNT_], 0.0)) > f_topk)
    def _():
        sf_s[CNT_] = f_topk - count(lambda k, j, rs: k > si_s[T_, rs, :])
        n_bits = max(1, int(math.ceil(math.log2(key_s.shape[0] * st_w))))
        si_s[MIDX_] = jnp.zeros((nq, LANE), jnp.int32)

        def idx_body(i, carry):
            mp = si_s[MIDX_]
            cand = mp + jnp.left_shift(jnp.int32(1), n_bits - 1 - i)
            si_s[CAND_] = cand
            c = count(lambda k, j, rs: jnp.where(k == si_s[T_, rs, :], j, INT_MAX) < si_s[CAND_, rs, :])
            si_s[MIDX_] = jnp.where(c < sf_s[CNT_], cand, mp)
            return carry

        lax.fori_loop(0, n_bits, idx_body, 0)
        si_s[MIDX_] = jnp.where(si_s[T_] != KEY_NEG_INF, si_s[MIDX_] + 1, 0)

    def p2(st, carry):
        for r0 in range(0, nq, rb):
            rs = slice(r0, r0 + rb)
            t = si_s[T_, rs, :]
            m_idx = si_s[MIDX_, rs, :]
            for g in range(ng):
                k = key_s[st, rs, grp(g)]
                jj = jnp.where(k == t, st * st_w + g * LANE + lane_rb, INT_MAX)
                mb = jnp.where(k > t, 0.0, jnp.where(jj < m_idx, 0.0, MASK_NEG))
                key_s[st, rs, grp(g)] = pltpu.bitcast(mb, jnp.int32)
        return carry

    lax.fori_loop(0, n_st, p2, 0)

    def attend(st, mode):
        off = pl.multiple_of(st * st_w, st_w)
        for h in range(H_B):
            sl = slice(h * hd, (h + 1) * hd)
            s = _dot_nt(qs_s[h], k_ref[pl.ds(off, st_w), sl]) + pltpu.bitcast(key_s[st], _F32)
            if mode:
                s_s[...] = s
                b_prev = bias_ref[h, :, 0:LANE]
                if mode == 1:
                    s_s[0:LANE, (ng - 1) * LANE:ng * LANE] += b_prev
                else:
                    b_diag = bias_ref[h, :, LANE:2 * LANE]
                    for rt in range(ng):
                        s_s[grp(rt), grp(rt)] += b_diag
                        if rt:
                            s_s[grp(rt), grp(rt - 1)] += b_prev
                s = s_s[...]
            m_old = m_s[h]
            gmax = s[:, grp(0)]
            for g in range(1, ng):
                gmax = jnp.maximum(gmax, s[:, grp(g)])
            m_new = jnp.maximum(m_old, jnp.max(gmax, axis=1, keepdims=True))
            alpha = jnp.exp2(m_old - m_new)
            pieces = [jnp.exp2(s[:, grp(g)] - m_new) for g in range(ng)]
            psum = pieces[0]
            for g in range(1, ng):
                psum = psum + pieces[g]
            p = jnp.concatenate(pieces, axis=1).astype(_BF16)
            m_s[h] = m_new
            l_s[h] = alpha * l_s[h] + psum
            acc_s[h] = alpha * acc_s[h] + _dot(p, v_ref[pl.ds(off, st_w), sl])

    def p3(st, carry):
        attend(st, 0)
        return carry

    lax.fori_loop(0, qb - 1, p3, 0)

    @pl.when(qb > 0)
    def _():
        attend(qb - 1, 1)

    attend(qb, 2)

    for h in range(H_B):
        sl = slice(h * hd, (h + 1) * hd)
        l_fin = jnp.sum(l_s[h], axis=1, keepdims=True)
        o_ref[:, sl] = (acc_s[h] / l_fin).astype(o_ref.dtype)


def _dsa(pb, ps, bias, b_, t_, col0):
    nq = KEY_TILE
    nb = t_ // nq
    hd = H_B * D_HEAD_B
    topk = min(TOPK_MAX, t_ // 4)
    iq_w = H_IDX * D_IDX
    ik_w = 2 * LANE
    assert col0 % hd == 0 and (col0 + 3 * hd) % iq_w == 0 and (col0 + 3 * hd + iq_w) % ik_w == 0
    cq = col0 // hd
    resident = functools.partial(pl.BlockSpec, pipeline_mode=pl.Buffered(1))
    return pl.pallas_call(
        functools.partial(_dsa_kernel, topk=topk),
        grid=(b_, nb),
        in_specs=[
            pl.BlockSpec((nq, hd), lambda b, i: (b * nb + i, cq)),
            resident((t_, hd), lambda b, i: (b, cq + 1)),
            resident((t_, hd), lambda b, i: (b, cq + 2)),
            pl.BlockSpec((nq, iq_w), lambda b, i: (b * nb + i, (col0 + 3 * hd) // iq_w)),
            resident((t_, ik_w), lambda b, i: (b, (col0 + 3 * hd + iq_w) // ik_w)),
            pl.BlockSpec((nq, LANE), lambda b, i: (b * nb + i, 0)),
            resident((H_B, Q_BLOCK, 2 * LANE), lambda b, i: (0, 0, 0)),
        ],
        out_specs=pl.BlockSpec((nq, hd), lambda b, i: (b * nb + i, 0)),
        out_shape=jax.ShapeDtypeStruct((b_ * t_, hd), _BF16),
        scratch_shapes=[
            pltpu.VMEM((nb, nq, KEY_TILE), jnp.int32),
            pltpu.VMEM((H_B, nq, D_HEAD_B), _BF16),
            pltpu.VMEM((H_B, nq, LANE), _F32),
            pltpu.VMEM((H_B, nq, LANE), _F32),
            pltpu.VMEM((H_B, nq, LANE), _F32),
            pltpu.VMEM((H_B, nq, D_HEAD_B), _F32),
            pltpu.VMEM((nq, KEY_TILE), _F32),
            pltpu.VMEM((3, nq, LANE), jnp.int32),
            pltpu.VMEM((2, nq, LANE), _F32),
        ],
        compiler_params=pltpu.CompilerParams(
            dimension_semantics=("arbitrary", "arbitrary"), vmem_limit_bytes=VMEM_LIMIT),
        name="dsa",
    )(pb, pb, pb, pb, pb, ps, bias)


def _merge_kernel(oa_ref, ob_ref, g_ref, x_ref, bg_ref, wa_ref, wb_ref, wo_ref, o_ref):
    d = x_ref.shape[1]
    a = _dot(oa_ref[...], wa_ref[...])
    b = _dot(ob_ref[...], wb_ref[...])
    g = _sigmoid(g_ref[...].astype(_F32) + bg_ref[...])
    merged = g[:, 0:d] * a + g[:, d:2 * d] * b
    o_ref[...] = x_ref[...] + _dot(merged.astype(_BF16), wo_ref[...])


def _merge(oa, ob, pb, x2d, bg, wa, wb, wo, tn):
    n, d = x2d.shape
    row = lambda w: pl.BlockSpec((tn, w), lambda i: (i, 0))
    const = lambda a: pl.BlockSpec(a.shape, lambda i: (0, 0))
    return pl.pallas_call(
        _merge_kernel,
        grid=(n // tn,),
        in_specs=[row(oa.shape[1]), row(ob.shape[1]), row(2 * d), row(d),
                  const(bg), const(wa), const(wb), const(wo)],
        out_specs=row(d),
        out_shape=jax.ShapeDtypeStruct((n, d), _F32),
        compiler_params=pltpu.CompilerParams(
            dimension_semantics=("arbitrary",), vmem_limit_bytes=VMEM_LIMIT),
        name="merge",
    )(oa, ob, pb, x2d, bg, wa, wb, wo)


def _mlp_kernel(x_ref, n2_ref, w1_ref, w2_ref, nf_ref, o_ref, *, final_norm):
    x = x_ref[...]
    h2 = _rms(x, n2_ref[...]).astype(_BF16)
    hid = jnp.maximum(_dot(h2, w1_ref[...]), 0.0)
    y = x + _dot((hid * hid).astype(_BF16), w2_ref[...])
    o_ref[...] = _rms(y, nf_ref[...]) if final_norm else y


def _mlp(x1, n2, w1, w2, nf, tn, final_norm):
    n, d = x1.shape
    row = pl.BlockSpec((tn, d), lambda i: (i, 0))
    const = lambda a: pl.BlockSpec(a.shape, lambda i: (0, 0))
    return pl.pallas_call(
        functools.partial(_mlp_kernel, final_norm=final_norm),
        grid=(n // tn,),
        in_specs=[row, const(n2), const(w1), const(w2), const(nf)],
        out_specs=row,
        out_shape=jax.ShapeDtypeStruct((n, d), _F32),
        compiler_params=pltpu.CompilerParams(
            dimension_semantics=("arbitrary",), vmem_limit_bytes=VMEM_LIMIT),
        name="mlp",
    )(x1, n2, w1, w2, nf)


def _t5_bucket(rel):
    half = N_BUCKETS // 2
    max_exact = half // 2
    base = jnp.where(rel > 0, half, 0)
    n = jnp.abs(rel)
    n_f = jnp.maximum(n, 1).astype(jnp.float32)
    large = max_exact + (jnp.log(n_f / max_exact) / math.log(MAX_DISTANCE / max_exact)
                         * (half - max_exact)).astype(jnp.int32)
    large = jnp.minimum(large, half - 1)
    return base + jnp.where(n < max_exact, n, large)


def _pick_tile(n, prefs):
    for t in prefs:
        if n % t == 0:
            return t
    raise ValueError(f"no tile in {prefs} divides {n}")


def kernel(x, norm1_w, w_in, conv_a_w, a_log, dt_bias, norm_a_w, rel_bias_table, w_gate, b_gate,
           w_proj_a, w_proj_b, w_out, norm2_w, w_ff1, w_ff2, norm_final_w):
    b_, t_, d = x.shape
    depth = norm1_w.shape[0]
    n = b_ * t_
    ha, hb = H_A * DK_A, H_B * D_HEAD_B
    assert t_ % KEY_TILE == 0 and t_ % GDN_CHUNK == 0 and d % LANE == 0
    assert DK_A == DV_A == D_HEAD_B == LANE and 2 * D_IDX == LANE
    assert Q_BLOCK >= MAX_DISTANCE

    o_za = 3 * ha
    o_ba = 4 * ha
    o_aa = o_ba + H_A
    o_qb = o_aa + H_A
    o_iq = o_qb + 3 * hb
    o_ik = o_iq + H_IDX * D_IDX
    o_iw = o_ik + D_IDX

    rel = (jnp.arange(2 * Q_BLOCK, dtype=jnp.int32)[None, :] - Q_BLOCK) \
        - jnp.arange(Q_BLOCK, dtype=jnp.int32)[:, None]
    far = jnp.full((Q_BLOCK, LANE), -(Q_BLOCK + 1), jnp.int32)
    bucket = _t5_bucket(jnp.concatenate([rel, far], axis=1))
    bias = _bias_tiles(rel_bias_table.astype(_F32), bucket)

    x2d = x.reshape(n, d)
    tn = _pick_tile(n, (512, 256, 128))
    for layer in range(depth):
        wi = w_in[layer]
        zpad = lambda w: jnp.zeros((d, w), wi.dtype)
        ws = jnp.concatenate([wi[:, o_ba:o_qb], wi[:, o_iw:o_iw + H_IDX], zpad(LANE - 3 * H_A)],
                             axis=1).astype(_BF16)
        ik = wi[:, o_ik:o_iw]
        wb16 = jnp.concatenate([w_gate[layer], wi[:, 0:o_ba], wi[:, o_qb:o_ik],
                                ik, zpad(D_IDX), zpad(D_IDX), ik], axis=1).astype(_BF16)
        nw = norm1_w[layer].reshape(1, d).astype(_F32)
        pb, ps = _norm_proj(x2d, nw, wb16, ws, tn)

        col_a = 2 * d
        col_b = col_a + 4 * ha
        assert col_a % ha == 0
        lanes = lambda v, off: jnp.zeros((1, LANE), _F32).at[0, off:off + H_A].set(v.astype(_F32))
        oa = _gdn(pb, ps, conv_a_w[layer].astype(_F32), lanes(a_log[layer], H_A), lanes(dt_bias[layer], H_A),
                  norm_a_w[layer].reshape(1, DV_A).astype(_F32), b_, t_, col_a)
        ob = _dsa(pb, ps, bias, b_, t_, col_b)

        x2d = _merge(oa, ob, pb, x2d, b_gate[layer].reshape(1, 2 * d).astype(_F32),
                     w_proj_a[layer].astype(_BF16), w_proj_b[layer].astype(_BF16),
                     w_out[layer].astype(_BF16), tn)
        x2d = _mlp(x2d, norm2_w[layer].reshape(1, d).astype(_F32), w_ff1[layer].astype(_BF16),
                   w_ff2[layer].astype(_BF16), norm_final_w.reshape(1, d).astype(_F32),
                   _pick_tile(n, (256, 128)), final_norm=layer == depth - 1)
    return x2d.reshape(b_, t_, d)
```

```python
import functools
import math

import jax
import jax.numpy as jnp
import numpy as np
from jax import lax
from jax.experimental import pallas as pl
from jax.experimental.pallas import tpu as pltpu

EPS = 1e-6
H_A = 8
DK_A = 128
DV_A = 128
CONV_K = 4
GDN_CHUNK = 128
H_B = 8
D_HEAD_B = 128
H_IDX = 8
D_IDX = 64
TOPK_MAX = 256
Q_BLOCK = 128
CHUNK = 64
KEY_TILE = 512
N_BUCKETS = 32
MAX_DISTANCE = 128

LANE = 128
VMEM_LIMIT = 56 * 1024 * 1024

INT_MIN = -(2**31)
INT_MAX = 2**31 - 1
KEY_NEG_INF = -2139095041
MASK_NEG = -1e30
LOG2E = math.log2(math.e)

_F32 = jnp.float32
_BF16 = jnp.bfloat16


def _dot(a, b):
    return jnp.dot(a, b, preferred_element_type=_F32)


def _dot_nt(a, b):
    return lax.dot_general(a, b, (((1,), (1,)), ((), ())), preferred_element_type=_F32)


def _sigmoid(x):
    return 1.0 / (1.0 + jnp.exp(-x))


def _rms(x, w):
    return x * lax.rsqrt(jnp.mean(x * x, axis=-1, keepdims=True) + EPS) * w


def _norm_proj_kernel(x_ref, nw_ref, w_ref, ws_ref, o_ref, os_ref, *, chunk):
    h = _rms(x_ref[...], nw_ref[...]).astype(_BF16)
    os_ref[...] = _dot(h, ws_ref[...])
    for c0 in range(0, w_ref.shape[1], chunk):
        o_ref[:, c0:c0 + chunk] = _dot(h, w_ref[:, c0:c0 + chunk]).astype(o_ref.dtype)


def _norm_proj(x2d, nw, w, ws, tn):
    n, d = x2d.shape
    c = w.shape[1]
    const = functools.partial(pl.BlockSpec, pipeline_mode=pl.Buffered(1))
    return pl.pallas_call(
        functools.partial(_norm_proj_kernel, chunk=_pick_tile(c, (768, 512, 256, 128))),
        grid=(n // tn,),
        in_specs=[
            pl.BlockSpec((tn, d), lambda i: (i, 0)),
            const((1, d), lambda i: (0, 0)),
            const((d, c), lambda i: (0, 0)),
            const((d, LANE), lambda i: (0, 0)),
        ],
        out_specs=[pl.BlockSpec((tn, c), lambda i: (i, 0)), pl.BlockSpec((tn, LANE), lambda i: (i, 0))],
        out_shape=[jax.ShapeDtypeStruct((n, c), _BF16), jax.ShapeDtypeStruct((n, LANE), _F32)],
        compiler_params=pltpu.CompilerParams(
            dimension_semantics=("arbitrary",), vmem_limit_bytes=VMEM_LIMIT),
        name="proj",
    )(x2d, nw, w, ws)


def _gdn_kernel(q_ref, k_ref, v_ref, z_ref, sm_ref, cw_ref, arow_ref, dtrow_ref, naw_ref, o_ref,
                xbuf, s_ref, kn_s, kb_s, qn_s, rhs_s, dec_s, qdec_s, ktt_s, eg_s, m_s, x_s):
    tt = q_ref.shape[0]
    hd = H_A * DK_A

    @pl.when(pl.program_id(1) == 0)
    def _():
        xbuf[:, 0:8, :] = jnp.zeros((3, 8, hd), _F32)
        s_ref[...] = jnp.zeros_like(s_ref)

    for idx, ref in enumerate((q_ref, k_ref, v_ref)):
        xbuf[idx, 8:8 + tt, :] = ref[...].astype(_F32)

    sm = sm_ref[...]
    beta_full = _sigmoid(sm)
    xg = sm + dtrow_ref[...]
    softplus = jnp.maximum(xg, 0.0) + jnp.log(1.0 + jnp.exp(-jnp.abs(xg)))
    g_full = -jnp.exp(arow_ref[...]) * softplus
    row = lax.broadcasted_iota(jnp.int32, (tt, tt), 0)
    col = lax.broadcasted_iota(jnp.int32, (tt, tt), 1)
    tri = (col <= row).astype(_F32)
    gcum = jnp.dot(tri, g_full, preferred_element_type=_F32, precision=lax.Precision.HIGHEST)
    gcum_t = gcum.T
    strict = col < row
    eye = (col == row).astype(_F32)

    for h in range(H_A):
        sl = slice(h * DK_A, (h + 1) * DK_A)
        conv = []
        for idx in range(3):
            acc = None
            for j in range(CONV_K):
                term = cw_ref[j:j + 1, idx * hd + h * DK_A: idx * hd + (h + 1) * DK_A] * \
                    xbuf[idx, 8 - (CONV_K - 1) + j: 8 - (CONV_K - 1) + j + tt, sl]
                acc = term if acc is None else acc + term
            conv.append(acc * _sigmoid(acc))
        qh, kh, vh = conv
        qn = qh * lax.rsqrt(jnp.sum(qh * qh, axis=-1, keepdims=True) + EPS) * (DK_A ** -0.5)
        kn = kh * lax.rsqrt(jnp.sum(kh * kh, axis=-1, keepdims=True) + EPS)
        bcol = beta_full[:, h:h + 1]
        gcol = gcum[:, H_A + h:H_A + h + 1]
        grow = gcum_t[H_A + h:H_A + h + 1, :]
        glast = gcum[tt - 1:tt, H_A + h:H_A + h + 1]
        kb = kn * bcol
        eg = jnp.exp(gcol)
        rhs_s[h, :, 0:DV_A] = vh * bcol
        rhs_s[h, :, DV_A:DV_A + DK_A] = kb * eg
        dec_s[h] = jnp.where(strict, jnp.exp(jnp.where(strict, gcol - grow, 0.0)), 0.0)
        qdec_s[h] = (qn * eg).astype(_BF16)
        ktt_s[h] = (kn * jnp.exp(glast - gcol)).T.astype(_BF16)
        kn_s[h] = kn.astype(_BF16)
        kb_s[h] = kb.astype(_BF16)
        qn_s[h] = qn.astype(_BF16)
        eg_s[h] = jnp.broadcast_to(jnp.exp(glast), (DK_A, DV_A))

    for idx in range(3):
        xbuf[idx, 0:8, :] = xbuf[idx, tt:tt + 8, :]

    n_sq = int(math.log2(tt))

    heads = range(H_A)
    for h in heads:
        m = -(_dot_nt(kb_s[h], kn_s[h]) * dec_s[h])
        m_s[h] = m
        x_s[h] = m
    for _ in range(n_sq - 1):
        for h in heads:
            mb = m_s[h].astype(_BF16)
            m = _dot(mb, mb)
            m_s[h] = m
            x_s[h] = x_s[h] + m + _dot(x_s[h].astype(_BF16), m.astype(_BF16))
    for h in heads:
        rhs = rhs_s[h]
        rhs_s[h] = rhs + _dot(x_s[h].astype(_BF16), rhs.astype(_BF16))
    for h in heads:
        m_s[h] = _dot_nt(qn_s[h], kn_s[h]) * (dec_s[h] + eye)
    for h in heads:
        sb = s_ref[h].astype(_BF16)
        vnew = rhs_s[h, :, 0:DV_A] - _dot(rhs_s[h, :, DV_A:DV_A + DK_A].astype(_BF16), sb)
        vb = vnew.astype(_BF16)
        o = _dot(qdec_s[h], sb) + _dot(m_s[h].astype(_BF16), vb)
        s_ref[h] = s_ref[h] * eg_s[h] + _dot(ktt_s[h], vb)
        rhs_s[h, :, 0:DV_A] = o

    naw = naw_ref[...]
    for h in range(H_A):
        sl = slice(h * DV_A, (h + 1) * DV_A)
        o = rhs_s[h, :, 0:DV_A]
        z = z_ref[:, sl].astype(_F32)
        on = o * lax.rsqrt(jnp.mean(o * o, axis=-1, keepdims=True) + EPS) * naw
        o_ref[:, sl] = (on * (z * _sigmoid(z))).astype(o_ref.dtype)


def _gdn(pb, ps, cw, arow, dtrow, naw, b_, t_, col0):
    tt = GDN_CHUNK
    hd = H_A * DK_A
    nt = t_ // tt
    cb = col0 // hd

    def rowblk(j):
        return pl.BlockSpec((tt, hd), lambda b, t: (b * nt + t, cb + j))

    const = lambda shape: pl.BlockSpec(shape, lambda b, t: (0, 0))
    return pl.pallas_call(
        _gdn_kernel,
        grid=(b_, nt),
        in_specs=[rowblk(0), rowblk(1), rowblk(2), rowblk(3),
                  pl.BlockSpec((tt, LANE), lambda b, t: (b * nt + t, 0)),
                  const((CONV_K, 3 * hd)), const((1, LANE)), const((1, LANE)), const((1, DV_A))],
        out_specs=pl.BlockSpec((tt, hd), lambda b, t: (b * nt + t, 0)),
        out_shape=jax.ShapeDtypeStruct((b_ * t_, hd), _BF16),
        scratch_shapes=[
            pltpu.VMEM((3, tt + 8, hd), _F32),
            pltpu.VMEM((H_A, DK_A, DV_A), _F32),
            pltpu.VMEM((H_A, tt, DK_A), _BF16),
            pltpu.VMEM((H_A, tt, DK_A), _BF16),
            pltpu.VMEM((H_A, tt, DK_A), _BF16),
            pltpu.VMEM((H_A, tt, DV_A + DK_A), _F32),
            pltpu.VMEM((H_A, tt, tt), _F32),
            pltpu.VMEM((H_A, tt, DK_A), _BF16),
            pltpu.VMEM((H_A, DK_A, tt), _BF16),
            pltpu.VMEM((H_A, DK_A, DV_A), _F32),
            pltpu.VMEM((H_A, tt, tt), _F32),
            pltpu.VMEM((H_A, tt, tt), _F32),
        ],
        compiler_params=pltpu.CompilerParams(
            dimension_semantics=("arbitrary", "arbitrary"), vmem_limit_bytes=VMEM_LIMIT),
        name="gdn",
    )(pb, pb, pb, pb, ps, cw, arow, dtrow, naw)


def _bias_kernel(tab_ref, bucket_ref, o_ref):
    bucket = bucket_ref[...]
    nq, w = bucket.shape
    for h in range(H_B):
        acc = jnp.zeros((nq, w), _F32)
        for b in range(N_BUCKETS):
            acc = acc + jnp.where(bucket == b, tab_ref[b, h], 0.0)
        far = acc[:, 2 * LANE:3 * LANE]
        o_ref[h] = (acc[:, 0:2 * LANE] - jnp.concatenate([far, far], axis=1)) * LOG2E


def _bias_tiles(rel_table, bucket):
    return pl.pallas_call(
        _bias_kernel,
        in_specs=[pl.BlockSpec(memory_space=pltpu.SMEM),
                  pl.BlockSpec(bucket.shape, lambda: (0, 0))],
        out_specs=pl.BlockSpec((H_B, Q_BLOCK, 2 * LANE), lambda: (0, 0, 0)),
        out_shape=jax.ShapeDtypeStruct((H_B, Q_BLOCK, 2 * LANE), _F32),
        name="rel_bias_tiles",
    )(rel_table, bucket)


def _dsa_kernel(q_ref, k_ref, v_ref, iq_ref, ik_ref, sm_ref, bias_ref, o_ref,
                key_s, qs_s, iwb_s, m_s, l_s, acc_s, s_s, si_s, sf_s, *, topk):
    nq = q_ref.shape[0]
    st_w = KEY_TILE
    assert nq == st_w
    ng = st_w // LANE
    qb = pl.program_id(1)
    n_st = qb + 1
    hd = D_HEAD_B
    scale = D_HEAD_B ** -0.5 * LOG2E
    f_topk = float(topk)

    lane_j = lax.broadcasted_iota(jnp.int32, (nq, LANE), 1)
    row_i = lax.broadcasted_iota(jnp.int32, (nq, LANE), 0)

    def grp(g):
        return slice(g * LANE, (g + 1) * LANE)

    iw = sm_ref[...]
    for h in range(H_B):
        sl = slice(h * hd, (h + 1) * hd)
        qs_s[h] = (q_ref[:, sl].astype(_F32) * scale).astype(_BF16)
        iwb_s[h] = jnp.broadcast_to(iw[:, 2 * H_A + h:2 * H_A + h + 1], (nq, LANE))
        m_s[h] = jnp.full((nq, LANE), MASK_NEG, _F32)
        l_s[h] = jnp.zeros((nq, LANE), _F32)
        acc_s[h] = jnp.zeros((nq, hd), _F32)

    def scores(st, diag):
        ikt = ik_ref[pl.ds(pl.multiple_of(st * st_w, st_w), st_w), :]
        acc = [jnp.zeros((nq, LANE), _F32) for _ in range(ng)]
        for p in range(H_IDX // 2):
            xq = iq_ref[:, p * LANE:(p + 1) * LANE]
            for half in range(2):
                h = 2 * p + half
                s = jnp.maximum(_dot_nt(xq, ikt[:, half * LANE:(half + 1) * LANE]), 0.0)
                w = iwb_s[h]
                for g in range(ng):
                    acc[g] = acc[g] + w * s[:, grp(g)]
        for g in range(ng):
            a = acc[g]
            if diag:
                a = jnp.where((g * LANE + lane_j) // CHUNK <= row_i // CHUNK, a, -jnp.inf)
            bits = pltpu.bitcast(a, jnp.int32)
            key_s[st, :, grp(g)] = bits ^ ((bits >> 31) & INT_MAX)

    def p1(st, carry):
        scores(st, False)
        return carry

    lax.fori_loop(0, qb, p1, 0)
    scores(qb, True)

    ones_b = jnp.ones((LANE, LANE), _BF16)
    assert key_s.shape[0] * ng <= 256

    T_, MIDX_ = 0, 1
    CNT_ = 0
    rb = 64
    blocks = [slice(r0, r0 + rb) for r0 in range(0, nq, rb)]
    lane_rb = lax.broadcasted_iota(jnp.int32, (rb, LANE), 1)

    def search(n_tiles):
        def count(pred_fn, rs):
            acc = jnp.zeros((rb, LANE), _F32)
            for st in range(n_tiles):
                for g in range(ng):
                    acc = acc + jnp.where(
                        pred_fn(key_s[st, rs, grp(g)], st * st_w + g * LANE + lane_rb), 1.0, 0.0)
            return _dot(acc.astype(_BF16), ones_b)

        for rs in blocks:
            c0 = count(lambda k, j: k >= 0, rs)
            si_s[T_, rs, :] = jnp.where(c0 >= f_topk, 0, INT_MIN)

        def bit_body(i, carry):
            bit = jnp.left_shift(jnp.int32(1), 30 - i)
            for rs in blocks:
                t_old = si_s[T_, rs, :]
                cand = t_old + bit
                c = count(lambda k, j: k >= cand, rs)
                si_s[T_, rs, :] = jnp.where(c >= f_topk, cand, t_old)
            return carry

        lax.fori_loop(0, 31, bit_body, 0)

        for rs in blocks:
            t = si_s[T_, rs, :]
            finite_thr = t != KEY_NEG_INF
            sf_s[CNT_, rs, :] = jnp.where(finite_thr, count(lambda k, j: k >= t, rs), 0.0)
            si_s[MIDX_, rs, :] = jnp.where(finite_thr, INT_MAX, 0)

        @pl.when(jnp.max(sf_s[CNT_]) > f_topk)
        def _():
            n_bits = max(1, int(math.ceil(math.log2(n_tiles * st_w))))
            for rs in blocks:
                t = si_s[T_, rs, :]
                sf_s[CNT_, rs, :] = f_topk - count(lambda k, j: k > t, rs)
                si_s[MIDX_, rs, :] = jnp.zeros((rb, LANE), jnp.int32)

            def idx_body(i, carry):
                bit = jnp.left_shift(jnp.int32(1), n_bits - 1 - i)
                for rs in blocks:
                    t = si_s[T_, rs, :]
                    mp = si_s[MIDX_, rs, :]
                    cand = mp + bit
                    c = count(lambda k, j: jnp.where(k == t, j, INT_MAX) < cand, rs)
                    si_s[MIDX_, rs, :] = jnp.where(c < sf_s[CNT_, rs, :], cand, mp)
                return carry

            lax.fori_loop(0, n_bits, idx_body, 0)
            for rs in blocks:
                si_s[MIDX_, rs, :] = jnp.where(si_s[T_, rs, :] != KEY_NEG_INF, si_s[MIDX_, rs, :] + 1, 0)

        for st in range(n_tiles):
            for rs in blocks:
                t = si_s[T_, rs, :]
                m_idx = si_s[MIDX_, rs, :]
                for g in range(ng):
                    k = key_s[st, rs, grp(g)]
                    jj = jnp.where(k == t, st * st_w + g * LANE + lane_rb, INT_MAX)
                    mb = jnp.where(k > t, 0.0, jnp.where(jj < m_idx, 0.0, MASK_NEG))
                    key_s[st, rs, grp(g)] = pltpu.bitcast(mb, jnp.int32)

    for v in range(key_s.shape[0]):
        pl.when(qb == v)(functools.partial(search, v + 1))

    def attend(st, mode):
        off = pl.multiple_of(st * st_w, st_w)
        for h in range(H_B):
            sl = slice(h * hd, (h + 1) * hd)
            s = _dot_nt(qs_s[h], k_ref[pl.ds(off, st_w), sl]) + pltpu.bitcast(key_s[st], _F32)
            if mode:
                s_s[...] = s
                b_prev = bias_ref[h, :, 0:LANE]
                if mode == 1:
                    s_s[0:LANE, (ng - 1) * LANE:ng * LANE] += b_prev
                else:
                    b_diag = bias_ref[h, :, LANE:2 * LANE]
                    for rt in range(ng):
                        s_s[grp(rt), grp(rt)] += b_diag
                        if rt:
                            s_s[grp(rt), grp(rt - 1)] += b_prev
                s = s_s[...]
            m_old = m_s[h]
            gmax = s[:, grp(0)]
            for g in range(1, ng):
                gmax = jnp.maximum(gmax, s[:, grp(g)])
            m_new = jnp.maximum(m_old, jnp.max(gmax, axis=1, keepdims=True))
            alpha = jnp.exp2(m_old - m_new)
            pieces = [jnp.exp2(s[:, grp(g)] - m_new) for g in range(ng)]
            psum = pieces[0]
            for g in range(1, ng):
                psum = psum + pieces[g]
            p = jnp.concatenate(pieces, axis=1).astype(_BF16)
            m_s[h] = m_new
            l_s[h] = alpha * l_s[h] + psum
            acc_s[h] = alpha * acc_s[h] + _dot(p, v_ref[pl.ds(off, st_w), sl])

    def p3(st, carry):
        attend(st, 0)
        return carry

    lax.fori_loop(0, qb - 1, p3, 0)

    @pl.when(qb > 0)
    def _():
        attend(qb - 1, 1)

    attend(qb, 2)

    for h in range(H_B):
        sl = slice(h * hd, (h + 1) * hd)
        l_fin = jnp.sum(l_s[h], axis=1, keepdims=True)
        o_ref[:, sl] = (acc_s[h] / l_fin).astype(o_ref.dtype)


def _dsa(pb, ps, bias, b_, t_, col0):
    nq = KEY_TILE
    nb = t_ // nq
    hd = H_B * D_HEAD_B
    topk = min(TOPK_MAX, t_ // 4)
    iq_w = H_IDX * D_IDX
    ik_w = 2 * LANE
    assert col0 % hd == 0 and (col0 + 3 * hd) % iq_w == 0 and (col0 + 3 * hd + iq_w) % ik_w == 0
    cq = col0 // hd
    resident = functools.partial(pl.BlockSpec, pipeline_mode=pl.Buffered(1))
    return pl.pallas_call(
        functools.partial(_dsa_kernel, topk=topk),
        grid=(b_, nb),
        in_specs=[
            pl.BlockSpec((nq, hd), lambda b, i: (b * nb + i, cq)),
            resident((t_, hd), lambda b, i: (b, cq + 1)),
            resident((t_, hd), lambda b, i: (b, cq + 2)),
            pl.BlockSpec((nq, iq_w), lambda b, i: (b * nb + i, (col0 + 3 * hd) // iq_w)),
            resident((t_, ik_w), lambda b, i: (b, (col0 + 3 * hd + iq_w) // ik_w)),
            pl.BlockSpec((nq, LANE), lambda b, i: (b * nb + i, 0)),
            resident((H_B, Q_BLOCK, 2 * LANE), lambda b, i: (0, 0, 0)),
        ],
        out_specs=pl.BlockSpec((nq, hd), lambda b, i: (b * nb + i, 0)),
        out_shape=jax.ShapeDtypeStruct((b_ * t_, hd), _BF16),
        scratch_shapes=[
            pltpu.VMEM((nb, nq, KEY_TILE), jnp.int32),
            pltpu.VMEM((H_B, nq, D_HEAD_B), _BF16),
            pltpu.VMEM((H_B, nq, LANE), _F32),
            pltpu.VMEM((H_B, nq, LANE), _F32),
            pltpu.VMEM((H_B, nq, LANE), _F32),
            pltpu.VMEM((H_B, nq, D_HEAD_B), _F32),
            pltpu.VMEM((nq, KEY_TILE), _F32),
            pltpu.VMEM((2, nq, LANE), jnp.int32),
            pltpu.VMEM((1, nq, LANE), _F32),
        ],
        compiler_params=pltpu.CompilerParams(
            dimension_semantics=("arbitrary", "arbitrary"), vmem_limit_bytes=VMEM_LIMIT),
        name="dsa",
    )(pb, pb, pb, pb, pb, ps, bias)


def _merge_kernel(oa_ref, ob_ref, g_ref, x_ref, bg_ref, wa_ref, wb_ref, wo_ref, o_ref):
    d = x_ref.shape[1]
    a = _dot(oa_ref[...], wa_ref[...])
    b = _dot(ob_ref[...], wb_ref[...])
    g = _sigmoid(g_ref[...].astype(_F32) + bg_ref[...])
    merged = g[:, 0:d] * a + g[:, d:2 * d] * b
    o_ref[...] = x_ref[...] + _dot(merged.astype(_BF16), wo_ref[...])


def _merge(oa, ob, pb, x2d, bg, wa, wb, wo, tn):
    n, d = x2d.shape
    row = lambda w: pl.BlockSpec((tn, w), lambda i: (i, 0))
    const = lambda a: pl.BlockSpec(a.shape, lambda i: (0, 0))
    return pl.pallas_call(
        _merge_kernel,
        grid=(n // tn,),
        in_specs=[row(oa.shape[1]), row(ob.shape[1]), row(2 * d), row(d),
                  const(bg), const(wa), const(wb), const(wo)],
        out_specs=row(d),
        out_shape=jax.ShapeDtypeStruct((n, d), _F32),
        compiler_params=pltpu.CompilerParams(
            dimension_semantics=("arbitrary",), vmem_limit_bytes=VMEM_LIMIT),
        name="merge",
    )(oa, ob, pb, x2d, bg, wa, wb, wo)


def _mlp_kernel(x_ref, n2_ref, w1_ref, w2_ref, nf_ref, o_ref, *, final_norm):
    x = x_ref[...]
    h2 = _rms(x, n2_ref[...]).astype(_BF16)
    hid = jnp.maximum(_dot(h2, w1_ref[...]), 0.0)
    y = x + _dot((hid * hid).astype(_BF16), w2_ref[...])
    o_ref[...] = _rms(y, nf_ref[...]) if final_norm else y


def _mlp(x1, n2, w1, w2, nf, tn, final_norm):
    n, d = x1.shape
    row = pl.BlockSpec((tn, d), lambda i: (i, 0))
    const = lambda a: pl.BlockSpec(a.shape, lambda i: (0, 0))
    return pl.pallas_call(
        functools.partial(_mlp_kernel, final_norm=final_norm),
        grid=(n // tn,),
        in_specs=[row, const(n2), const(w1), const(w2), const(nf)],
        out_specs=row,
        out_shape=jax.ShapeDtypeStruct((n, d), _F32),
        compiler_params=pltpu.CompilerParams(
            dimension_semantics=("arbitrary",), vmem_limit_bytes=VMEM_LIMIT),
        name="mlp",
    )(x1, n2, w1, w2, nf)


def _t5_bucket(rel):
    half = N_BUCKETS // 2
    max_exact = half // 2
    base = jnp.where(rel > 0, half, 0)
    n = jnp.abs(rel)
    n_f = jnp.maximum(n, 1).astype(jnp.float32)
    large = max_exact + (jnp.log(n_f / max_exact) / math.log(MAX_DISTANCE / max_exact)
                         * (half - max_exact)).astype(jnp.int32)
    large = jnp.minimum(large, half - 1)
    return base + jnp.where(n < max_exact, n, large)


def _pick_tile(n, prefs):
    for t in prefs:
        if n % t == 0:
            return t
    raise ValueError(f"no tile in {prefs} divides {n}")


def kernel(x, norm1_w, w_in, conv_a_w, a_log, dt_bias, norm_a_w, rel_bias_table, w_gate, b_gate,
           w_proj_a, w_proj_b, w_out, norm2_w, w_ff1, w_ff2, norm_final_w):
    b_, t_, d = x.shape
    depth = norm1_w.shape[0]
    n = b_ * t_
    ha, hb = H_A * DK_A, H_B * D_HEAD_B
    assert t_ % KEY_TILE == 0 and t_ % GDN_CHUNK == 0 and d % LANE == 0
    assert DK_A == DV_A == D_HEAD_B == LANE and 2 * D_IDX == LANE
    assert Q_BLOCK >= MAX_DISTANCE

    o_za = 3 * ha
    o_ba = 4 * ha
    o_aa = o_ba + H_A
    o_qb = o_aa + H_A
    o_iq = o_qb + 3 * hb
    o_ik = o_iq + H_IDX * D_IDX
    o_iw = o_ik + D_IDX

    rel = (jnp.arange(2 * Q_BLOCK, dtype=jnp.int32)[None, :] - Q_BLOCK) \
        - jnp.arange(Q_BLOCK, dtype=jnp.int32)[:, None]
    far = jnp.full((Q_BLOCK, LANE), -(Q_BLOCK + 1), jnp.int32)
    bucket = _t5_bucket(jnp.concatenate([rel, far], axis=1))
    bias = _bias_tiles(rel_bias_table.astype(_F32), bucket)

    x2d = x.reshape(n, d)
    tn = _pick_tile(n, (512, 256, 128))
    for layer in range(depth):
        wi = w_in[layer]
        zpad = lambda w: jnp.zeros((d, w), wi.dtype)
        ws = jnp.concatenate([wi[:, o_ba:o_qb], wi[:, o_iw:o_iw + H_IDX], zpad(LANE - 3 * H_A)],
                             axis=1).astype(_BF16)
        ik = wi[:, o_ik:o_iw]
        wb16 = jnp.concatenate([w_gate[layer], wi[:, 0:o_ba], wi[:, o_qb:o_ik],
                                ik, zpad(D_IDX), zpad(D_IDX), ik], axis=1).astype(_BF16)
        nw = norm1_w[layer].reshape(1, d).astype(_F32)
        pb, ps = _norm_proj(x2d, nw, wb16, ws, tn)

        col_a = 2 * d
        col_b = col_a + 4 * ha
        assert col_a % ha == 0
        lanes = lambda v, off: jnp.zeros((1, LANE), _F32).at[0, off:off + H_A].set(v.astype(_F32))
        oa = _gdn(pb, ps, conv_a_w[layer].astype(_F32), lanes(a_log[layer], H_A), lanes(dt_bias[layer], H_A),
                  norm_a_w[layer].reshape(1, DV_A).astype(_F32), b_, t_, col_a)
        ob = _dsa(pb, ps, bias, b_, t_, col_b)

        x2d = _merge(oa, ob, pb, x2d, b_gate[layer].reshape(1, 2 * d).astype(_F32),
                     w_proj_a[layer].astype(_BF16), w_proj_b[layer].astype(_BF16),
                     w_out[layer].astype(_BF16), tn)
        x2d = _mlp(x2d, norm2_w[layer].reshape(1, d).astype(_F32), w_ff1[layer].astype(_BF16),
                   w_ff2[layer].astype(_BF16), norm_final_w.reshape(1, d).astype(_F32),
                   _pick_tile(n, (256, 128)), final_norm=layer == depth - 1)
    return x2d.reshape(b_, t_, d)
```

```python
import functools
import math

import jax
import jax.numpy as jnp
import numpy as np
from jax import lax
from jax.experimental import pallas as pl
from jax.experimental.pallas import tpu as pltpu

EPS = 1e-6
H_A = 8
DK_A = 128
DV_A = 128
CONV_K = 4
GDN_CHUNK = 128
H_B = 8
D_HEAD_B = 128
H_IDX = 8
D_IDX = 64
TOPK_MAX = 256
Q_BLOCK = 128
CHUNK = 64
KEY_TILE = 512
N_BUCKETS = 32
MAX_DISTANCE = 128

LANE = 128
VMEM_LIMIT = 56 * 1024 * 1024

INT_MIN = -(2**31)
INT_MAX = 2**31 - 1
KEY_NEG_INF = -2139095041
MASK_NEG = -1e30
LOG2E = math.log2(math.e)

_F32 = jnp.float32
_BF16 = jnp.bfloat16


def _dot(a, b):
    return jnp.dot(a, b, preferred_element_type=_F32)


def _dot_nt(a, b):
    return lax.dot_general(a, b, (((1,), (1,)), ((), ())), preferred_element_type=_F32)


def _sigmoid(x):
    return 1.0 / (1.0 + jnp.exp(-x))


def _rms(x, w):
    return x * lax.rsqrt(jnp.mean(x * x, axis=-1, keepdims=True) + EPS) * w


def _norm_proj_kernel(x_ref, nw_ref, w_ref, ws_ref, o_ref, os_ref, *, chunk):
    h = _rms(x_ref[...], nw_ref[...]).astype(_BF16)
    os_ref[...] = _dot(h, ws_ref[...])
    for c0 in range(0, w_ref.shape[1], chunk):
        o_ref[:, c0:c0 + chunk] = _dot(h, w_ref[:, c0:c0 + chunk]).astype(o_ref.dtype)


def _norm_proj(x2d, nw, w, ws, tn):
    n, d = x2d.shape
    c = w.shape[1]
    const = functools.partial(pl.BlockSpec, pipeline_mode=pl.Buffered(1))
    return pl.pallas_call(
        functools.partial(_norm_proj_kernel, chunk=_pick_tile(c, (768, 512, 256, 128))),
        grid=(n // tn,),
        in_specs=[
            pl.BlockSpec((tn, d), lambda i: (i, 0)),
            const((1, d), lambda i: (0, 0)),
            const((d, c), lambda i: (0, 0)),
            const((d, LANE), lambda i: (0, 0)),
        ],
        out_specs=[pl.BlockSpec((tn, c), lambda i: (i, 0)), pl.BlockSpec((tn, LANE), lambda i: (i, 0))],
        out_shape=[jax.ShapeDtypeStruct((n, c), _BF16), jax.ShapeDtypeStruct((n, LANE), _F32)],
        compiler_params=pltpu.CompilerParams(
            dimension_semantics=("arbitrary",), vmem_limit_bytes=VMEM_LIMIT),
        name="proj",
    )(x2d, nw, w, ws)


def _gdn_kernel(q_ref, k_ref, v_ref, z_ref, sm_ref, cw_ref, arow_ref, dtrow_ref, naw_ref, o_ref,
                xbuf, s_ref, kn_s, kb_s, qn_s, rhs_s, dec_s, qdec_s, ktt_s, eg_s, m_s, x_s):
    tt = q_ref.shape[0]
    hd = H_A * DK_A

    @pl.when(pl.program_id(1) == 0)
    def _():
        xbuf[:, 0:8, :] = jnp.zeros((3, 8, hd), _F32)
        s_ref[...] = jnp.zeros_like(s_ref)

    for idx, ref in enumerate((q_ref, k_ref, v_ref)):
        xbuf[idx, 8:8 + tt, :] = ref[...].astype(_F32)

    sm = sm_ref[...]
    beta_full = _sigmoid(sm)
    xg = sm + dtrow_ref[...]
    softplus = jnp.maximum(xg, 0.0) + jnp.log(1.0 + jnp.exp(-jnp.abs(xg)))
    g_full = -jnp.exp(arow_ref[...]) * softplus
    row = lax.broadcasted_iota(jnp.int32, (tt, tt), 0)
    col = lax.broadcasted_iota(jnp.int32, (tt, tt), 1)
    tri = (col <= row).astype(_F32)
    gcum = jnp.dot(tri, g_full, preferred_element_type=_F32, precision=lax.Precision.HIGHEST)
    gcum_t = gcum.T
    strict = col < row
    eye = (col == row).astype(_F32)

    for h in range(H_A):
        sl = slice(h * DK_A, (h + 1) * DK_A)
        conv = []
        for idx in range(3):
            acc = None
            for j in range(CONV_K):
                term = cw_ref[j:j + 1, idx * hd + h * DK_A: idx * hd + (h + 1) * DK_A] * \
                    xbuf[idx, 8 - (CONV_K - 1) + j: 8 - (CONV_K - 1) + j + tt, sl]
                acc = term if acc is None else acc + term
            conv.append(acc * _sigmoid(acc))
        qh, kh, vh = conv
        qn = qh * lax.rsqrt(jnp.sum(qh * qh, axis=-1, keepdims=True) + EPS) * (DK_A ** -0.5)
        kn = kh * lax.rsqrt(jnp.sum(kh * kh, axis=-1, keepdims=True) + EPS)
        bcol = beta_full[:, h:h + 1]
        gcol = gcum[:, H_A + h:H_A + h + 1]
        grow = gcum_t[H_A + h:H_A + h + 1, :]
        glast = gcum[tt - 1:tt, H_A + h:H_A + h + 1]
        kb = kn * bcol
        eg = jnp.exp(gcol)
        rhs_s[h, :, 0:DV_A] = vh * bcol
        rhs_s[h, :, DV_A:DV_A + DK_A] = kb * eg
        dec_s[h] = jnp.where(strict, jnp.exp(jnp.where(strict, gcol - grow, 0.0)), 0.0)
        qdec_s[h] = (qn * eg).astype(_BF16)
        ktt_s[h] = (kn * jnp.exp(glast - gcol)).T.astype(_BF16)
        kn_s[h] = kn.astype(_BF16)
        kb_s[h] = kb.astype(_BF16)
        qn_s[h] = qn.astype(_BF16)
        eg_s[h] = jnp.broadcast_to(jnp.exp(glast), (DK_A, DV_A))

    for idx in range(3):
        xbuf[idx, 0:8, :] = xbuf[idx, tt:tt + 8, :]

    n_sq = int(math.log2(tt))

    heads = range(H_A)
    for h in heads:
        m = -(_dot_nt(kb_s[h], kn_s[h]) * dec_s[h])
        m_s[h] = m
        x_s[h] = m
    for _ in range(n_sq - 1):
        for h in heads:
            mb = m_s[h].astype(_BF16)
            m = _dot(mb, mb)
            m_s[h] = m
            x_s[h] = x_s[h] + m + _dot(x_s[h].astype(_BF16), m.astype(_BF16))
    for h in heads:
        rhs = rhs_s[h]
        rhs_s[h] = rhs + _dot(x_s[h].astype(_BF16), rhs.astype(_BF16))
    for h in heads:
        m_s[h] = _dot_nt(qn_s[h], kn_s[h]) * (dec_s[h] + eye)
    for h in heads:
        sb = s_ref[h].astype(_BF16)
        vnew = rhs_s[h, :, 0:DV_A] - _dot(rhs_s[h, :, DV_A:DV_A + DK_A].astype(_BF16), sb)
        vb = vnew.astype(_BF16)
        o = _dot(qdec_s[h], sb) + _dot(m_s[h].astype(_BF16), vb)
        s_ref[h] = s_ref[h] * eg_s[h] + _dot(ktt_s[h], vb)
        rhs_s[h, :, 0:DV_A] = o

    naw = naw_ref[...]
    for h in range(H_A):
        sl = slice(h * DV_A, (h + 1) * DV_A)
        o = rhs_s[h, :, 0:DV_A]
        z = z_ref[:, sl].astype(_F32)
        on = o * lax.rsqrt(jnp.mean(o * o, axis=-1, keepdims=True) + EPS) * naw
        o_ref[:, sl] = (on * (z * _sigmoid(z))).astype(o_ref.dtype)


def _gdn(pb, ps, cw, arow, dtrow, naw, b_, t_, col0):
    tt = GDN_CHUNK
    hd = H_A * DK_A
    nt = t_ // tt
    cb = col0 // hd

    def rowblk(j):
        return pl.BlockSpec((tt, hd), lambda b, t: (b * nt + t, cb + j))

    const = lambda shape: pl.BlockSpec(shape, lambda b, t: (0, 0))
    return pl.pallas_call(
        _gdn_kernel,
        grid=(b_, nt),
        in_specs=[rowblk(0), rowblk(1), rowblk(2), rowblk(3),
                  pl.BlockSpec((tt, LANE), lambda b, t: (b * nt + t, 0)),
                  const((CONV_K, 3 * hd)), const((1, LANE)), const((1, LANE)), const((1, DV_A))],
        out_specs=pl.BlockSpec((tt, hd), lambda b, t: (b * nt + t, 0)),
        out_shape=jax.ShapeDtypeStruct((b_ * t_, hd), _BF16),
        scratch_shapes=[
            pltpu.VMEM((3, tt + 8, hd), _F32),
            pltpu.VMEM((H_A, DK_A, DV_A), _F32),
            pltpu.VMEM((H_A, tt, DK_A), _BF16),
            pltpu.VMEM((H_A, tt, DK_A), _BF16),
            pltpu.VMEM((H_A, tt, DK_A), _BF16),
            pltpu.VMEM((H_A, tt, DV_A + DK_A), _F32),
            pltpu.VMEM((H_A, tt, tt), _F32),
            pltpu.VMEM((H_A, tt, DK_A), _BF16),
            pltpu.VMEM((H_A, DK_A, tt), _BF16),
            pltpu.VMEM((H_A, DK_A, DV_A), _F32),
            pltpu.VMEM((H_A, tt, tt), _F32),
            pltpu.VMEM((H_A, tt, tt), _F32),
        ],
        compiler_params=pltpu.CompilerParams(
            dimension_semantics=("arbitrary", "arbitrary"), vmem_limit_bytes=VMEM_LIMIT),
        name="gdn",
    )(pb, pb, pb, pb, ps, cw, arow, dtrow, naw)


def _bias_kernel(tab_ref, bucket_ref, o_ref):
    bucket = bucket_ref[...]
    nq, w = bucket.shape
    for h in range(H_B):
        acc = jnp.zeros((nq, w), _F32)
        for b in range(N_BUCKETS):
            acc = acc + jnp.where(bucket == b, tab_ref[b, h], 0.0)
        far = acc[:, 2 * LANE:3 * LANE]
        o_ref[h] = (acc[:, 0:2 * LANE] - jnp.concatenate([far, far], axis=1)) * LOG2E


def _bias_tiles(rel_table, bucket):
    return pl.pallas_call(
        _bias_kernel,
        in_specs=[pl.BlockSpec(memory_space=pltpu.SMEM),
                  pl.BlockSpec(bucket.shape, lambda: (0, 0))],
        out_specs=pl.BlockSpec((H_B, Q_BLOCK, 2 * LANE), lambda: (0, 0, 0)),
        out_shape=jax.ShapeDtypeStruct((H_B, Q_BLOCK, 2 * LANE), _F32),
        name="rel_bias_tiles",
    )(rel_table, bucket)


def _dsa_kernel(q_ref, k_ref, v_ref, iq_ref, ik_ref, sm_ref, bias_ref, o_ref,
                key_s, qs_s, iwb_s, m_s, l_s, acc_s, s_s, si_s, sf_s, *, topk):
    nq = q_ref.shape[0]
    st_w = KEY_TILE
    assert nq == st_w
    ng = st_w // LANE
    qb = pl.program_id(1)
    n_st = qb + 1
    hd = D_HEAD_B
    scale = D_HEAD_B ** -0.5 * LOG2E
    f_topk = float(topk)

    lane_j = lax.broadcasted_iota(jnp.int32, (nq, LANE), 1)
    row_i = lax.broadcasted_iota(jnp.int32, (nq, LANE), 0)

    def grp(g):
        return slice(g * LANE, (g + 1) * LANE)

    iw = sm_ref[...]
    for h in range(H_B):
        sl = slice(h * hd, (h + 1) * hd)
        qs_s[h] = (q_ref[:, sl].astype(_F32) * scale).astype(_BF16)
        iwb_s[h] = jnp.broadcast_to(iw[:, 2 * H_A + h:2 * H_A + h + 1], (nq, LANE))
        m_s[h] = jnp.full((nq, LANE), MASK_NEG, _F32)
        l_s[h] = jnp.zeros((nq, LANE), _F32)
        acc_s[h] = jnp.zeros((nq, hd), _F32)

    def scores(st, diag):
        ikt = ik_ref[pl.ds(pl.multiple_of(st * st_w, st_w), st_w), :]
        acc = [jnp.zeros((nq, LANE), _F32) for _ in range(ng)]
        for p in range(H_IDX // 2):
            xq = iq_ref[:, p * LANE:(p + 1) * LANE]
            for half in range(2):
                h = 2 * p + half
                s = jnp.maximum(_dot_nt(xq, ikt[:, half * LANE:(half + 1) * LANE]), 0.0)
                w = iwb_s[h]
                for g in range(ng):
                    acc[g] = acc[g] + w * s[:, grp(g)]
        for g in range(ng):
            a = acc[g]
            if diag:
                a = jnp.where((g * LANE + lane_j) // CHUNK <= row_i // CHUNK, a, -jnp.inf)
            bits = pltpu.bitcast(a, jnp.int32)
            key_s[st, g] = bits ^ ((bits >> 31) & INT_MAX)

    def p1(st, carry):
        scores(st, False)
        return carry

    lax.fori_loop(0, qb, p1, 0)
    scores(qb, True)

    ones_b = jnp.ones((LANE, LANE), _BF16)
    assert key_s.shape[0] * ng <= 256

    T_, CAND_, MIDX_ = 0, 1, 2
    ACC_, CNT_ = 0, 1
    rb = 64
    lane_rb = lax.broadcasted_iota(jnp.int32, (rb, LANE), 1)

    def count(pred_fn):
        sf_s[ACC_] = jnp.zeros((nq, LANE), _F32)

        def body(st, carry):
            for r0 in range(0, nq, rb):
                rs = slice(r0, r0 + rb)
                acc = sf_s[ACC_, rs, :]
                for g in range(ng):
                    acc = acc + jnp.where(
                        pred_fn(key_s[st, g, rs, :], st * st_w + g * LANE + lane_rb, rs), 1.0, 0.0)
                sf_s[ACC_, rs, :] = acc
            return carry

        lax.fori_loop(0, n_st, body, 0)
        return _dot(sf_s[ACC_].astype(_BF16), ones_b)

    c0 = count(lambda k, j, rs: k >= 0)
    ok0 = c0 >= f_topk
    si_s[T_] = jnp.where(ok0, 0, INT_MIN)
    sf_s[CNT_] = jnp.where(ok0, c0, (n_st * st_w).astype(_F32))

    def bit_body(i, carry):
        t_old = si_s[T_]
        cand = t_old + jnp.left_shift(jnp.int32(1), 30 - i)
        si_s[CAND_] = cand
        c = count(lambda k, j, rs: k >= si_s[CAND_, rs, :])
        ok = c >= f_topk
        si_s[T_] = jnp.where(ok, cand, t_old)
        sf_s[CNT_] = jnp.where(ok, c, sf_s[CNT_])
        return carry

    lax.fori_loop(0, 31, bit_body, 0)
    finite_thr = si_s[T_] != KEY_NEG_INF
    si_s[MIDX_] = jnp.where(finite_thr, INT_MAX, 0)

    @pl.when(jnp.max(jnp.where(finite_thr, sf_s[CNT_], 0.0)) > f_topk)
    def _():
        sf_s[CNT_] = f_topk - count(lambda k, j, rs: k > si_s[T_, rs, :])
        n_bits = max(1, int(math.ceil(math.log2(key_s.shape[0] * st_w))))
        si_s[MIDX_] = jnp.zeros((nq, LANE), jnp.int32)

        def idx_body(i, carry):
            mp = si_s[MIDX_]
            cand = mp + jnp.left_shift(jnp.int32(1), n_bits - 1 - i)
            si_s[CAND_] = cand
            c = count(lambda k, j, rs: jnp.where(k == si_s[T_, rs, :], j, INT_MAX) < si_s[CAND_, rs, :])
            si_s[MIDX_] = jnp.where(c < sf_s[CNT_], cand, mp)
            return carry

        lax.fori_loop(0, n_bits, idx_body, 0)
        si_s[MIDX_] = jnp.where(si_s[T_] != KEY_NEG_INF, si_s[MIDX_] + 1, 0)

    def p2(st, carry):
        for r0 in range(0, nq, rb):
            rs = slice(r0, r0 + rb)
            t = si_s[T_, rs, :]
            m_idx = si_s[MIDX_, rs, :]
            for g in range(ng):
                k = key_s[st, g, rs, :]
                jj = jnp.where(k == t, st * st_w + g * LANE + lane_rb, INT_MAX)
                mb = jnp.where(k > t, 0.0, jnp.where(jj < m_idx, 0.0, MASK_NEG))
                key_s[st, g, rs, :] = pltpu.bitcast(mb, jnp.int32)
        return carry

    lax.fori_loop(0, n_st, p2, 0)

    def attend(st, mode):
        off = pl.multiple_of(st * st_w, st_w)
        for h in range(H_B):
            sl = slice(h * hd, (h + 1) * hd)
            mask = jnp.concatenate([pltpu.bitcast(key_s[st, g], _F32) for g in range(ng)], axis=1)
            s = _dot_nt(qs_s[h], k_ref[pl.ds(off, st_w), sl]) + mask
            if mode:
                s_s[...] = s
                b_prev = bias_ref[h, :, 0:LANE]
                if mode == 1:
                    s_s[0:LANE, (ng - 1) * LANE:ng * LANE] += b_prev
                else:
                    b_diag = bias_ref[h, :, LANE:2 * LANE]
                    for rt in range(ng):
                        s_s[grp(rt), grp(rt)] += b_diag
                        if rt:
                            s_s[grp(rt), grp(rt - 1)] += b_prev
                s = s_s[...]
            m_old = m_s[h]
            gmax = s[:, grp(0)]
            for g in range(1, ng):
                gmax = jnp.maximum(gmax, s[:, grp(g)])
            m_new = jnp.maximum(m_old, jnp.max(gmax, axis=1, keepdims=True))
            alpha = jnp.exp2(m_old - m_new)
            pieces = [jnp.exp2(s[:, grp(g)] - m_new) for g in range(ng)]
            psum = pieces[0]
            for g in range(1, ng):
                psum = psum + pieces[g]
            p = jnp.concatenate(pieces, axis=1).astype(_BF16)
            m_s[h] = m_new
            l_s[h] = alpha * l_s[h] + psum
            acc_s[h] = alpha * acc_s[h] + _dot(p, v_ref[pl.ds(off, st_w), sl])

    def p3(st, carry):
        attend(st, 0)
        return carry

    lax.fori_loop(0, qb - 1, p3, 0)

    @pl.when(qb > 0)
    def _():
        attend(qb - 1, 1)

    attend(qb, 2)

    for h in range(H_B):
        sl = slice(h * hd, (h + 1) * hd)
        l_fin = jnp.sum(l_s[h], axis=1, keepdims=True)
        o_ref[:, sl] = (acc_s[h] / l_fin).astype(o_ref.dtype)


def _dsa(pb, ps, bias, b_, t_, col0):
    nq = KEY_TILE
    nb = t_ // nq
    hd = H_B * D_HEAD_B
    topk = min(TOPK_MAX, t_ // 4)
    iq_w = H_IDX * D_IDX
    ik_w = 2 * LANE
    assert col0 % hd == 0 and (col0 + 3 * hd) % iq_w == 0 and (col0 + 3 * hd + iq_w) % ik_w == 0
    cq = col0 // hd
    resident = functools.partial(pl.BlockSpec, pipeline_mode=pl.Buffered(1))
    return pl.pallas_call(
        functools.partial(_dsa_kernel, topk=topk),
        grid=(b_, nb),
        in_specs=[
            pl.BlockSpec((nq, hd), lambda b, i: (b * nb + i, cq)),
            resident((t_, hd), lambda b, i: (b, cq + 1)),
            resident((t_, hd), lambda b, i: (b, cq + 2)),
            pl.BlockSpec((nq, iq_w), lambda b, i: (b * nb + i, (col0 + 3 * hd) // iq_w)),
            resident((t_, ik_w), lambda b, i: (b, (col0 + 3 * hd + iq_w) // ik_w)),
            pl.BlockSpec((nq, LANE), lambda b, i: (b * nb + i, 0)),
            resident((H_B, Q_BLOCK, 2 * LANE), lambda b, i: (0, 0, 0)),
        ],
        out_specs=pl.BlockSpec((nq, hd), lambda b, i: (b * nb + i, 0)),
        out_shape=jax.ShapeDtypeStruct((b_ * t_, hd), _BF16),
        scratch_shapes=[
            pltpu.VMEM((nb, KEY_TILE // LANE, nq, LANE), jnp.int32),
            pltpu.VMEM((H_B, nq, D_HEAD_B), _BF16),
            pltpu.VMEM((H_B, nq, LANE), _F32),
            pltpu.VMEM((H_B, nq, LANE), _F32),
            pltpu.VMEM((H_B, nq, LANE), _F32),
            pltpu.VMEM((H_B, nq, D_HEAD_B), _F32),
            pltpu.VMEM((nq, KEY_TILE), _F32),
            pltpu.VMEM((3, nq, LANE), jnp.int32),
            pltpu.VMEM((2, nq, LANE), _F32),
        ],
        compiler_params=pltpu.CompilerParams(
            dimension_semantics=("arbitrary", "arbitrary"), vmem_limit_bytes=VMEM_LIMIT),
        name="dsa",
    )(pb, pb, pb, pb, pb, ps, bias)


def _merge_kernel(oa_ref, ob_ref, g_ref, x_ref, bg_ref, wa_ref, wb_ref, wo_ref, o_ref):
    d = x_ref.shape[1]
    a = _dot(oa_ref[...], wa_ref[...])
    b = _dot(ob_ref[...], wb_ref[...])
    g = _sigmoid(g_ref[...].astype(_F32) + bg_ref[...])
    merged = g[:, 0:d] * a + g[:, d:2 * d] * b
    o_ref[...] = x_ref[...] + _dot(merged.astype(_BF16), wo_ref[...])


def _merge(oa, ob, pb, x2d, bg, wa, wb, wo, tn):
    n, d = x2d.shape
    row = lambda w: pl.BlockSpec((tn, w), lambda i: (i, 0))
    const = lambda a: pl.BlockSpec(a.shape, lambda i: (0, 0))
    return pl.pallas_call(
        _merge_kernel,
        grid=(n // tn,),
        in_specs=[row(oa.shape[1]), row(ob.shape[1]), row(2 * d), row(d),
                  const(bg), const(wa), const(wb), const(wo)],
        out_specs=row(d),
        out_shape=jax.ShapeDtypeStruct((n, d), _F32),
        compiler_params=pltpu.CompilerParams(
            dimension_semantics=("arbitrary",), vmem_limit_bytes=VMEM_LIMIT),
        name="merge",
    )(oa, ob, pb, x2d, bg, wa, wb, wo)


def _mlp_kernel(x_ref, n2_ref, w1_ref, w2_ref, nf_ref, o_ref, *, final_norm):
    x = x_ref[...]
    h2 = _rms(x, n2_ref[...]).astype(_BF16)
    hid = jnp.maximum(_dot(h2, w1_ref[...]), 0.0)
    y = x + _dot((hid * hid).astype(_BF16), w2_ref[...])
    o_ref[...] = _rms(y, nf_ref[...]) if final_norm else y


def _mlp(x1, n2, w1, w2, nf, tn, final_norm):
    n, d = x1.shape
    row = pl.BlockSpec((tn, d), lambda i: (i, 0))
    const = lambda a: pl.BlockSpec(a.shape, lambda i: (0, 0))
    return pl.pallas_call(
        functools.partial(_mlp_kernel, final_norm=final_norm),
        grid=(n // tn,),
        in_specs=[row, const(n2), const(w1), const(w2), const(nf)],
        out_specs=row,
        out_shape=jax.ShapeDtypeStruct((n, d), _F32),
        compiler_params=pltpu.CompilerParams(
            dimension_semantics=("arbitrary",), vmem_limit_bytes=VMEM_LIMIT),
        name="mlp",
    )(x1, n2, w1, w2, nf)


def _t5_bucket(rel):
    half = N_BUCKETS // 2
    max_exact = half // 2
    base = jnp.where(rel > 0, half, 0)
    n = jnp.abs(rel)
    n_f = jnp.maximum(n, 1).astype(jnp.float32)
    large = max_exact + (jnp.log(n_f / max_exact) / math.log(MAX_DISTANCE / max_exact)
                         * (half - max_exact)).astype(jnp.int32)
    large = jnp.minimum(large, half - 1)
    return base + jnp.where(n < max_exact, n, large)


def _pick_tile(n, prefs):
    for t in prefs:
        if n % t == 0:
            return t
    raise ValueError(f"no tile in {prefs} divides {n}")


def kernel(x, norm1_w, w_in, conv_a_w, a_log, dt_bias, norm_a_w, rel_bias_table, w_gate, b_gate,
           w_proj_a, w_proj_b, w_out, norm2_w, w_ff1, w_ff2, norm_final_w):
    b_, t_, d = x.shape
    depth = norm1_w.shape[0]
    n = b_ * t_
    ha, hb = H_A * DK_A, H_B * D_HEAD_B
    assert t_ % KEY_TILE == 0 and t_ % GDN_CHUNK == 0 and d % LANE == 0
    assert DK_A == DV_A == D_HEAD_B == LANE and 2 * D_IDX == LANE
    assert Q_BLOCK >= MAX_DISTANCE

    o_za = 3 * ha
    o_ba = 4 * ha
    o_aa = o_ba + H_A
    o_qb = o_aa + H_A
    o_iq = o_qb + 3 * hb
    o_ik = o_iq + H_IDX * D_IDX
    o_iw = o_ik + D_IDX

    rel = (jnp.arange(2 * Q_BLOCK, dtype=jnp.int32)[None, :] - Q_BLOCK) \
        - jnp.arange(Q_BLOCK, dtype=jnp.int32)[:, None]
    far = jnp.full((Q_BLOCK, LANE), -(Q_BLOCK + 1), jnp.int32)
    bucket = _t5_bucket(jnp.concatenate([rel, far], axis=1))
    bias = _bias_tiles(rel_bias_table.astype(_F32), bucket)

    x2d = x.reshape(n, d)
    tn = _pick_tile(n, (512, 256, 128))
    for layer in range(depth):
        wi = w_in[layer]
        zpad = lambda w: jnp.zeros((d, w), wi.dtype)
        ws = jnp.concatenate([wi[:, o_ba:o_qb], wi[:, o_iw:o_iw + H_IDX], zpad(LANE - 3 * H_A)],
                             axis=1).astype(_BF16)
        ik = wi[:, o_ik:o_iw]
        wb16 = jnp.concatenate([w_gate[layer], wi[:, 0:o_ba], wi[:, o_qb:o_ik],
                                ik, zpad(D_IDX), zpad(D_IDX), ik], axis=1).astype(_BF16)
        nw = norm1_w[layer].reshape(1, d).astype(_F32)
        pb, ps = _norm_proj(x2d, nw, wb16, ws, tn)

        col_a = 2 * d
        col_b = col_a + 4 * ha
        assert col_a % ha == 0
        lanes = lambda v, off: jnp.zeros((1, LANE), _F32).at[0, off:off + H_A].set(v.astype(_F32))
        oa = _gdn(pb, ps, conv_a_w[layer].astype(_F32), lanes(a_log[layer], H_A), lanes(dt_bias[layer], H_A),
                  norm_a_w[layer].reshape(1, DV_A).astype(_F32), b_, t_, col_a)
        ob = _dsa(pb, ps, bias, b_, t_, col_b)

        x2d = _merge(oa, ob, pb, x2d, b_gate[layer].reshape(1, 2 * d).astype(_F32),
                     w_proj_a[layer].astype(_BF16), w_proj_b[layer].astype(_BF16),
                     w_out[layer].astype(_BF16), tn)
        x2d = _mlp(x2d, norm2_w[layer].reshape(1, d).astype(_F32), w_ff1[layer].astype(_BF16),
                   w_ff2[layer].astype(_BF16), norm_final_w.reshape(1, d).astype(_F32),
                   _pick_tile(n, (256, 128)), final_norm=layer == depth - 1)
    return x2d.reshape(b_, t_, d)
```

```python
import functools
import math

import jax
import jax.numpy as jnp
import numpy as np
from jax import lax
from jax.experimental import pallas as pl
from jax.experimental.pallas import tpu as pltpu

EPS = 1e-6
H_A = 8
DK_A = 128
DV_A = 128
CONV_K = 4
GDN_CHUNK = 128
GDN_ROW_CHUNK = 32
H_B = 8
D_HEAD_B = 128
H_IDX = 8
D_IDX = 64
TOPK_MAX = 256
Q_BLOCK = 128
CHUNK = 64
KEY_TILE = 512
N_BUCKETS = 32
MAX_DISTANCE = 128

LANE = 128
VMEM_LIMIT = 56 * 1024 * 1024

INT_MIN = -(2**31)
INT_MAX = 2**31 - 1
KEY_NEG_INF = -2139095041
MASK_NEG = -1e30
LOG2E = math.log2(math.e)

_F32 = jnp.float32
_BF16 = jnp.bfloat16


def _dot(a, b):
    return jnp.dot(a, b, preferred_element_type=_F32)


def _dot_nt(a, b):
    return lax.dot_general(a, b, (((1,), (1,)), ((), ())), preferred_element_type=_F32)


def _sigmoid(x):
    return 1.0 / (1.0 + jnp.exp(-x))


def _rms(x, w):
    return x * lax.rsqrt(jnp.mean(x * x, axis=-1, keepdims=True) + EPS) * w


def _norm_proj_kernel(x_ref, nw_ref, w_ref, ws_ref, o_ref, os_ref, *, chunk):
    h = _rms(x_ref[...], nw_ref[...]).astype(_BF16)
    os_ref[...] = _dot(h, ws_ref[...])
    for c0 in range(0, w_ref.shape[1], chunk):
        o_ref[:, c0:c0 + chunk] = _dot(h, w_ref[:, c0:c0 + chunk]).astype(o_ref.dtype)


def _norm_proj(x2d, nw, w, ws, tn):
    n, d = x2d.shape
    c = w.shape[1]
    const = functools.partial(pl.BlockSpec, pipeline_mode=pl.Buffered(1))
    return pl.pallas_call(
        functools.partial(_norm_proj_kernel, chunk=_pick_tile(c, (768, 512, 256, 128))),
        grid=(n // tn,),
        in_specs=[
            pl.BlockSpec((tn, d), lambda i: (i, 0)),
            const((1, d), lambda i: (0, 0)),
            const((d, c), lambda i: (0, 0)),
            const((d, LANE), lambda i: (0, 0)),
        ],
        out_specs=[pl.BlockSpec((tn, c), lambda i: (i, 0)), pl.BlockSpec((tn, LANE), lambda i: (i, 0))],
        out_shape=[jax.ShapeDtypeStruct((n, c), _BF16), jax.ShapeDtypeStruct((n, LANE), _F32)],
        compiler_params=pltpu.CompilerParams(
            dimension_semantics=("arbitrary",), vmem_limit_bytes=VMEM_LIMIT),
        name="proj",
    )(x2d, nw, w, ws)


def _gdn_kernel(q_ref, k_ref, v_ref, z_ref, sm_ref, cw_ref, arow_ref, dtrow_ref, naw_ref, o_ref,
                xbuf, s_ref, kn_s, kb_s, qn_s, rhs_s, dec_s, qdec_s, ktt_s, eg_s, m_s, x_s, sc_s):
    tt = q_ref.shape[0]
    hd = H_A * DK_A

    @pl.when(pl.program_id(1) == 0)
    def _():
        xbuf[:, 0:8, :] = jnp.zeros((3, 8, hd), _F32)
        s_ref[...] = jnp.zeros_like(s_ref)

    for idx, ref in enumerate((q_ref, k_ref, v_ref)):
        xbuf[idx, 8:8 + tt, :] = ref[...].astype(_F32)

    sm = sm_ref[...]
    beta_full = _sigmoid(sm)
    xg = sm + dtrow_ref[...]
    softplus = jnp.maximum(xg, 0.0) + jnp.log(1.0 + jnp.exp(-jnp.abs(xg)))
    g_full = -jnp.exp(arow_ref[...]) * softplus
    row = lax.broadcasted_iota(jnp.int32, (tt, tt), 0)
    col = lax.broadcasted_iota(jnp.int32, (tt, tt), 1)
    tri = (col <= row).astype(_F32)
    gcum = jnp.dot(tri, g_full, preferred_element_type=_F32, precision=lax.Precision.HIGHEST)
    GC_, GCT_, BETA_ = 0, 1, 2
    sc_s[GC_] = gcum
    sc_s[GCT_] = gcum.T
    sc_s[BETA_] = beta_full
    eye = (col == row).astype(_F32)

    rc = GDN_ROW_CHUNK
    separate = pl.program_id(0) >= 0

    def prepare(r0, h):
        rs = slice(r0, r0 + rc)
        sl = slice(h * DK_A, (h + 1) * DK_A)
        strict = lax.broadcasted_iota(jnp.int32, (rc, tt), 1) < r0 + lax.broadcasted_iota(jnp.int32, (rc, tt), 0)
        conv = []
        for idx in range(3):
            acc = None
            for j in range(CONV_K):
                lo = r0 + 8 - (CONV_K - 1) + j
                term = cw_ref[j:j + 1, idx * hd + h * DK_A: idx * hd + (h + 1) * DK_A] * xbuf[idx, lo:lo + rc, sl]
                acc = term if acc is None else acc + term
            conv.append(acc * _sigmoid(acc))
        qh, kh, vh = conv
        qn = qh * lax.rsqrt(jnp.sum(qh * qh, axis=-1, keepdims=True) + EPS) * (DK_A ** -0.5)
        kn = kh * lax.rsqrt(jnp.sum(kh * kh, axis=-1, keepdims=True) + EPS)
        bcol = sc_s[BETA_, rs, h:h + 1]
        gcol = sc_s[GC_, rs, H_A + h:H_A + h + 1]
        grow = sc_s[GCT_, H_A + h:H_A + h + 1, :]
        glast = sc_s[GC_, tt - 1:tt, H_A + h:H_A + h + 1]
        kb = kn * bcol
        eg = jnp.exp(gcol)
        rhs_s[h, rs, 0:DV_A] = vh * bcol
        rhs_s[h, rs, DV_A:DV_A + DK_A] = kb * eg
        dec_s[h, rs, :] = jnp.where(strict, jnp.exp(jnp.where(strict, gcol - grow, 0.0)), 0.0)
        qdec_s[h, rs, :] = (qn * eg).astype(_BF16)
        x_s[h, rs, :] = kn * jnp.exp(glast - gcol)
        kn_s[h, rs, :] = kn.astype(_BF16)
        kb_s[h, rs, :] = kb.astype(_BF16)
        qn_s[h, rs, :] = qn.astype(_BF16)

    def prepare_pair(r0, h0):
        prepare(r0, h0)
        prepare(r0, h0 + 1)

    for r0 in range(0, tt, rc):
        for h in range(0, H_A, 2):
            pl.when(separate)(functools.partial(prepare_pair, r0, h))

    for h in range(H_A):
        ktt_s[h] = x_s[h].T.astype(_BF16)
        eg_s[h] = jnp.broadcast_to(jnp.exp(sc_s[GC_, tt - 1:tt, H_A + h:H_A + h + 1]), (DK_A, DV_A))

    for idx in range(3):
        xbuf[idx, 0:8, :] = xbuf[idx, tt:tt + 8, :]

    n_sq = int(math.log2(tt))

    heads = range(H_A)
    for h in heads:
        m = -(_dot_nt(kb_s[h], kn_s[h]) * dec_s[h])
        m_s[h] = m
        x_s[h] = m
    for _ in range(n_sq - 1):
        for h in heads:
            mb = m_s[h].astype(_BF16)
            m = _dot(mb, mb)
            m_s[h] = m
            x_s[h] = x_s[h] + m + _dot(x_s[h].astype(_BF16), m.astype(_BF16))
    for h in heads:
        rhs = rhs_s[h]
        rhs_s[h] = rhs + _dot(x_s[h].astype(_BF16), rhs.astype(_BF16))
    for h in heads:
        m_s[h] = _dot_nt(qn_s[h], kn_s[h]) * (dec_s[h] + eye)
    for h in heads:
        sb = s_ref[h].astype(_BF16)
        vnew = rhs_s[h, :, 0:DV_A] - _dot(rhs_s[h, :, DV_A:DV_A + DK_A].astype(_BF16), sb)
        vb = vnew.astype(_BF16)
        o = _dot(qdec_s[h], sb) + _dot(m_s[h].astype(_BF16), vb)
        s_ref[h] = s_ref[h] * eg_s[h] + _dot(ktt_s[h], vb)
        rhs_s[h, :, 0:DV_A] = o

    naw = naw_ref[...]
    for h in range(H_A):
        sl = slice(h * DV_A, (h + 1) * DV_A)
        o = rhs_s[h, :, 0:DV_A]
        z = z_ref[:, sl].astype(_F32)
        on = o * lax.rsqrt(jnp.mean(o * o, axis=-1, keepdims=True) + EPS) * naw
        o_ref[:, sl] = (on * (z * _sigmoid(z))).astype(o_ref.dtype)


def _gdn(pb, ps, cw, arow, dtrow, naw, b_, t_, col0):
    tt = GDN_CHUNK
    hd = H_A * DK_A
    nt = t_ // tt
    cb = col0 // hd

    def rowblk(j):
        return pl.BlockSpec((tt, hd), lambda b, t: (b * nt + t, cb + j))

    const = lambda shape: pl.BlockSpec(shape, lambda b, t: (0, 0))
    return pl.pallas_call(
        _gdn_kernel,
        grid=(b_, nt),
        in_specs=[rowblk(0), rowblk(1), rowblk(2), rowblk(3),
                  pl.BlockSpec((tt, LANE), lambda b, t: (b * nt + t, 0)),
                  const((CONV_K, 3 * hd)), const((1, LANE)), const((1, LANE)), const((1, DV_A))],
        out_specs=pl.BlockSpec((tt, hd), lambda b, t: (b * nt + t, 0)),
        out_shape=jax.ShapeDtypeStruct((b_ * t_, hd), _BF16),
        scratch_shapes=[
            pltpu.VMEM((3, tt + 8, hd), _F32),
            pltpu.VMEM((H_A, DK_A, DV_A), _F32),
            pltpu.VMEM((H_A, tt, DK_A), _BF16),
            pltpu.VMEM((H_A, tt, DK_A), _BF16),
            pltpu.VMEM((H_A, tt, DK_A), _BF16),
            pltpu.VMEM((H_A, tt, DV_A + DK_A), _F32),
            pltpu.VMEM((H_A, tt, tt), _F32),
            pltpu.VMEM((H_A, tt, DK_A), _BF16),
            pltpu.VMEM((H_A, DK_A, tt), _BF16),
            pltpu.VMEM((H_A, DK_A, DV_A), _F32),
            pltpu.VMEM((H_A, tt, tt), _F32),
            pltpu.VMEM((H_A, tt, tt), _F32),
            pltpu.VMEM((3, tt, LANE), _F32),
        ],
        compiler_params=pltpu.CompilerParams(
            dimension_semantics=("arbitrary", "arbitrary"), vmem_limit_bytes=VMEM_LIMIT),
        name="gdn",
    )(pb, pb, pb, pb, ps, cw, arow, dtrow, naw)


def _bias_kernel(tab_ref, bucket_ref, o_ref):
    bucket = bucket_ref[...]
    nq, w = bucket.shape
    for h in range(H_B):
        acc = jnp.zeros((nq, w), _F32)
        for b in range(N_BUCKETS):
            acc = acc + jnp.where(bucket == b, tab_ref[b, h], 0.0)
        far = acc[:, 2 * LANE:3 * LANE]
        o_ref[h] = (acc[:, 0:2 * LANE] - jnp.concatenate([far, far], axis=1)) * LOG2E


def _bias_tiles(rel_table, bucket):
    return pl.pallas_call(
        _bias_kernel,
        in_specs=[pl.BlockSpec(memory_space=pltpu.SMEM),
                  pl.BlockSpec(bucket.shape, lambda: (0, 0))],
        out_specs=pl.BlockSpec((H_B, Q_BLOCK, 2 * LANE), lambda: (0, 0, 0)),
        out_shape=jax.ShapeDtypeStruct((H_B, Q_BLOCK, 2 * LANE), _F32),
        name="rel_bias_tiles",
    )(rel_table, bucket)


def _dsa_kernel(q_ref, k_ref, v_ref, iq_ref, ik_ref, sm_ref, bias_ref, o_ref,
                key_s, qs_s, iwb_s, m_s, l_s, acc_s, s_s, si_s, sf_s, *, topk):
    nq = q_ref.shape[0]
    st_w = KEY_TILE
    assert nq == st_w
    ng = st_w // LANE
    qb = pl.program_id(1)
    n_st = qb + 1
    hd = D_HEAD_B
    scale = D_HEAD_B ** -0.5 * LOG2E
    f_topk = float(topk)

    lane_j = lax.broadcasted_iota(jnp.int32, (nq, LANE), 1)
    row_i = lax.broadcasted_iota(jnp.int32, (nq, LANE), 0)

    def grp(g):
        return slice(g * LANE, (g + 1) * LANE)

    iw = sm_ref[...]
    for h in range(H_B):
        sl = slice(h * hd, (h + 1) * hd)
        qs_s[h] = (q_ref[:, sl].astype(_F32) * scale).astype(_BF16)
        iwb_s[h] = jnp.broadcast_to(iw[:, 2 * H_A + h:2 * H_A + h + 1], (nq, LANE))
        m_s[h] = jnp.full((nq, LANE), MASK_NEG, _F32)
        l_s[h] = jnp.zeros((nq, LANE), _F32)
        acc_s[h] = jnp.zeros((nq, hd), _F32)

    def scores(st, diag):
        ikt = ik_ref[pl.ds(pl.multiple_of(st * st_w, st_w), st_w), :]
        acc = [jnp.zeros((nq, LANE), _F32) for _ in range(ng)]
        for p in range(H_IDX // 2):
            xq = iq_ref[:, p * LANE:(p + 1) * LANE]
            for half in range(2):
                h = 2 * p + half
                s = jnp.maximum(_dot_nt(xq, ikt[:, half * LANE:(half + 1) * LANE]), 0.0)
                w = iwb_s[h]
                for g in range(ng):
                    acc[g] = acc[g] + w * s[:, grp(g)]
        for g in range(ng):
            a = acc[g]
            if diag:
                a = jnp.where((g * LANE + lane_j) // CHUNK <= row_i // CHUNK, a, -jnp.inf)
            bits = pltpu.bitcast(a, jnp.int32)
            key_s[st, g] = bits ^ ((bits >> 31) & INT_MAX)

    def p1(st, carry):
        scores(st, False)
        return carry

    lax.fori_loop(0, qb, p1, 0)
    scores(qb, True)

    ones_b = jnp.ones((LANE, LANE), _BF16)
    assert key_s.shape[0] * ng <= 256

    T_, CAND_, MIDX_ = 0, 1, 2
    ACC_, CNT_ = 0, 1
    rb = 64
    lane_rb = lax.broadcasted_iota(jnp.int32, (rb, LANE), 1)

    def count(pred_fn):
        sf_s[ACC_] = jnp.zeros((nq, LANE), _F32)

        def body(st, carry):
            for r0 in range(0, nq, rb):
                rs = slice(r0, r0 + rb)
                acc = sf_s[ACC_, rs, :]
                for g in range(ng):
                    acc = acc + jnp.where(
                        pred_fn(key_s[st, g, rs, :], st * st_w + g * LANE + lane_rb, rs), 1.0, 0.0)
                sf_s[ACC_, rs, :] = acc
            return carry

        lax.fori_loop(0, n_st, body, 0)
        return _dot(sf_s[ACC_].astype(_BF16), ones_b)

    c0 = count(lambda k, j, rs: k >= 0)
    ok0 = c0 >= f_topk
    si_s[T_] = jnp.where(ok0, 0, INT_MIN)
    sf_s[CNT_] = jnp.where(ok0, c0, (n_st * st_w).astype(_F32))

    def bit_body(i, carry):
        t_old = si_s[T_]
        cand = t_old + jnp.left_shift(jnp.int32(1), 30 - i)
        si_s[CAND_] = cand
        c = count(lambda k, j, rs: k >= si_s[CAND_, rs, :])
        ok = c >= f_topk
        si_s[T_] = jnp.where(ok, cand, t_old)
        sf_s[CNT_] = jnp.where(ok, c, sf_s[CNT_])
        return carry

    lax.fori_loop(0, 31, bit_body, 0)
    finite_thr = si_s[T_] != KEY_NEG_INF
    si_s[MIDX_] = jnp.where(finite_thr, INT_MAX, 0)

    @pl.when(jnp.max(jnp.where(finite_thr, sf_s[CNT_], 0.0)) > f_topk)
    def _():
        sf_s[CNT_] = f_topk - count(lambda k, j, rs: k > si_s[T_, rs, :])
        n_bits = max(1, int(math.ceil(math.log2(key_s.shape[0] * st_w))))
        si_s[MIDX_] = jnp.zeros((nq, LANE), jnp.int32)

        def idx_body(i, carry):
            mp = si_s[MIDX_]
            cand = mp + jnp.left_shift(jnp.int32(1), n_bits - 1 - i)
            si_s[CAND_] = cand
            c = count(lambda k, j, rs: jnp.where(k == si_s[T_, rs, :], j, INT_MAX) < si_s[CAND_, rs, :])
            si_s[MIDX_] = jnp.where(c < sf_s[CNT_], cand, mp)
            return carry

        lax.fori_loop(0, n_bits, idx_body, 0)
        si_s[MIDX_] = jnp.where(si_s[T_] != KEY_NEG_INF, si_s[MIDX_] + 1, 0)

    def p2(st, carry):
        for r0 in range(0, nq, rb):
            rs = slice(r0, r0 + rb)
            t = si_s[T_, rs, :]
            m_idx = si_s[MIDX_, rs, :]
            for g in range(ng):
                k = key_s[st, g, rs, :]
                jj = jnp.where(k == t, st * st_w + g * LANE + lane_rb, INT_MAX)
                mb = jnp.where(k > t, 0.0, jnp.where(jj < m_idx, 0.0, MASK_NEG))
                key_s[st, g, rs, :] = pltpu.bitcast(mb, jnp.int32)
        return carry

    lax.fori_loop(0, n_st, p2, 0)

    def attend(st, mode):
        off = pl.multiple_of(st * st_w, st_w)
        for h in range(H_B):
            sl = slice(h * hd, (h + 1) * hd)
            mask = jnp.concatenate([pltpu.bitcast(key_s[st, g], _F32) for g in range(ng)], axis=1)
            s = _dot_nt(qs_s[h], k_ref[pl.ds(off, st_w), sl]) + mask
            if mode:
                s_s[...] = s
                b_prev = bias_ref[h, :, 0:LANE]
                if mode == 1:
                    s_s[0:LANE, (ng - 1) * LANE:ng * LANE] += b_prev
                else:
                    b_diag = bias_ref[h, :, LANE:2 * LANE]
                    for rt in range(ng):
                        s_s[grp(rt), grp(rt)] += b_diag
                        if rt:
                            s_s[grp(rt), grp(rt - 1)] += b_prev
                s = s_s[...]
            m_old = m_s[h]
            gmax = s[:, grp(0)]
            for g in range(1, ng):
                gmax = jnp.maximum(gmax, s[:, grp(g)])
            m_new = jnp.maximum(m_old, jnp.max(gmax, axis=1, keepdims=True))
            alpha = jnp.exp2(m_old - m_new)
            pieces = [jnp.exp2(s[:, grp(g)] - m_new) for g in range(ng)]
            psum = pieces[0]
            for g in range(1, ng):
                psum = psum + pieces[g]
            p = jnp.concatenate(pieces, axis=1).astype(_BF16)
            m_s[h] = m_new
            l_s[h] = alpha * l_s[h] + psum
            acc_s[h] = alpha * acc_s[h] + _dot(p, v_ref[pl.ds(off, st_w), sl])

    def p3(st, carry):
        attend(st, 0)
        return carry

    lax.fori_loop(0, qb - 1, p3, 0)

    @pl.when(qb > 0)
    def _():
        attend(qb - 1, 1)

    attend(qb, 2)

    for h in range(H_B):
        sl = slice(h * hd, (h + 1) * hd)
        l_fin = jnp.sum(l_s[h], axis=1, keepdims=True)
        o_ref[:, sl] = (acc_s[h] / l_fin).astype(o_ref.dtype)


def _dsa(pb, ps, bias, b_, t_, col0):
    nq = KEY_TILE
    nb = t_ // nq
    hd = H_B * D_HEAD_B
    topk = min(TOPK_MAX, t_ // 4)
    iq_w = H_IDX * D_IDX
    ik_w = 2 * LANE
    assert col0 % hd == 0 and (col0 + 3 * hd) % iq_w == 0 and (col0 + 3 * hd + iq_w) % ik_w == 0
    cq = col0 // hd
    resident = functools.partial(pl.BlockSpec, pipeline_mode=pl.Buffered(1))
    return pl.pallas_call(
        functools.partial(_dsa_kernel, topk=topk),
        grid=(b_, nb),
        in_specs=[
            pl.BlockSpec((nq, hd), lambda b, i: (b * nb + i, cq)),
            resident((t_, hd), lambda b, i: (b, cq + 1)),
            resident((t_, hd), lambda b, i: (b, cq + 2)),
            pl.BlockSpec((nq, iq_w), lambda b, i: (b * nb + i, (col0 + 3 * hd) // iq_w)),
            resident((t_, ik_w), lambda b, i: (b, (col0 + 3 * hd + iq_w) // ik_w)),
            pl.BlockSpec((nq, LANE), lambda b, i: (b * nb + i, 0)),
            resident((H_B, Q_BLOCK, 2 * LANE), lambda b, i: (0, 0, 0)),
        ],
        out_specs=pl.BlockSpec((nq, hd), lambda b, i: (b * nb + i, 0)),
        out_shape=jax.ShapeDtypeStruct((b_ * t_, hd), _BF16),
        scratch_shapes=[
            pltpu.VMEM((nb, KEY_TILE // LANE, nq, LANE), jnp.int32),
            pltpu.VMEM((H_B, nq, D_HEAD_B), _BF16),
            pltpu.VMEM((H_B, nq, LANE), _F32),
            pltpu.VMEM((H_B, nq, LANE), _F32),
            pltpu.VMEM((H_B, nq, LANE), _F32),
            pltpu.VMEM((H_B, nq, D_HEAD_B), _F32),
            pltpu.VMEM((nq, KEY_TILE), _F32),
            pltpu.VMEM((3, nq, LANE), jnp.int32),
            pltpu.VMEM((2, nq, LANE), _F32),
        ],
        compiler_params=pltpu.CompilerParams(
            dimension_semantics=("arbitrary", "arbitrary"), vmem_limit_bytes=VMEM_LIMIT),
        name="dsa",
    )(pb, pb, pb, pb, pb, ps, bias)


def _merge_kernel(oa_ref, ob_ref, g_ref, x_ref, bg_ref, wa_ref, wb_ref, wo_ref, o_ref):
    d = x_ref.shape[1]
    a = _dot(oa_ref[...], wa_ref[...])
    b = _dot(ob_ref[...], wb_ref[...])
    g = _sigmoid(g_ref[...].astype(_F32) + bg_ref[...])
    merged = g[:, 0:d] * a + g[:, d:2 * d] * b
    o_ref[...] = x_ref[...] + _dot(merged.astype(_BF16), wo_ref[...])


def _merge(oa, ob, pb, x2d, bg, wa, wb, wo, tn):
    n, d = x2d.shape
    row = lambda w: pl.BlockSpec((tn, w), lambda i: (i, 0))
    const = lambda a: pl.BlockSpec(a.shape, lambda i: (0, 0))
    return pl.pallas_call(
        _merge_kernel,
        grid=(n // tn,),
        in_specs=[row(oa.shape[1]), row(ob.shape[1]), row(2 * d), row(d),
                  const(bg), const(wa), const(wb), const(wo)],
        out_specs=row(d),
        out_shape=jax.ShapeDtypeStruct((n, d), _F32),
        compiler_params=pltpu.CompilerParams(
            dimension_semantics=("arbitrary",), vmem_limit_bytes=VMEM_LIMIT),
        name="merge",
    )(oa, ob, pb, x2d, bg, wa, wb, wo)


def _mlp_kernel(x_ref, n2_ref, w1_ref, w2_ref, nf_ref, o_ref, *, final_norm):
    x = x_ref[...]
    h2 = _rms(x, n2_ref[...]).astype(_BF16)
    hid = jnp.maximum(_dot(h2, w1_ref[...]), 0.0)
    y = x + _dot((hid * hid).astype(_BF16), w2_ref[...])
    o_ref[...] = _rms(y, nf_ref[...]) if final_norm else y


def _mlp(x1, n2, w1, w2, nf, tn, final_norm):
    n, d = x1.shape
    row = pl.BlockSpec((tn, d), lambda i: (i, 0))
    const = lambda a: pl.BlockSpec(a.shape, lambda i: (0, 0))
    return pl.pallas_call(
        functools.partial(_mlp_kernel, final_norm=final_norm),
        grid=(n // tn,),
        in_specs=[row, const(n2), const(w1), const(w2), const(nf)],
        out_specs=row,
        out_shape=jax.ShapeDtypeStruct((n, d), _F32),
        compiler_params=pltpu.CompilerParams(
            dimension_semantics=("arbitrary",), vmem_limit_bytes=VMEM_LIMIT),
        name="mlp",
    )(x1, n2, w1, w2, nf)


def _t5_bucket(rel):
    half = N_BUCKETS // 2
    max_exact = half // 2
    base = jnp.where(rel > 0, half, 0)
    n = jnp.abs(rel)
    n_f = jnp.maximum(n, 1).astype(jnp.float32)
    large = max_exact + (jnp.log(n_f / max_exact) / math.log(MAX_DISTANCE / max_exact)
                         * (half - max_exact)).astype(jnp.int32)
    large = jnp.minimum(large, half - 1)
    return base + jnp.where(n < max_exact, n, large)


def _pick_tile(n, prefs):
    for t in prefs:
        if n % t == 0:
            return t
    raise ValueError(f"no tile in {prefs} divides {n}")


def kernel(x, norm1_w, w_in, conv_a_w, a_log, dt_bias, norm_a_w, rel_bias_table, w_gate, b_gate,
           w_proj_a, w_proj_b, w_out, norm2_w, w_ff1, w_ff2, norm_final_w):
    b_, t_, d = x.shape
    depth = norm1_w.shape[0]
    n = b_ * t_
    ha, hb = H_A * DK_A, H_B * D_HEAD_B
    assert t_ % KEY_TILE == 0 and t_ % GDN_CHUNK == 0 and d % LANE == 0
    assert DK_A == DV_A == D_HEAD_B == LANE and 2 * D_IDX == LANE
    assert Q_BLOCK >= MAX_DISTANCE

    o_za = 3 * ha
    o_ba = 4 * ha
    o_aa = o_ba + H_A
    o_qb = o_aa + H_A
    o_iq = o_qb + 3 * hb
    o_ik = o_iq + H_IDX * D_IDX
    o_iw = o_ik + D_IDX

    rel = (jnp.arange(2 * Q_BLOCK, dtype=jnp.int32)[None, :] - Q_BLOCK) \
        - jnp.arange(Q_BLOCK, dtype=jnp.int32)[:, None]
    far = jnp.full((Q_BLOCK, LANE), -(Q_BLOCK + 1), jnp.int32)
    bucket = _t5_bucket(jnp.concatenate([rel, far], axis=1))
    bias = _bias_tiles(rel_bias_table.astype(_F32), bucket)

    x2d = x.reshape(n, d)
    tn = _pick_tile(n, (512, 256, 128))
    for layer in range(depth):
        wi = w_in[layer]
        zpad = lambda w: jnp.zeros((d, w), wi.dtype)
        ws = jnp.concatenate([wi[:, o_ba:o_qb], wi[:, o_iw:o_iw + H_IDX], zpad(LANE - 3 * H_A)],
                             axis=1).astype(_BF16)
        ik = wi[:, o_ik:o_iw]
        wb16 = jnp.concatenate([w_gate[layer], wi[:, 0:o_ba], wi[:, o_qb:o_ik],
                                ik, zpad(D_IDX), zpad(D_IDX), ik], axis=1).astype(_BF16)
        nw = norm1_w[layer].reshape(1, d).astype(_F32)
        pb, ps = _norm_proj(x2d, nw, wb16, ws, tn)

        col_a = 2 * d
        col_b = col_a + 4 * ha
        assert col_a % ha == 0
        lanes = lambda v, off: jnp.zeros((1, LANE), _F32).at[0, off:off + H_A].set(v.astype(_F32))
        oa = _gdn(pb, ps, conv_a_w[layer].astype(_F32), lanes(a_log[layer], H_A), lanes(dt_bias[layer], H_A),
                  norm_a_w[layer].reshape(1, DV_A).astype(_F32), b_, t_, col_a)
        ob = _dsa(pb, ps, bias, b_, t_, col_b)

        x2d = _merge(oa, ob, pb, x2d, b_gate[layer].reshape(1, 2 * d).astype(_F32),
                     w_proj_a[layer].astype(_BF16), w_proj_b[layer].astype(_BF16),
                     w_out[layer].astype(_BF16), tn)
        x2d = _mlp(x2d, norm2_w[layer].reshape(1, d).astype(_F32), w_ff1[layer].astype(_BF16),
                   w_ff2[layer].astype(_BF16), norm_final_w.reshape(1, d).astype(_F32),
                   _pick_tile(n, (256, 128)), final_norm=layer == depth - 1)
    return x2d.reshape(b_, t_, d)
```

```python
import functools
import math

import jax
import jax.numpy as jnp
import numpy as np
from jax import lax
from jax.experimental import pallas as pl
from jax.experimental.pallas import tpu as pltpu

EPS = 1e-6
H_A = 8
DK_A = 128
DV_A = 128
CONV_K = 4
GDN_CHUNK = 128
H_B = 8
D_HEAD_B = 128
H_IDX = 8
D_IDX = 64
TOPK_MAX = 256
Q_BLOCK = 128
CHUNK = 64
KEY_TILE = 512
N_BUCKETS = 32
MAX_DISTANCE = 128

LANE = 128
VMEM_LIMIT = 56 * 1024 * 1024

INT_MIN = -(2**31)
INT_MAX = 2**31 - 1
KEY_NEG_INF = -2139095041
MASK_NEG = -1e30
LOG2E = math.log2(math.e)

_F32 = jnp.float32
_BF16 = jnp.bfloat16


def _dot(a, b):
    return jnp.dot(a, b, preferred_element_type=_F32)


def _dot_nt(a, b):
    return lax.dot_general(a, b, (((1,), (1,)), ((), ())), preferred_element_type=_F32)


def _sigmoid(x):
    return 1.0 / (1.0 + jnp.exp(-x))


def _rms(x, w):
    return x * lax.rsqrt(jnp.mean(x * x, axis=-1, keepdims=True) + EPS) * w


def _norm_proj_kernel(x_ref, nw_ref, w_ref, ws_ref, o_ref, os_ref, *, chunk):
    h = _rms(x_ref[...], nw_ref[...]).astype(_BF16)
    os_ref[...] = _dot(h, ws_ref[...])
    for c0 in range(0, w_ref.shape[1], chunk):
        o_ref[:, c0:c0 + chunk] = _dot(h, w_ref[:, c0:c0 + chunk]).astype(o_ref.dtype)


def _norm_proj(x2d, nw, w, ws, tn):
    n, d = x2d.shape
    c = w.shape[1]
    const = functools.partial(pl.BlockSpec, pipeline_mode=pl.Buffered(1))
    return pl.pallas_call(
        functools.partial(_norm_proj_kernel, chunk=_pick_tile(c, (768, 512, 256, 128))),
        grid=(n // tn,),
        in_specs=[
            pl.BlockSpec((tn, d), lambda i: (i, 0)),
            const((1, d), lambda i: (0, 0)),
            const((d, c), lambda i: (0, 0)),
            const((d, LANE), lambda i: (0, 0)),
        ],
        out_specs=[pl.BlockSpec((tn, c), lambda i: (i, 0)), pl.BlockSpec((tn, LANE), lambda i: (i, 0))],
        out_shape=[jax.ShapeDtypeStruct((n, c), _BF16), jax.ShapeDtypeStruct((n, LANE), _F32)],
        compiler_params=pltpu.CompilerParams(
            dimension_semantics=("arbitrary",), vmem_limit_bytes=VMEM_LIMIT),
        name="proj",
    )(x2d, nw, w, ws)


def _gdn_kernel(q_ref, k_ref, v_ref, z_ref, sm_ref, cw_ref, arow_ref, dtrow_ref, naw_ref, o_ref,
                xbuf, s_ref, kn_s, kb_s, qn_s, rhs_s, dec_s, qdec_s, ktt_s, eg_s, m_s, x_s):
    tt = q_ref.shape[0]
    hd = H_A * DK_A

    @pl.when(pl.program_id(1) == 0)
    def _():
        xbuf[:, 0:8, :] = jnp.zeros((3, 8, hd), _F32)
        s_ref[...] = jnp.zeros_like(s_ref)

    for idx, ref in enumerate((q_ref, k_ref, v_ref)):
        xbuf[idx, 8:8 + tt, :] = ref[...].astype(_F32)

    sm = sm_ref[...]
    beta_full = _sigmoid(sm)
    xg = sm + dtrow_ref[...]
    softplus = jnp.maximum(xg, 0.0) + jnp.log(1.0 + jnp.exp(-jnp.abs(xg)))
    g_full = -jnp.exp(arow_ref[...]) * softplus
    row = lax.broadcasted_iota(jnp.int32, (tt, tt), 0)
    col = lax.broadcasted_iota(jnp.int32, (tt, tt), 1)
    tri = (col <= row).astype(_F32)
    gcum = jnp.dot(tri, g_full, preferred_element_type=_F32, precision=lax.Precision.HIGHEST)
    gcum_t = gcum.T
    strict = col < row
    eye = (col == row).astype(_F32)

    for h in range(H_A):
        sl = slice(h * DK_A, (h + 1) * DK_A)
        conv = []
        for idx in range(3):
            acc = None
            for j in range(CONV_K):
                term = cw_ref[j:j + 1, idx * hd + h * DK_A: idx * hd + (h + 1) * DK_A] * \
                    xbuf[idx, 8 - (CONV_K - 1) + j: 8 - (CONV_K - 1) + j + tt, sl]
                acc = term if acc is None else acc + term
            conv.append(acc * _sigmoid(acc))
        qh, kh, vh = conv
        qn = qh * lax.rsqrt(jnp.sum(qh * qh, axis=-1, keepdims=True) + EPS) * (DK_A ** -0.5)
        kn = kh * lax.rsqrt(jnp.sum(kh * kh, axis=-1, keepdims=True) + EPS)
        bcol = beta_full[:, h:h + 1]
        gcol = gcum[:, H_A + h:H_A + h + 1]
        grow = gcum_t[H_A + h:H_A + h + 1, :]
        glast = gcum[tt - 1:tt, H_A + h:H_A + h + 1]
        kb = kn * bcol
        eg = jnp.exp(gcol)
        rhs_s[h, :, 0:DV_A] = vh * bcol
        rhs_s[h, :, DV_A:DV_A + DK_A] = kb * eg
        dec_s[h] = jnp.where(strict, jnp.exp(jnp.where(strict, gcol - grow, 0.0)), 0.0)
        qdec_s[h] = (qn * eg).astype(_BF16)
        ktt_s[h] = (kn * jnp.exp(glast - gcol)).T.astype(_BF16)
        kn_s[h] = kn.astype(_BF16)
        kb_s[h] = kb.astype(_BF16)
        qn_s[h] = qn.astype(_BF16)
        eg_s[h] = jnp.broadcast_to(jnp.exp(glast), (DK_A, DV_A))

    for idx in range(3):
        xbuf[idx, 0:8, :] = xbuf[idx, tt:tt + 8, :]

    n_sq = int(math.log2(tt))

    heads = range(H_A)
    for h in heads:
        m = -(_dot_nt(kb_s[h], kn_s[h]) * dec_s[h])
        m_s[h] = m
        x_s[h] = m
    for _ in range(n_sq - 1):
        for h in heads:
            mb = m_s[h].astype(_BF16)
            m = _dot(mb, mb)
            m_s[h] = m
            x_s[h] = x_s[h] + m + _dot(x_s[h].astype(_BF16), m.astype(_BF16))
    for h in heads:
        rhs = rhs_s[h]
        rhs_s[h] = rhs + _dot(x_s[h].astype(_BF16), rhs.astype(_BF16))
    for h in heads:
        m_s[h] = _dot_nt(qn_s[h], kn_s[h]) * (dec_s[h] + eye)
    for h in heads:
        sb = s_ref[h].astype(_BF16)
        vnew = rhs_s[h, :, 0:DV_A] - _dot(rhs_s[h, :, DV_A:DV_A + DK_A].astype(_BF16), sb)
        vb = vnew.astype(_BF16)
        o = _dot(qdec_s[h], sb) + _dot(m_s[h].astype(_BF16), vb)
        s_ref[h] = s_ref[h] * eg_s[h] + _dot(ktt_s[h], vb)
        rhs_s[h, :, 0:DV_A] = o

    naw = naw_ref[...]
    for h in range(H_A):
        sl = slice(h * DV_A, (h + 1) * DV_A)
        o = rhs_s[h, :, 0:DV_A]
        z = z_ref[:, sl].astype(_F32)
        on = o * lax.rsqrt(jnp.mean(o * o, axis=-1, keepdims=True) + EPS) * naw
        o_ref[:, sl] = (on * (z * _sigmoid(z))).astype(o_ref.dtype)


def _gdn(pb, ps, cw, arow, dtrow, naw, b_, t_, col0):
    tt = GDN_CHUNK
    hd = H_A * DK_A
    nt = t_ // tt
    cb = col0 // hd

    def rowblk(j):
        return pl.BlockSpec((tt, hd), lambda b, t: (b * nt + t, cb + j))

    const = lambda shape: pl.BlockSpec(shape, lambda b, t: (0, 0))
    return pl.pallas_call(
        _gdn_kernel,
        grid=(b_, nt),
        in_specs=[rowblk(0), rowblk(1), rowblk(2), rowblk(3),
                  pl.BlockSpec((tt, LANE), lambda b, t: (b * nt + t, 0)),
                  const((CONV_K, 3 * hd)), const((1, LANE)), const((1, LANE)), const((1, DV_A))],
        out_specs=pl.BlockSpec((tt, hd), lambda b, t: (b * nt + t, 0)),
        out_shape=jax.ShapeDtypeStruct((b_ * t_, hd), _BF16),
        scratch_shapes=[
            pltpu.VMEM((3, tt + 8, hd), _F32),
            pltpu.VMEM((H_A, DK_A, DV_A), _F32),
            pltpu.VMEM((H_A, tt, DK_A), _BF16),
            pltpu.VMEM((H_A, tt, DK_A), _BF16),
            pltpu.VMEM((H_A, tt, DK_A), _BF16),
            pltpu.VMEM((H_A, tt, DV_A + DK_A), _F32),
            pltpu.VMEM((H_A, tt, tt), _F32),
            pltpu.VMEM((H_A, tt, DK_A), _BF16),
            pltpu.VMEM((H_A, DK_A, tt), _BF16),
            pltpu.VMEM((H_A, DK_A, DV_A), _F32),
            pltpu.VMEM((H_A, tt, tt), _F32),
            pltpu.VMEM((H_A, tt, tt), _F32),
        ],
        compiler_params=pltpu.CompilerParams(
            dimension_semantics=("arbitrary", "arbitrary"), vmem_limit_bytes=VMEM_LIMIT),
        name="gdn",
    )(pb, pb, pb, pb, ps, cw, arow, dtrow, naw)


def _bias_kernel(tab_ref, bucket_ref, o_ref):
    bucket = bucket_ref[...]
    nq, w = bucket.shape
    for h in range(H_B):
        acc = jnp.zeros((nq, w), _F32)
        for b in range(N_BUCKETS):
            acc = acc + jnp.where(bucket == b, tab_ref[b, h], 0.0)
        far = acc[:, 2 * LANE:3 * LANE]
        o_ref[h] = (acc[:, 0:2 * LANE] - jnp.concatenate([far, far], axis=1)) * LOG2E


def _bias_tiles(rel_table, bucket):
    return pl.pallas_call(
        _bias_kernel,
        in_specs=[pl.BlockSpec(memory_space=pltpu.SMEM),
                  pl.BlockSpec(bucket.shape, lambda: (0, 0))],
        out_specs=pl.BlockSpec((H_B, Q_BLOCK, 2 * LANE), lambda: (0, 0, 0)),
        out_shape=jax.ShapeDtypeStruct((H_B, Q_BLOCK, 2 * LANE), _F32),
        name="rel_bias_tiles",
    )(rel_table, bucket)


def _dsa_kernel(q_ref, k_ref, v_ref, iq_ref, ik_ref, sm_ref, bias_ref, o_ref,
                khi_s, klo_s, mask_s, qs_s, iwb_s, m_s, l_s, acc_s, s_s, si_s, sf_s, s16_s, acc16_s, *, topk):
    nq = q_ref.shape[0]
    st_w = KEY_TILE
    assert nq == st_w
    ng = st_w // LANE
    qb = pl.program_id(1)
    n_st = qb + 1
    hd = D_HEAD_B
    scale = D_HEAD_B ** -0.5 * LOG2E
    f_topk = float(topk)

    lane_j = lax.broadcasted_iota(jnp.int32, (nq, LANE), 1)
    row_i = lax.broadcasted_iota(jnp.int32, (nq, LANE), 0)

    def grp(g):
        return slice(g * LANE, (g + 1) * LANE)

    iw = sm_ref[...]
    for h in range(H_B):
        sl = slice(h * hd, (h + 1) * hd)
        qs_s[h] = (q_ref[:, sl].astype(_F32) * scale).astype(_BF16)
        iwb_s[h] = jnp.broadcast_to(iw[:, 2 * H_A + h:2 * H_A + h + 1], (nq, LANE))
        m_s[h] = jnp.full((nq, LANE), MASK_NEG, _F32)
        l_s[h] = jnp.zeros((nq, LANE), _F32)
        acc_s[h] = jnp.zeros((nq, hd), _F32)

    def scores(st, diag):
        ikt = ik_ref[pl.ds(pl.multiple_of(st * st_w, st_w), st_w), :]
        acc = [jnp.zeros((nq, LANE), _F32) for _ in range(ng)]
        for p in range(H_IDX // 2):
            xq = iq_ref[:, p * LANE:(p + 1) * LANE]
            for half in range(2):
                h = 2 * p + half
                s = jnp.maximum(_dot_nt(xq, ikt[:, half * LANE:(half + 1) * LANE]), 0.0)
                w = iwb_s[h]
                for g in range(ng):
                    acc[g] = acc[g] + w * s[:, grp(g)]
        for g in range(ng):
            a = acc[g]
            if diag:
                a = jnp.where((g * LANE + lane_j) // CHUNK <= row_i // CHUNK, a, -jnp.inf)
            bits = pltpu.bitcast(a, jnp.int32)
            key = bits ^ ((bits >> 31) & INT_MAX)
            khi_s[st, g] = (key >> 16).astype(jnp.int16)
            klo_s[st, g] = ((key & 0xFFFF) - 0x8000).astype(jnp.int16)

    def p1(st, carry):
        scores(st, False)
        return carry

    lax.fori_loop(0, qb, p1, 0)
    scores(qb, True)

    ones_b = jnp.ones((LANE, LANE), _BF16)
    n_tiles_max = khi_s.shape[0]
    assert n_tiles_max * ng <= 256
    assert n_tiles_max * st_w < 2 ** 15
    i16 = jnp.int16
    one16, zero16, min16, max16 = i16(1), i16(0), i16(-2 ** 15), i16(2 ** 15 - 1)
    CAND_, THI_, TLO_, MIDX_ = 0, 1, 2, 3
    T_, THI32_ = 0, 1
    NEED_, CNT_ = 0, 1
    rb = 128
    blocks = [slice(r0, r0 + rb) for r0 in range(0, nq, rb)]
    lane_rb = lax.broadcasted_iota(jnp.int32, (rb, LANE), 1)

    def pos16(st, g):
        return (st * st_w + g * LANE + lane_rb).astype(i16)

    def count(plane, pred_fn):
        acc16_s[...] = jnp.zeros((nq, LANE), i16)

        def body(st, carry):
            for rs in blocks:
                acc = acc16_s[rs, :]
                for g in range(ng):
                    acc = acc + jnp.where(pred_fn(plane[st, g, rs, :], rs, st, g), one16, zero16)
                acc16_s[rs, :] = acc
            return carry

        lax.fori_loop(0, n_st, body, 0)
        return _dot(acc16_s[...].astype(jnp.int32).astype(_F32).astype(_BF16), ones_b)

    def digit_search(plane):
        c0 = count(plane, lambda k, rs, st, g: k >= zero16)
        si_s[T_] = jnp.where(c0 >= sf_s[NEED_], 0, -2 ** 15)

        def bit_body(i, carry):
            t_old = si_s[T_]
            cand = t_old + jnp.left_shift(jnp.int32(1), 14 - i)
            s16_s[CAND_] = cand.astype(i16)
            c = count(plane, lambda k, rs, st, g: k >= s16_s[CAND_, rs, :])
            si_s[T_] = jnp.where(c >= sf_s[NEED_], cand, t_old)
            return carry

        lax.fori_loop(0, 15, bit_body, 0)

    sf_s[NEED_] = jnp.full((nq, LANE), f_topk, _F32)
    digit_search(khi_s)
    si_s[THI32_] = si_s[T_]
    s16_s[THI_] = si_s[T_].astype(i16)
    sf_s[NEED_] = f_topk - count(khi_s, lambda k, rs, st, g: k > s16_s[THI_, rs, :])

    def restrict_lo(st, carry):
        for rs in blocks:
            thi = s16_s[THI_, rs, :]
            for g in range(ng):
                klo_s[st, g, rs, :] = jnp.where(khi_s[st, g, rs, :] == thi, klo_s[st, g, rs, :], min16)
        return carry

    lax.fori_loop(0, n_st, restrict_lo, 0)
    digit_search(klo_s)
    s16_s[TLO_] = si_s[T_].astype(i16)
    sf_s[CNT_] = count(klo_s, lambda k, rs, st, g: k >= s16_s[TLO_, rs, :])
    finite_thr = si_s[THI32_] * 2 ** 16 + si_s[T_] + 2 ** 15 != KEY_NEG_INF
    s16_s[MIDX_] = jnp.where(finite_thr, 2 ** 15 - 1, 0).astype(i16)

    @pl.when(jnp.max(jnp.where(finite_thr, sf_s[CNT_] - sf_s[NEED_], 0.0)) > 0.0)
    def _():
        sf_s[NEED_] = sf_s[NEED_] - count(klo_s, lambda k, rs, st, g: k > s16_s[TLO_, rs, :])
        n_bits = max(1, int(math.ceil(math.log2(n_tiles_max * st_w))))
        si_s[T_] = jnp.zeros((nq, LANE), jnp.int32)

        def is_tie_below(k, rs, st, g):
            tie = jnp.where(khi_s[st, g, rs, :] == s16_s[THI_, rs, :],
                            jnp.where(k == s16_s[TLO_, rs, :], pos16(st, g), max16), max16)
            return tie < s16_s[CAND_, rs, :]

        def idx_body(i, carry):
            mp = si_s[T_]
            cand = mp + jnp.left_shift(jnp.int32(1), n_bits - 1 - i)
            s16_s[CAND_] = cand.astype(i16)
            c = count(klo_s, is_tie_below)
            si_s[T_] = jnp.where(c < sf_s[NEED_], cand, mp)
            return carry

        lax.fori_loop(0, n_bits, idx_body, 0)
        s16_s[MIDX_] = jnp.where(finite_thr, si_s[T_] + 1, 0).astype(i16)

    def p2(st, carry):
        keep, drop = jnp.zeros((), _BF16), jnp.full((), MASK_NEG, _BF16)
        for rs in blocks:
            thi, tlo, m_idx = s16_s[THI_, rs, :], s16_s[TLO_, rs, :], s16_s[MIDX_, rs, :]
            for g in range(ng):
                hi, lo = khi_s[st, g, rs, :], klo_s[st, g, rs, :]
                at_lo = jnp.where(lo > tlo, one16,
                                  jnp.where(lo == tlo, jnp.where(pos16(st, g) < m_idx, one16, zero16), zero16))
                sel = jnp.where(hi > thi, one16, jnp.where(hi == thi, at_lo, zero16))
                mask_s[st, g, rs, :] = jnp.where(sel == one16, keep, drop)
        return carry

    lax.fori_loop(0, n_st, p2, 0)

    def attend(st, mode):
        off = pl.multiple_of(st * st_w, st_w)
        for h in range(H_B):
            sl = slice(h * hd, (h + 1) * hd)
            mask = jnp.concatenate([mask_s[st, g].astype(_F32) for g in range(ng)], axis=1)
            s = _dot_nt(qs_s[h], k_ref[pl.ds(off, st_w), sl]) + mask
            if mode:
                s_s[...] = s
                b_prev = bias_ref[h, :, 0:LANE]
                if mode == 1:
                    s_s[0:LANE, (ng - 1) * LANE:ng * LANE] += b_prev
                else:
                    b_diag = bias_ref[h, :, LANE:2 * LANE]
                    for rt in range(ng):
                        s_s[grp(rt), grp(rt)] += b_diag
                        if rt:
                            s_s[grp(rt), grp(rt - 1)] += b_prev
                s = s_s[...]
            m_old = m_s[h]
            gmax = s[:, grp(0)]
            for g in range(1, ng):
                gmax = jnp.maximum(gmax, s[:, grp(g)])
            m_new = jnp.maximum(m_old, jnp.max(gmax, axis=1, keepdims=True))
            alpha = jnp.exp2(m_old - m_new)
            pieces = [jnp.exp2(s[:, grp(g)] - m_new) for g in range(ng)]
            psum = pieces[0]
            for g in range(1, ng):
                psum = psum + pieces[g]
            p = jnp.concatenate(pieces, axis=1).astype(_BF16)
            m_s[h] = m_new
            l_s[h] = alpha * l_s[h] + psum
            acc_s[h] = alpha * acc_s[h] + _dot(p, v_ref[pl.ds(off, st_w), sl])

    def p3(st, carry):
        attend(st, 0)
        return carry

    lax.fori_loop(0, qb - 1, p3, 0)

    @pl.when(qb > 0)
    def _():
        attend(qb - 1, 1)

    attend(qb, 2)

    for h in range(H_B):
        sl = slice(h * hd, (h + 1) * hd)
        l_fin = jnp.sum(l_s[h], axis=1, keepdims=True)
        o_ref[:, sl] = (acc_s[h] / l_fin).astype(o_ref.dtype)


def _dsa(pb, ps, bias, b_, t_, col0):
    nq = KEY_TILE
    nb = t_ // nq
    hd = H_B * D_HEAD_B
    topk = min(TOPK_MAX, t_ // 4)
    iq_w = H_IDX * D_IDX
    ik_w = 2 * LANE
    assert col0 % hd == 0 and (col0 + 3 * hd) % iq_w == 0 and (col0 + 3 * hd + iq_w) % ik_w == 0
    cq = col0 // hd
    resident = functools.partial(pl.BlockSpec, pipeline_mode=pl.Buffered(1))
    return pl.pallas_call(
        functools.partial(_dsa_kernel, topk=topk),
        grid=(b_, nb),
        in_specs=[
            pl.BlockSpec((nq, hd), lambda b, i: (b * nb + i, cq)),
            resident((t_, hd), lambda b, i: (b, cq + 1)),
            resident((t_, hd), lambda b, i: (b, cq + 2)),
            pl.BlockSpec((nq, iq_w), lambda b, i: (b * nb + i, (col0 + 3 * hd) // iq_w)),
            resident((t_, ik_w), lambda b, i: (b, (col0 + 3 * hd + iq_w) // ik_w)),
            pl.BlockSpec((nq, LANE), lambda b, i: (b * nb + i, 0)),
            resident((H_B, Q_BLOCK, 2 * LANE), lambda b, i: (0, 0, 0)),
        ],
        out_specs=pl.BlockSpec((nq, hd), lambda b, i: (b * nb + i, 0)),
        out_shape=jax.ShapeDtypeStruct((b_ * t_, hd), _BF16),
        scratch_shapes=[
            pltpu.VMEM((nb, KEY_TILE // LANE, nq, LANE), jnp.int16),
            pltpu.VMEM((nb, KEY_TILE // LANE, nq, LANE), jnp.int16),
            pltpu.VMEM((nb, KEY_TILE // LANE, nq, LANE), _BF16),
            pltpu.VMEM((H_B, nq, D_HEAD_B), _BF16),
            pltpu.VMEM((H_B, nq, LANE), _F32),
            pltpu.VMEM((H_B, nq, LANE), _F32),
            pltpu.VMEM((H_B, nq, LANE), _F32),
            pltpu.VMEM((H_B, nq, D_HEAD_B), _F32),
            pltpu.VMEM((nq, KEY_TILE), _F32),
            pltpu.VMEM((2, nq, LANE), jnp.int32),
            pltpu.VMEM((2, nq, LANE), _F32),
            pltpu.VMEM((4, nq, LANE), jnp.int16),
            pltpu.VMEM((nq, LANE), jnp.int16),
        ],
        compiler_params=pltpu.CompilerParams(
            dimension_semantics=("arbitrary", "arbitrary"), vmem_limit_bytes=VMEM_LIMIT),
        name="dsa",
    )(pb, pb, pb, pb, pb, ps, bias)


def _merge_kernel(oa_ref, ob_ref, g_ref, x_ref, bg_ref, wa_ref, wb_ref, wo_ref, o_ref):
    d = x_ref.shape[1]
    a = _dot(oa_ref[...], wa_ref[...])
    b = _dot(ob_ref[...], wb_ref[...])
    g = _sigmoid(g_ref[...].astype(_F32) + bg_ref[...])
    merged = g[:, 0:d] * a + g[:, d:2 * d] * b
    o_ref[...] = x_ref[...] + _dot(merged.astype(_BF16), wo_ref[...])


def _merge(oa, ob, pb, x2d, bg, wa, wb, wo, tn):
    n, d = x2d.shape
    row = lambda w: pl.BlockSpec((tn, w), lambda i: (i, 0))
    const = lambda a: pl.BlockSpec(a.shape, lambda i: (0, 0))
    return pl.pallas_call(
        _merge_kernel,
        grid=(n // tn,),
        in_specs=[row(oa.shape[1]), row(ob.shape[1]), row(2 * d), row(d),
                  const(bg), const(wa), const(wb), const(wo)],
        out_specs=row(d),
        out_shape=jax.ShapeDtypeStruct((n, d), _F32),
        compiler_params=pltpu.CompilerParams(
            dimension_semantics=("arbitrary",), vmem_limit_bytes=VMEM_LIMIT),
        name="merge",
    )(oa, ob, pb, x2d, bg, wa, wb, wo)


def _mlp_kernel(x_ref, n2_ref, w1_ref, w2_ref, nf_ref, o_ref, *, final_norm):
    x = x_ref[...]
    h2 = _rms(x, n2_ref[...]).astype(_BF16)
    hid = jnp.maximum(_dot(h2, w1_ref[...]), 0.0)
    y = x + _dot((hid * hid).astype(_BF16), w2_ref[...])
    o_ref[...] = _rms(y, nf_ref[...]) if final_norm else y


def _mlp(x1, n2, w1, w2, nf, tn, final_norm):
    n, d = x1.shape
    row = pl.BlockSpec((tn, d), lambda i: (i, 0))
    const = lambda a: pl.BlockSpec(a.shape, lambda i: (0, 0))
    return pl.pallas_call(
        functools.partial(_mlp_kernel, final_norm=final_norm),
        grid=(n // tn,),
        in_specs=[row, const(n2), const(w1), const(w2), const(nf)],
        out_specs=row,
        out_shape=jax.ShapeDtypeStruct((n, d), _F32),
        compiler_params=pltpu.CompilerParams(
            dimension_semantics=("arbitrary",), vmem_limit_bytes=VMEM_LIMIT),
        name="mlp",
    )(x1, n2, w1, w2, nf)


def _t5_bucket(rel):
    half = N_BUCKETS // 2
    max_exact = half // 2
    base = jnp.where(rel > 0, half, 0)
    n = jnp.abs(rel)
    n_f = jnp.maximum(n, 1).astype(jnp.float32)
    large = max_exact + (jnp.log(n_f / max_exact) / math.log(MAX_DISTANCE / max_exact)
                         * (half - max_exact)).astype(jnp.int32)
    large = jnp.minimum(large, half - 1)
    return base + jnp.where(n < max_exact, n, large)


def _pick_tile(n, prefs):
    for t in prefs:
        if n % t == 0:
            return t
    raise ValueError(f"no tile in {prefs} divides {n}")


def kernel(x, norm1_w, w_in, conv_a_w, a_log, dt_bias, norm_a_w, rel_bias_table, w_gate, b_gate,
           w_proj_a, w_proj_b, w_out, norm2_w, w_ff1, w_ff2, norm_final_w):
    b_, t_, d = x.shape
    depth = norm1_w.shape[0]
    n = b_ * t_
    ha, hb = H_A * DK_A, H_B * D_HEAD_B
    assert t_ % KEY_TILE == 0 and t_ % GDN_CHUNK == 0 and d % LANE == 0
    assert DK_A == DV_A == D_HEAD_B == LANE and 2 * D_IDX == LANE
    assert Q_BLOCK >= MAX_DISTANCE

    o_za = 3 * ha
    o_ba = 4 * ha
    o_aa = o_ba + H_A
    o_qb = o_aa + H_A
    o_iq = o_qb + 3 * hb
    o_ik = o_iq + H_IDX * D_IDX
    o_iw = o_ik + D_IDX

    rel = (jnp.arange(2 * Q_BLOCK, dtype=jnp.int32)[None, :] - Q_BLOCK) \
        - jnp.arange(Q_BLOCK, dtype=jnp.int32)[:, None]
    far = jnp.full((Q_BLOCK, LANE), -(Q_BLOCK + 1), jnp.int32)
    bucket = _t5_bucket(jnp.concatenate([rel, far], axis=1))
    bias = _bias_tiles(rel_bias_table.astype(_F32), bucket)

    x2d = x.reshape(n, d)
    tn = _pick_tile(n, (512, 256, 128))
    for layer in range(depth):
        wi = w_in[layer]
        zpad = lambda w: jnp.zeros((d, w), wi.dtype)
        ws = jnp.concatenate([wi[:, o_ba:o_qb], wi[:, o_iw:o_iw + H_IDX], zpad(LANE - 3 * H_A)],
                             axis=1).astype(_BF16)
        ik = wi[:, o_ik:o_iw]
        wb16 = jnp.concatenate([w_gate[layer], wi[:, 0:o_ba], wi[:, o_qb:o_ik],
                                ik, zpad(D_IDX), zpad(D_IDX), ik], axis=1).astype(_BF16)
        nw = norm1_w[layer].reshape(1, d).astype(_F32)
        pb, ps = _norm_proj(x2d, nw, wb16, ws, tn)

        col_a = 2 * d
        col_b = col_a + 4 * ha
        assert col_a % ha == 0
        lanes = lambda v, off: jnp.zeros((1, LANE), _F32).at[0, off:off + H_A].set(v.astype(_F32))
        oa = _gdn(pb, ps, conv_a_w[layer].astype(_F32), lanes(a_log[layer], H_A), lanes(dt_bias[layer], H_A),
                  norm_a_w[layer].reshape(1, DV_A).astype(_F32), b_, t_, col_a)
        ob = _dsa(pb, ps, bias, b_, t_, col_b)

        x2d = _merge(oa, ob, pb, x2d, b_gate[layer].reshape(1, 2 * d).astype(_F32),
                     w_proj_a[layer].astype(_BF16), w_proj_b[layer].astype(_BF16),
                     w_out[layer].astype(_BF16), tn)
        x2d = _mlp(x2d, norm2_w[layer].reshape(1, d).astype(_F32), w_ff1[layer].astype(_BF16),
                   w_ff2[layer].astype(_BF16), norm_final_w.reshape(1, d).astype(_F32),
                   _pick_tile(n, (256, 128)), final_norm=layer == depth - 1)
    return x2d.reshape(b_, t_, d)
```

```python
import functools
import math

import jax
import jax.numpy as jnp
import numpy as np
from jax import lax
from jax.experimental import pallas as pl
from jax.experimental.pallas import tpu as pltpu

EPS = 1e-6
H_A = 8
DK_A = 128
DV_A = 128
CONV_K = 4
GDN_CHUNK = 128
H_B = 8
D_HEAD_B = 128
H_IDX = 8
D_IDX = 64
TOPK_MAX = 256
Q_BLOCK = 128
CHUNK = 64
KEY_TILE = 512
N_BUCKETS = 32
MAX_DISTANCE = 128

LANE = 128
VMEM_LIMIT = 56 * 1024 * 1024

INT_MIN = -(2**31)
INT_MAX = 2**31 - 1
KEY_NEG_INF = -2139095041
MASK_NEG = -1e30
LOG2E = math.log2(math.e)

_F32 = jnp.float32
_BF16 = jnp.bfloat16


def _dot(a, b):
    return jnp.dot(a, b, preferred_element_type=_F32)


def _dot_nt(a, b):
    return lax.dot_general(a, b, (((1,), (1,)), ((), ())), preferred_element_type=_F32)


def _sigmoid(x):
    return 1.0 / (1.0 + jnp.exp(-x))


def _rms(x, w):
    return x * lax.rsqrt(jnp.mean(x * x, axis=-1, keepdims=True) + EPS) * w


def _norm_proj_kernel(x_ref, nw_ref, w_ref, ws_ref, o_ref, os_ref, *, chunk):
    h = _rms(x_ref[...], nw_ref[...]).astype(_BF16)
    os_ref[...] = _dot(h, ws_ref[...])
    for c0 in range(0, w_ref.shape[1], chunk):
        o_ref[:, c0:c0 + chunk] = _dot(h, w_ref[:, c0:c0 + chunk]).astype(o_ref.dtype)


def _norm_proj(x2d, nw, w, ws, tn):
    n, d = x2d.shape
    c = w.shape[1]
    const = functools.partial(pl.BlockSpec, pipeline_mode=pl.Buffered(1))
    return pl.pallas_call(
        functools.partial(_norm_proj_kernel, chunk=_pick_tile(c, (768, 512, 256, 128))),
        grid=(n // tn,),
        in_specs=[
            pl.BlockSpec((tn, d), lambda i: (i, 0)),
            const((1, d), lambda i: (0, 0)),
            const((d, c), lambda i: (0, 0)),
            const((d, LANE), lambda i: (0, 0)),
        ],
        out_specs=[pl.BlockSpec((tn, c), lambda i: (i, 0)), pl.BlockSpec((tn, LANE), lambda i: (i, 0))],
        out_shape=[jax.ShapeDtypeStruct((n, c), _BF16), jax.ShapeDtypeStruct((n, LANE), _F32)],
        compiler_params=pltpu.CompilerParams(
            dimension_semantics=("arbitrary",), vmem_limit_bytes=VMEM_LIMIT),
        name="proj",
    )(x2d, nw, w, ws)


def _gdn_kernel(q_ref, k_ref, v_ref, z_ref, sm_ref, cw_ref, arow_ref, dtrow_ref, naw_ref, o_ref,
                xbuf, s_ref, kn_s, kb_s, qn_s, rhs_s, dec_s, qdec_s, ktt_s, eg_s, m_s, x_s):
    tt = q_ref.shape[0]
    hd = H_A * DK_A

    @pl.when(pl.program_id(1) == 0)
    def _():
        xbuf[:, 0:8, :] = jnp.zeros((3, 8, hd), _F32)
        s_ref[...] = jnp.zeros_like(s_ref)

    for idx, ref in enumerate((q_ref, k_ref, v_ref)):
        xbuf[idx, 8:8 + tt, :] = ref[...].astype(_F32)

    sm = sm_ref[...]
    beta_full = _sigmoid(sm)
    xg = sm + dtrow_ref[...]
    softplus = jnp.maximum(xg, 0.0) + jnp.log(1.0 + jnp.exp(-jnp.abs(xg)))
    g_full = -jnp.exp(arow_ref[...]) * softplus
    row = lax.broadcasted_iota(jnp.int32, (tt, tt), 0)
    col = lax.broadcasted_iota(jnp.int32, (tt, tt), 1)
    tri = (col <= row).astype(_F32)
    gcum = jnp.dot(tri, g_full, preferred_element_type=_F32, precision=lax.Precision.HIGHEST)
    gcum_t = gcum.T
    strict = col < row
    eye = (col == row).astype(_F32)

    for h in range(H_A):
        sl = slice(h * DK_A, (h + 1) * DK_A)
        conv = []
        for idx in range(3):
            acc = None
            for j in range(CONV_K):
                term = cw_ref[j:j + 1, idx * hd + h * DK_A: idx * hd + (h + 1) * DK_A] * \
                    xbuf[idx, 8 - (CONV_K - 1) + j: 8 - (CONV_K - 1) + j + tt, sl]
                acc = term if acc is None else acc + term
            conv.append(acc * _sigmoid(acc))
        qh, kh, vh = conv
        qn = qh * lax.rsqrt(jnp.sum(qh * qh, axis=-1, keepdims=True) + EPS) * (DK_A ** -0.5)
        kn = kh * lax.rsqrt(jnp.sum(kh * kh, axis=-1, keepdims=True) + EPS)
        bcol = beta_full[:, h:h + 1]
        gcol = gcum[:, H_A + h:H_A + h + 1]
        grow = gcum_t[H_A + h:H_A + h + 1, :]
        glast = gcum[tt - 1:tt, H_A + h:H_A + h + 1]
        kb = kn * bcol
        eg = jnp.exp(gcol)
        rhs_s[h, :, 0:DV_A] = vh * bcol
        rhs_s[h, :, DV_A:DV_A + DK_A] = kb * eg
        dec_s[h] = jnp.where(strict, jnp.exp(jnp.where(strict, gcol - grow, 0.0)), 0.0)
        qdec_s[h] = (qn * eg).astype(_BF16)
        ktt_s[h] = (kn * jnp.exp(glast - gcol)).T.astype(_BF16)
        kn_s[h] = kn.astype(_BF16)
        kb_s[h] = kb.astype(_BF16)
        qn_s[h] = qn.astype(_BF16)
        eg_s[h] = jnp.broadcast_to(jnp.exp(glast), (DK_A, DV_A))

    for idx in range(3):
        xbuf[idx, 0:8, :] = xbuf[idx, tt:tt + 8, :]

    n_sq = int(math.log2(tt))

    heads = range(H_A)
    for h in heads:
        m = -(_dot_nt(kb_s[h], kn_s[h]) * dec_s[h])
        m_s[h] = m
        x_s[h] = m
    for _ in range(n_sq - 1):
        for h in heads:
            mb = m_s[h].astype(_BF16)
            m = _dot(mb, mb)
            m_s[h] = m
            x_s[h] = x_s[h] + m + _dot(x_s[h].astype(_BF16), m.astype(_BF16))
    for h in heads:
        rhs = rhs_s[h]
        rhs_s[h] = rhs + _dot(x_s[h].astype(_BF16), rhs.astype(_BF16))
    for h in heads:
        m_s[h] = _dot_nt(qn_s[h], kn_s[h]) * (dec_s[h] + eye)
    for h in heads:
        sb = s_ref[h].astype(_BF16)
        vnew = rhs_s[h, :, 0:DV_A] - _dot(rhs_s[h, :, DV_A:DV_A + DK_A].astype(_BF16), sb)
        vb = vnew.astype(_BF16)
        o = _dot(qdec_s[h], sb) + _dot(m_s[h].astype(_BF16), vb)
        s_ref[h] = s_ref[h] * eg_s[h] + _dot(ktt_s[h], vb)
        rhs_s[h, :, 0:DV_A] = o

    naw = naw_ref[...]
    for h in range(H_A):
        sl = slice(h * DV_A, (h + 1) * DV_A)
        o = rhs_s[h, :, 0:DV_A]
        z = z_ref[:, sl].astype(_F32)
        on = o * lax.rsqrt(jnp.mean(o * o, axis=-1, keepdims=True) + EPS) * naw
        o_ref[:, sl] = (on * (z * _sigmoid(z))).astype(o_ref.dtype)


def _gdn(pb, ps, cw, arow, dtrow, naw, b_, t_, col0):
    tt = GDN_CHUNK
    hd = H_A * DK_A
    nt = t_ // tt
    cb = col0 // hd

    def rowblk(j):
        return pl.BlockSpec((tt, hd), lambda b, t: (b * nt + t, cb + j))

    const = lambda shape: pl.BlockSpec(shape, lambda b, t: (0, 0))
    return pl.pallas_call(
        _gdn_kernel,
        grid=(b_, nt),
        in_specs=[rowblk(0), rowblk(1), rowblk(2), rowblk(3),
                  pl.BlockSpec((tt, LANE), lambda b, t: (b * nt + t, 0)),
                  const((CONV_K, 3 * hd)), const((1, LANE)), const((1, LANE)), const((1, DV_A))],
        out_specs=pl.BlockSpec((tt, hd), lambda b, t: (b * nt + t, 0)),
        out_shape=jax.ShapeDtypeStruct((b_ * t_, hd), _BF16),
        scratch_shapes=[
            pltpu.VMEM((3, tt + 8, hd), _F32),
            pltpu.VMEM((H_A, DK_A, DV_A), _F32),
            pltpu.VMEM((H_A, tt, DK_A), _BF16),
            pltpu.VMEM((H_A, tt, DK_A), _BF16),
            pltpu.VMEM((H_A, tt, DK_A), _BF16),
            pltpu.VMEM((H_A, tt, DV_A + DK_A), _F32),
            pltpu.VMEM((H_A, tt, tt), _F32),
            pltpu.VMEM((H_A, tt, DK_A), _BF16),
            pltpu.VMEM((H_A, DK_A, tt), _BF16),
            pltpu.VMEM((H_A, DK_A, DV_A), _F32),
            pltpu.VMEM((H_A, tt, tt), _F32),
            pltpu.VMEM((H_A, tt, tt), _F32),
        ],
        compiler_params=pltpu.CompilerParams(
            dimension_semantics=("arbitrary", "arbitrary"), vmem_limit_bytes=VMEM_LIMIT),
        name="gdn",
    )(pb, pb, pb, pb, ps, cw, arow, dtrow, naw)


def _bias_kernel(tab_ref, bucket_ref, o_ref):
    bucket = bucket_ref[...]
    nq, w = bucket.shape
    for h in range(H_B):
        acc = jnp.zeros((nq, w), _F32)
        for b in range(N_BUCKETS):
            acc = acc + jnp.where(bucket == b, tab_ref[b, h], 0.0)
        far = acc[:, 2 * LANE:3 * LANE]
        o_ref[h] = (acc[:, 0:2 * LANE] - jnp.concatenate([far, far], axis=1)) * LOG2E


def _bias_tiles(rel_table, bucket):
    return pl.pallas_call(
        _bias_kernel,
        in_specs=[pl.BlockSpec(memory_space=pltpu.SMEM),
                  pl.BlockSpec(bucket.shape, lambda: (0, 0))],
        out_specs=pl.BlockSpec((H_B, Q_BLOCK, 2 * LANE), lambda: (0, 0, 0)),
        out_shape=jax.ShapeDtypeStruct((H_B, Q_BLOCK, 2 * LANE), _F32),
        name="rel_bias_tiles",
    )(rel_table, bucket)


def _dsa_kernel(q_ref, k_ref, v_ref, iq_ref, ik_ref, sm_ref, bias_ref, o_ref,
                khi_s, klo_s, mask_s, qs_s, iwb_s, m_s, l_s, acc_s, s_s, si_s, sf_s, s16_s, acc16_s, *, topk):
    nq = q_ref.shape[0]
    st_w = KEY_TILE
    assert nq == st_w
    ng = st_w // LANE
    qb = pl.program_id(1)
    n_st = qb + 1
    hd = D_HEAD_B
    scale = D_HEAD_B ** -0.5 * LOG2E
    f_topk = float(topk)

    lane_j = lax.broadcasted_iota(jnp.int32, (nq, LANE), 1)
    row_i = lax.broadcasted_iota(jnp.int32, (nq, LANE), 0)

    def grp(g):
        return slice(g * LANE, (g + 1) * LANE)

    iw = sm_ref[...]
    for h in range(H_B):
        sl = slice(h * hd, (h + 1) * hd)
        qs_s[h] = (q_ref[:, sl].astype(_F32) * scale).astype(_BF16)
        iwb_s[h] = jnp.broadcast_to(iw[:, 2 * H_A + h:2 * H_A + h + 1], (nq, LANE))
        m_s[h] = jnp.full((nq, LANE), MASK_NEG, _F32)
        l_s[h] = jnp.zeros((nq, LANE), _F32)
        acc_s[h] = jnp.zeros((nq, hd), _F32)

    def scores(st, diag):
        ikt = ik_ref[pl.ds(pl.multiple_of(st * st_w, st_w), st_w), :]
        acc = [jnp.zeros((nq, LANE), _F32) for _ in range(ng)]
        for p in range(H_IDX // 2):
            xq = iq_ref[:, p * LANE:(p + 1) * LANE]
            for half in range(2):
                h = 2 * p + half
                s = jnp.maximum(_dot_nt(xq, ikt[:, half * LANE:(half + 1) * LANE]), 0.0)
                w = iwb_s[h]
                for g in range(ng):
                    acc[g] = acc[g] + w * s[:, grp(g)]
        for g in range(ng):
            a = acc[g]
            if diag:
                a = jnp.where((g * LANE + lane_j) // CHUNK <= row_i // CHUNK, a, -jnp.inf)
            bits = pltpu.bitcast(a, jnp.int32)
            key = bits ^ ((bits >> 31) & INT_MAX)
            khi_s[st, g] = (key >> 16).astype(jnp.int16)
            klo_s[st, g] = ((key & 0xFFFF) - 0x8000).astype(jnp.int16)

    def p1(st, carry):
        scores(st, False)
        return carry

    lax.fori_loop(0, qb, p1, 0)
    scores(qb, True)

    ones_b = jnp.ones((LANE, LANE), _BF16)
    n_tiles_max = khi_s.shape[0]
    assert n_tiles_max * ng <= 256
    assert n_tiles_max * st_w < 2 ** 15
    i16 = jnp.int16
    one16, zero16, min16, max16 = i16(1), i16(0), i16(-2 ** 15), i16(2 ** 15 - 1)
    CAND_, THI_, TLO_, MIDX_ = 0, 1, 2, 3
    T_, THI32_ = 0, 1
    NEED_, CNT_ = 0, 1
    rb = 128
    blocks = [slice(r0, r0 + rb) for r0 in range(0, nq, rb)]
    lane_rb = lax.broadcasted_iota(jnp.int32, (rb, LANE), 1)

    def pos16(st, g):
        return (st * st_w + g * LANE + lane_rb).astype(i16)

    def phase2(n_static):
        def pad(st, carry):
            for g in range(ng):
                khi_s[st, g] = jnp.full((nq, LANE), min16, i16)
                klo_s[st, g] = jnp.full((nq, LANE), min16, i16)
            return carry

        lax.fori_loop(n_st, n_static, pad, 0)

        def count(plane, pred_fn):
            parts = []
            for rs in blocks:
                a = jnp.zeros((rb, LANE), i16)
                for st in range(n_static):
                    for g in range(ng):
                        a = a + jnp.where(pred_fn(plane[st, g, rs, :], rs, st, g), one16, zero16)
                parts.append(a)
            acc = jnp.concatenate(parts, axis=0)
            return _dot(acc.astype(jnp.int32).astype(_F32).astype(_BF16), ones_b)

        def digit_search(plane):
            c0 = count(plane, lambda k, rs, st, g: k >= zero16)
            si_s[T_] = jnp.where(c0 >= sf_s[NEED_], 0, -2 ** 15)

            def bit_body(i, carry):
                t_old = si_s[T_]
                cand = t_old + jnp.left_shift(jnp.int32(1), 14 - i)
                s16_s[CAND_] = cand.astype(i16)
                c = count(plane, lambda k, rs, st, g: k >= s16_s[CAND_, rs, :])
                si_s[T_] = jnp.where(c >= sf_s[NEED_], cand, t_old)
                return carry

            lax.fori_loop(0, 15, bit_body, 0)

        sf_s[NEED_] = jnp.full((nq, LANE), f_topk, _F32)
        digit_search(khi_s)
        si_s[THI32_] = si_s[T_]
        s16_s[THI_] = si_s[T_].astype(i16)
        sf_s[NEED_] = f_topk - count(khi_s, lambda k, rs, st, g: k > s16_s[THI_, rs, :])

        def restrict_lo(st, carry):
            for rs in blocks:
                thi = s16_s[THI_, rs, :]
                for g in range(ng):
                    klo_s[st, g, rs, :] = jnp.where(khi_s[st, g, rs, :] == thi, klo_s[st, g, rs, :], min16)
            return carry

        lax.fori_loop(0, n_st, restrict_lo, 0)
        digit_search(klo_s)
        s16_s[TLO_] = si_s[T_].astype(i16)
        sf_s[CNT_] = count(klo_s, lambda k, rs, st, g: k >= s16_s[TLO_, rs, :])
        finite_thr = si_s[THI32_] * 2 ** 16 + si_s[T_] + 2 ** 15 != KEY_NEG_INF
        s16_s[MIDX_] = jnp.where(finite_thr, 2 ** 15 - 1, 0).astype(i16)

        @pl.when(jnp.max(jnp.where(finite_thr, sf_s[CNT_] - sf_s[NEED_], 0.0)) > 0.0)
        def _():
            sf_s[NEED_] = sf_s[NEED_] - count(klo_s, lambda k, rs, st, g: k > s16_s[TLO_, rs, :])
            n_bits = max(1, int(math.ceil(math.log2(n_tiles_max * st_w))))
            si_s[T_] = jnp.zeros((nq, LANE), jnp.int32)

            def is_tie_below(k, rs, st, g):
                tie = jnp.where(khi_s[st, g, rs, :] == s16_s[THI_, rs, :],
                                jnp.where(k == s16_s[TLO_, rs, :], pos16(st, g), max16), max16)
                return tie < s16_s[CAND_, rs, :]

            def idx_body(i, carry):
                mp = si_s[T_]
                cand = mp + jnp.left_shift(jnp.int32(1), n_bits - 1 - i)
                s16_s[CAND_] = cand.astype(i16)
                c = count(klo_s, is_tie_below)
                si_s[T_] = jnp.where(c < sf_s[NEED_], cand, mp)
                return carry

            lax.fori_loop(0, n_bits, idx_body, 0)
            s16_s[MIDX_] = jnp.where(finite_thr, si_s[T_] + 1, 0).astype(i16)

        def p2(st, carry):
            keep, drop = jnp.zeros((), _BF16), jnp.full((), MASK_NEG, _BF16)
            for rs in blocks:
                thi, tlo, m_idx = s16_s[THI_, rs, :], s16_s[TLO_, rs, :], s16_s[MIDX_, rs, :]
                for g in range(ng):
                    hi, lo = khi_s[st, g, rs, :], klo_s[st, g, rs, :]
                    at_lo = jnp.where(lo > tlo, one16,
                                      jnp.where(lo == tlo, jnp.where(pos16(st, g) < m_idx, one16, zero16), zero16))
                    sel = jnp.where(hi > thi, one16, jnp.where(hi == thi, at_lo, zero16))
                    mask_s[st, g, rs, :] = jnp.where(sel == one16, keep, drop)
            return carry

        lax.fori_loop(0, n_st, p2, 0)

    half = max(1, n_tiles_max // 2)
    if half < n_tiles_max:
        pl.when(n_st <= half)(functools.partial(phase2, half))
        pl.when(n_st > half)(functools.partial(phase2, n_tiles_max))
    else:
        phase2(n_tiles_max)

    def attend(st, mode):
        off = pl.multiple_of(st * st_w, st_w)
        for h in range(H_B):
            sl = slice(h * hd, (h + 1) * hd)
            mask = jnp.concatenate([mask_s[st, g].astype(_F32) for g in range(ng)], axis=1)
            s = _dot_nt(qs_s[h], k_ref[pl.ds(off, st_w), sl]) + mask
            if mode:
                s_s[...] = s
                b_prev = bias_ref[h, :, 0:LANE]
                if mode == 1:
                    s_s[0:LANE, (ng - 1) * LANE:ng * LANE] += b_prev
                else:
                    b_diag = bias_ref[h, :, LANE:2 * LANE]
                    for rt in range(ng):
                        s_s[grp(rt), grp(rt)] += b_diag
                        if rt:
                            s_s[grp(rt), grp(rt - 1)] += b_prev
                s = s_s[...]
            m_old = m_s[h]
            gmax = s[:, grp(0)]
            for g in range(1, ng):
                gmax = jnp.maximum(gmax, s[:, grp(g)])
            m_new = jnp.maximum(m_old, jnp.max(gmax, axis=1, keepdims=True))
            alpha = jnp.exp2(m_old - m_new)
            pieces = [jnp.exp2(s[:, grp(g)] - m_new) for g in range(ng)]
            psum = pieces[0]
            for g in range(1, ng):
                psum = psum + pieces[g]
            p = jnp.concatenate(pieces, axis=1).astype(_BF16)
            m_s[h] = m_new
            l_s[h] = alpha * l_s[h] + psum
            acc_s[h] = alpha * acc_s[h] + _dot(p, v_ref[pl.ds(off, st_w), sl])

    def p3(st, carry):
        attend(st, 0)
        return carry

    lax.fori_loop(0, qb - 1, p3, 0)

    @pl.when(qb > 0)
    def _():
        attend(qb - 1, 1)

    attend(qb, 2)

    for h in range(H_B):
        sl = slice(h * hd, (h + 1) * hd)
        l_fin = jnp.sum(l_s[h], axis=1, keepdims=True)
        o_ref[:, sl] = (acc_s[h] / l_fin).astype(o_ref.dtype)


def _dsa(pb, ps, bias, b_, t_, col0):
    nq = KEY_TILE
    nb = t_ // nq
    hd = H_B * D_HEAD_B
    topk = min(TOPK_MAX, t_ // 4)
    iq_w = H_IDX * D_IDX
    ik_w = 2 * LANE
    assert col0 % hd == 0 and (col0 + 3 * hd) % iq_w == 0 and (col0 + 3 * hd + iq_w) % ik_w == 0
    cq = col0 // hd
    resident = functools.partial(pl.BlockSpec, pipeline_mode=pl.Buffered(1))
    return pl.pallas_call(
        functools.partial(_dsa_kernel, topk=topk),
        grid=(b_, nb),
        in_specs=[
            pl.BlockSpec((nq, hd), lambda b, i: (b * nb + i, cq)),
            resident((t_, hd), lambda b, i: (b, cq + 1)),
            resident((t_, hd), lambda b, i: (b, cq + 2)),
            pl.BlockSpec((nq, iq_w), lambda b, i: (b * nb + i, (col0 + 3 * hd) // iq_w)),
            resident((t_, ik_w), lambda b, i: (b, (col0 + 3 * hd + iq_w) // ik_w)),
            pl.BlockSpec((nq, LANE), lambda b, i: (b * nb + i, 0)),
            resident((H_B, Q_BLOCK, 2 * LANE), lambda b, i: (0, 0, 0)),
        ],
        out_specs=pl.BlockSpec((nq, hd), lambda b, i: (b * nb + i, 0)),
        out_shape=jax.ShapeDtypeStruct((b_ * t_, hd), _BF16),
        scratch_shapes=[
            pltpu.VMEM((nb, KEY_TILE // LANE, nq, LANE), jnp.int16),
            pltpu.VMEM((nb, KEY_TILE // LANE, nq, LANE), jnp.int16),
            pltpu.VMEM((nb, KEY_TILE // LANE, nq, LANE), _BF16),
            pltpu.VMEM((H_B, nq, D_HEAD_B), _BF16),
            pltpu.VMEM((H_B, nq, LANE), _F32),
            pltpu.VMEM((H_B, nq, LANE), _F32),
            pltpu.VMEM((H_B, nq, LANE), _F32),
            pltpu.VMEM((H_B, nq, D_HEAD_B), _F32),
            pltpu.VMEM((nq, KEY_TILE), _F32),
            pltpu.VMEM((2, nq, LANE), jnp.int32),
            pltpu.VMEM((2, nq, LANE), _F32),
            pltpu.VMEM((4, nq, LANE), jnp.int16),
            pltpu.VMEM((nq, LANE), jnp.int16),
        ],
        compiler_params=pltpu.CompilerParams(
            dimension_semantics=("arbitrary", "arbitrary"), vmem_limit_bytes=VMEM_LIMIT),
        name="dsa",
    )(pb, pb, pb, pb, pb, ps, bias)


def _merge_kernel(oa_ref, ob_ref, g_ref, x_ref, bg_ref, wa_ref, wb_ref, wo_ref, o_ref):
    d = x_ref.shape[1]
    a = _dot(oa_ref[...], wa_ref[...])
    b = _dot(ob_ref[...], wb_ref[...])
    g = _sigmoid(g_ref[...].astype(_F32) + bg_ref[...])
    merged = g[:, 0:d] * a + g[:, d:2 * d] * b
    o_ref[...] = x_ref[...] + _dot(merged.astype(_BF16), wo_ref[...])


def _merge(oa, ob, pb, x2d, bg, wa, wb, wo, tn):
    n, d = x2d.shape
    row = lambda w: pl.BlockSpec((tn, w), lambda i: (i, 0))
    const = lambda a: pl.BlockSpec(a.shape, lambda i: (0, 0))
    return pl.pallas_call(
        _merge_kernel,
        grid=(n // tn,),
        in_specs=[row(oa.shape[1]), row(ob.shape[1]), row(2 * d), row(d),
                  const(bg), const(wa), const(wb), const(wo)],
        out_specs=row(d),
        out_shape=jax.ShapeDtypeStruct((n, d), _F32),
        compiler_params=pltpu.CompilerParams(
            dimension_semantics=("arbitrary",), vmem_limit_bytes=VMEM_LIMIT),
        name="merge",
    )(oa, ob, pb, x2d, bg, wa, wb, wo)


def _mlp_kernel(x_ref, n2_ref, w1_ref, w2_ref, nf_ref, o_ref, *, final_norm):
    x = x_ref[...]
    h2 = _rms(x, n2_ref[...]).astype(_BF16)
    hid = jnp.maximum(_dot(h2, w1_ref[...]), 0.0)
    y = x + _dot((hid * hid).astype(_BF16), w2_ref[...])
    o_ref[...] = _rms(y, nf_ref[...]) if final_norm else y


def _mlp(x1, n2, w1, w2, nf, tn, final_norm):
    n, d = x1.shape
    row = pl.BlockSpec((tn, d), lambda i: (i, 0))
    const = lambda a: pl.BlockSpec(a.shape, lambda i: (0, 0))
    return pl.pallas_call(
        functools.partial(_mlp_kernel, final_norm=final_norm),
        grid=(n // tn,),
        in_specs=[row, const(n2), const(w1), const(w2), const(nf)],
        out_specs=row,
        out_shape=jax.ShapeDtypeStruct((n, d), _F32),
        compiler_params=pltpu.CompilerParams(
            dimension_semantics=("arbitrary",), vmem_limit_bytes=VMEM_LIMIT),
        name="mlp",
    )(x1, n2, w1, w2, nf)


def _t5_bucket(rel):
    half = N_BUCKETS // 2
    max_exact = half // 2
    base = jnp.where(rel > 0, half, 0)
    n = jnp.abs(rel)
    n_f = jnp.maximum(n, 1).astype(jnp.float32)
    large = max_exact + (jnp.log(n_f / max_exact) / math.log(MAX_DISTANCE / max_exact)
                         * (half - max_exact)).astype(jnp.int32)
    large = jnp.minimum(large, half - 1)
    return base + jnp.where(n < max_exact, n, large)


def _pick_tile(n, prefs):
    for t in prefs:
        if n % t == 0:
            return t
    raise ValueError(f"no tile in {prefs} divides {n}")


def kernel(x, norm1_w, w_in, conv_a_w, a_log, dt_bias, norm_a_w, rel_bias_table, w_gate, b_gate,
           w_proj_a, w_proj_b, w_out, norm2_w, w_ff1, w_ff2, norm_final_w):
    b_, t_, d = x.shape
    depth = norm1_w.shape[0]
    n = b_ * t_
    ha, hb = H_A * DK_A, H_B * D_HEAD_B
    assert t_ % KEY_TILE == 0 and t_ % GDN_CHUNK == 0 and d % LANE == 0
    assert DK_A == DV_A == D_HEAD_B == LANE and 2 * D_IDX == LANE
    assert Q_BLOCK >= MAX_DISTANCE

    o_za = 3 * ha
    o_ba = 4 * ha
    o_aa = o_ba + H_A
    o_qb = o_aa + H_A
    o_iq = o_qb + 3 * hb
    o_ik = o_iq + H_IDX * D_IDX
    o_iw = o_ik + D_IDX

    rel = (jnp.arange(2 * Q_BLOCK, dtype=jnp.int32)[None, :] - Q_BLOCK) \
        - jnp.arange(Q_BLOCK, dtype=jnp.int32)[:, None]
    far = jnp.full((Q_BLOCK, LANE), -(Q_BLOCK + 1), jnp.int32)
    bucket = _t5_bucket(jnp.concatenate([rel, far], axis=1))
    bias = _bias_tiles(rel_bias_table.astype(_F32), bucket)

    x2d = x.reshape(n, d)
    tn = _pick_tile(n, (512, 256, 128))
    for layer in range(depth):
        wi = w_in[layer]
        zpad = lambda w: jnp.zeros((d, w), wi.dtype)
        ws = jnp.concatenate([wi[:, o_ba:o_qb], wi[:, o_iw:o_iw + H_IDX], zpad(LANE - 3 * H_A)],
                             axis=1).astype(_BF16)
        ik = wi[:, o_ik:o_iw]
        wb16 = jnp.concatenate([w_gate[layer], wi[:, 0:o_ba], wi[:, o_qb:o_ik],
                                ik, zpad(D_IDX), zpad(D_IDX), ik], axis=1).astype(_BF16)
        nw = norm1_w[layer].reshape(1, d).astype(_F32)
        pb, ps = _norm_proj(x2d, nw, wb16, ws, tn)

        col_a = 2 * d
        col_b = col_a + 4 * ha
        assert col_a % ha == 0
        lanes = lambda v, off: jnp.zeros((1, LANE), _F32).at[0, off:off + H_A].set(v.astype(_F32))
        oa = _gdn(pb, ps, conv_a_w[layer].astype(_F32), lanes(a_log[layer], H_A), lanes(dt_bias[layer], H_A),
                  norm_a_w[layer].reshape(1, DV_A).astype(_F32), b_, t_, col_a)
        ob = _dsa(pb, ps, bias, b_, t_, col_b)

        x2d = _merge(oa, ob, pb, x2d, b_gate[layer].reshape(1, 2 * d).astype(_F32),
                     w_proj_a[layer].astype(_BF16), w_proj_b[layer].astype(_BF16),
                     w_out[layer].astype(_BF16), tn)
        x2d = _mlp(x2d, norm2_w[layer].reshape(1, d).astype(_F32), w_ff1[layer].astype(_BF16),
                   w_ff2[layer].astype(_BF16), norm_final_w.reshape(1, d).astype(_F32),
                   _pick_tile(n, (256, 128)), final_norm=layer == depth - 1)
    return x2d.reshape(b_, t_, d)
```

```python
import functools
import math

import jax
import jax.numpy as jnp
import numpy as np
from jax import lax
from jax.experimental import pallas as pl
from jax.experimental.pallas import tpu as pltpu

EPS = 1e-6
H_A = 8
DK_A = 128
DV_A = 128
CONV_K = 4
GDN_CHUNK = 128
H_B = 8
D_HEAD_B = 128
H_IDX = 8
D_IDX = 64
TOPK_MAX = 256
Q_BLOCK = 128
CHUNK = 64
KEY_TILE = 512
N_BUCKETS = 32
MAX_DISTANCE = 128

LANE = 128
VMEM_LIMIT = 56 * 1024 * 1024

INT_MIN = -(2**31)
INT_MAX = 2**31 - 1
KEY_NEG_INF = -2139095041
MASK_NEG = -1e30
CODE_KEEP = -1
CODE_DROP = INT_MAX
TIE_BOUND_ALL = 2**30
LOG2E = math.log2(math.e)

_F32 = jnp.float32
_BF16 = jnp.bfloat16


def _dot(a, b):
    return jnp.dot(a, b, preferred_element_type=_F32)


def _dot_nt(a, b):
    return lax.dot_general(a, b, (((1,), (1,)), ((), ())), preferred_element_type=_F32)


def _sigmoid(x):
    return 1.0 / (1.0 + jnp.exp(-x))


def _rms(x, w):
    return x * lax.rsqrt(jnp.mean(x * x, axis=-1, keepdims=True) + EPS) * w


def _norm_proj_kernel(x_ref, nw_ref, w_ref, ws_ref, o_ref, os_ref, *, chunk):
    h = _rms(x_ref[...], nw_ref[...]).astype(_BF16)
    os_ref[...] = _dot(h, ws_ref[...])
    for c0 in range(0, w_ref.shape[1], chunk):
        o_ref[:, c0:c0 + chunk] = _dot(h, w_ref[:, c0:c0 + chunk]).astype(o_ref.dtype)


def _norm_proj(x2d, nw, w, ws, tn):
    n, d = x2d.shape
    c = w.shape[1]
    const = functools.partial(pl.BlockSpec, pipeline_mode=pl.Buffered(1))
    return pl.pallas_call(
        functools.partial(_norm_proj_kernel, chunk=_pick_tile(c, (768, 512, 256, 128))),
        grid=(n // tn,),
        in_specs=[
            pl.BlockSpec((tn, d), lambda i: (i, 0)),
            const((1, d), lambda i: (0, 0)),
            const((d, c), lambda i: (0, 0)),
            const((d, LANE), lambda i: (0, 0)),
        ],
        out_specs=[pl.BlockSpec((tn, c), lambda i: (i, 0)), pl.BlockSpec((tn, LANE), lambda i: (i, 0))],
        out_shape=[jax.ShapeDtypeStruct((n, c), _BF16), jax.ShapeDtypeStruct((n, LANE), _F32)],
        compiler_params=pltpu.CompilerParams(
            dimension_semantics=("arbitrary",), vmem_limit_bytes=VMEM_LIMIT),
        name="proj",
    )(x2d, nw, w, ws)


def _gdn_kernel(q_ref, k_ref, v_ref, z_ref, sm_ref, cw_ref, arow_ref, dtrow_ref, naw_ref, o_ref,
                xbuf, s_ref, kn_s, kb_s, qn_s, rhs_s, dec_s, qdec_s, ktt_s, eg_s, m_s, x_s):
    tt = q_ref.shape[0]
    hd = H_A * DK_A

    @pl.when(pl.program_id(1) == 0)
    def _():
        xbuf[:, 0:8, :] = jnp.zeros((3, 8, hd), _F32)
        s_ref[...] = jnp.zeros_like(s_ref)

    for idx, ref in enumerate((q_ref, k_ref, v_ref)):
        xbuf[idx, 8:8 + tt, :] = ref[...].astype(_F32)

    sm = sm_ref[...]
    beta_full = _sigmoid(sm)
    xg = sm + dtrow_ref[...]
    softplus = jnp.maximum(xg, 0.0) + jnp.log(1.0 + jnp.exp(-jnp.abs(xg)))
    g_full = -jnp.exp(arow_ref[...]) * softplus
    row = lax.broadcasted_iota(jnp.int32, (tt, tt), 0)
    col = lax.broadcasted_iota(jnp.int32, (tt, tt), 1)
    tri = (col <= row).astype(_F32)
    gcum = jnp.dot(tri, g_full, preferred_element_type=_F32, precision=lax.Precision.HIGHEST)
    gcum_t = gcum.T
    strict = col < row
    eye = (col == row).astype(_F32)

    for h in range(H_A):
        sl = slice(h * DK_A, (h + 1) * DK_A)
        conv = []
        for idx in range(3):
            acc = None
            for j in range(CONV_K):
                term = cw_ref[j:j + 1, idx * hd + h * DK_A: idx * hd + (h + 1) * DK_A] * \
                    xbuf[idx, 8 - (CONV_K - 1) + j: 8 - (CONV_K - 1) + j + tt, sl]
                acc = term if acc is None else acc + term
            conv.append(acc * _sigmoid(acc))
        qh, kh, vh = conv
        qn = qh * lax.rsqrt(jnp.sum(qh * qh, axis=-1, keepdims=True) + EPS) * (DK_A ** -0.5)
        kn = kh * lax.rsqrt(jnp.sum(kh * kh, axis=-1, keepdims=True) + EPS)
        bcol = beta_full[:, h:h + 1]
        gcol = gcum[:, H_A + h:H_A + h + 1]
        grow = gcum_t[H_A + h:H_A + h + 1, :]
        glast = gcum[tt - 1:tt, H_A + h:H_A + h + 1]
        kb = kn * bcol
        eg = jnp.exp(gcol)
        rhs_s[h, :, 0:DV_A] = vh * bcol
        rhs_s[h, :, DV_A:DV_A + DK_A] = kb * eg
        dec_s[h] = jnp.where(strict, jnp.exp(jnp.where(strict, gcol - grow, 0.0)), 0.0)
        qdec_s[h] = (qn * eg).astype(_BF16)
        ktt_s[h] = (kn * jnp.exp(glast - gcol)).T.astype(_BF16)
        kn_s[h] = kn.astype(_BF16)
        kb_s[h] = kb.astype(_BF16)
        qn_s[h] = qn.astype(_BF16)
        eg_s[h] = jnp.broadcast_to(jnp.exp(glast), (DK_A, DV_A))

    for idx in range(3):
        xbuf[idx, 0:8, :] = xbuf[idx, tt:tt + 8, :]

    n_sq = int(math.log2(tt))

    heads = range(H_A)
    for h in heads:
        m = -(_dot_nt(kb_s[h], kn_s[h]) * dec_s[h])
        m_s[h] = m
        x_s[h] = m
    for _ in range(n_sq - 1):
        for h in heads:
            mb = m_s[h].astype(_BF16)
            m = _dot(mb, mb)
            m_s[h] = m
            x_s[h] = x_s[h] + m + _dot(x_s[h].astype(_BF16), m.astype(_BF16))
    for h in heads:
        rhs = rhs_s[h]
        rhs_s[h] = rhs + _dot(x_s[h].astype(_BF16), rhs.astype(_BF16))
    for h in heads:
        m_s[h] = _dot_nt(qn_s[h], kn_s[h]) * (dec_s[h] + eye)
    for h in heads:
        sb = s_ref[h].astype(_BF16)
        vnew = rhs_s[h, :, 0:DV_A] - _dot(rhs_s[h, :, DV_A:DV_A + DK_A].astype(_BF16), sb)
        vb = vnew.astype(_BF16)
        o = _dot(qdec_s[h], sb) + _dot(m_s[h].astype(_BF16), vb)
        s_ref[h] = s_ref[h] * eg_s[h] + _dot(ktt_s[h], vb)
        rhs_s[h, :, 0:DV_A] = o

    naw = naw_ref[...]
    for h in range(H_A):
        sl = slice(h * DV_A, (h + 1) * DV_A)
        o = rhs_s[h, :, 0:DV_A]
        z = z_ref[:, sl].astype(_F32)
        on = o * lax.rsqrt(jnp.mean(o * o, axis=-1, keepdims=True) + EPS) * naw
        o_ref[:, sl] = (on * (z * _sigmoid(z))).astype(o_ref.dtype)


def _gdn(pb, ps, cw, arow, dtrow, naw, b_, t_, col0):
    tt = GDN_CHUNK
    hd = H_A * DK_A
    nt = t_ // tt
    cb = col0 // hd

    def rowblk(j):
        return pl.BlockSpec((tt, hd), lambda b, t: (b * nt + t, cb + j))

    const = lambda shape: pl.BlockSpec(shape, lambda b, t: (0, 0))
    return pl.pallas_call(
        _gdn_kernel,
        grid=(b_, nt),
        in_specs=[rowblk(0), rowblk(1), rowblk(2), rowblk(3),
                  pl.BlockSpec((tt, LANE), lambda b, t: (b * nt + t, 0)),
                  const((CONV_K, 3 * hd)), const((1, LANE)), const((1, LANE)), const((1, DV_A))],
        out_specs=pl.BlockSpec((tt, hd), lambda b, t: (b * nt + t, 0)),
        out_shape=jax.ShapeDtypeStruct((b_ * t_, hd), _BF16),
        scratch_shapes=[
            pltpu.VMEM((3, tt + 8, hd), _F32),
            pltpu.VMEM((H_A, DK_A, DV_A), _F32),
            pltpu.VMEM((H_A, tt, DK_A), _BF16),
            pltpu.VMEM((H_A, tt, DK_A), _BF16),
            pltpu.VMEM((H_A, tt, DK_A), _BF16),
            pltpu.VMEM((H_A, tt, DV_A + DK_A), _F32),
            pltpu.VMEM((H_A, tt, tt), _F32),
            pltpu.VMEM((H_A, tt, DK_A), _BF16),
            pltpu.VMEM((H_A, DK_A, tt), _BF16),
            pltpu.VMEM((H_A, DK_A, DV_A), _F32),
            pltpu.VMEM((H_A, tt, tt), _F32),
            pltpu.VMEM((H_A, tt, tt), _F32),
        ],
        compiler_params=pltpu.CompilerParams(
            dimension_semantics=("arbitrary", "arbitrary"), vmem_limit_bytes=VMEM_LIMIT),
        name="gdn",
    )(pb, pb, pb, pb, ps, cw, arow, dtrow, naw)


def _bias_kernel(tab_ref, bucket_ref, o_ref):
    bucket = bucket_ref[...]
    nq, w = bucket.shape
    for h in range(H_B):
        acc = jnp.zeros((nq, w), _F32)
        for b in range(N_BUCKETS):
            acc = acc + jnp.where(bucket == b, tab_ref[b, h], 0.0)
        far = acc[:, 2 * LANE:3 * LANE]
        o_ref[h] = (acc[:, 0:2 * LANE] - jnp.concatenate([far, far], axis=1)) * LOG2E


def _bias_tiles(rel_table, bucket):
    return pl.pallas_call(
        _bias_kernel,
        in_specs=[pl.BlockSpec(memory_space=pltpu.SMEM),
                  pl.BlockSpec(bucket.shape, lambda: (0, 0))],
        out_specs=pl.BlockSpec((H_B, Q_BLOCK, 2 * LANE), lambda: (0, 0, 0)),
        out_shape=jax.ShapeDtypeStruct((H_B, Q_BLOCK, 2 * LANE), _F32),
        name="rel_bias_tiles",
    )(rel_table, bucket)


def _dsa_kernel(q_ref, k_ref, v_ref, iq_ref, ik_ref, sm_ref, bias_ref, o_ref,
                key_s, qs_s, iwb_s, m_s, l_s, acc_s, s_s, si_s, sf_s, *, topk):
    nq = q_ref.shape[0]
    st_w = KEY_TILE
    assert nq == st_w
    ng = st_w // LANE
    qb = pl.program_id(1)
    n_st = qb + 1
    hd = D_HEAD_B
    scale = D_HEAD_B ** -0.5 * LOG2E
    f_topk = float(topk)

    lane_j = lax.broadcasted_iota(jnp.int32, (nq, LANE), 1)
    row_i = lax.broadcasted_iota(jnp.int32, (nq, LANE), 0)

    def grp(g):
        return slice(g * LANE, (g + 1) * LANE)

    iw = sm_ref[...]
    for h in range(H_B):
        sl = slice(h * hd, (h + 1) * hd)
        qs_s[h] = (q_ref[:, sl].astype(_F32) * scale).astype(_BF16)
        iwb_s[h] = jnp.broadcast_to(iw[:, 2 * H_A + h:2 * H_A + h + 1], (nq, LANE))
        m_s[h] = jnp.full((nq, LANE), MASK_NEG, _F32)
        l_s[h] = jnp.zeros((nq, LANE), _F32)
        acc_s[h] = jnp.zeros((nq, hd), _F32)

    def scores(st, diag):
        ikt = ik_ref[pl.ds(pl.multiple_of(st * st_w, st_w), st_w), :]
        acc = [jnp.zeros((nq, LANE), _F32) for _ in range(ng)]
        for p in range(H_IDX // 2):
            xq = iq_ref[:, p * LANE:(p + 1) * LANE]
            for half in range(2):
                h = 2 * p + half
                s = jnp.maximum(_dot_nt(xq, ikt[:, half * LANE:(half + 1) * LANE]), 0.0)
                w = iwb_s[h]
                for g in range(ng):
                    acc[g] = acc[g] + w * s[:, grp(g)]
        for g in range(ng):
            a = acc[g]
            if diag:
                a = jnp.where((g * LANE + lane_j) // CHUNK <= row_i // CHUNK, a, -jnp.inf)
            bits = pltpu.bitcast(a, jnp.int32)
            key_s[st, g] = bits ^ ((bits >> 31) & INT_MAX)

    def p1(st, carry):
        scores(st, False)
        return carry

    lax.fori_loop(0, qb, p1, 0)
    scores(qb, True)

    ones_b = jnp.ones((LANE, LANE), _BF16)
    assert key_s.shape[0] * ng <= 256

    T_, CAND_, MIDX_ = 0, 1, 2
    ACC_, CNT_ = 0, 1
    rb = 64
    lane_rb = lax.broadcasted_iota(jnp.int32, (rb, LANE), 1)

    def count(pred_fn):
        sf_s[ACC_] = jnp.zeros((nq, LANE), _F32)

        def body(st, carry):
            for r0 in range(0, nq, rb):
                rs = slice(r0, r0 + rb)
                acc = sf_s[ACC_, rs, :]
                for g in range(ng):
                    acc = acc + jnp.where(pred_fn(key_s[st, g, rs, :], rs), 1.0, 0.0)
                sf_s[ACC_, rs, :] = acc
            return carry

        lax.fori_loop(0, n_st, body, 0)
        return _dot(sf_s[ACC_].astype(_BF16), ones_b)

    c0 = count(lambda k, rs: k >= 0)
    ok0 = c0 >= f_topk
    si_s[T_] = jnp.where(ok0, 0, INT_MIN)
    sf_s[CNT_] = jnp.where(ok0, c0, (n_st * st_w).astype(_F32))

    def bit_body(i, carry):
        t_old = si_s[T_]
        cand = t_old + jnp.left_shift(jnp.int32(1), 30 - i)
        si_s[CAND_] = cand
        c = count(lambda k, rs: k >= si_s[CAND_, rs, :])
        ok = c >= f_topk
        si_s[T_] = jnp.where(ok, cand, t_old)
        sf_s[CNT_] = jnp.where(ok, c, sf_s[CNT_])
        return carry

    lax.fori_loop(0, 31, bit_body, 0)

    def encode(st, carry):
        for r0 in range(0, nq, rb):
            rs = slice(r0, r0 + rb)
            t = si_s[T_, rs, :]
            for g in range(ng):
                k = key_s[st, g, rs, :]
                pos = st * st_w + g * LANE + lane_rb
                key_s[st, g, rs, :] = jnp.where(k > t, CODE_KEEP, jnp.where(k == t, pos, CODE_DROP))
        return carry

    lax.fori_loop(0, n_st, encode, 0)
    finite_thr = si_s[T_] != KEY_NEG_INF
    si_s[MIDX_] = jnp.where(finite_thr, TIE_BOUND_ALL, 0)

    @pl.when(jnp.max(jnp.where(finite_thr, sf_s[CNT_], 0.0)) > f_topk)
    def _():
        n_bits = max(1, int(math.ceil(math.log2(key_s.shape[0] * st_w))))
        si_s[MIDX_] = jnp.zeros((nq, LANE), jnp.int32)

        def idx_body(i, carry):
            mp = si_s[MIDX_]
            cand = mp + jnp.left_shift(jnp.int32(1), n_bits - 1 - i)
            si_s[CAND_] = cand
            c = count(lambda code, rs: code < si_s[CAND_, rs, :])
            si_s[MIDX_] = jnp.where(c < f_topk, cand, mp)
            return carry

        lax.fori_loop(0, n_bits, idx_body, 0)
        si_s[MIDX_] = jnp.where(si_s[T_] != KEY_NEG_INF, si_s[MIDX_] + 1, 0)

    def p2(st, carry):
        for r0 in range(0, nq, rb):
            rs = slice(r0, r0 + rb)
            bound = si_s[MIDX_, rs, :]
            for g in range(ng):
                mb = jnp.where(key_s[st, g, rs, :] < bound, 0.0, MASK_NEG)
                key_s[st, g, rs, :] = pltpu.bitcast(mb, jnp.int32)
        return carry

    lax.fori_loop(0, n_st, p2, 0)

    def attend(st, mode):
        off = pl.multiple_of(st * st_w, st_w)
        for h in range(H_B):
            sl = slice(h * hd, (h + 1) * hd)
            mask = jnp.concatenate([pltpu.bitcast(key_s[st, g], _F32) for g in range(ng)], axis=1)
            s = _dot_nt(qs_s[h], k_ref[pl.ds(off, st_w), sl]) + mask
            if mode:
                s_s[...] = s
                b_prev = bias_ref[h, :, 0:LANE]
                if mode == 1:
                    s_s[0:LANE, (ng - 1) * LANE:ng * LANE] += b_prev
                else:
                    b_diag = bias_ref[h, :, LANE:2 * LANE]
                    for rt in range(ng):
                        s_s[grp(rt), grp(rt)] += b_diag
                        if rt:
                            s_s[grp(rt), grp(rt - 1)] += b_prev
                s = s_s[...]
            m_old = m_s[h]
            gmax = s[:, grp(0)]
            for g in range(1, ng):
                gmax = jnp.maximum(gmax, s[:, grp(g)])
            m_new = jnp.maximum(m_old, jnp.max(gmax, axis=1, keepdims=True))
            alpha = jnp.exp2(m_old - m_new)
            pieces = [jnp.exp2(s[:, grp(g)] - m_new) for g in range(ng)]
            psum = pieces[0]
            for g in range(1, ng):
                psum = psum + pieces[g]
            p = jnp.concatenate(pieces, axis=1).astype(_BF16)
            m_s[h] = m_new
            l_s[h] = alpha * l_s[h] + psum
            acc_s[h] = alpha * acc_s[h] + _dot(p, v_ref[pl.ds(off, st_w), sl])

    def p3(st, carry):
        attend(st, 0)
        return carry

    lax.fori_loop(0, qb - 1, p3, 0)

    @pl.when(qb > 0)
    def _():
        attend(qb - 1, 1)

    attend(qb, 2)

    for h in range(H_B):
        sl = slice(h * hd, (h + 1) * hd)
        l_fin = jnp.sum(l_s[h], axis=1, keepdims=True)
        o_ref[:, sl] = (acc_s[h] / l_fin).astype(o_ref.dtype)


def _dsa(pb, ps, bias, b_, t_, col0):
    nq = KEY_TILE
    nb = t_ // nq
    hd = H_B * D_HEAD_B
    topk = min(TOPK_MAX, t_ // 4)
    iq_w = H_IDX * D_IDX
    ik_w = 2 * LANE
    assert col0 % hd == 0 and (col0 + 3 * hd) % iq_w == 0 and (col0 + 3 * hd + iq_w) % ik_w == 0
    cq = col0 // hd
    resident = functools.partial(pl.BlockSpec, pipeline_mode=pl.Buffered(1))
    return pl.pallas_call(
        functools.partial(_dsa_kernel, topk=topk),
        grid=(b_, nb),
        in_specs=[
            pl.BlockSpec((nq, hd), lambda b, i: (b * nb + i, cq)),
            resident((t_, hd), lambda b, i: (b, cq + 1)),
            resident((t_, hd), lambda b, i: (b, cq + 2)),
            pl.BlockSpec((nq, iq_w), lambda b, i: (b * nb + i, (col0 + 3 * hd) // iq_w)),
            resident((t_, ik_w), lambda b, i: (b, (col0 + 3 * hd + iq_w) // ik_w)),
            pl.BlockSpec((nq, LANE), lambda b, i: (b * nb + i, 0)),
            resident((H_B, Q_BLOCK, 2 * LANE), lambda b, i: (0, 0, 0)),
        ],
        out_specs=pl.BlockSpec((nq, hd), lambda b, i: (b * nb + i, 0)),
        out_shape=jax.ShapeDtypeStruct((b_ * t_, hd), _BF16),
        scratch_shapes=[
            pltpu.VMEM((nb, KEY_TILE // LANE, nq, LANE), jnp.int32),
            pltpu.VMEM((H_B, nq, D_HEAD_B), _BF16),
            pltpu.VMEM((H_B, nq, LANE), _F32),
            pltpu.VMEM((H_B, nq, LANE), _F32),
            pltpu.VMEM((H_B, nq, LANE), _F32),
            pltpu.VMEM((H_B, nq, D_HEAD_B), _F32),
            pltpu.VMEM((nq, KEY_TILE), _F32),
            pltpu.VMEM((3, nq, LANE), jnp.int32),
            pltpu.VMEM((2, nq, LANE), _F32),
        ],
        compiler_params=pltpu.CompilerParams(
            dimension_semantics=("arbitrary", "arbitrary"), vmem_limit_bytes=VMEM_LIMIT),
        name="dsa",
    )(pb, pb, pb, pb, pb, ps, bias)


def _merge_kernel(oa_ref, ob_ref, g_ref, x_ref, bg_ref, wa_ref, wb_ref, wo_ref, o_ref):
    d = x_ref.shape[1]
    a = _dot(oa_ref[...], wa_ref[...])
    b = _dot(ob_ref[...], wb_ref[...])
    g = _sigmoid(g_ref[...].astype(_F32) + bg_ref[...])
    merged = g[:, 0:d] * a + g[:, d:2 * d] * b
    o_ref[...] = x_ref[...] + _dot(merged.astype(_BF16), wo_ref[...])


def _merge(oa, ob, pb, x2d, bg, wa, wb, wo, tn):
    n, d = x2d.shape
    row = lambda w: pl.BlockSpec((tn, w), lambda i: (i, 0))
    const = lambda a: pl.BlockSpec(a.shape, lambda i: (0, 0))
    return pl.pallas_call(
        _merge_kernel,
        grid=(n // tn,),
        in_specs=[row(oa.shape[1]), row(ob.shape[1]), row(2 * d), row(d),
                  const(bg), const(wa), const(wb), const(wo)],
        out_specs=row(d),
        out_shape=jax.ShapeDtypeStruct((n, d), _F32),
        compiler_params=pltpu.CompilerParams(
            dimension_semantics=("arbitrary",), vmem_limit_bytes=VMEM_LIMIT),
        name="merge",
    )(oa, ob, pb, x2d, bg, wa, wb, wo)


def _mlp_kernel(x_ref, n2_ref, w1_ref, w2_ref, nf_ref, o_ref, *, final_norm):
    x = x_ref[...]
    h2 = _rms(x, n2_ref[...]).astype(_BF16)
    hid = jnp.maximum(_dot(h2, w1_ref[...]), 0.0)
    y = x + _dot((hid * hid).astype(_BF16), w2_ref[...])
    o_ref[...] = _rms(y, nf_ref[...]) if final_norm else y


def _mlp(x1, n2, w1, w2, nf, tn, final_norm):
    n, d = x1.shape
    row = pl.BlockSpec((tn, d), lambda i: (i, 0))
    const = lambda a: pl.BlockSpec(a.shape, lambda i: (0, 0))
    return pl.pallas_call(
        functools.partial(_mlp_kernel, final_norm=final_norm),
        grid=(n // tn,),
        in_specs=[row, const(n2), const(w1), const(w2), const(nf)],
        out_specs=row,
        out_shape=jax.ShapeDtypeStruct((n, d), _F32),
        compiler_params=pltpu.CompilerParams(
            dimension_semantics=("arbitrary",), vmem_limit_bytes=VMEM_LIMIT),
        name="mlp",
    )(x1, n2, w1, w2, nf)


def _t5_bucket(rel):
    half = N_BUCKETS // 2
    max_exact = half // 2
    base = jnp.where(rel > 0, half, 0)
    n = jnp.abs(rel)
    n_f = jnp.maximum(n, 1).astype(jnp.float32)
    large = max_exact + (jnp.log(n_f / max_exact) / math.log(MAX_DISTANCE / max_exact)
                         * (half - max_exact)).astype(jnp.int32)
    large = jnp.minimum(large, half - 1)
    return base + jnp.where(n < max_exact, n, large)


def _pick_tile(n, prefs):
    for t in prefs:
        if n % t == 0:
            return t
    raise ValueError(f"no tile in {prefs} divides {n}")


def kernel(x, norm1_w, w_in, conv_a_w, a_log, dt_bias, norm_a_w, rel_bias_table, w_gate, b_gate,
           w_proj_a, w_proj_b, w_out, norm2_w, w_ff1, w_ff2, norm_final_w):
    b_, t_, d = x.shape
    depth = norm1_w.shape[0]
    n = b_ * t_
    ha, hb = H_A * DK_A, H_B * D_HEAD_B
    assert t_ % KEY_TILE == 0 and t_ % GDN_CHUNK == 0 and d % LANE == 0
    assert DK_A == DV_A == D_HEAD_B == LANE and 2 * D_IDX == LANE
    assert Q_BLOCK >= MAX_DISTANCE

    o_za = 3 * ha
    o_ba = 4 * ha
    o_aa = o_ba + H_A
    o_qb = o_aa + H_A
    o_iq = o_qb + 3 * hb
    o_ik = o_iq + H_IDX * D_IDX
    o_iw = o_ik + D_IDX

    rel = (jnp.arange(2 * Q_BLOCK, dtype=jnp.int32)[None, :] - Q_BLOCK) \
        - jnp.arange(Q_BLOCK, dtype=jnp.int32)[:, None]
    far = jnp.full((Q_BLOCK, LANE), -(Q_BLOCK + 1), jnp.int32)
    bucket = _t5_bucket(jnp.concatenate([rel, far], axis=1))
    bias = _bias_tiles(rel_bias_table.astype(_F32), bucket)

    x2d = x.reshape(n, d)
    tn = _pick_tile(n, (512, 256, 128))
    for layer in range(depth):
        wi = w_in[layer]
        zpad = lambda w: jnp.zeros((d, w), wi.dtype)
        ws = jnp.concatenate([wi[:, o_ba:o_qb], wi[:, o_iw:o_iw + H_IDX], zpad(LANE - 3 * H_A)],
                             axis=1).astype(_BF16)
        ik = wi[:, o_ik:o_iw]
        wb16 = jnp.concatenate([w_gate[layer], wi[:, 0:o_ba], wi[:, o_qb:o_ik],
                                ik, zpad(D_IDX), zpad(D_IDX), ik], axis=1).astype(_BF16)
        nw = norm1_w[layer].reshape(1, d).astype(_F32)
        pb, ps = _norm_proj(x2d, nw, wb16, ws, tn)

        col_a = 2 * d
        col_b = col_a + 4 * ha
        assert col_a % ha == 0
        lanes = lambda v, off: jnp.zeros((1, LANE), _F32).at[0, off:off + H_A].set(v.astype(_F32))
        oa = _gdn(pb, ps, conv_a_w[layer].astype(_F32), lanes(a_log[layer], H_A), lanes(dt_bias[layer], H_A),
                  norm_a_w[layer].reshape(1, DV_A).astype(_F32), b_, t_, col_a)
        ob = _dsa(pb, ps, bias, b_, t_, col_b)

        x2d = _merge(oa, ob, pb, x2d, b_gate[layer].reshape(1, 2 * d).astype(_F32),
                     w_proj_a[layer].astype(_BF16), w_proj_b[layer].astype(_BF16),
                     w_out[layer].astype(_BF16), tn)
        x2d = _mlp(x2d, norm2_w[layer].reshape(1, d).astype(_F32), w_ff1[layer].astype(_BF16),
                   w_ff2[layer].astype(_BF16), norm_final_w.reshape(1, d).astype(_F32),
                   _pick_tile(n, (256, 128)), final_norm=layer == depth - 1)
    return x2d.reshape(b_, t_, d)
```

```python
import functools
import math

import jax
import jax.numpy as jnp
import numpy as np
from jax import lax
from jax.experimental import pallas as pl
from jax.experimental.pallas import tpu as pltpu

EPS = 1e-6
H_A = 8
DK_A = 128
DV_A = 128
CONV_K = 4
GDN_CHUNK = 128
H_B = 8
D_HEAD_B = 128
H_IDX = 8
D_IDX = 64
TOPK_MAX = 256
Q_BLOCK = 128
CHUNK = 64
KEY_TILE = 512
N_BUCKETS = 32
MAX_DISTANCE = 128

LANE = 128
VMEM_LIMIT = 56 * 1024 * 1024

INT_MIN = -(2**31)
INT_MAX = 2**31 - 1
KEY_NEG_INF = -2139095041
MASK_NEG = -1e30
CODE_KEEP = -1
CODE_DROP = INT_MAX
TIE_BOUND_ALL = 2**30
LOG2E = math.log2(math.e)

_F32 = jnp.float32
_BF16 = jnp.bfloat16


def _dot(a, b):
    return jnp.dot(a, b, preferred_element_type=_F32)


def _dot_nt(a, b):
    return lax.dot_general(a, b, (((1,), (1,)), ((), ())), preferred_element_type=_F32)


def _sigmoid(x):
    return 1.0 / (1.0 + jnp.exp(-x))


def _rms(x, w):
    return x * lax.rsqrt(jnp.mean(x * x, axis=-1, keepdims=True) + EPS) * w


def _norm_proj_kernel(x_ref, nw_ref, w_ref, ws_ref, o_ref, os_ref, *, chunk):
    h = _rms(x_ref[...], nw_ref[...]).astype(_BF16)
    os_ref[...] = _dot(h, ws_ref[...])
    for c0 in range(0, w_ref.shape[1], chunk):
        o_ref[:, c0:c0 + chunk] = _dot(h, w_ref[:, c0:c0 + chunk]).astype(o_ref.dtype)


def _norm_proj(x2d, nw, w, ws, tn):
    n, d = x2d.shape
    c = w.shape[1]
    const = functools.partial(pl.BlockSpec, pipeline_mode=pl.Buffered(1))
    return pl.pallas_call(
        functools.partial(_norm_proj_kernel, chunk=_pick_tile(c, (768, 512, 256, 128))),
        grid=(n // tn,),
        in_specs=[
            pl.BlockSpec((tn, d), lambda i: (i, 0)),
            const((1, d), lambda i: (0, 0)),
            const((d, c), lambda i: (0, 0)),
            const((d, LANE), lambda i: (0, 0)),
        ],
        out_specs=[pl.BlockSpec((tn, c), lambda i: (i, 0)), pl.BlockSpec((tn, LANE), lambda i: (i, 0))],
        out_shape=[jax.ShapeDtypeStruct((n, c), _BF16), jax.ShapeDtypeStruct((n, LANE), _F32)],
        compiler_params=pltpu.CompilerParams(
            dimension_semantics=("arbitrary",), vmem_limit_bytes=VMEM_LIMIT),
        name="proj",
    )(x2d, nw, w, ws)


def _gdn_kernel(qkvz_ref, sm_ref, cw_ref, arow_ref, dtrow_ref, naw_ref, o_ref,
                xbuf, s_ref, kn_s, kb_s, qn_s, rhs_s, dec_s, qdec_s, ktt_s, eg_s, m_s, x_s):
    tt = qkvz_ref.shape[0]
    hd = H_A * DK_A

    @pl.when(pl.program_id(1) == 0)
    def _():
        xbuf[:, 0:8, :] = jnp.zeros((3, 8, hd), _F32)
        s_ref[...] = jnp.zeros_like(s_ref)

    for idx in range(3):
        xbuf[idx, 8:8 + tt, :] = qkvz_ref[:, idx * hd:(idx + 1) * hd].astype(_F32)

    sm = sm_ref[...]
    beta_full = _sigmoid(sm)
    xg = sm + dtrow_ref[...]
    softplus = jnp.maximum(xg, 0.0) + jnp.log(1.0 + jnp.exp(-jnp.abs(xg)))
    g_full = -jnp.exp(arow_ref[...]) * softplus
    row = lax.broadcasted_iota(jnp.int32, (tt, tt), 0)
    col = lax.broadcasted_iota(jnp.int32, (tt, tt), 1)
    tri = (col <= row).astype(_F32)
    gcum = jnp.dot(tri, g_full, preferred_element_type=_F32, precision=lax.Precision.HIGHEST)
    gcum_t = gcum.T
    strict = col < row
    eye = (col == row).astype(_F32)

    for h in range(H_A):
        sl = slice(h * DK_A, (h + 1) * DK_A)
        conv = []
        for idx in range(3):
            acc = None
            for j in range(CONV_K):
                term = cw_ref[j:j + 1, idx * hd + h * DK_A: idx * hd + (h + 1) * DK_A] * \
                    xbuf[idx, 8 - (CONV_K - 1) + j: 8 - (CONV_K - 1) + j + tt, sl]
                acc = term if acc is None else acc + term
            conv.append(acc * _sigmoid(acc))
        qh, kh, vh = conv
        qn = qh * lax.rsqrt(jnp.sum(qh * qh, axis=-1, keepdims=True) + EPS) * (DK_A ** -0.5)
        kn = kh * lax.rsqrt(jnp.sum(kh * kh, axis=-1, keepdims=True) + EPS)
        bcol = beta_full[:, h:h + 1]
        gcol = gcum[:, H_A + h:H_A + h + 1]
        grow = gcum_t[H_A + h:H_A + h + 1, :]
        glast = gcum[tt - 1:tt, H_A + h:H_A + h + 1]
        kb = kn * bcol
        eg = jnp.exp(gcol)
        rhs_s[h, :, 0:DV_A] = vh * bcol
        rhs_s[h, :, DV_A:DV_A + DK_A] = kb * eg
        dec_s[h] = jnp.where(strict, jnp.exp(jnp.where(strict, gcol - grow, 0.0)), 0.0)
        qdec_s[h] = (qn * eg).astype(_BF16)
        ktt_s[h] = (kn * jnp.exp(glast - gcol)).T.astype(_BF16)
        kn_s[h] = kn.astype(_BF16)
        kb_s[h] = kb.astype(_BF16)
        qn_s[h] = qn.astype(_BF16)
        eg_s[h] = jnp.broadcast_to(jnp.exp(glast), (DK_A, DV_A))

    for idx in range(3):
        xbuf[idx, 0:8, :] = xbuf[idx, tt:tt + 8, :]

    n_sq = int(math.log2(tt))

    heads = range(H_A)
    for h in heads:
        m = -(_dot_nt(kb_s[h], kn_s[h]) * dec_s[h])
        m_s[h] = m
        x_s[h] = m
    for _ in range(n_sq - 1):
        for h in heads:
            mb = m_s[h].astype(_BF16)
            m = _dot(mb, mb)
            m_s[h] = m
            x_s[h] = x_s[h] + m + _dot(x_s[h].astype(_BF16), m.astype(_BF16))
    for h in heads:
        rhs = rhs_s[h]
        rhs_s[h] = rhs + _dot(x_s[h].astype(_BF16), rhs.astype(_BF16))
    for h in heads:
        m_s[h] = _dot_nt(qn_s[h], kn_s[h]) * (dec_s[h] + eye)
    for h in heads:
        sb = s_ref[h].astype(_BF16)
        vnew = rhs_s[h, :, 0:DV_A] - _dot(rhs_s[h, :, DV_A:DV_A + DK_A].astype(_BF16), sb)
        vb = vnew.astype(_BF16)
        o = _dot(qdec_s[h], sb) + _dot(m_s[h].astype(_BF16), vb)
        s_ref[h] = s_ref[h] * eg_s[h] + _dot(ktt_s[h], vb)
        rhs_s[h, :, 0:DV_A] = o

    naw = naw_ref[...]
    for h in range(H_A):
        sl = slice(h * DV_A, (h + 1) * DV_A)
        o = rhs_s[h, :, 0:DV_A]
        z = qkvz_ref[:, 3 * hd + h * DV_A:3 * hd + (h + 1) * DV_A].astype(_F32)
        on = o * lax.rsqrt(jnp.mean(o * o, axis=-1, keepdims=True) + EPS) * naw
        o_ref[:, sl] = (on * (z * _sigmoid(z))).astype(o_ref.dtype)


def _gdn(pb, ps, cw, arow, dtrow, naw, b_, t_, col0):
    tt = GDN_CHUNK
    hd = H_A * DK_A
    nt = t_ // tt
    assert col0 % (4 * hd) == 0
    cb = col0 // (4 * hd)
    const = lambda shape: pl.BlockSpec(shape, lambda b, t: (0, 0))
    return pl.pallas_call(
        _gdn_kernel,
        grid=(b_, nt),
        in_specs=[pl.BlockSpec((tt, 4 * hd), lambda b, t: (b * nt + t, cb)),
                  pl.BlockSpec((tt, LANE), lambda b, t: (b * nt + t, 0)),
                  const((CONV_K, 3 * hd)), const((1, LANE)), const((1, LANE)), const((1, DV_A))],
        out_specs=pl.BlockSpec((tt, hd), lambda b, t: (b * nt + t, 0)),
        out_shape=jax.ShapeDtypeStruct((b_ * t_, hd), _BF16),
        scratch_shapes=[
            pltpu.VMEM((3, tt + 8, hd), _F32),
            pltpu.VMEM((H_A, DK_A, DV_A), _F32),
            pltpu.VMEM((H_A, tt, DK_A), _BF16),
            pltpu.VMEM((H_A, tt, DK_A), _BF16),
            pltpu.VMEM((H_A, tt, DK_A), _BF16),
            pltpu.VMEM((H_A, tt, DV_A + DK_A), _F32),
            pltpu.VMEM((H_A, tt, tt), _F32),
            pltpu.VMEM((H_A, tt, DK_A), _BF16),
            pltpu.VMEM((H_A, DK_A, tt), _BF16),
            pltpu.VMEM((H_A, DK_A, DV_A), _F32),
            pltpu.VMEM((H_A, tt, tt), _F32),
            pltpu.VMEM((H_A, tt, tt), _F32),
        ],
        compiler_params=pltpu.CompilerParams(
            dimension_semantics=("arbitrary", "arbitrary"), vmem_limit_bytes=VMEM_LIMIT),
        name="gdn",
    )(pb, ps, cw, arow, dtrow, naw)


def _bias_kernel(tab_ref, bucket_ref, o_ref):
    bucket = bucket_ref[...]
    nq, w = bucket.shape
    for h in range(H_B):
        acc = jnp.zeros((nq, w), _F32)
        for b in range(N_BUCKETS):
            acc = acc + jnp.where(bucket == b, tab_ref[b, h], 0.0)
        far = acc[:, 2 * LANE:3 * LANE]
        o_ref[h] = (acc[:, 0:2 * LANE] - jnp.concatenate([far, far], axis=1)) * LOG2E


def _bias_tiles(rel_table, bucket):
    return pl.pallas_call(
        _bias_kernel,
        in_specs=[pl.BlockSpec(memory_space=pltpu.SMEM),
                  pl.BlockSpec(bucket.shape, lambda: (0, 0))],
        out_specs=pl.BlockSpec((H_B, Q_BLOCK, 2 * LANE), lambda: (0, 0, 0)),
        out_shape=jax.ShapeDtypeStruct((H_B, Q_BLOCK, 2 * LANE), _F32),
        name="rel_bias_tiles",
    )(rel_table, bucket)


def _dsa_kernel(q_ref, k_ref, v_ref, iq_ref, ik_ref, sm_ref, bias_ref, o_ref,
                key_s, qs_s, iwb_s, m_s, l_s, acc_s, s_s, si_s, sf_s, *, topk):
    nq = q_ref.shape[0]
    st_w = KEY_TILE
    assert nq == st_w
    ng = st_w // LANE
    qb = pl.program_id(1)
    n_st = qb + 1
    hd = D_HEAD_B
    scale = D_HEAD_B ** -0.5 * LOG2E
    f_topk = float(topk)

    lane_j = lax.broadcasted_iota(jnp.int32, (nq, LANE), 1)
    row_i = lax.broadcasted_iota(jnp.int32, (nq, LANE), 0)

    def grp(g):
        return slice(g * LANE, (g + 1) * LANE)

    iw = sm_ref[...]
    for h in range(H_B):
        sl = slice(h * hd, (h + 1) * hd)
        qs_s[h] = (q_ref[:, sl].astype(_F32) * scale).astype(_BF16)
        iwb_s[h] = jnp.broadcast_to(iw[:, 2 * H_A + h:2 * H_A + h + 1], (nq, LANE))
        m_s[h] = jnp.full((nq, LANE), MASK_NEG, _F32)
        l_s[h] = jnp.zeros((nq, LANE), _F32)
        acc_s[h] = jnp.zeros((nq, hd), _F32)

    def scores(st, diag):
        ikt = ik_ref[pl.ds(pl.multiple_of(st * st_w, st_w), st_w), :]
        acc = [jnp.zeros((nq, LANE), _F32) for _ in range(ng)]
        for p in range(H_IDX // 2):
            xq = iq_ref[:, p * LANE:(p + 1) * LANE]
            for half in range(2):
                h = 2 * p + half
                s = jnp.maximum(_dot_nt(xq, ikt[:, half * LANE:(half + 1) * LANE]), 0.0)
                w = iwb_s[h]
                for g in range(ng):
                    acc[g] = acc[g] + w * s[:, grp(g)]
        for g in range(ng):
            a = acc[g]
            if diag:
                a = jnp.where((g * LANE + lane_j) // CHUNK <= row_i // CHUNK, a, -jnp.inf)
            bits = pltpu.bitcast(a, jnp.int32)
            key_s[st, g] = bits ^ ((bits >> 31) & INT_MAX)

    def p1(st, carry):
        scores(st, False)
        return carry

    lax.fori_loop(0, qb, p1, 0)
    scores(qb, True)

    ones_b = jnp.ones((LANE, LANE), _BF16)
    assert key_s.shape[0] * ng <= 256

    T_, CAND_, MIDX_ = 0, 1, 2
    ACC_, CNT_ = 0, 1
    rb = 64
    lane_rb = lax.broadcasted_iota(jnp.int32, (rb, LANE), 1)

    def count(pred_fn):
        sf_s[ACC_] = jnp.zeros((nq, LANE), _F32)

        def body(st, carry):
            for r0 in range(0, nq, rb):
                rs = slice(r0, r0 + rb)
                acc = sf_s[ACC_, rs, :]
                for g in range(ng):
                    acc = acc + jnp.where(pred_fn(key_s[st, g, rs, :], rs), 1.0, 0.0)
                sf_s[ACC_, rs, :] = acc
            return carry

        lax.fori_loop(0, n_st, body, 0)
        return _dot(sf_s[ACC_].astype(_BF16), ones_b)

    c0 = count(lambda k, rs: k >= 0)
    ok0 = c0 >= f_topk
    si_s[T_] = jnp.where(ok0, 0, INT_MIN)
    sf_s[CNT_] = jnp.where(ok0, c0, (n_st * st_w).astype(_F32))

    def bit_body(i, carry):
        t_old = si_s[T_]
        cand = t_old + jnp.left_shift(jnp.int32(1), 30 - i)
        si_s[CAND_] = cand
        c = count(lambda k, rs: k >= si_s[CAND_, rs, :])
        ok = c >= f_topk
        si_s[T_] = jnp.where(ok, cand, t_old)
        sf_s[CNT_] = jnp.where(ok, c, sf_s[CNT_])
        return carry

    lax.fori_loop(0, 31, bit_body, 0)

    def encode(st, carry):
        for r0 in range(0, nq, rb):
            rs = slice(r0, r0 + rb)
            t = si_s[T_, rs, :]
            for g in range(ng):
                k = key_s[st, g, rs, :]
                pos = st * st_w + g * LANE + lane_rb
                key_s[st, g, rs, :] = jnp.where(k > t, CODE_KEEP, jnp.where(k == t, pos, CODE_DROP))
        return carry

    lax.fori_loop(0, n_st, encode, 0)
    finite_thr = si_s[T_] != KEY_NEG_INF
    si_s[MIDX_] = jnp.where(finite_thr, TIE_BOUND_ALL, 0)

    @pl.when(jnp.max(jnp.where(finite_thr, sf_s[CNT_], 0.0)) > f_topk)
    def _():
        n_bits = max(1, int(math.ceil(math.log2(key_s.shape[0] * st_w))))
        si_s[MIDX_] = jnp.zeros((nq, LANE), jnp.int32)

        def idx_body(i, carry):
            mp = si_s[MIDX_]
            cand = mp + jnp.left_shift(jnp.int32(1), n_bits - 1 - i)
            si_s[CAND_] = cand
            c = count(lambda code, rs: code < si_s[CAND_, rs, :])
            si_s[MIDX_] = jnp.where(c < f_topk, cand, mp)
            return carry

        lax.fori_loop(0, n_bits, idx_body, 0)
        si_s[MIDX_] = jnp.where(si_s[T_] != KEY_NEG_INF, si_s[MIDX_] + 1, 0)

    def p2(st, carry):
        for r0 in range(0, nq, rb):
            rs = slice(r0, r0 + rb)
            bound = si_s[MIDX_, rs, :]
            for g in range(ng):
                mb = jnp.where(key_s[st, g, rs, :] < bound, 0.0, MASK_NEG)
                key_s[st, g, rs, :] = pltpu.bitcast(mb, jnp.int32)
        return carry

    lax.fori_loop(0, n_st, p2, 0)

    def attend(st, mode):
        off = pl.multiple_of(st * st_w, st_w)
        for h in range(H_B):
            sl = slice(h * hd, (h + 1) * hd)
            mask = jnp.concatenate([pltpu.bitcast(key_s[st, g], _F32) for g in range(ng)], axis=1)
            s = _dot_nt(qs_s[h], k_ref[pl.ds(off, st_w), sl]) + mask
            if mode:
                s_s[...] = s
                b_prev = bias_ref[h, :, 0:LANE]
                if mode == 1:
                    s_s[0:LANE, (ng - 1) * LANE:ng * LANE] += b_prev
                else:
                    b_diag = bias_ref[h, :, LANE:2 * LANE]
                    for rt in range(ng):
                        s_s[grp(rt), grp(rt)] += b_diag
                        if rt:
                            s_s[grp(rt), grp(rt - 1)] += b_prev
                s = s_s[...]
            m_old = m_s[h]
            gmax = s[:, grp(0)]
            for g in range(1, ng):
                gmax = jnp.maximum(gmax, s[:, grp(g)])
            m_new = jnp.maximum(m_old, jnp.max(gmax, axis=1, keepdims=True))
            alpha = jnp.exp2(m_old - m_new)
            pieces = [jnp.exp2(s[:, grp(g)] - m_new) for g in range(ng)]
            psum = pieces[0]
            for g in range(1, ng):
                psum = psum + pieces[g]
            p = jnp.concatenate(pieces, axis=1).astype(_BF16)
            m_s[h] = m_new
            l_s[h] = alpha * l_s[h] + psum
            acc_s[h] = alpha * acc_s[h] + _dot(p, v_ref[pl.ds(off, st_w), sl])

    def p3(st, carry):
        attend(st, 0)
        return carry

    lax.fori_loop(0, qb - 1, p3, 0)

    @pl.when(qb > 0)
    def _():
        attend(qb - 1, 1)

    attend(qb, 2)

    for h in range(H_B):
        sl = slice(h * hd, (h + 1) * hd)
        l_fin = jnp.sum(l_s[h], axis=1, keepdims=True)
        o_ref[:, sl] = (acc_s[h] / l_fin).astype(o_ref.dtype)


def _dsa(pb, ps, bias, b_, t_, col0):
    nq = KEY_TILE
    nb = t_ // nq
    hd = H_B * D_HEAD_B
    topk = min(TOPK_MAX, t_ // 4)
    iq_w = H_IDX * D_IDX
    ik_w = 2 * LANE
    assert col0 % hd == 0 and (col0 + 3 * hd) % iq_w == 0 and (col0 + 3 * hd + iq_w) % ik_w == 0
    cq = col0 // hd
    resident = functools.partial(pl.BlockSpec, pipeline_mode=pl.Buffered(1))
    return pl.pallas_call(
        functools.partial(_dsa_kernel, topk=topk),
        grid=(b_, nb),
        in_specs=[
            pl.BlockSpec((nq, hd), lambda b, i: (b * nb + i, cq)),
            resident((t_, hd), lambda b, i: (b, cq + 1)),
            resident((t_, hd), lambda b, i: (b, cq + 2)),
            pl.BlockSpec((nq, iq_w), lambda b, i: (b * nb + i, (col0 + 3 * hd) // iq_w)),
            resident((t_, ik_w), lambda b, i: (b, (col0 + 3 * hd + iq_w) // ik_w)),
            pl.BlockSpec((nq, LANE), lambda b, i: (b * nb + i, 0)),
            resident((H_B, Q_BLOCK, 2 * LANE), lambda b, i: (0, 0, 0)),
        ],
        out_specs=pl.BlockSpec((nq, hd), lambda b, i: (b * nb + i, 0)),
        out_shape=jax.ShapeDtypeStruct((b_ * t_, hd), _BF16),
        scratch_shapes=[
            pltpu.VMEM((nb, KEY_TILE // LANE, nq, LANE), jnp.int32),
            pltpu.VMEM((H_B, nq, D_HEAD_B), _BF16),
            pltpu.VMEM((H_B, nq, LANE), _F32),
            pltpu.VMEM((H_B, nq, LANE), _F32),
            pltpu.VMEM((H_B, nq, LANE), _F32),
            pltpu.VMEM((H_B, nq, D_HEAD_B), _F32),
            pltpu.VMEM((nq, KEY_TILE), _F32),
            pltpu.VMEM((3, nq, LANE), jnp.int32),
            pltpu.VMEM((2, nq, LANE), _F32),
        ],
        compiler_params=pltpu.CompilerParams(
            dimension_semantics=("arbitrary", "arbitrary"), vmem_limit_bytes=VMEM_LIMIT),
        name="dsa",
    )(pb, pb, pb, pb, pb, ps, bias)


def _merge_kernel(oa_ref, ob_ref, g_ref, x_ref, bg_ref, wa_ref, wb_ref, wo_ref, o_ref):
    d = x_ref.shape[1]
    a = _dot(oa_ref[...], wa_ref[...])
    b = _dot(ob_ref[...], wb_ref[...])
    g = _sigmoid(g_ref[...].astype(_F32) + bg_ref[...])
    merged = g[:, 0:d] * a + g[:, d:2 * d] * b
    o_ref[...] = x_ref[...] + _dot(merged.astype(_BF16), wo_ref[...])


def _merge(oa, ob, pb, x2d, bg, wa, wb, wo, tn, col_g):
    n, d = x2d.shape
    row = lambda w: pl.BlockSpec((tn, w), lambda i: (i, 0))
    const = lambda a: pl.BlockSpec(a.shape, lambda i: (0, 0))
    return pl.pallas_call(
        _merge_kernel,
        grid=(n // tn,),
        in_specs=[row(oa.shape[1]), row(ob.shape[1]),
                  pl.BlockSpec((tn, 2 * d), lambda i: (i, col_g // (2 * d))), row(d),
                  const(bg), const(wa), const(wb), const(wo)],
        out_specs=row(d),
        out_shape=jax.ShapeDtypeStruct((n, d), _F32),
        compiler_params=pltpu.CompilerParams(
            dimension_semantics=("arbitrary",), vmem_limit_bytes=VMEM_LIMIT),
        name="merge",
    )(oa, ob, pb, x2d, bg, wa, wb, wo)


def _mlp_kernel(x_ref, n2_ref, w1_ref, w2_ref, nf_ref, o_ref, *, final_norm):
    x = x_ref[...]
    h2 = _rms(x, n2_ref[...]).astype(_BF16)
    hid = jnp.maximum(_dot(h2, w1_ref[...]), 0.0)
    y = x + _dot((hid * hid).astype(_BF16), w2_ref[...])
    o_ref[...] = _rms(y, nf_ref[...]) if final_norm else y


def _mlp(x1, n2, w1, w2, nf, tn, final_norm):
    n, d = x1.shape
    row = pl.BlockSpec((tn, d), lambda i: (i, 0))
    const = lambda a: pl.BlockSpec(a.shape, lambda i: (0, 0))
    return pl.pallas_call(
        functools.partial(_mlp_kernel, final_norm=final_norm),
        grid=(n // tn,),
        in_specs=[row, const(n2), const(w1), const(w2), const(nf)],
        out_specs=row,
        out_shape=jax.ShapeDtypeStruct((n, d), _F32),
        compiler_params=pltpu.CompilerParams(
            dimension_semantics=("arbitrary",), vmem_limit_bytes=VMEM_LIMIT),
        name="mlp",
    )(x1, n2, w1, w2, nf)


def _t5_bucket(rel):
    half = N_BUCKETS // 2
    max_exact = half // 2
    base = jnp.where(rel > 0, half, 0)
    n = jnp.abs(rel)
    n_f = jnp.maximum(n, 1).astype(jnp.float32)
    large = max_exact + (jnp.log(n_f / max_exact) / math.log(MAX_DISTANCE / max_exact)
                         * (half - max_exact)).astype(jnp.int32)
    large = jnp.minimum(large, half - 1)
    return base + jnp.where(n < max_exact, n, large)


def _pick_tile(n, prefs):
    for t in prefs:
        if n % t == 0:
            return t
    raise ValueError(f"no tile in {prefs} divides {n}")


def kernel(x, norm1_w, w_in, conv_a_w, a_log, dt_bias, norm_a_w, rel_bias_table, w_gate, b_gate,
           w_proj_a, w_proj_b, w_out, norm2_w, w_ff1, w_ff2, norm_final_w):
    b_, t_, d = x.shape
    depth = norm1_w.shape[0]
    n = b_ * t_
    ha, hb = H_A * DK_A, H_B * D_HEAD_B
    assert t_ % KEY_TILE == 0 and t_ % GDN_CHUNK == 0 and d % LANE == 0
    assert DK_A == DV_A == D_HEAD_B == LANE and 2 * D_IDX == LANE
    assert Q_BLOCK >= MAX_DISTANCE

    o_za = 3 * ha
    o_ba = 4 * ha
    o_aa = o_ba + H_A
    o_qb = o_aa + H_A
    o_iq = o_qb + 3 * hb
    o_ik = o_iq + H_IDX * D_IDX
    o_iw = o_ik + D_IDX

    rel = (jnp.arange(2 * Q_BLOCK, dtype=jnp.int32)[None, :] - Q_BLOCK) \
        - jnp.arange(Q_BLOCK, dtype=jnp.int32)[:, None]
    far = jnp.full((Q_BLOCK, LANE), -(Q_BLOCK + 1), jnp.int32)
    bucket = _t5_bucket(jnp.concatenate([rel, far], axis=1))
    bias = _bias_tiles(rel_bias_table.astype(_F32), bucket)

    x2d = x.reshape(n, d)
    tn = _pick_tile(n, (512, 256, 128))
    for layer in range(depth):
        wi = w_in[layer]
        zpad = lambda w: jnp.zeros((d, w), wi.dtype)
        ws = jnp.concatenate([wi[:, o_ba:o_qb], wi[:, o_iw:o_iw + H_IDX], zpad(LANE - 3 * H_A)],
                             axis=1).astype(_BF16)
        ik = wi[:, o_ik:o_iw]
        wb16 = jnp.concatenate([wi[:, 0:o_ba], w_gate[layer], wi[:, o_qb:o_ik],
                                ik, zpad(D_IDX), zpad(D_IDX), ik], axis=1).astype(_BF16)
        nw = norm1_w[layer].reshape(1, d).astype(_F32)
        pb, ps = _norm_proj(x2d, nw, wb16, ws, tn)

        col_a = 0
        col_g = 4 * ha
        col_b = col_g + 2 * d
        assert col_g % (2 * d) == 0
        lanes = lambda v, off: jnp.zeros((1, LANE), _F32).at[0, off:off + H_A].set(v.astype(_F32))
        oa = _gdn(pb, ps, conv_a_w[layer].astype(_F32), lanes(a_log[layer], H_A), lanes(dt_bias[layer], H_A),
                  norm_a_w[layer].reshape(1, DV_A).astype(_F32), b_, t_, col_a)
        ob = _dsa(pb, ps, bias, b_, t_, col_b)

        x2d = _merge(oa, ob, pb, x2d, b_gate[layer].reshape(1, 2 * d).astype(_F32),
                     w_proj_a[layer].astype(_BF16), w_proj_b[layer].astype(_BF16),
                     w_out[layer].astype(_BF16), tn, col_g)
        x2d = _mlp(x2d, norm2_w[layer].reshape(1, d).astype(_F32), w_ff1[layer].astype(_BF16),
                   w_ff2[layer].astype(_BF16), norm_final_w.reshape(1, d).astype(_F32),
                   _pick_tile(n, (256, 128)), final_norm=layer == depth - 1)
    return x2d.reshape(b_, t_, d)
```

```python
import functools
import math

import jax
import jax.numpy as jnp
import numpy as np
from jax import lax
from jax.experimental import pallas as pl
from jax.experimental.pallas import tpu as pltpu

EPS = 1e-6
H_A = 8
DK_A = 128
DV_A = 128
CONV_K = 4
GDN_CHUNK = 128
GDN_STEP_CHUNKS = 2
H_B = 8
D_HEAD_B = 128
H_IDX = 8
D_IDX = 64
TOPK_MAX = 256
Q_BLOCK = 128
CHUNK = 64
KEY_TILE = 512
N_BUCKETS = 32
MAX_DISTANCE = 128

LANE = 128
VMEM_LIMIT = 56 * 1024 * 1024

INT_MIN = -(2**31)
INT_MAX = 2**31 - 1
KEY_NEG_INF = -2139095041
MASK_NEG = -1e30
CODE_KEEP = -1
CODE_DROP = INT_MAX
TIE_BOUND_ALL = 2**30
LOG2E = math.log2(math.e)

_F32 = jnp.float32
_BF16 = jnp.bfloat16


def _dot(a, b):
    return jnp.dot(a, b, preferred_element_type=_F32)


def _dot_nt(a, b):
    return lax.dot_general(a, b, (((1,), (1,)), ((), ())), preferred_element_type=_F32)


def _sigmoid(x):
    return 1.0 / (1.0 + jnp.exp(-x))


def _rms(x, w):
    return x * lax.rsqrt(jnp.mean(x * x, axis=-1, keepdims=True) + EPS) * w


def _norm_proj_kernel(x_ref, nw_ref, w_ref, ws_ref, o_ref, os_ref, *, chunk):
    h = _rms(x_ref[...], nw_ref[...]).astype(_BF16)
    os_ref[...] = _dot(h, ws_ref[...])
    for c0 in range(0, w_ref.shape[1], chunk):
        o_ref[:, c0:c0 + chunk] = _dot(h, w_ref[:, c0:c0 + chunk]).astype(o_ref.dtype)


def _norm_proj(x2d, nw, w, ws, tn):
    n, d = x2d.shape
    c = w.shape[1]
    const = functools.partial(pl.BlockSpec, pipeline_mode=pl.Buffered(1))
    return pl.pallas_call(
        functools.partial(_norm_proj_kernel, chunk=_pick_tile(c, (768, 512, 256, 128))),
        grid=(n // tn,),
        in_specs=[
            pl.BlockSpec((tn, d), lambda i: (i, 0)),
            const((1, d), lambda i: (0, 0)),
            const((d, c), lambda i: (0, 0)),
            const((d, LANE), lambda i: (0, 0)),
        ],
        out_specs=[pl.BlockSpec((tn, c), lambda i: (i, 0)), pl.BlockSpec((tn, LANE), lambda i: (i, 0))],
        out_shape=[jax.ShapeDtypeStruct((n, c), _BF16), jax.ShapeDtypeStruct((n, LANE), _F32)],
        compiler_params=pltpu.CompilerParams(
            dimension_semantics=("arbitrary",), vmem_limit_bytes=VMEM_LIMIT),
        name="proj",
    )(x2d, nw, w, ws)


def _gdn_kernel(qkvz_ref, sm_ref, cw_ref, arow_ref, dtrow_ref, naw_ref, o_ref,
                xbuf, s_ref, kn_s, kb_s, qn_s, rhs_s, dec_s, qdec_s, ktt_s, eg_s, m_s, x_s):
    tt = qkvz_ref.shape[0]
    ct = GDN_CHUNK
    nc = tt // ct
    hd = H_A * DK_A

    @pl.when(pl.program_id(1) == 0)
    def _():
        xbuf[:, 0:8, :] = jnp.zeros((3, 8, hd), _F32)
        s_ref[...] = jnp.zeros_like(s_ref)

    for idx in range(3):
        xbuf[idx, 8:8 + tt, :] = qkvz_ref[:, idx * hd:(idx + 1) * hd].astype(_F32)

    row = lax.broadcasted_iota(jnp.int32, (ct, ct), 0)
    col = lax.broadcasted_iota(jnp.int32, (ct, ct), 1)
    tri = (col <= row).astype(_F32)
    strict = col < row
    eye = (col == row).astype(_F32)
    n_sq = int(math.log2(ct))
    heads = range(H_A)

    def prepare(c):
        r0 = c * ct
        sm = sm_ref[r0:r0 + ct, :]
        beta_full = _sigmoid(sm)
        xg = sm + dtrow_ref[...]
        softplus = jnp.maximum(xg, 0.0) + jnp.log(1.0 + jnp.exp(-jnp.abs(xg)))
        g_full = -jnp.exp(arow_ref[...]) * softplus
        gcum = jnp.dot(tri, g_full, preferred_element_type=_F32, precision=lax.Precision.HIGHEST)
        gcum_t = gcum.T
        for h in heads:
            sl = slice(h * DK_A, (h + 1) * DK_A)
            conv = []
            for idx in range(3):
                acc = None
                for j in range(CONV_K):
                    lo = r0 + 8 - (CONV_K - 1) + j
                    term = cw_ref[j:j + 1, idx * hd + h * DK_A: idx * hd + (h + 1) * DK_A] * xbuf[idx, lo:lo + ct, sl]
                    acc = term if acc is None else acc + term
                conv.append(acc * _sigmoid(acc))
            qh, kh, vh = conv
            qn = qh * lax.rsqrt(jnp.sum(qh * qh, axis=-1, keepdims=True) + EPS) * (DK_A ** -0.5)
            kn = kh * lax.rsqrt(jnp.sum(kh * kh, axis=-1, keepdims=True) + EPS)
            bcol = beta_full[:, h:h + 1]
            gcol = gcum[:, H_A + h:H_A + h + 1]
            grow = gcum_t[H_A + h:H_A + h + 1, :]
            glast = gcum[ct - 1:ct, H_A + h:H_A + h + 1]
            kb = kn * bcol
            eg = jnp.exp(gcol)
            rhs_s[c, h, :, 0:DV_A] = vh * bcol
            rhs_s[c, h, :, DV_A:DV_A + DK_A] = kb * eg
            dec_s[c, h] = jnp.where(strict, jnp.exp(jnp.where(strict, gcol - grow, 0.0)), 0.0)
            qdec_s[c, h] = (qn * eg).astype(_BF16)
            ktt_s[c, h] = (kn * jnp.exp(glast - gcol)).T.astype(_BF16)
            kn_s[c, h] = kn.astype(_BF16)
            kb_s[c, h] = kb.astype(_BF16)
            qn_s[c, h] = qn.astype(_BF16)
            eg_s[c, h] = jnp.broadcast_to(jnp.exp(glast), (DK_A, DV_A))

    def solve(c):
        for h in heads:
            m = -(_dot_nt(kb_s[c, h], kn_s[c, h]) * dec_s[c, h])
            m_s[c, h] = m
            x_s[c, h] = m
        for _ in range(n_sq - 1):
            for h in heads:
                mb = m_s[c, h].astype(_BF16)
                m = _dot(mb, mb)
                m_s[c, h] = m
                x_s[c, h] = x_s[c, h] + m + _dot(x_s[c, h].astype(_BF16), m.astype(_BF16))
        for h in heads:
            rhs = rhs_s[c, h]
            rhs_s[c, h] = rhs + _dot(x_s[c, h].astype(_BF16), rhs.astype(_BF16))
        for h in heads:
            m_s[c, h] = _dot_nt(qn_s[c, h], kn_s[c, h]) * (dec_s[c, h] + eye)

    def recur(c):
        for h in heads:
            sb = s_ref[h].astype(_BF16)
            vnew = rhs_s[c, h, :, 0:DV_A] - _dot(rhs_s[c, h, :, DV_A:DV_A + DK_A].astype(_BF16), sb)
            vb = vnew.astype(_BF16)
            o = _dot(qdec_s[c, h], sb) + _dot(m_s[c, h].astype(_BF16), vb)
            s_ref[h] = s_ref[h] * eg_s[c, h] + _dot(ktt_s[c, h], vb)
            rhs_s[c, h, :, 0:DV_A] = o

    def finish(c):
        r0 = c * ct
        naw = naw_ref[...]
        for h in heads:
            sl = slice(h * DV_A, (h + 1) * DV_A)
            o = rhs_s[c, h, :, 0:DV_A]
            z = qkvz_ref[r0:r0 + ct, 3 * hd + h * DV_A:3 * hd + (h + 1) * DV_A].astype(_F32)
            on = o * lax.rsqrt(jnp.mean(o * o, axis=-1, keepdims=True) + EPS) * naw
            o_ref[r0:r0 + ct, sl] = (on * (z * _sigmoid(z))).astype(o_ref.dtype)

    for c in range(nc):
        prepare(c)
    for idx in range(3):
        xbuf[idx, 0:8, :] = xbuf[idx, tt:tt + 8, :]
    for c in range(nc):
        solve(c)
    for c in range(nc):
        recur(c)
    for c in range(nc):
        finish(c)


def _gdn(pb, ps, cw, arow, dtrow, naw, b_, t_, col0):
    ct = GDN_CHUNK
    nc = GDN_STEP_CHUNKS if t_ % (GDN_STEP_CHUNKS * ct) == 0 else 1
    tt = nc * ct
    hd = H_A * DK_A
    nt = t_ // tt
    assert col0 % (4 * hd) == 0
    cb = col0 // (4 * hd)
    const = lambda shape: pl.BlockSpec(shape, lambda b, t: (0, 0))
    return pl.pallas_call(
        _gdn_kernel,
        grid=(b_, nt),
        in_specs=[pl.BlockSpec((tt, 4 * hd), lambda b, t: (b * nt + t, cb)),
                  pl.BlockSpec((tt, LANE), lambda b, t: (b * nt + t, 0)),
                  const((CONV_K, 3 * hd)), const((1, LANE)), const((1, LANE)), const((1, DV_A))],
        out_specs=pl.BlockSpec((tt, hd), lambda b, t: (b * nt + t, 0)),
        out_shape=jax.ShapeDtypeStruct((b_ * t_, hd), _BF16),
        scratch_shapes=[
            pltpu.VMEM((3, tt + 8, hd), _F32),
            pltpu.VMEM((H_A, DK_A, DV_A), _F32),
            pltpu.VMEM((nc, H_A, ct, DK_A), _BF16),
            pltpu.VMEM((nc, H_A, ct, DK_A), _BF16),
            pltpu.VMEM((nc, H_A, ct, DK_A), _BF16),
            pltpu.VMEM((nc, H_A, ct, DV_A + DK_A), _F32),
            pltpu.VMEM((nc, H_A, ct, ct), _F32),
            pltpu.VMEM((nc, H_A, ct, DK_A), _BF16),
            pltpu.VMEM((nc, H_A, DK_A, ct), _BF16),
            pltpu.VMEM((nc, H_A, DK_A, DV_A), _F32),
            pltpu.VMEM((nc, H_A, ct, ct), _F32),
            pltpu.VMEM((nc, H_A, ct, ct), _F32),
        ],
        compiler_params=pltpu.CompilerParams(
            dimension_semantics=("arbitrary", "arbitrary"), vmem_limit_bytes=VMEM_LIMIT),
        name="gdn",
    )(pb, ps, cw, arow, dtrow, naw)


def _bias_kernel(tab_ref, bucket_ref, o_ref):
    bucket = bucket_ref[...]
    nq, w = bucket.shape
    for h in range(H_B):
        acc = jnp.zeros((nq, w), _F32)
        for b in range(N_BUCKETS):
            acc = acc + jnp.where(bucket == b, tab_ref[b, h], 0.0)
        far = acc[:, 2 * LANE:3 * LANE]
        o_ref[h] = (acc[:, 0:2 * LANE] - jnp.concatenate([far, far], axis=1)) * LOG2E


def _bias_tiles(rel_table, bucket):
    return pl.pallas_call(
        _bias_kernel,
        in_specs=[pl.BlockSpec(memory_space=pltpu.SMEM),
                  pl.BlockSpec(bucket.shape, lambda: (0, 0))],
        out_specs=pl.BlockSpec((H_B, Q_BLOCK, 2 * LANE), lambda: (0, 0, 0)),
        out_shape=jax.ShapeDtypeStruct((H_B, Q_BLOCK, 2 * LANE), _F32),
        name="rel_bias_tiles",
    )(rel_table, bucket)


def _dsa_kernel(q_ref, k_ref, v_ref, iq_ref, ik_ref, sm_ref, bias_ref, o_ref,
                key_s, qs_s, iwb_s, m_s, l_s, acc_s, s_s, si_s, sf_s, *, topk):
    nq = q_ref.shape[0]
    st_w = KEY_TILE
    assert nq == st_w
    ng = st_w // LANE
    qb = pl.program_id(1)
    n_st = qb + 1
    hd = D_HEAD_B
    scale = D_HEAD_B ** -0.5 * LOG2E
    f_topk = float(topk)

    lane_j = lax.broadcasted_iota(jnp.int32, (nq, LANE), 1)
    row_i = lax.broadcasted_iota(jnp.int32, (nq, LANE), 0)

    def grp(g):
        return slice(g * LANE, (g + 1) * LANE)

    iw = sm_ref[...]
    for h in range(H_B):
        sl = slice(h * hd, (h + 1) * hd)
        qs_s[h] = (q_ref[:, sl].astype(_F32) * scale).astype(_BF16)
        iwb_s[h] = jnp.broadcast_to(iw[:, 2 * H_A + h:2 * H_A + h + 1], (nq, LANE))
        m_s[h] = jnp.full((nq, LANE), MASK_NEG, _F32)
        l_s[h] = jnp.zeros((nq, LANE), _F32)
        acc_s[h] = jnp.zeros((nq, hd), _F32)

    def scores(st, diag):
        ikt = ik_ref[pl.ds(pl.multiple_of(st * st_w, st_w), st_w), :]
        acc = [jnp.zeros((nq, LANE), _F32) for _ in range(ng)]
        for p in range(H_IDX // 2):
            xq = iq_ref[:, p * LANE:(p + 1) * LANE]
            for half in range(2):
                h = 2 * p + half
                s = jnp.maximum(_dot_nt(xq, ikt[:, half * LANE:(half + 1) * LANE]), 0.0)
                w = iwb_s[h]
                for g in range(ng):
                    acc[g] = acc[g] + w * s[:, grp(g)]
        for g in range(ng):
            a = acc[g]
            if diag:
                a = jnp.where((g * LANE + lane_j) // CHUNK <= row_i // CHUNK, a, -jnp.inf)
            bits = pltpu.bitcast(a, jnp.int32)
            key_s[st, g] = bits ^ ((bits >> 31) & INT_MAX)

    def p1(st, carry):
        scores(st, False)
        return carry

    lax.fori_loop(0, qb, p1, 0)
    scores(qb, True)

    ones_b = jnp.ones((LANE, LANE), _BF16)
    assert key_s.shape[0] * ng <= 256

    T_, CAND_, MIDX_ = 0, 1, 2
    ACC_, CNT_ = 0, 1
    rb = 64
    lane_rb = lax.broadcasted_iota(jnp.int32, (rb, LANE), 1)

    def count(pred_fn):
        sf_s[ACC_] = jnp.zeros((nq, LANE), _F32)

        def body(st, carry):
            for r0 in range(0, nq, rb):
                rs = slice(r0, r0 + rb)
                acc = sf_s[ACC_, rs, :]
                for g in range(ng):
                    acc = acc + jnp.where(pred_fn(key_s[st, g, rs, :], rs), 1.0, 0.0)
                sf_s[ACC_, rs, :] = acc
            return carry

        lax.fori_loop(0, n_st, body, 0)
        return _dot(sf_s[ACC_].astype(_BF16), ones_b)

    c0 = count(lambda k, rs: k >= 0)
    ok0 = c0 >= f_topk
    si_s[T_] = jnp.where(ok0, 0, INT_MIN)
    sf_s[CNT_] = jnp.where(ok0, c0, (n_st * st_w).astype(_F32))

    def bit_body(i, carry):
        t_old = si_s[T_]
        cand = t_old + jnp.left_shift(jnp.int32(1), 30 - i)
        si_s[CAND_] = cand
        c = count(lambda k, rs: k >= si_s[CAND_, rs, :])
        ok = c >= f_topk
        si_s[T_] = jnp.where(ok, cand, t_old)
        sf_s[CNT_] = jnp.where(ok, c, sf_s[CNT_])
        return carry

    lax.fori_loop(0, 31, bit_body, 0)

    def encode(st, carry):
        for r0 in range(0, nq, rb):
            rs = slice(r0, r0 + rb)
            t = si_s[T_, rs, :]
            for g in range(ng):
                k = key_s[st, g, rs, :]
                pos = st * st_w + g * LANE + lane_rb
                key_s[st, g, rs, :] = jnp.where(k > t, CODE_KEEP, jnp.where(k == t, pos, CODE_DROP))
        return carry

    lax.fori_loop(0, n_st, encode, 0)
    finite_thr = si_s[T_] != KEY_NEG_INF
    si_s[MIDX_] = jnp.where(finite_thr, TIE_BOUND_ALL, 0)

    @pl.when(jnp.max(jnp.where(finite_thr, sf_s[CNT_], 0.0)) > f_topk)
    def _():
        n_bits = max(1, int(math.ceil(math.log2(key_s.shape[0] * st_w))))
        si_s[MIDX_] = jnp.zeros((nq, LANE), jnp.int32)

        def idx_body(i, carry):
            mp = si_s[MIDX_]
            cand = mp + jnp.left_shift(jnp.int32(1), n_bits - 1 - i)
            si_s[CAND_] = cand
            c = count(lambda code, rs: code < si_s[CAND_, rs, :])
            si_s[MIDX_] = jnp.where(c < f_topk, cand, mp)
            return carry

        lax.fori_loop(0, n_bits, idx_body, 0)
        si_s[MIDX_] = jnp.where(si_s[T_] != KEY_NEG_INF, si_s[MIDX_] + 1, 0)

    def p2(st, carry):
        for r0 in range(0, nq, rb):
            rs = slice(r0, r0 + rb)
            bound = si_s[MIDX_, rs, :]
            for g in range(ng):
                mb = jnp.where(key_s[st, g, rs, :] < bound, 0.0, MASK_NEG)
                key_s[st, g, rs, :] = pltpu.bitcast(mb, jnp.int32)
        return carry

    lax.fori_loop(0, n_st, p2, 0)

    def attend(st, mode):
        off = pl.multiple_of(st * st_w, st_w)
        for h in range(H_B):
            sl = slice(h * hd, (h + 1) * hd)
            mask = jnp.concatenate([pltpu.bitcast(key_s[st, g], _F32) for g in range(ng)], axis=1)
            s = _dot_nt(qs_s[h], k_ref[pl.ds(off, st_w), sl]) + mask
            if mode:
                s_s[...] = s
                b_prev = bias_ref[h, :, 0:LANE]
                if mode == 1:
                    s_s[0:LANE, (ng - 1) * LANE:ng * LANE] += b_prev
                else:
                    b_diag = bias_ref[h, :, LANE:2 * LANE]
                    for rt in range(ng):
                        s_s[grp(rt), grp(rt)] += b_diag
                        if rt:
                            s_s[grp(rt), grp(rt - 1)] += b_prev
                s = s_s[...]
            m_old = m_s[h]
            gmax = s[:, grp(0)]
            for g in range(1, ng):
                gmax = jnp.maximum(gmax, s[:, grp(g)])
            m_new = jnp.maximum(m_old, jnp.max(gmax, axis=1, keepdims=True))
            alpha = jnp.exp2(m_old - m_new)
            pieces = [jnp.exp2(s[:, grp(g)] - m_new) for g in range(ng)]
            psum = pieces[0]
            for g in range(1, ng):
                psum = psum + pieces[g]
            p = jnp.concatenate(pieces, axis=1).astype(_BF16)
            m_s[h] = m_new
            l_s[h] = alpha * l_s[h] + psum
            acc_s[h] = alpha * acc_s[h] + _dot(p, v_ref[pl.ds(off, st_w), sl])

    def p3(st, carry):
        attend(st, 0)
        return carry

    lax.fori_loop(0, qb - 1, p3, 0)

    @pl.when(qb > 0)
    def _():
        attend(qb - 1, 1)

    attend(qb, 2)

    for h in range(H_B):
        sl = slice(h * hd, (h + 1) * hd)
        l_fin = jnp.sum(l_s[h], axis=1, keepdims=True)
        o_ref[:, sl] = (acc_s[h] / l_fin).astype(o_ref.dtype)


def _dsa(pb, ps, bias, b_, t_, col0):
    nq = KEY_TILE
    nb = t_ // nq
    hd = H_B * D_HEAD_B
    topk = min(TOPK_MAX, t_ // 4)
    iq_w = H_IDX * D_IDX
    ik_w = 2 * LANE
    assert col0 % hd == 0 and (col0 + 3 * hd) % iq_w == 0 and (col0 + 3 * hd + iq_w) % ik_w == 0
    cq = col0 // hd
    resident = functools.partial(pl.BlockSpec, pipeline_mode=pl.Buffered(1))
    return pl.pallas_call(
        functools.partial(_dsa_kernel, topk=topk),
        grid=(b_, nb),
        in_specs=[
            pl.BlockSpec((nq, hd), lambda b, i: (b * nb + i, cq)),
            resident((t_, hd), lambda b, i: (b, cq + 1)),
            resident((t_, hd), lambda b, i: (b, cq + 2)),
            pl.BlockSpec((nq, iq_w), lambda b, i: (b * nb + i, (col0 + 3 * hd) // iq_w)),
            resident((t_, ik_w), lambda b, i: (b, (col0 + 3 * hd + iq_w) // ik_w)),
            pl.BlockSpec((nq, LANE), lambda b, i: (b * nb + i, 0)),
            resident((H_B, Q_BLOCK, 2 * LANE), lambda b, i: (0, 0, 0)),
        ],
        out_specs=pl.BlockSpec((nq, hd), lambda b, i: (b * nb + i, 0)),
        out_shape=jax.ShapeDtypeStruct((b_ * t_, hd), _BF16),
        scratch_shapes=[
            pltpu.VMEM((nb, KEY_TILE // LANE, nq, LANE), jnp.int32),
            pltpu.VMEM((H_B, nq, D_HEAD_B), _BF16),
            pltpu.VMEM((H_B, nq, LANE), _F32),
            pltpu.VMEM((H_B, nq, LANE), _F32),
            pltpu.VMEM((H_B, nq, LANE), _F32),
            pltpu.VMEM((H_B, nq, D_HEAD_B), _F32),
            pltpu.VMEM((nq, KEY_TILE), _F32),
            pltpu.VMEM((3, nq, LANE), jnp.int32),
            pltpu.VMEM((2, nq, LANE), _F32),
        ],
        compiler_params=pltpu.CompilerParams(
            dimension_semantics=("arbitrary", "arbitrary"), vmem_limit_bytes=VMEM_LIMIT),
        name="dsa",
    )(pb, pb, pb, pb, pb, ps, bias)


def _merge_kernel(oa_ref, ob_ref, g_ref, x_ref, bg_ref, wa_ref, wb_ref, wo_ref, o_ref):
    d = x_ref.shape[1]
    a = _dot(oa_ref[...], wa_ref[...])
    b = _dot(ob_ref[...], wb_ref[...])
    g = _sigmoid(g_ref[...].astype(_F32) + bg_ref[...])
    merged = g[:, 0:d] * a + g[:, d:2 * d] * b
    o_ref[...] = x_ref[...] + _dot(merged.astype(_BF16), wo_ref[...])


def _merge(oa, ob, pb, x2d, bg, wa, wb, wo, tn, col_g):
    n, d = x2d.shape
    row = lambda w: pl.BlockSpec((tn, w), lambda i: (i, 0))
    const = lambda a: pl.BlockSpec(a.shape, lambda i: (0, 0))
    return pl.pallas_call(
        _merge_kernel,
        grid=(n // tn,),
        in_specs=[row(oa.shape[1]), row(ob.shape[1]),
                  pl.BlockSpec((tn, 2 * d), lambda i: (i, col_g // (2 * d))), row(d),
                  const(bg), const(wa), const(wb), const(wo)],
        out_specs=row(d),
        out_shape=jax.ShapeDtypeStruct((n, d), _F32),
        compiler_params=pltpu.CompilerParams(
            dimension_semantics=("arbitrary",), vmem_limit_bytes=VMEM_LIMIT),
        name="merge",
    )(oa, ob, pb, x2d, bg, wa, wb, wo)


def _mlp_kernel(x_ref, n2_ref, w1_ref, w2_ref, nf_ref, o_ref, *, final_norm):
    x = x_ref[...]
    h2 = _rms(x, n2_ref[...]).astype(_BF16)
    hid = jnp.maximum(_dot(h2, w1_ref[...]), 0.0)
    y = x + _dot((hid * hid).astype(_BF16), w2_ref[...])
    o_ref[...] = _rms(y, nf_ref[...]) if final_norm else y


def _mlp(x1, n2, w1, w2, nf, tn, final_norm):
    n, d = x1.shape
    row = pl.BlockSpec((tn, d), lambda i: (i, 0))
    const = lambda a: pl.BlockSpec(a.shape, lambda i: (0, 0))
    return pl.pallas_call(
        functools.partial(_mlp_kernel, final_norm=final_norm),
        grid=(n // tn,),
        in_specs=[row, const(n2), const(w1), const(w2), const(nf)],
        out_specs=row,
        out_shape=jax.ShapeDtypeStruct((n, d), _F32),
        compiler_params=pltpu.CompilerParams(
            dimension_semantics=("arbitrary",), vmem_limit_bytes=VMEM_LIMIT),
        name="mlp",
    )(x1, n2, w1, w2, nf)


def _t5_bucket(rel):
    half = N_BUCKETS // 2
    max_exact = half // 2
    base = jnp.where(rel > 0, half, 0)
    n = jnp.abs(rel)
    n_f = jnp.maximum(n, 1).astype(jnp.float32)
    large = max_exact + (jnp.log(n_f / max_exact) / math.log(MAX_DISTANCE / max_exact)
                         * (half - max_exact)).astype(jnp.int32)
    large = jnp.minimum(large, half - 1)
    return base + jnp.where(n < max_exact, n, large)


def _pick_tile(n, prefs):
    for t in prefs:
        if n % t == 0:
            return t
    raise ValueError(f"no tile in {prefs} divides {n}")


def kernel(x, norm1_w, w_in, conv_a_w, a_log, dt_bias, norm_a_w, rel_bias_table, w_gate, b_gate,
           w_proj_a, w_proj_b, w_out, norm2_w, w_ff1, w_ff2, norm_final_w):
    b_, t_, d = x.shape
    depth = norm1_w.shape[0]
    n = b_ * t_
    ha, hb = H_A * DK_A, H_B * D_HEAD_B
    assert t_ % KEY_TILE == 0 and t_ % GDN_CHUNK == 0 and d % LANE == 0
    assert DK_A == DV_A == D_HEAD_B == LANE and 2 * D_IDX == LANE
    assert Q_BLOCK >= MAX_DISTANCE

    o_za = 3 * ha
    o_ba = 4 * ha
    o_aa = o_ba + H_A
    o_qb = o_aa + H_A
    o_iq = o_qb + 3 * hb
    o_ik = o_iq + H_IDX * D_IDX
    o_iw = o_ik + D_IDX

    rel = (jnp.arange(2 * Q_BLOCK, dtype=jnp.int32)[None, :] - Q_BLOCK) \
        - jnp.arange(Q_BLOCK, dtype=jnp.int32)[:, None]
    far = jnp.full((Q_BLOCK, LANE), -(Q_BLOCK + 1), jnp.int32)
    bucket = _t5_bucket(jnp.concatenate([rel, far], axis=1))
    bias = _bias_tiles(rel_bias_table.astype(_F32), bucket)

    x2d = x.reshape(n, d)
    tn = _pick_tile(n, (512, 256, 128))
    for layer in range(depth):
        wi = w_in[layer]
        zpad = lambda w: jnp.zeros((d, w), wi.dtype)
        ws = jnp.concatenate([wi[:, o_ba:o_qb], wi[:, o_iw:o_iw + H_IDX], zpad(LANE - 3 * H_A)],
                             axis=1).astype(_BF16)
        ik = wi[:, o_ik:o_iw]
        wb16 = jnp.concatenate([wi[:, 0:o_ba], w_gate[layer], wi[:, o_qb:o_ik],
                                ik, zpad(D_IDX), zpad(D_IDX), ik], axis=1).astype(_BF16)
        nw = norm1_w[layer].reshape(1, d).astype(_F32)
        pb, ps = _norm_proj(x2d, nw, wb16, ws, tn)

        col_a = 0
        col_g = 4 * ha
        col_b = col_g + 2 * d
        assert col_g % (2 * d) == 0
        lanes = lambda v, off: jnp.zeros((1, LANE), _F32).at[0, off:off + H_A].set(v.astype(_F32))
        oa = _gdn(pb, ps, conv_a_w[layer].astype(_F32), lanes(a_log[layer], H_A), lanes(dt_bias[layer], H_A),
                  norm_a_w[layer].reshape(1, DV_A).astype(_F32), b_, t_, col_a)
        ob = _dsa(pb, ps, bias, b_, t_, col_b)

        x2d = _merge(oa, ob, pb, x2d, b_gate[layer].reshape(1, 2 * d).astype(_F32),
                     w_proj_a[layer].astype(_BF16), w_proj_b[layer].astype(_BF16),
                     w_out[layer].astype(_BF16), tn, col_g)
        x2d = _mlp(x2d, norm2_w[layer].reshape(1, d).astype(_F32), w_ff1[layer].astype(_BF16),
                   w_ff2[layer].astype(_BF16), norm_final_w.reshape(1, d).astype(_F32),
                   _pick_tile(n, (256, 128)), final_norm=layer == depth - 1)
    return x2d.reshape(b_, t_, d)
```

```python
import functools
import math

import jax
import jax.numpy as jnp
import numpy as np
from jax import lax
from jax.experimental import pallas as pl
from jax.experimental.pallas import tpu as pltpu

EPS = 1e-6
H_A = 8
DK_A = 128
DV_A = 128
CONV_K = 4
GDN_CHUNK = 128
GDN_STEP_CHUNKS = 2
H_B = 8
D_HEAD_B = 128
H_IDX = 8
D_IDX = 64
TOPK_MAX = 256
Q_BLOCK = 128
CHUNK = 64
KEY_TILE = 512
N_BUCKETS = 32
MAX_DISTANCE = 128

LANE = 128
VMEM_LIMIT = 56 * 1024 * 1024

INT_MIN = -(2**31)
INT_MAX = 2**31 - 1
KEY_NEG_INF = -2139095041
MASK_NEG = -1e30
CODE_KEEP = -1
CODE_DROP = INT_MAX
TIE_BOUND_ALL = 2**30
LOG2E = math.log2(math.e)

_F32 = jnp.float32
_BF16 = jnp.bfloat16


def _dot(a, b):
    return jnp.dot(a, b, preferred_element_type=_F32)


def _dot_nt(a, b):
    return lax.dot_general(a, b, (((1,), (1,)), ((), ())), preferred_element_type=_F32)


def _sigmoid(x):
    return 1.0 / (1.0 + jnp.exp(-x))


def _rms(x, w):
    return x * lax.rsqrt(jnp.mean(x * x, axis=-1, keepdims=True) + EPS) * w


def _norm_proj_kernel(x_ref, nw_ref, w_ref, ws_ref, o_ref, os_ref, *, chunk):
    h = _rms(x_ref[...], nw_ref[...]).astype(_BF16)
    os_ref[...] = _dot(h, ws_ref[...])
    for c0 in range(0, w_ref.shape[1], chunk):
        o_ref[:, c0:c0 + chunk] = _dot(h, w_ref[:, c0:c0 + chunk]).astype(o_ref.dtype)


def _norm_proj(x2d, nw, w, ws, tn):
    n, d = x2d.shape
    c = w.shape[1]
    const = functools.partial(pl.BlockSpec, pipeline_mode=pl.Buffered(1))
    return pl.pallas_call(
        functools.partial(_norm_proj_kernel, chunk=_pick_tile(c, (768, 512, 256, 128))),
        grid=(n // tn,),
        in_specs=[
            pl.BlockSpec((tn, d), lambda i: (i, 0)),
            const((1, d), lambda i: (0, 0)),
            const((d, c), lambda i: (0, 0)),
            const((d, LANE), lambda i: (0, 0)),
        ],
        out_specs=[pl.BlockSpec((tn, c), lambda i: (i, 0)), pl.BlockSpec((tn, LANE), lambda i: (i, 0))],
        out_shape=[jax.ShapeDtypeStruct((n, c), _BF16), jax.ShapeDtypeStruct((n, LANE), _F32)],
        compiler_params=pltpu.CompilerParams(
            dimension_semantics=("arbitrary",), vmem_limit_bytes=VMEM_LIMIT),
        name="proj",
    )(x2d, nw, w, ws)


def _gdn_kernel(qkvz_ref, sm_ref, cw_ref, arow_ref, dtrow_ref, naw_ref, o_ref,
                xbuf, s_ref, kn_s, kb_s, qn_s, rhs_s, dec_s, qdec_s, ktt_s, eg_s, m_s, x_s):
    tt = qkvz_ref.shape[0]
    ct = GDN_CHUNK
    nc = tt // ct
    hd = H_A * DK_A

    @pl.when(pl.program_id(1) == 0)
    def _():
        xbuf[:, 0:8, :] = jnp.zeros((3, 8, hd), _F32)
        s_ref[...] = jnp.zeros_like(s_ref)

    for idx in range(3):
        xbuf[idx, 8:8 + tt, :] = qkvz_ref[:, idx * hd:(idx + 1) * hd].astype(_F32)

    row = lax.broadcasted_iota(jnp.int32, (ct, ct), 0)
    col = lax.broadcasted_iota(jnp.int32, (ct, ct), 1)
    tri = (col <= row).astype(_F32)
    strict = col < row
    eye = (col == row).astype(_F32)
    n_sq = int(math.log2(ct))
    heads = range(H_A)

    def prepare(c):
        r0 = c * ct
        sm = sm_ref[r0:r0 + ct, :]
        beta_full = _sigmoid(sm)
        xg = sm + dtrow_ref[...]
        softplus = jnp.maximum(xg, 0.0) + jnp.log(1.0 + jnp.exp(-jnp.abs(xg)))
        g_full = -jnp.exp(arow_ref[...]) * softplus
        gcum = jnp.dot(tri, g_full, preferred_element_type=_F32, precision=lax.Precision.HIGHEST)
        gcum_t = gcum.T
        for h in heads:
            sl = slice(h * DK_A, (h + 1) * DK_A)
            conv = []
            for idx in range(3):
                acc = None
                for j in range(CONV_K):
                    lo = r0 + 8 - (CONV_K - 1) + j
                    term = cw_ref[j:j + 1, idx * hd + h * DK_A: idx * hd + (h + 1) * DK_A] * xbuf[idx, lo:lo + ct, sl]
                    acc = term if acc is None else acc + term
                conv.append(acc * _sigmoid(acc))
            qh, kh, vh = conv
            qn = qh * lax.rsqrt(jnp.sum(qh * qh, axis=-1, keepdims=True) + EPS) * (DK_A ** -0.5)
            kn = kh * lax.rsqrt(jnp.sum(kh * kh, axis=-1, keepdims=True) + EPS)
            bcol = beta_full[:, h:h + 1]
            gcol = gcum[:, H_A + h:H_A + h + 1]
            grow = gcum_t[H_A + h:H_A + h + 1, :]
            glast = gcum[ct - 1:ct, H_A + h:H_A + h + 1]
            kb = kn * bcol
            eg = jnp.exp(gcol)
            rhs_s[c, h, :, 0:DV_A] = vh * bcol
            rhs_s[c, h, :, DV_A:DV_A + DK_A] = kb * eg
            dec_s[c, h] = jnp.where(strict, jnp.exp(jnp.where(strict, gcol - grow, 0.0)), 0.0)
            qdec_s[c, h] = (qn * eg).astype(_BF16)
            ktt_s[c, h] = (kn * jnp.exp(glast - gcol)).T.astype(_BF16)
            kn_s[c, h] = kn.astype(_BF16)
            kb_s[c, h] = kb.astype(_BF16)
            qn_s[c, h] = qn.astype(_BF16)
            eg_s[c, h] = jnp.broadcast_to(jnp.exp(glast), (DK_A, DV_A))

    def solve(c):
        for h in heads:
            m = -(_dot_nt(kb_s[c, h], kn_s[c, h]) * dec_s[c, h])
            m_s[c, h] = m
            x_s[c, h] = m
        for _ in range(n_sq - 1):
            for h in heads:
                mb = m_s[c, h].astype(_BF16)
                m = _dot(mb, mb)
                m_s[c, h] = m
                x_s[c, h] = x_s[c, h] + m + _dot(x_s[c, h].astype(_BF16), m.astype(_BF16))
        for h in heads:
            rhs = rhs_s[c, h]
            rhs_s[c, h] = rhs + _dot(x_s[c, h].astype(_BF16), rhs.astype(_BF16))
        for h in heads:
            m_s[c, h] = _dot_nt(qn_s[c, h], kn_s[c, h]) * (dec_s[c, h] + eye)

    def recur(c):
        for h in heads:
            sb = s_ref[h].astype(_BF16)
            vnew = rhs_s[c, h, :, 0:DV_A] - _dot(rhs_s[c, h, :, DV_A:DV_A + DK_A].astype(_BF16), sb)
            vb = vnew.astype(_BF16)
            o = _dot(qdec_s[c, h], sb) + _dot(m_s[c, h].astype(_BF16), vb)
            s_ref[h] = s_ref[h] * eg_s[c, h] + _dot(ktt_s[c, h], vb)
            rhs_s[c, h, :, 0:DV_A] = o

    def finish(c):
        r0 = c * ct
        naw = naw_ref[...]
        for h in heads:
            sl = slice(h * DV_A, (h + 1) * DV_A)
            o = rhs_s[c, h, :, 0:DV_A]
            z = qkvz_ref[r0:r0 + ct, 3 * hd + h * DV_A:3 * hd + (h + 1) * DV_A].astype(_F32)
            on = o * lax.rsqrt(jnp.mean(o * o, axis=-1, keepdims=True) + EPS) * naw
            o_ref[r0:r0 + ct, sl] = (on * (z * _sigmoid(z))).astype(o_ref.dtype)

    for c in range(nc):
        prepare(c)
    for idx in range(3):
        xbuf[idx, 0:8, :] = xbuf[idx, tt:tt + 8, :]
    for c in range(nc):
        solve(c)
    for c in range(nc):
        recur(c)
    for c in range(nc):
        finish(c)


def _gdn(pb, ps, cw, arow, dtrow, naw, b_, t_, col0):
    ct = GDN_CHUNK
    nc = GDN_STEP_CHUNKS if t_ % (GDN_STEP_CHUNKS * ct) == 0 else 1
    tt = nc * ct
    hd = H_A * DK_A
    nt = t_ // tt
    assert col0 % (4 * hd) == 0
    cb = col0 // (4 * hd)
    const = lambda shape: pl.BlockSpec(shape, lambda b, t: (0, 0))
    return pl.pallas_call(
        _gdn_kernel,
        grid=(b_, nt),
        in_specs=[pl.BlockSpec((tt, 4 * hd), lambda b, t: (b * nt + t, cb)),
                  pl.BlockSpec((tt, LANE), lambda b, t: (b * nt + t, 0)),
                  const((CONV_K, 3 * hd)), const((1, LANE)), const((1, LANE)), const((1, DV_A))],
        out_specs=pl.BlockSpec((tt, hd), lambda b, t: (b * nt + t, 0)),
        out_shape=jax.ShapeDtypeStruct((b_ * t_, hd), _BF16),
        scratch_shapes=[
            pltpu.VMEM((3, tt + 8, hd), _F32),
            pltpu.VMEM((H_A, DK_A, DV_A), _F32),
            pltpu.VMEM((nc, H_A, ct, DK_A), _BF16),
            pltpu.VMEM((nc, H_A, ct, DK_A), _BF16),
            pltpu.VMEM((nc, H_A, ct, DK_A), _BF16),
            pltpu.VMEM((nc, H_A, ct, DV_A + DK_A), _F32),
            pltpu.VMEM((nc, H_A, ct, ct), _F32),
            pltpu.VMEM((nc, H_A, ct, DK_A), _BF16),
            pltpu.VMEM((nc, H_A, DK_A, ct), _BF16),
            pltpu.VMEM((nc, H_A, DK_A, DV_A), _F32),
            pltpu.VMEM((nc, H_A, ct, ct), _F32),
            pltpu.VMEM((nc, H_A, ct, ct), _F32),
        ],
        compiler_params=pltpu.CompilerParams(
            dimension_semantics=("arbitrary", "arbitrary"), vmem_limit_bytes=VMEM_LIMIT),
        name="gdn",
    )(pb, ps, cw, arow, dtrow, naw)


def _bias_kernel(tab_ref, bucket_ref, o_ref):
    bucket = bucket_ref[...]
    nq, w = bucket.shape
    for h in range(H_B):
        acc = jnp.zeros((nq, w), _F32)
        for b in range(N_BUCKETS):
            acc = acc + jnp.where(bucket == b, tab_ref[b, h], 0.0)
        far = acc[:, 2 * LANE:3 * LANE]
        o_ref[h] = (acc[:, 0:2 * LANE] - jnp.concatenate([far, far], axis=1)) * LOG2E


def _bias_tiles(rel_table, bucket):
    return pl.pallas_call(
        _bias_kernel,
        in_specs=[pl.BlockSpec(memory_space=pltpu.SMEM),
                  pl.BlockSpec(bucket.shape, lambda: (0, 0))],
        out_specs=pl.BlockSpec((H_B, Q_BLOCK, 2 * LANE), lambda: (0, 0, 0)),
        out_shape=jax.ShapeDtypeStruct((H_B, Q_BLOCK, 2 * LANE), _F32),
        name="rel_bias_tiles",
    )(rel_table, bucket)


def _dsa_kernel(q_ref, k_ref, v_ref, iq_ref, ik_ref, sm_ref, bias_ref, o_ref,
                key_s, qs_s, iwb_s, m_s, l_s, acc_s, s_s, si_s, sf_s, *, topk):
    nq = q_ref.shape[0]
    st_w = KEY_TILE
    assert nq == st_w
    ng = st_w // LANE
    qb = pl.program_id(1)
    n_st = qb + 1
    hd = D_HEAD_B
    scale = D_HEAD_B ** -0.5 * LOG2E
    f_topk = float(topk)

    lane_j = lax.broadcasted_iota(jnp.int32, (nq, LANE), 1)
    row_i = lax.broadcasted_iota(jnp.int32, (nq, LANE), 0)

    def grp(g):
        return slice(g * LANE, (g + 1) * LANE)

    iw = sm_ref[...]
    for h in range(H_B):
        sl = slice(h * hd, (h + 1) * hd)
        qs_s[h] = (q_ref[:, sl].astype(_F32) * scale).astype(_BF16)
        iwb_s[h] = jnp.broadcast_to(iw[:, 2 * H_A + h:2 * H_A + h + 1], (nq, LANE))
        m_s[h] = jnp.full((nq, LANE), MASK_NEG, _F32)
        l_s[h] = jnp.zeros((nq, LANE), _F32)
        acc_s[h] = jnp.zeros((nq, hd), _F32)

    def scores(st, diag):
        ikt = ik_ref[pl.ds(pl.multiple_of(st * st_w, st_w), st_w), :]
        acc = [jnp.zeros((nq, LANE), _F32) for _ in range(ng)]
        for p in range(H_IDX // 2):
            xq = iq_ref[:, p * LANE:(p + 1) * LANE]
            for half in range(2):
                h = 2 * p + half
                s = jnp.maximum(_dot_nt(xq, ikt[:, half * LANE:(half + 1) * LANE]), 0.0)
                w = iwb_s[h]
                for g in range(ng):
                    acc[g] = acc[g] + w * s[:, grp(g)]
        for g in range(ng):
            a = acc[g]
            if diag:
                a = jnp.where((g * LANE + lane_j) // CHUNK <= row_i // CHUNK, a, -jnp.inf)
            bits = pltpu.bitcast(a, jnp.int32)
            key_s[st, g] = bits ^ ((bits >> 31) & INT_MAX)

    def p1(st, carry):
        scores(st, False)
        return carry

    lax.fori_loop(0, qb, p1, 0)
    scores(qb, True)

    ones_b = jnp.ones((LANE, LANE), _BF16)
    assert key_s.shape[0] * ng <= 256

    T_, CAND_, MIDX_ = 0, 1, 2
    ACC_, CNT_ = 0, 1
    rb = 64
    lane_rb = lax.broadcasted_iota(jnp.int32, (rb, LANE), 1)

    def count(pred_fn):
        sf_s[ACC_] = jnp.zeros((nq, LANE), _F32)

        def body(st, carry):
            for r0 in range(0, nq, rb):
                rs = slice(r0, r0 + rb)
                acc = sf_s[ACC_, rs, :]
                for g in range(ng):
                    acc = acc + jnp.where(pred_fn(key_s[st, g, rs, :], rs), 1.0, 0.0)
                sf_s[ACC_, rs, :] = acc
            return carry

        lax.fori_loop(0, n_st, body, 0)
        return _dot(sf_s[ACC_].astype(_BF16), ones_b)

    c0 = count(lambda k, rs: k >= 0)
    ok0 = c0 >= f_topk
    si_s[T_] = jnp.where(ok0, 0, INT_MIN)
    sf_s[CNT_] = jnp.where(ok0, c0, (n_st * st_w).astype(_F32))

    def bit_body(i, carry):
        t_old = si_s[T_]
        cand = t_old + jnp.left_shift(jnp.int32(1), 30 - i)
        si_s[CAND_] = cand
        c = count(lambda k, rs: k >= si_s[CAND_, rs, :])
        ok = c >= f_topk
        si_s[T_] = jnp.where(ok, cand, t_old)
        sf_s[CNT_] = jnp.where(ok, c, sf_s[CNT_])
        return carry

    lax.fori_loop(0, 31, bit_body, 0)

    def encode(st, carry):
        for r0 in range(0, nq, rb):
            rs = slice(r0, r0 + rb)
            t = si_s[T_, rs, :]
            for g in range(ng):
                k = key_s[st, g, rs, :]
                pos = st * st_w + g * LANE + lane_rb
                key_s[st, g, rs, :] = jnp.where(k > t, CODE_KEEP, jnp.where(k == t, pos, CODE_DROP))
        return carry

    lax.fori_loop(0, n_st, encode, 0)
    finite_thr = si_s[T_] != KEY_NEG_INF
    si_s[MIDX_] = jnp.where(finite_thr, TIE_BOUND_ALL, 0)

    excess = jnp.max(jnp.where(finite_thr, sf_s[CNT_], 0.0)) > f_topk

    @pl.when(excess)
    def _():
        r_keep = f_topk - count(lambda code, rs: code < 0)
        sf_s[CNT_] = jnp.where(finite_thr, r_keep, 0.0)
        sf_s[ACC_] = jnp.zeros((nq, LANE), _F32)
        upper = (lax.broadcasted_iota(jnp.int32, (LANE, LANE), 0)
                 <= lax.broadcasted_iota(jnp.int32, (LANE, LANE), 1)).astype(_BF16)

        def rank_ties(st, carry):
            for r0 in range(0, nq, rb):
                rs = slice(r0, r0 + rb)
                r_blk = sf_s[CNT_, rs, :]
                before = sf_s[ACC_, rs, :]
                for g in range(ng):
                    code = key_s[st, g, rs, :]
                    tie = jnp.where(code >= 0, jnp.where(code < CODE_DROP, 1.0, 0.0), 0.0)
                    tie_b = tie.astype(_BF16)
                    rank = before + _dot(tie_b, upper)
                    kept_tie = jnp.where(rank <= r_blk, tie, 0.0)
                    mb = jnp.where(code < 0, 0.0, jnp.where(kept_tie > 0.0, 0.0, MASK_NEG))
                    key_s[st, g, rs, :] = pltpu.bitcast(mb, jnp.int32)
                    before = before + _dot(tie_b, ones_b)
                sf_s[ACC_, rs, :] = before
            return carry

        lax.fori_loop(0, n_st, rank_ties, 0)

    @pl.when(jnp.logical_not(excess))
    def _():
        def p2(st, carry):
            for r0 in range(0, nq, rb):
                rs = slice(r0, r0 + rb)
                bound = si_s[MIDX_, rs, :]
                for g in range(ng):
                    mb = jnp.where(key_s[st, g, rs, :] < bound, 0.0, MASK_NEG)
                    key_s[st, g, rs, :] = pltpu.bitcast(mb, jnp.int32)
            return carry

        lax.fori_loop(0, n_st, p2, 0)

    def attend(st, mode):
        off = pl.multiple_of(st * st_w, st_w)
        for h in range(H_B):
            sl = slice(h * hd, (h + 1) * hd)
            mask = jnp.concatenate([pltpu.bitcast(key_s[st, g], _F32) for g in range(ng)], axis=1)
            s = _dot_nt(qs_s[h], k_ref[pl.ds(off, st_w), sl]) + mask
            if mode:
                s_s[...] = s
                b_prev = bias_ref[h, :, 0:LANE]
                if mode == 1:
                    s_s[0:LANE, (ng - 1) * LANE:ng * LANE] += b_prev
                else:
                    b_diag = bias_ref[h, :, LANE:2 * LANE]
                    for rt in range(ng):
                        s_s[grp(rt), grp(rt)] += b_diag
                        if rt:
                            s_s[grp(rt), grp(rt - 1)] += b_prev
                s = s_s[...]
            m_old = m_s[h]
            gmax = s[:, grp(0)]
            for g in range(1, ng):
                gmax = jnp.maximum(gmax, s[:, grp(g)])
            m_new = jnp.maximum(m_old, jnp.max(gmax, axis=1, keepdims=True))
            alpha = jnp.exp2(m_old - m_new)
            pieces = [jnp.exp2(s[:, grp(g)] - m_new) for g in range(ng)]
            psum = pieces[0]
            for g in range(1, ng):
                psum = psum + pieces[g]
            p = jnp.concatenate(pieces, axis=1).astype(_BF16)
            m_s[h] = m_new
            l_s[h] = alpha * l_s[h] + psum
            acc_s[h] = alpha * acc_s[h] + _dot(p, v_ref[pl.ds(off, st_w), sl])

    def p3(st, carry):
        attend(st, 0)
        return carry

    lax.fori_loop(0, qb - 1, p3, 0)

    @pl.when(qb > 0)
    def _():
        attend(qb - 1, 1)

    attend(qb, 2)

    for h in range(H_B):
        sl = slice(h * hd, (h + 1) * hd)
        l_fin = jnp.sum(l_s[h], axis=1, keepdims=True)
        o_ref[:, sl] = (acc_s[h] / l_fin).astype(o_ref.dtype)


def _dsa(pb, ps, bias, b_, t_, col0):
    nq = KEY_TILE
    nb = t_ // nq
    hd = H_B * D_HEAD_B
    topk = min(TOPK_MAX, t_ // 4)
    iq_w = H_IDX * D_IDX
    ik_w = 2 * LANE
    assert col0 % hd == 0 and (col0 + 3 * hd) % iq_w == 0 and (col0 + 3 * hd + iq_w) % ik_w == 0
    cq = col0 // hd
    resident = functools.partial(pl.BlockSpec, pipeline_mode=pl.Buffered(1))
    return pl.pallas_call(
        functools.partial(_dsa_kernel, topk=topk),
        grid=(b_, nb),
        in_specs=[
            pl.BlockSpec((nq, hd), lambda b, i: (b * nb + i, cq)),
            resident((t_, hd), lambda b, i: (b, cq + 1)),
            resident((t_, hd), lambda b, i: (b, cq + 2)),
            pl.BlockSpec((nq, iq_w), lambda b, i: (b * nb + i, (col0 + 3 * hd) // iq_w)),
            resident((t_, ik_w), lambda b, i: (b, (col0 + 3 * hd + iq_w) // ik_w)),
            pl.BlockSpec((nq, LANE), lambda b, i: (b * nb + i, 0)),
            resident((H_B, Q_BLOCK, 2 * LANE), lambda b, i: (0, 0, 0)),
        ],
        out_specs=pl.BlockSpec((nq, hd), lambda b, i: (b * nb + i, 0)),
        out_shape=jax.ShapeDtypeStruct((b_ * t_, hd), _BF16),
        scratch_shapes=[
            pltpu.VMEM((nb, KEY_TILE // LANE, nq, LANE), jnp.int32),
            pltpu.VMEM((H_B, nq, D_HEAD_B), _BF16),
            pltpu.VMEM((H_B, nq, LANE), _F32),
            pltpu.VMEM((H_B, nq, LANE), _F32),
            pltpu.VMEM((H_B, nq, LANE), _F32),
            pltpu.VMEM((H_B, nq, D_HEAD_B), _F32),
            pltpu.VMEM((nq, KEY_TILE), _F32),
            pltpu.VMEM((3, nq, LANE), jnp.int32),
            pltpu.VMEM((2, nq, LANE), _F32),
        ],
        compiler_params=pltpu.CompilerParams(
            dimension_semantics=("arbitrary", "arbitrary"), vmem_limit_bytes=VMEM_LIMIT),
        name="dsa",
    )(pb, pb, pb, pb, pb, ps, bias)


def _merge_kernel(oa_ref, ob_ref, g_ref, x_ref, bg_ref, wa_ref, wb_ref, wo_ref, o_ref):
    d = x_ref.shape[1]
    a = _dot(oa_ref[...], wa_ref[...])
    b = _dot(ob_ref[...], wb_ref[...])
    g = _sigmoid(g_ref[...].astype(_F32) + bg_ref[...])
    merged = g[:, 0:d] * a + g[:, d:2 * d] * b
    o_ref[...] = x_ref[...] + _dot(merged.astype(_BF16), wo_ref[...])


def _merge(oa, ob, pb, x2d, bg, wa, wb, wo, tn, col_g):
    n, d = x2d.shape
    row = lambda w: pl.BlockSpec((tn, w), lambda i: (i, 0))
    const = lambda a: pl.BlockSpec(a.shape, lambda i: (0, 0))
    return pl.pallas_call(
        _merge_kernel,
        grid=(n // tn,),
        in_specs=[row(oa.shape[1]), row(ob.shape[1]),
                  pl.BlockSpec((tn, 2 * d), lambda i: (i, col_g // (2 * d))), row(d),
                  const(bg), const(wa), const(wb), const(wo)],
        out_specs=row(d),
        out_shape=jax.ShapeDtypeStruct((n, d), _F32),
        compiler_params=pltpu.CompilerParams(
            dimension_semantics=("arbitrary",), vmem_limit_bytes=VMEM_LIMIT),
        name="merge",
    )(oa, ob, pb, x2d, bg, wa, wb, wo)


def _mlp_kernel(x_ref, n2_ref, w1_ref, w2_ref, nf_ref, o_ref, *, final_norm):
    x = x_ref[...]
    h2 = _rms(x, n2_ref[...]).astype(_BF16)
    hid = jnp.maximum(_dot(h2, w1_ref[...]), 0.0)
    y = x + _dot((hid * hid).astype(_BF16), w2_ref[...])
    o_ref[...] = _rms(y, nf_ref[...]) if final_norm else y


def _mlp(x1, n2, w1, w2, nf, tn, final_norm):
    n, d = x1.shape
    row = pl.BlockSpec((tn, d), lambda i: (i, 0))
    const = lambda a: pl.BlockSpec(a.shape, lambda i: (0, 0))
    return pl.pallas_call(
        functools.partial(_mlp_kernel, final_norm=final_norm),
        grid=(n // tn,),
        in_specs=[row, const(n2), const(w1), const(w2), const(nf)],
        out_specs=row,
        out_shape=jax.ShapeDtypeStruct((n, d), _F32),
        compiler_params=pltpu.CompilerParams(
            dimension_semantics=("arbitrary",), vmem_limit_bytes=VMEM_LIMIT),
        name="mlp",
    )(x1, n2, w1, w2, nf)


def _t5_bucket(rel):
    half = N_BUCKETS // 2
    max_exact = half // 2
    base = jnp.where(rel > 0, half, 0)
    n = jnp.abs(rel)
    n_f = jnp.maximum(n, 1).astype(jnp.float32)
    large = max_exact + (jnp.log(n_f / max_exact) / math.log(MAX_DISTANCE / max_exact)
                         * (half - max_exact)).astype(jnp.int32)
    large = jnp.minimum(large, half - 1)
    return base + jnp.where(n < max_exact, n, large)


def _pick_tile(n, prefs):
    for t in prefs:
        if n % t == 0:
            return t
    raise ValueError(f"no tile in {prefs} divides {n}")


def kernel(x, norm1_w, w_in, conv_a_w, a_log, dt_bias, norm_a_w, rel_bias_table, w_gate, b_gate,
           w_proj_a, w_proj_b, w_out, norm2_w, w_ff1, w_ff2, norm_final_w):
    b_, t_, d = x.shape
    depth = norm1_w.shape[0]
    n = b_ * t_
    ha, hb = H_A * DK_A, H_B * D_HEAD_B
    assert t_ % KEY_TILE == 0 and t_ % GDN_CHUNK == 0 and d % LANE == 0
    assert DK_A == DV_A == D_HEAD_B == LANE and 2 * D_IDX == LANE
    assert Q_BLOCK >= MAX_DISTANCE

    o_za = 3 * ha
    o_ba = 4 * ha
    o_aa = o_ba + H_A
    o_qb = o_aa + H_A
    o_iq = o_qb + 3 * hb
    o_ik = o_iq + H_IDX * D_IDX
    o_iw = o_ik + D_IDX

    rel = (jnp.arange(2 * Q_BLOCK, dtype=jnp.int32)[None, :] - Q_BLOCK) \
        - jnp.arange(Q_BLOCK, dtype=jnp.int32)[:, None]
    far = jnp.full((Q_BLOCK, LANE), -(Q_BLOCK + 1), jnp.int32)
    bucket = _t5_bucket(jnp.concatenate([rel, far], axis=1))
    bias = _bias_tiles(rel_bias_table.astype(_F32), bucket)

    x2d = x.reshape(n, d)
    tn = _pick_tile(n, (512, 256, 128))
    for layer in range(depth):
        wi = w_in[layer]
        zpad = lambda w: jnp.zeros((d, w), wi.dtype)
        ws = jnp.concatenate([wi[:, o_ba:o_qb], wi[:, o_iw:o_iw + H_IDX], zpad(LANE - 3 * H_A)],
                             axis=1).astype(_BF16)
        ik = wi[:, o_ik:o_iw]
        wb16 = jnp.concatenate([wi[:, 0:o_ba], w_gate[layer], wi[:, o_qb:o_ik],
                                ik, zpad(D_IDX), zpad(D_IDX), ik], axis=1).astype(_BF16)
        nw = norm1_w[layer].reshape(1, d).astype(_F32)
        pb, ps = _norm_proj(x2d, nw, wb16, ws, tn)

        col_a = 0
        col_g = 4 * ha
        col_b = col_g + 2 * d
        assert col_g % (2 * d) == 0
        lanes = lambda v, off: jnp.zeros((1, LANE), _F32).at[0, off:off + H_A].set(v.astype(_F32))
        oa = _gdn(pb, ps, conv_a_w[layer].astype(_F32), lanes(a_log[layer], H_A), lanes(dt_bias[layer], H_A),
                  norm_a_w[layer].reshape(1, DV_A).astype(_F32), b_, t_, col_a)
        ob = _dsa(pb, ps, bias, b_, t_, col_b)

        x2d = _merge(oa, ob, pb, x2d, b_gate[layer].reshape(1, 2 * d).astype(_F32),
                     w_proj_a[layer].astype(_BF16), w_proj_b[layer].astype(_BF16),
                     w_out[layer].astype(_BF16), tn, col_g)
        x2d = _mlp(x2d, norm2_w[layer].reshape(1, d).astype(_F32), w_ff1[layer].astype(_BF16),
                   w_ff2[layer].astype(_BF16), norm_final_w.reshape(1, d).astype(_F32),
                   _pick_tile(n, (256, 128)), final_norm=layer == depth - 1)
    return x2d.reshape(b_, t_, d)
```

```python
import functools
import math

import jax
import jax.numpy as jnp
import numpy as np
from jax import lax
from jax.experimental import pallas as pl
from jax.experimental.pallas import tpu as pltpu

EPS = 1e-6
H_A = 8
DK_A = 128
DV_A = 128
CONV_K = 4
GDN_CHUNK = 128
GDN_STEP_CHUNKS = 2
H_B = 8
D_HEAD_B = 128
H_IDX = 8
D_IDX = 64
TOPK_MAX = 256
Q_BLOCK = 128
CHUNK = 64
KEY_TILE = 512
N_BUCKETS = 32
MAX_DISTANCE = 128

LANE = 128
VMEM_LIMIT = 56 * 1024 * 1024

INT_MIN = -(2**31)
INT_MAX = 2**31 - 1
KEY_NEG_INF = -2139095041
MASK_NEG = -1e30
CODE_KEEP = -1
CODE_DROP = INT_MAX
TIE_BOUND_ALL = 2**30
LOG2E = math.log2(math.e)

_F32 = jnp.float32
_BF16 = jnp.bfloat16


def _dot(a, b):
    return jnp.dot(a, b, preferred_element_type=_F32)


def _dot_nt(a, b):
    return lax.dot_general(a, b, (((1,), (1,)), ((), ())), preferred_element_type=_F32)


def _sigmoid(x):
    return 1.0 / (1.0 + jnp.exp(-x))


def _rms(x, w):
    return x * lax.rsqrt(jnp.mean(x * x, axis=-1, keepdims=True) + EPS) * w


def _norm_proj_kernel(x_ref, nw_ref, w_ref, ws_ref, o_ref, os_ref, *, chunk):
    h = _rms(x_ref[...], nw_ref[...]).astype(_BF16)
    os_ref[...] = _dot(h, ws_ref[...])
    for c0 in range(0, w_ref.shape[1], chunk):
        o_ref[:, c0:c0 + chunk] = _dot(h, w_ref[:, c0:c0 + chunk]).astype(o_ref.dtype)


def _norm_proj(x2d, nw, w, ws, tn):
    n, d = x2d.shape
    c = w.shape[1]
    const = functools.partial(pl.BlockSpec, pipeline_mode=pl.Buffered(1))
    return pl.pallas_call(
        functools.partial(_norm_proj_kernel, chunk=_pick_tile(c, (768, 512, 256, 128))),
        grid=(n // tn,),
        in_specs=[
            pl.BlockSpec((tn, d), lambda i: (i, 0)),
            const((1, d), lambda i: (0, 0)),
            const((d, c), lambda i: (0, 0)),
            const((d, LANE), lambda i: (0, 0)),
        ],
        out_specs=[pl.BlockSpec((tn, c), lambda i: (i, 0)), pl.BlockSpec((tn, LANE), lambda i: (i, 0))],
        out_shape=[jax.ShapeDtypeStruct((n, c), _BF16), jax.ShapeDtypeStruct((n, LANE), _F32)],
        compiler_params=pltpu.CompilerParams(
            dimension_semantics=("arbitrary",), vmem_limit_bytes=VMEM_LIMIT),
        name="proj",
    )(x2d, nw, w, ws)


def _gdn_kernel(qkvz_ref, sm_ref, cw_ref, arow_ref, dtrow_ref, naw_ref, o_ref,
                xbuf, s_ref, kn_s, kb_s, qn_s, rhs_s, dec_s, qdec_s, ktt_s, eg_s, m_s, x_s):
    tt = qkvz_ref.shape[0]
    ct = GDN_CHUNK
    nc = tt // ct
    hd = H_A * DK_A

    @pl.when(pl.program_id(1) == 0)
    def _():
        xbuf[:, 0:8, :] = jnp.zeros((3, 8, hd), _F32)
        s_ref[...] = jnp.zeros_like(s_ref)

    for idx in range(3):
        xbuf[idx, 8:8 + tt, :] = qkvz_ref[:, idx * hd:(idx + 1) * hd].astype(_F32)

    row = lax.broadcasted_iota(jnp.int32, (ct, ct), 0)
    col = lax.broadcasted_iota(jnp.int32, (ct, ct), 1)
    tri = (col <= row).astype(_F32)
    strict = col < row
    eye = (col == row).astype(_F32)
    n_sq = int(math.log2(ct))
    heads = range(H_A)

    def prepare(c):
        r0 = c * ct
        sm = sm_ref[r0:r0 + ct, :]
        beta_full = _sigmoid(sm)
        xg = sm + dtrow_ref[...]
        softplus = jnp.maximum(xg, 0.0) + jnp.log(1.0 + jnp.exp(-jnp.abs(xg)))
        g_full = -jnp.exp(arow_ref[...]) * softplus
        gcum = jnp.dot(tri, g_full, preferred_element_type=_F32, precision=lax.Precision.HIGHEST)
        gcum_t = gcum.T
        for h in heads:
            sl = slice(h * DK_A, (h + 1) * DK_A)
            conv = []
            for idx in range(3):
                acc = None
                for j in range(CONV_K):
                    lo = r0 + 8 - (CONV_K - 1) + j
                    term = cw_ref[j:j + 1, idx * hd + h * DK_A: idx * hd + (h + 1) * DK_A] * xbuf[idx, lo:lo + ct, sl]
                    acc = term if acc is None else acc + term
                conv.append(acc * _sigmoid(acc))
            qh, kh, vh = conv
            qn = qh * lax.rsqrt(jnp.sum(qh * qh, axis=-1, keepdims=True) + EPS) * (DK_A ** -0.5)
            kn = kh * lax.rsqrt(jnp.sum(kh * kh, axis=-1, keepdims=True) + EPS)
            bcol = beta_full[:, h:h + 1]
            gcol = gcum[:, H_A + h:H_A + h + 1]
            grow = gcum_t[H_A + h:H_A + h + 1, :]
            glast = gcum[ct - 1:ct, H_A + h:H_A + h + 1]
            kb = kn * bcol
            eg = jnp.exp(gcol)
            rhs_s[c, h, :, 0:DV_A] = vh * bcol
            rhs_s[c, h, :, DV_A:DV_A + DK_A] = kb * eg
            dec_s[c, h] = jnp.where(strict, jnp.exp(jnp.where(strict, gcol - grow, 0.0)), 0.0)
            qdec_s[c, h] = (qn * eg).astype(_BF16)
            ktt_s[c, h] = (kn * jnp.exp(glast - gcol)).T.astype(_BF16)
            kn_s[c, h] = kn.astype(_BF16)
            kb_s[c, h] = kb.astype(_BF16)
            qn_s[c, h] = qn.astype(_BF16)
            eg_s[c, h] = jnp.broadcast_to(jnp.exp(glast), (DK_A, DV_A))

    def solve(c):
        for h in heads:
            m = -(_dot_nt(kb_s[c, h], kn_s[c, h]) * dec_s[c, h])
            m_s[c, h] = m
            x_s[c, h] = m
        for _ in range(n_sq - 1):
            for h in heads:
                mb = m_s[c, h].astype(_BF16)
                m = _dot(mb, mb)
                m_s[c, h] = m
                x_s[c, h] = x_s[c, h] + m + _dot(x_s[c, h].astype(_BF16), m.astype(_BF16))
        for h in heads:
            rhs = rhs_s[c, h]
            rhs_s[c, h] = rhs + _dot(x_s[c, h].astype(_BF16), rhs.astype(_BF16))
        for h in heads:
            m_s[c, h] = _dot_nt(qn_s[c, h], kn_s[c, h]) * (dec_s[c, h] + eye)

    def recur(c):
        for h in heads:
            sb = s_ref[h].astype(_BF16)
            vnew = rhs_s[c, h, :, 0:DV_A] - _dot(rhs_s[c, h, :, DV_A:DV_A + DK_A].astype(_BF16), sb)
            vb = vnew.astype(_BF16)
            o = _dot(qdec_s[c, h], sb) + _dot(m_s[c, h].astype(_BF16), vb)
            s_ref[h] = s_ref[h] * eg_s[c, h] + _dot(ktt_s[c, h], vb)
            rhs_s[c, h, :, 0:DV_A] = o

    def finish(c):
        r0 = c * ct
        naw = naw_ref[...]
        for h in heads:
            sl = slice(h * DV_A, (h + 1) * DV_A)
            o = rhs_s[c, h, :, 0:DV_A]
            z = qkvz_ref[r0:r0 + ct, 3 * hd + h * DV_A:3 * hd + (h + 1) * DV_A].astype(_F32)
            on = o * lax.rsqrt(jnp.mean(o * o, axis=-1, keepdims=True) + EPS) * naw
            o_ref[r0:r0 + ct, sl] = (on * (z * _sigmoid(z))).astype(o_ref.dtype)

    for c in range(nc):
        prepare(c)
    for idx in range(3):
        xbuf[idx, 0:8, :] = xbuf[idx, tt:tt + 8, :]
    for c in range(nc):
        solve(c)
    for c in range(nc):
        recur(c)
    for c in range(nc):
        finish(c)


def _gdn(pb, ps, cw, arow, dtrow, naw, b_, t_, col0):
    ct = GDN_CHUNK
    nc = GDN_STEP_CHUNKS if t_ % (GDN_STEP_CHUNKS * ct) == 0 else 1
    tt = nc * ct
    hd = H_A * DK_A
    nt = t_ // tt
    assert col0 % (4 * hd) == 0
    cb = col0 // (4 * hd)
    const = lambda shape: pl.BlockSpec(shape, lambda b, t: (0, 0))
    return pl.pallas_call(
        _gdn_kernel,
        grid=(b_, nt),
        in_specs=[pl.BlockSpec((tt, 4 * hd), lambda b, t: (b * nt + t, cb)),
                  pl.BlockSpec((tt, LANE), lambda b, t: (b * nt + t, 0)),
                  const((CONV_K, 3 * hd)), const((1, LANE)), const((1, LANE)), const((1, DV_A))],
        out_specs=pl.BlockSpec((tt, hd), lambda b, t: (b * nt + t, 0)),
        out_shape=jax.ShapeDtypeStruct((b_ * t_, hd), _BF16),
        scratch_shapes=[
            pltpu.VMEM((3, tt + 8, hd), _F32),
            pltpu.VMEM((H_A, DK_A, DV_A), _F32),
            pltpu.VMEM((nc, H_A, ct, DK_A), _BF16),
            pltpu.VMEM((nc, H_A, ct, DK_A), _BF16),
            pltpu.VMEM((nc, H_A, ct, DK_A), _BF16),
            pltpu.VMEM((nc, H_A, ct, DV_A + DK_A), _F32),
            pltpu.VMEM((nc, H_A, ct, ct), _F32),
            pltpu.VMEM((nc, H_A, ct, DK_A), _BF16),
            pltpu.VMEM((nc, H_A, DK_A, ct), _BF16),
            pltpu.VMEM((nc, H_A, DK_A, DV_A), _F32),
            pltpu.VMEM((nc, H_A, ct, ct), _F32),
            pltpu.VMEM((nc, H_A, ct, ct), _F32),
        ],
        compiler_params=pltpu.CompilerParams(
            dimension_semantics=("arbitrary", "arbitrary"), vmem_limit_bytes=VMEM_LIMIT),
        name="gdn",
    )(pb, ps, cw, arow, dtrow, naw)


def _bias_kernel(tab_ref, bucket_ref, o_ref):
    bucket = bucket_ref[...]
    nq, w = bucket.shape
    for h in range(H_B):
        acc = jnp.zeros((nq, w), _F32)
        for b in range(N_BUCKETS):
            acc = acc + jnp.where(bucket == b, tab_ref[b, h], 0.0)
        far = acc[:, 2 * LANE:3 * LANE]
        o_ref[h] = (acc[:, 0:2 * LANE] - jnp.concatenate([far, far], axis=1)) * LOG2E


def _bias_tiles(rel_table, bucket):
    return pl.pallas_call(
        _bias_kernel,
        in_specs=[pl.BlockSpec(memory_space=pltpu.SMEM),
                  pl.BlockSpec(bucket.shape, lambda: (0, 0))],
        out_specs=pl.BlockSpec((H_B, Q_BLOCK, 2 * LANE), lambda: (0, 0, 0)),
        out_shape=jax.ShapeDtypeStruct((H_B, Q_BLOCK, 2 * LANE), _F32),
        name="rel_bias_tiles",
    )(rel_table, bucket)


def _dsa_kernel(q_ref, k_ref, v_ref, iq_ref, ik_ref, sm_ref, bias_ref, o_ref,
                key_s, qs_s, iwb_s, m_s, l_s, acc_s, s_s, si_s, sf_s, *, topk):
    nq = q_ref.shape[0]
    st_w = KEY_TILE
    assert nq == st_w
    ng = st_w // LANE
    qb = pl.program_id(1)
    n_st = qb + 1
    hd = D_HEAD_B
    scale = D_HEAD_B ** -0.5 * LOG2E
    f_topk = float(topk)

    lane_j = lax.broadcasted_iota(jnp.int32, (nq, LANE), 1)
    row_i = lax.broadcasted_iota(jnp.int32, (nq, LANE), 0)

    def grp(g):
        return slice(g * LANE, (g + 1) * LANE)

    iw = sm_ref[...]
    for h in range(H_B):
        sl = slice(h * hd, (h + 1) * hd)
        qs_s[h] = (q_ref[:, sl].astype(_F32) * scale).astype(_BF16)
        iwb_s[h] = jnp.broadcast_to(iw[:, 2 * H_A + h:2 * H_A + h + 1], (nq, LANE))
        m_s[h] = jnp.full((nq, LANE), MASK_NEG, _F32)
        l_s[h] = jnp.zeros((nq, LANE), _F32)
        acc_s[h] = jnp.zeros((nq, hd), _F32)

    def scores(st, diag):
        ikt = ik_ref[pl.ds(pl.multiple_of(st * st_w, st_w), st_w), :]
        acc = [jnp.zeros((nq, LANE), _F32) for _ in range(ng)]
        for p in range(H_IDX // 2):
            xq = iq_ref[:, p * LANE:(p + 1) * LANE]
            for half in range(2):
                h = 2 * p + half
                s = jnp.maximum(_dot_nt(xq, ikt[:, half * LANE:(half + 1) * LANE]), 0.0)
                w = iwb_s[h]
                for g in range(ng):
                    acc[g] = acc[g] + w * s[:, grp(g)]
        for g in range(ng):
            a = acc[g]
            if diag:
                a = jnp.where((g * LANE + lane_j) // CHUNK <= row_i // CHUNK, a, -jnp.inf)
            bits = pltpu.bitcast(a, jnp.int32)
            key_s[st, g] = bits ^ ((bits >> 31) & INT_MAX)

    def p1(st, carry):
        scores(st, False)
        return carry

    lax.fori_loop(0, qb, p1, 0)
    scores(qb, True)

    ones_b = jnp.ones((LANE, LANE), _BF16)
    assert key_s.shape[0] * ng <= 256

    T_, CAND_, MIDX_ = 0, 1, 2
    ACC_, CNT_ = 0, 1
    rb = 64
    lane_rb = lax.broadcasted_iota(jnp.int32, (rb, LANE), 1)

    def count(pred_fn):
        def tile(st, first):
            for r0 in range(0, nq, rb):
                rs = slice(r0, r0 + rb)
                acc = None if first else sf_s[ACC_, rs, :]
                for g in range(ng):
                    one = jnp.where(pred_fn(key_s[st, g, rs, :], rs), 1.0, 0.0)
                    acc = one if acc is None else acc + one
                sf_s[ACC_, rs, :] = acc

        def body(st, carry):
            tile(st, False)
            return carry

        tile(0, True)
        lax.fori_loop(1, n_st, body, 0)
        return _dot(sf_s[ACC_].astype(_BF16), ones_b)

    si_s[CAND_] = jnp.zeros((nq, LANE), jnp.int32)

    def bit_body(i, carry):
        half = jnp.left_shift(jnp.int32(1), 30 - i)
        c = count(lambda k, rs: k >= si_s[CAND_, rs, :])
        si_s[CAND_] = si_s[CAND_] + jnp.where(c >= f_topk, half, -half)
        return carry

    lax.fori_loop(0, 31, bit_body, 0)
    c_last = count(lambda k, rs: k >= si_s[CAND_, rs, :])
    si_s[T_] = jnp.where(c_last >= f_topk, si_s[CAND_], si_s[CAND_] - 1)
    sf_s[CNT_] = count(lambda k, rs: k >= si_s[T_, rs, :])

    def encode(st, carry):
        for r0 in range(0, nq, rb):
            rs = slice(r0, r0 + rb)
            t = si_s[T_, rs, :]
            for g in range(ng):
                k = key_s[st, g, rs, :]
                pos = st * st_w + g * LANE + lane_rb
                key_s[st, g, rs, :] = jnp.where(k > t, CODE_KEEP, jnp.where(k == t, pos, CODE_DROP))
        return carry

    lax.fori_loop(0, n_st, encode, 0)
    finite_thr = si_s[T_] != KEY_NEG_INF
    si_s[MIDX_] = jnp.where(finite_thr, TIE_BOUND_ALL, 0)

    excess = jnp.max(jnp.where(finite_thr, sf_s[CNT_], 0.0)) > f_topk

    @pl.when(excess)
    def _():
        r_keep = f_topk - count(lambda code, rs: code < 0)
        sf_s[CNT_] = jnp.where(finite_thr, r_keep, 0.0)
        sf_s[ACC_] = jnp.zeros((nq, LANE), _F32)
        upper = (lax.broadcasted_iota(jnp.int32, (LANE, LANE), 0)
                 <= lax.broadcasted_iota(jnp.int32, (LANE, LANE), 1)).astype(_BF16)

        def rank_ties(st, carry):
            for r0 in range(0, nq, rb):
                rs = slice(r0, r0 + rb)
                r_blk = sf_s[CNT_, rs, :]
                before = sf_s[ACC_, rs, :]
                for g in range(ng):
                    code = key_s[st, g, rs, :]
                    tie = jnp.where(code >= 0, jnp.where(code < CODE_DROP, 1.0, 0.0), 0.0)
                    tie_b = tie.astype(_BF16)
                    rank = before + _dot(tie_b, upper)
                    kept_tie = jnp.where(rank <= r_blk, tie, 0.0)
                    mb = jnp.where(code < 0, 0.0, jnp.where(kept_tie > 0.0, 0.0, MASK_NEG))
                    key_s[st, g, rs, :] = pltpu.bitcast(mb, jnp.int32)
                    before = before + _dot(tie_b, ones_b)
                sf_s[ACC_, rs, :] = before
            return carry

        lax.fori_loop(0, n_st, rank_ties, 0)

    @pl.when(jnp.logical_not(excess))
    def _():
        def p2(st, carry):
            for r0 in range(0, nq, rb):
                rs = slice(r0, r0 + rb)
                bound = si_s[MIDX_, rs, :]
                for g in range(ng):
                    mb = jnp.where(key_s[st, g, rs, :] < bound, 0.0, MASK_NEG)
                    key_s[st, g, rs, :] = pltpu.bitcast(mb, jnp.int32)
            return carry

        lax.fori_loop(0, n_st, p2, 0)

    def attend(st, mode):
        off = pl.multiple_of(st * st_w, st_w)
        for h in range(H_B):
            sl = slice(h * hd, (h + 1) * hd)
            mask = jnp.concatenate([pltpu.bitcast(key_s[st, g], _F32) for g in range(ng)], axis=1)
            s = _dot_nt(qs_s[h], k_ref[pl.ds(off, st_w), sl]) + mask
            if mode:
                s_s[...] = s
                b_prev = bias_ref[h, :, 0:LANE]
                if mode == 1:
                    s_s[0:LANE, (ng - 1) * LANE:ng * LANE] += b_prev
                else:
                    b_diag = bias_ref[h, :, LANE:2 * LANE]
                    for rt in range(ng):
                        s_s[grp(rt), grp(rt)] += b_diag
                        if rt:
                            s_s[grp(rt), grp(rt - 1)] += b_prev
                s = s_s[...]
            m_old = m_s[h]
            gmax = s[:, grp(0)]
            for g in range(1, ng):
                gmax = jnp.maximum(gmax, s[:, grp(g)])
            m_new = jnp.maximum(m_old, jnp.max(gmax, axis=1, keepdims=True))
            alpha = jnp.exp2(m_old - m_new)
            pieces = [jnp.exp2(s[:, grp(g)] - m_new) for g in range(ng)]
            psum = pieces[0]
            for g in range(1, ng):
                psum = psum + pieces[g]
            p = jnp.concatenate(pieces, axis=1).astype(_BF16)
            m_s[h] = m_new
            l_s[h] = alpha * l_s[h] + psum
            acc_s[h] = alpha * acc_s[h] + _dot(p, v_ref[pl.ds(off, st_w), sl])

    def p3(st, carry):
        attend(st, 0)
        return carry

    lax.fori_loop(0, qb - 1, p3, 0)

    @pl.when(qb > 0)
    def _():
        attend(qb - 1, 1)

    attend(qb, 2)

    for h in range(H_B):
        sl = slice(h * hd, (h + 1) * hd)
        l_fin = jnp.sum(l_s[h], axis=1, keepdims=True)
        o_ref[:, sl] = (acc_s[h] / l_fin).astype(o_ref.dtype)


def _dsa(pb, ps, bias, b_, t_, col0):
    nq = KEY_TILE
    nb = t_ // nq
    hd = H_B * D_HEAD_B
    topk = min(TOPK_MAX, t_ // 4)
    iq_w = H_IDX * D_IDX
    ik_w = 2 * LANE
    assert col0 % hd == 0 and (col0 + 3 * hd) % iq_w == 0 and (col0 + 3 * hd + iq_w) % ik_w == 0
    cq = col0 // hd
    resident = functools.partial(pl.BlockSpec, pipeline_mode=pl.Buffered(1))
    return pl.pallas_call(
        functools.partial(_dsa_kernel, topk=topk),
        grid=(b_, nb),
        in_specs=[
            pl.BlockSpec((nq, hd), lambda b, i: (b * nb + i, cq)),
            resident((t_, hd), lambda b, i: (b, cq + 1)),
            resident((t_, hd), lambda b, i: (b, cq + 2)),
            pl.BlockSpec((nq, iq_w), lambda b, i: (b * nb + i, (col0 + 3 * hd) // iq_w)),
            resident((t_, ik_w), lambda b, i: (b, (col0 + 3 * hd + iq_w) // ik_w)),
            pl.BlockSpec((nq, LANE), lambda b, i: (b * nb + i, 0)),
            resident((H_B, Q_BLOCK, 2 * LANE), lambda b, i: (0, 0, 0)),
        ],
        out_specs=pl.BlockSpec((nq, hd), lambda b, i: (b * nb + i, 0)),
        out_shape=jax.ShapeDtypeStruct((b_ * t_, hd), _BF16),
        scratch_shapes=[
            pltpu.VMEM((nb, KEY_TILE // LANE, nq, LANE), jnp.int32),
            pltpu.VMEM((H_B, nq, D_HEAD_B), _BF16),
            pltpu.VMEM((H_B, nq, LANE), _F32),
            pltpu.VMEM((H_B, nq, LANE), _F32),
            pltpu.VMEM((H_B, nq, LANE), _F32),
            pltpu.VMEM((H_B, nq, D_HEAD_B), _F32),
            pltpu.VMEM((nq, KEY_TILE), _F32),
            pltpu.VMEM((3, nq, LANE), jnp.int32),
            pltpu.VMEM((2, nq, LANE), _F32),
        ],
        compiler_params=pltpu.CompilerParams(
            dimension_semantics=("arbitrary", "arbitrary"), vmem_limit_bytes=VMEM_LIMIT),
        name="dsa",
    )(pb, pb, pb, pb, pb, ps, bias)


def _merge_kernel(oa_ref, ob_ref, g_ref, x_ref, bg_ref, wa_ref, wb_ref, wo_ref, o_ref):
    d = x_ref.shape[1]
    a = _dot(oa_ref[...], wa_ref[...])
    b = _dot(ob_ref[...], wb_ref[...])
    g = _sigmoid(g_ref[...].astype(_F32) + bg_ref[...])
    merged = g[:, 0:d] * a + g[:, d:2 * d] * b
    o_ref[...] = x_ref[...] + _dot(merged.astype(_BF16), wo_ref[...])


def _merge(oa, ob, pb, x2d, bg, wa, wb, wo, tn, col_g):
    n, d = x2d.shape
    row = lambda w: pl.BlockSpec((tn, w), lambda i: (i, 0))
    const = lambda a: pl.BlockSpec(a.shape, lambda i: (0, 0))
    return pl.pallas_call(
        _merge_kernel,
        grid=(n // tn,),
        in_specs=[row(oa.shape[1]), row(ob.shape[1]),
                  pl.BlockSpec((tn, 2 * d), lambda i: (i, col_g // (2 * d))), row(d),
                  const(bg), const(wa), const(wb), const(wo)],
        out_specs=row(d),
        out_shape=jax.ShapeDtypeStruct((n, d), _F32),
        compiler_params=pltpu.CompilerParams(
            dimension_semantics=("arbitrary",), vmem_limit_bytes=VMEM_LIMIT),
        name="merge",
    )(oa, ob, pb, x2d, bg, wa, wb, wo)


def _mlp_kernel(x_ref, n2_ref, w1_ref, w2_ref, nf_ref, o_ref, *, final_norm):
    x = x_ref[...]
    h2 = _rms(x, n2_ref[...]).astype(_BF16)
    hid = jnp.maximum(_dot(h2, w1_ref[...]), 0.0)
    y = x + _dot((hid * hid).astype(_BF16), w2_ref[...])
    o_ref[...] = _rms(y, nf_ref[...]) if final_norm else y


def _mlp(x1, n2, w1, w2, nf, tn, final_norm):
    n, d = x1.shape
    row = pl.BlockSpec((tn, d), lambda i: (i, 0))
    const = lambda a: pl.BlockSpec(a.shape, lambda i: (0, 0))
    return pl.pallas_call(
        functools.partial(_mlp_kernel, final_norm=final_norm),
        grid=(n // tn,),
        in_specs=[row, const(n2), const(w1), const(w2), const(nf)],
        out_specs=row,
        out_shape=jax.ShapeDtypeStruct((n, d), _F32),
        compiler_params=pltpu.CompilerParams(
            dimension_semantics=("arbitrary",), vmem_limit_bytes=VMEM_LIMIT),
        name="mlp",
    )(x1, n2, w1, w2, nf)


def _t5_bucket(rel):
    half = N_BUCKETS // 2
    max_exact = half // 2
    base = jnp.where(rel > 0, half, 0)
    n = jnp.abs(rel)
    n_f = jnp.maximum(n, 1).astype(jnp.float32)
    large = max_exact + (jnp.log(n_f / max_exact) / math.log(MAX_DISTANCE / max_exact)
                         * (half - max_exact)).astype(jnp.int32)
    large = jnp.minimum(large, half - 1)
    return base + jnp.where(n < max_exact, n, large)


def _pick_tile(n, prefs):
    for t in prefs:
        if n % t == 0:
            return t
    raise ValueError(f"no tile in {prefs} divides {n}")


def kernel(x, norm1_w, w_in, conv_a_w, a_log, dt_bias, norm_a_w, rel_bias_table, w_gate, b_gate,
           w_proj_a, w_proj_b, w_out, norm2_w, w_ff1, w_ff2, norm_final_w):
    b_, t_, d = x.shape
    depth = norm1_w.shape[0]
    n = b_ * t_
    ha, hb = H_A * DK_A, H_B * D_HEAD_B
    assert t_ % KEY_TILE == 0 and t_ % GDN_CHUNK == 0 and d % LANE == 0
    assert DK_A == DV_A == D_HEAD_B == LANE and 2 * D_IDX == LANE
    assert Q_BLOCK >= MAX_DISTANCE

    o_za = 3 * ha
    o_ba = 4 * ha
    o_aa = o_ba + H_A
    o_qb = o_aa + H_A
    o_iq = o_qb + 3 * hb
    o_ik = o_iq + H_IDX * D_IDX
    o_iw = o_ik + D_IDX

    rel = (jnp.arange(2 * Q_BLOCK, dtype=jnp.int32)[None, :] - Q_BLOCK) \
        - jnp.arange(Q_BLOCK, dtype=jnp.int32)[:, None]
    far = jnp.full((Q_BLOCK, LANE), -(Q_BLOCK + 1), jnp.int32)
    bucket = _t5_bucket(jnp.concatenate([rel, far], axis=1))
    bias = _bias_tiles(rel_bias_table.astype(_F32), bucket)

    x2d = x.reshape(n, d)
    tn = _pick_tile(n, (512, 256, 128))
    for layer in range(depth):
        wi = w_in[layer]
        zpad = lambda w: jnp.zeros((d, w), wi.dtype)
        ws = jnp.concatenate([wi[:, o_ba:o_qb], wi[:, o_iw:o_iw + H_IDX], zpad(LANE - 3 * H_A)],
                             axis=1).astype(_BF16)
        ik = wi[:, o_ik:o_iw]
        wb16 = jnp.concatenate([wi[:, 0:o_ba], w_gate[layer], wi[:, o_qb:o_ik],
                                ik, zpad(D_IDX), zpad(D_IDX), ik], axis=1).astype(_BF16)
        nw = norm1_w[layer].reshape(1, d).astype(_F32)
        pb, ps = _norm_proj(x2d, nw, wb16, ws, tn)

        col_a = 0
        col_g = 4 * ha
        col_b = col_g + 2 * d
        assert col_g % (2 * d) == 0
        lanes = lambda v, off: jnp.zeros((1, LANE), _F32).at[0, off:off + H_A].set(v.astype(_F32))
        oa = _gdn(pb, ps, conv_a_w[layer].astype(_F32), lanes(a_log[layer], H_A), lanes(dt_bias[layer], H_A),
                  norm_a_w[layer].reshape(1, DV_A).astype(_F32), b_, t_, col_a)
        ob = _dsa(pb, ps, bias, b_, t_, col_b)

        x2d = _merge(oa, ob, pb, x2d, b_gate[layer].reshape(1, 2 * d).astype(_F32),
                     w_proj_a[layer].astype(_BF16), w_proj_b[layer].astype(_BF16),
                     w_out[layer].astype(_BF16), tn, col_g)
        x2d = _mlp(x2d, norm2_w[layer].reshape(1, d).astype(_F32), w_ff1[layer].astype(_BF16),
                   w_ff2[layer].astype(_BF16), norm_final_w.reshape(1, d).astype(_F32),
                   _pick_tile(n, (256, 128)), final_norm=layer == depth - 1)
    return x2d.reshape(b_, t_, d)
```

```python
import functools
import math

import jax
import jax.numpy as jnp
import numpy as np
from jax import lax
from jax.experimental import pallas as pl
from jax.experimental.pallas import tpu as pltpu

EPS = 1e-6
H_A = 8
DK_A = 128
DV_A = 128
CONV_K = 4
GDN_CHUNK = 128
GDN_STEP_CHUNKS = 2
H_B = 8
D_HEAD_B = 128
H_IDX = 8
D_IDX = 64
TOPK_MAX = 256
Q_BLOCK = 128
CHUNK = 64
KEY_TILE = 512
N_BUCKETS = 32
MAX_DISTANCE = 128

LANE = 128
VMEM_LIMIT = 56 * 1024 * 1024

INT_MIN = -(2**31)
INT_MAX = 2**31 - 1
KEY_NEG_INF = -2139095041
MASK_NEG = -1e30
CODE_KEEP = -1
CODE_DROP = INT_MAX
TIE_BOUND_ALL = 2**30
LOG2E = math.log2(math.e)

_F32 = jnp.float32
_BF16 = jnp.bfloat16


def _dot(a, b):
    return jnp.dot(a, b, preferred_element_type=_F32)


def _dot_nt(a, b):
    return lax.dot_general(a, b, (((1,), (1,)), ((), ())), preferred_element_type=_F32)


def _sigmoid(x):
    return 1.0 / (1.0 + jnp.exp(-x))


def _rms(x, w):
    return x * lax.rsqrt(jnp.mean(x * x, axis=-1, keepdims=True) + EPS) * w


def _norm_proj_kernel(x_ref, nw_ref, w_ref, ws_ref, o_ref, os_ref, *, chunk):
    h = _rms(x_ref[...], nw_ref[...]).astype(_BF16)
    os_ref[...] = _dot(h, ws_ref[...])
    for c0 in range(0, w_ref.shape[1], chunk):
        o_ref[:, c0:c0 + chunk] = _dot(h, w_ref[:, c0:c0 + chunk]).astype(o_ref.dtype)


def _norm_proj(x2d, nw, w, ws, tn):
    n, d = x2d.shape
    c = w.shape[1]
    const = functools.partial(pl.BlockSpec, pipeline_mode=pl.Buffered(1))
    return pl.pallas_call(
        functools.partial(_norm_proj_kernel, chunk=_pick_tile(c, (768, 512, 256, 128))),
        grid=(n // tn,),
        in_specs=[
            pl.BlockSpec((tn, d), lambda i: (i, 0)),
            const((1, d), lambda i: (0, 0)),
            const((d, c), lambda i: (0, 0)),
            const((d, LANE), lambda i: (0, 0)),
        ],
        out_specs=[pl.BlockSpec((tn, c), lambda i: (i, 0)), pl.BlockSpec((tn, LANE), lambda i: (i, 0))],
        out_shape=[jax.ShapeDtypeStruct((n, c), _BF16), jax.ShapeDtypeStruct((n, LANE), _F32)],
        compiler_params=pltpu.CompilerParams(
            dimension_semantics=("arbitrary",), vmem_limit_bytes=VMEM_LIMIT),
        name="proj",
    )(x2d, nw, w, ws)


def _gdn_kernel(qkvz_ref, sm_ref, cw_ref, arow_ref, dtrow_ref, naw_ref, o_ref,
                xbuf, s_ref, kn_s, kb_s, qn_s, rhs_s, dec_s, qdec_s, ktt_s, eg_s, m_s, x_s):
    tt = qkvz_ref.shape[0]
    ct = GDN_CHUNK
    nc = tt // ct
    hd = H_A * DK_A

    @pl.when(pl.program_id(1) == 0)
    def _():
        xbuf[:, 0:8, :] = jnp.zeros((3, 8, hd), _F32)
        s_ref[...] = jnp.zeros_like(s_ref)

    for idx in range(3):
        xbuf[idx, 8:8 + tt, :] = qkvz_ref[:, idx * hd:(idx + 1) * hd].astype(_F32)

    row = lax.broadcasted_iota(jnp.int32, (ct, ct), 0)
    col = lax.broadcasted_iota(jnp.int32, (ct, ct), 1)
    tri = (col <= row).astype(_F32)
    strict = col < row
    eye = (col == row).astype(_F32)
    n_sq = int(math.log2(ct))
    heads = range(H_A)

    def prepare(c):
        r0 = c * ct
        sm = sm_ref[r0:r0 + ct, :]
        beta_full = _sigmoid(sm)
        xg = sm + dtrow_ref[...]
        softplus = jnp.maximum(xg, 0.0) + jnp.log(1.0 + jnp.exp(-jnp.abs(xg)))
        g_full = -jnp.exp(arow_ref[...]) * softplus
        gcum = jnp.dot(tri, g_full, preferred_element_type=_F32, precision=lax.Precision.HIGHEST)
        gcum_t = gcum.T
        for h in heads:
            sl = slice(h * DK_A, (h + 1) * DK_A)
            conv = []
            for idx in range(3):
                acc = None
                for j in range(CONV_K):
                    lo = r0 + 8 - (CONV_K - 1) + j
                    term = cw_ref[j:j + 1, idx * hd + h * DK_A: idx * hd + (h + 1) * DK_A] * xbuf[idx, lo:lo + ct, sl]
                    acc = term if acc is None else acc + term
                conv.append(acc * _sigmoid(acc))
            qh, kh, vh = conv
            qn = qh * lax.rsqrt(jnp.sum(qh * qh, axis=-1, keepdims=True) + EPS) * (DK_A ** -0.5)
            kn = kh * lax.rsqrt(jnp.sum(kh * kh, axis=-1, keepdims=True) + EPS)
            bcol = beta_full[:, h:h + 1]
            gcol = gcum[:, H_A + h:H_A + h + 1]
            grow = gcum_t[H_A + h:H_A + h + 1, :]
            glast = gcum[ct - 1:ct, H_A + h:H_A + h + 1]
            kb = kn * bcol
            eg = jnp.exp(gcol)
            rhs_s[c, h, :, 0:DV_A] = vh * bcol
            rhs_s[c, h, :, DV_A:DV_A + DK_A] = kb * eg
            dec_s[c, h] = jnp.where(strict, jnp.exp(jnp.where(strict, gcol - grow, 0.0)), 0.0)
            qdec_s[c, h] = (qn * eg).astype(_BF16)
            ktt_s[c, h] = (kn * jnp.exp(glast - gcol)).T.astype(_BF16)
            kn_s[c, h] = kn.astype(_BF16)
            kb_s[c, h] = kb.astype(_BF16)
            qn_s[c, h] = qn.astype(_BF16)
            eg_s[c, h] = jnp.broadcast_to(jnp.exp(glast), (DK_A, DV_A))

    def solve(c):
        for h in heads:
            m = -(_dot_nt(kb_s[c, h], kn_s[c, h]) * dec_s[c, h])
            m_s[c, h] = m
            x_s[c, h] = m
        for _ in range(n_sq - 1):
            for h in heads:
                mb = m_s[c, h].astype(_BF16)
                m = _dot(mb, mb)
                m_s[c, h] = m
                x_s[c, h] = x_s[c, h] + m + _dot(x_s[c, h].astype(_BF16), m.astype(_BF16))
        for h in heads:
            rhs = rhs_s[c, h]
            rhs_s[c, h] = rhs + _dot(x_s[c, h].astype(_BF16), rhs.astype(_BF16))
        for h in heads:
            m_s[c, h] = _dot_nt(qn_s[c, h], kn_s[c, h]) * (dec_s[c, h] + eye)

    def recur(c):
        for h in heads:
            sb = s_ref[h].astype(_BF16)
            vnew = rhs_s[c, h, :, 0:DV_A] - _dot(rhs_s[c, h, :, DV_A:DV_A + DK_A].astype(_BF16), sb)
            vb = vnew.astype(_BF16)
            o = _dot(qdec_s[c, h], sb) + _dot(m_s[c, h].astype(_BF16), vb)
            s_ref[h] = s_ref[h] * eg_s[c, h] + _dot(ktt_s[c, h], vb)
            rhs_s[c, h, :, 0:DV_A] = o

    def finish(c):
        r0 = c * ct
        naw = naw_ref[...]
        for h in heads:
            sl = slice(h * DV_A, (h + 1) * DV_A)
            o = rhs_s[c, h, :, 0:DV_A]
            z = qkvz_ref[r0:r0 + ct, 3 * hd + h * DV_A:3 * hd + (h + 1) * DV_A].astype(_F32)
            on = o * lax.rsqrt(jnp.mean(o * o, axis=-1, keepdims=True) + EPS) * naw
            o_ref[r0:r0 + ct, sl] = (on * (z * _sigmoid(z))).astype(o_ref.dtype)

    for c in range(nc):
        prepare(c)
    for idx in range(3):
        xbuf[idx, 0:8, :] = xbuf[idx, tt:tt + 8, :]
    for c in range(nc):
        solve(c)
    for c in range(nc):
        recur(c)
    for c in range(nc):
        finish(c)


def _gdn(pb, ps, cw, arow, dtrow, naw, b_, t_, col0):
    ct = GDN_CHUNK
    nc = GDN_STEP_CHUNKS if t_ % (GDN_STEP_CHUNKS * ct) == 0 else 1
    tt = nc * ct
    hd = H_A * DK_A
    nt = t_ // tt
    assert col0 % (4 * hd) == 0
    cb = col0 // (4 * hd)
    const = lambda shape: pl.BlockSpec(shape, lambda b, t: (0, 0))
    return pl.pallas_call(
        _gdn_kernel,
        grid=(b_, nt),
        in_specs=[pl.BlockSpec((tt, 4 * hd), lambda b, t: (b * nt + t, cb)),
                  pl.BlockSpec((tt, LANE), lambda b, t: (b * nt + t, 0)),
                  const((CONV_K, 3 * hd)), const((1, LANE)), const((1, LANE)), const((1, DV_A))],
        out_specs=pl.BlockSpec((tt, hd), lambda b, t: (b * nt + t, 0)),
        out_shape=jax.ShapeDtypeStruct((b_ * t_, hd), _BF16),
        scratch_shapes=[
            pltpu.VMEM((3, tt + 8, hd), _F32),
            pltpu.VMEM((H_A, DK_A, DV_A), _F32),
            pltpu.VMEM((nc, H_A, ct, DK_A), _BF16),
            pltpu.VMEM((nc, H_A, ct, DK_A), _BF16),
            pltpu.VMEM((nc, H_A, ct, DK_A), _BF16),
            pltpu.VMEM((nc, H_A, ct, DV_A + DK_A), _F32),
            pltpu.VMEM((nc, H_A, ct, ct), _F32),
            pltpu.VMEM((nc, H_A, ct, DK_A), _BF16),
            pltpu.VMEM((nc, H_A, DK_A, ct), _BF16),
            pltpu.VMEM((nc, H_A, DK_A, DV_A), _F32),
            pltpu.VMEM((nc, H_A, ct, ct), _F32),
            pltpu.VMEM((nc, H_A, ct, ct), _F32),
        ],
        compiler_params=pltpu.CompilerParams(
            dimension_semantics=("arbitrary", "arbitrary"), vmem_limit_bytes=VMEM_LIMIT),
        name="gdn",
    )(pb, ps, cw, arow, dtrow, naw)


def _bias_kernel(tab_ref, bucket_ref, o_ref):
    bucket = bucket_ref[...]
    nq, w = bucket.shape
    for h in range(H_B):
        acc = jnp.zeros((nq, w), _F32)
        for b in range(N_BUCKETS):
            acc = acc + jnp.where(bucket == b, tab_ref[b, h], 0.0)
        far = acc[:, 2 * LANE:3 * LANE]
        o_ref[h] = (acc[:, 0:2 * LANE] - jnp.concatenate([far, far], axis=1)) * LOG2E


def _bias_tiles(rel_table, bucket):
    return pl.pallas_call(
        _bias_kernel,
        in_specs=[pl.BlockSpec(memory_space=pltpu.SMEM),
                  pl.BlockSpec(bucket.shape, lambda: (0, 0))],
        out_specs=pl.BlockSpec((H_B, Q_BLOCK, 2 * LANE), lambda: (0, 0, 0)),
        out_shape=jax.ShapeDtypeStruct((H_B, Q_BLOCK, 2 * LANE), _F32),
        name="rel_bias_tiles",
    )(rel_table, bucket)


def _dsa_kernel(q_ref, k_ref, v_ref, iq_ref, ik_ref, sm_ref, bias_ref, o_ref,
                key_s, qs_s, iwb_s, m_s, l_s, acc_s, s_s, si_s, sf_s, *, topk):
    nq = q_ref.shape[0]
    st_w = KEY_TILE
    assert nq == st_w
    ng = st_w // LANE
    qb = pl.program_id(1)
    n_st = qb + 1
    hd = D_HEAD_B
    scale = D_HEAD_B ** -0.5 * LOG2E
    f_topk = float(topk)

    lane_j = lax.broadcasted_iota(jnp.int32, (nq, LANE), 1)
    row_i = lax.broadcasted_iota(jnp.int32, (nq, LANE), 0)

    def grp(g):
        return slice(g * LANE, (g + 1) * LANE)

    iw = sm_ref[...]
    for h in range(H_B):
        sl = slice(h * hd, (h + 1) * hd)
        qs_s[h] = (q_ref[:, sl].astype(_F32) * scale).astype(_BF16)
        iwb_s[h] = jnp.broadcast_to(iw[:, 2 * H_A + h:2 * H_A + h + 1], (nq, LANE))
        m_s[h] = jnp.full((nq, LANE), MASK_NEG, _F32)
        l_s[h] = jnp.zeros((nq, LANE), _F32)
        acc_s[h] = jnp.zeros((nq, hd), _F32)

    def scores(st, diag):
        ikt = ik_ref[pl.ds(pl.multiple_of(st * st_w, st_w), st_w), :]
        acc = [jnp.zeros((nq, LANE), _F32) for _ in range(ng)]
        for p in range(H_IDX // 2):
            xq = iq_ref[:, p * LANE:(p + 1) * LANE]
            for half in range(2):
                h = 2 * p + half
                s = jnp.maximum(_dot_nt(xq, ikt[:, half * LANE:(half + 1) * LANE]), 0.0)
                w = iwb_s[h]
                for g in range(ng):
                    acc[g] = acc[g] + w * s[:, grp(g)]
        for g in range(ng):
            a = acc[g]
            if diag:
                a = jnp.where((g * LANE + lane_j) // CHUNK <= row_i // CHUNK, a, -jnp.inf)
            bits = pltpu.bitcast(a, jnp.int32)
            key_s[st, g] = bits ^ ((bits >> 31) & INT_MAX)

    def p1(st, carry):
        scores(st, False)
        return carry

    lax.fori_loop(0, qb, p1, 0)
    scores(qb, True)

    ones_b = jnp.ones((LANE, LANE), _BF16)
    assert key_s.shape[0] * ng <= 256

    T_, CAND_, MIDX_ = 0, 1, 2
    ACC_, CNT_ = 0, 1
    rb = 64
    lane_rb = lax.broadcasted_iota(jnp.int32, (rb, LANE), 1)

    def count(pred_fn):
        def tile(st, first):
            for r0 in range(0, nq, rb):
                rs = slice(r0, r0 + rb)
                acc = None if first else sf_s[ACC_, rs, :]
                for g in range(ng):
                    one = jnp.where(pred_fn(key_s[st, g, rs, :], rs), 1.0, 0.0)
                    acc = one if acc is None else acc + one
                sf_s[ACC_, rs, :] = acc

        def body(st, carry):
            tile(st, False)
            return carry

        tile(0, True)
        lax.fori_loop(1, n_st, body, 0)
        return _dot(sf_s[ACC_].astype(_BF16), ones_b)

    si_s[CAND_] = jnp.zeros((nq, LANE), jnp.int32)

    def bit_body(i, carry):
        half = jnp.left_shift(jnp.int32(1), 30 - i)
        c = count(lambda k, rs: k >= si_s[CAND_, rs, :])
        si_s[CAND_] = si_s[CAND_] + jnp.where(c >= f_topk, half, -half)
        return carry

    lax.fori_loop(0, 31, bit_body, 0)
    c_last = count(lambda k, rs: k >= si_s[CAND_, rs, :])
    si_s[T_] = jnp.where(c_last >= f_topk, si_s[CAND_], si_s[CAND_] - 1)
    sf_s[CNT_] = count(lambda k, rs: k >= si_s[T_, rs, :])

    def encode(st, carry):
        for r0 in range(0, nq, rb):
            rs = slice(r0, r0 + rb)
            t = si_s[T_, rs, :]
            for g in range(ng):
                k = key_s[st, g, rs, :]
                pos = st * st_w + g * LANE + lane_rb
                key_s[st, g, rs, :] = jnp.where(k > t, CODE_KEEP, jnp.where(k == t, pos, CODE_DROP))
        return carry

    lax.fori_loop(0, n_st, encode, 0)
    finite_thr = si_s[T_] != KEY_NEG_INF
    si_s[MIDX_] = jnp.where(finite_thr, TIE_BOUND_ALL, 0)

    excess = jnp.max(jnp.where(finite_thr, sf_s[CNT_], 0.0)) > f_topk

    @pl.when(excess)
    def _():
        r_keep = f_topk - count(lambda code, rs: code < 0)
        sf_s[CNT_] = jnp.where(finite_thr, r_keep, 0.0)
        sf_s[ACC_] = jnp.zeros((nq, LANE), _F32)
        upper = (lax.broadcasted_iota(jnp.int32, (LANE, LANE), 0)
                 <= lax.broadcasted_iota(jnp.int32, (LANE, LANE), 1)).astype(_BF16)

        def rank_ties(st, carry):
            for r0 in range(0, nq, rb):
                rs = slice(r0, r0 + rb)
                r_blk = sf_s[CNT_, rs, :]
                before = sf_s[ACC_, rs, :]
                for g in range(ng):
                    code = key_s[st, g, rs, :]
                    tie = jnp.where(code >= 0, jnp.where(code < CODE_DROP, 1.0, 0.0), 0.0)
                    tie_b = tie.astype(_BF16)
                    rank = before + _dot(tie_b, upper)
                    kept_tie = jnp.where(rank <= r_blk, tie, 0.0)
                    mb = jnp.where(code < 0, 0.0, jnp.where(kept_tie > 0.0, 0.0, MASK_NEG))
                    key_s[st, g, rs, :] = pltpu.bitcast(mb, jnp.int32)
                    before = before + _dot(tie_b, ones_b)
                sf_s[ACC_, rs, :] = before
            return carry

        lax.fori_loop(0, n_st, rank_ties, 0)

    @pl.when(jnp.logical_not(excess))
    def _():
        def p2(st, carry):
            for r0 in range(0, nq, rb):
                rs = slice(r0, r0 + rb)
                bound = si_s[MIDX_, rs, :]
                for g in range(ng):
                    mb = jnp.where(key_s[st, g, rs, :] < bound, 0.0, MASK_NEG)
                    key_s[st, g, rs, :] = pltpu.bitcast(mb, jnp.int32)
            return carry

        lax.fori_loop(0, n_st, p2, 0)

    def attend(st, mode):
        off = pl.multiple_of(st * st_w, st_w)
        for h in range(H_B):
            sl = slice(h * hd, (h + 1) * hd)
            mask = jnp.concatenate([pltpu.bitcast(key_s[st, g], _F32) for g in range(ng)], axis=1)
            s = _dot_nt(qs_s[h], k_ref[pl.ds(off, st_w), sl]) + mask
            if mode:
                s_s[...] = s
                b_prev = bias_ref[h, :, 0:LANE]
                if mode == 1:
                    s_s[0:LANE, (ng - 1) * LANE:ng * LANE] += b_prev
                else:
                    b_diag = bias_ref[h, :, LANE:2 * LANE]
                    for rt in range(ng):
                        s_s[grp(rt), grp(rt)] += b_diag
                        if rt:
                            s_s[grp(rt), grp(rt - 1)] += b_prev
                s = s_s[...]
            m_old = m_s[h]
            gmax = s[:, grp(0)]
            for g in range(1, ng):
                gmax = jnp.maximum(gmax, s[:, grp(g)])
            m_new = jnp.maximum(m_old, jnp.max(gmax, axis=1, keepdims=True))
            alpha = jnp.exp2(m_old - m_new)
            pieces = [jnp.exp2(s[:, grp(g)] - m_new) for g in range(ng)]
            psum = pieces[0]
            for g in range(1, ng):
                psum = psum + pieces[g]
            p = jnp.concatenate(pieces, axis=1).astype(_BF16)
            m_s[h] = m_new
            l_s[h] = alpha * l_s[h] + psum
            acc_s[h] = alpha * acc_s[h] + _dot(p, v_ref[pl.ds(off, st_w), sl])

    def p3(st, carry):
        attend(st, 0)
        return carry

    lax.fori_loop(0, qb - 1, p3, 0)

    @pl.when(qb > 0)
    def _():
        attend(qb - 1, 1)

    attend(qb, 2)

    for h in range(H_B):
        sl = slice(h * hd, (h + 1) * hd)
        l_fin = jnp.sum(l_s[h], axis=1, keepdims=True)
        o_ref[:, sl] = (acc_s[h] / l_fin).astype(o_ref.dtype)


def _dsa(pb, ps, bias, b_, t_, col0):
    nq = KEY_TILE
    nb = t_ // nq
    hd = H_B * D_HEAD_B
    topk = min(TOPK_MAX, t_ // 4)
    iq_w = H_IDX * D_IDX
    ik_w = 2 * LANE
    assert col0 % hd == 0 and (col0 + 3 * hd) % iq_w == 0 and (col0 + 3 * hd + iq_w) % ik_w == 0
    cq = col0 // hd
    resident = functools.partial(pl.BlockSpec, pipeline_mode=pl.Buffered(1))
    return pl.pallas_call(
        functools.partial(_dsa_kernel, topk=topk),
        grid=(b_, nb),
        in_specs=[
            pl.BlockSpec((nq, hd), lambda b, i: (b * nb + i, cq)),
            resident((t_, hd), lambda b, i: (b, cq + 1)),
            resident((t_, hd), lambda b, i: (b, cq + 2)),
            pl.BlockSpec((nq, iq_w), lambda b, i: (b * nb + i, (col0 + 3 * hd) // iq_w)),
            resident((t_, ik_w), lambda b, i: (b, (col0 + 3 * hd + iq_w) // ik_w)),
            pl.BlockSpec((nq, LANE), lambda b, i: (b * nb + i, 0)),
            resident((H_B, Q_BLOCK, 2 * LANE), lambda b, i: (0, 0, 0)),
        ],
        out_specs=pl.BlockSpec((nq, hd), lambda b, i: (b * nb + i, 0)),
        out_shape=jax.ShapeDtypeStruct((b_ * t_, hd), _BF16),
        scratch_shapes=[
            pltpu.VMEM((nb, KEY_TILE // LANE, nq, LANE), jnp.int32),
            pltpu.VMEM((H_B, nq, D_HEAD_B), _BF16),
            pltpu.VMEM((H_B, nq, LANE), _F32),
            pltpu.VMEM((H_B, nq, LANE), _F32),
            pltpu.VMEM((H_B, nq, LANE), _F32),
            pltpu.VMEM((H_B, nq, D_HEAD_B), _F32),
            pltpu.VMEM((nq, KEY_TILE), _F32),
            pltpu.VMEM((3, nq, LANE), jnp.int32),
            pltpu.VMEM((2, nq, LANE), _F32),
        ],
        compiler_params=pltpu.CompilerParams(
            dimension_semantics=("arbitrary", "arbitrary"), vmem_limit_bytes=VMEM_LIMIT),
        name="dsa",
    )(pb, pb, pb, pb, pb, ps, bias)


def _merge_kernel(oa_ref, ob_ref, g_ref, x_ref, bg_ref, wa_ref, wb_ref, wo_ref, o_ref):
    d = x_ref.shape[1]
    a = _dot(oa_ref[...], wa_ref[...])
    b = _dot(ob_ref[...], wb_ref[...])
    g = _sigmoid(g_ref[...].astype(_F32) + bg_ref[...])
    merged = g[:, 0:d] * a + g[:, d:2 * d] * b
    o_ref[...] = x_ref[...] + _dot(merged.astype(_BF16), wo_ref[...])


def _merge(oa, ob, pb, x2d, bg, wa, wb, wo, tn, col_g):
    n, d = x2d.shape
    row = lambda w: pl.BlockSpec((tn, w), lambda i: (i, 0))
    const = lambda a: pl.BlockSpec(a.shape, lambda i: (0, 0))
    return pl.pallas_call(
        _merge_kernel,
        grid=(n // tn,),
        in_specs=[row(oa.shape[1]), row(ob.shape[1]),
                  pl.BlockSpec((tn, 2 * d), lambda i: (i, col_g // (2 * d))), row(d),
                  const(bg), const(wa), const(wb), const(wo)],
        out_specs=row(d),
        out_shape=jax.ShapeDtypeStruct((n, d), _F32),
        compiler_params=pltpu.CompilerParams(
            dimension_semantics=("arbitrary",), vmem_limit_bytes=VMEM_LIMIT),
        name="merge",
    )(oa, ob, pb, x2d, bg, wa, wb, wo)


def _mlp_kernel(x_ref, n2_ref, w1_ref, w2_ref, nf_ref, o_ref, *, final_norm):
    x = x_ref[...]
    h2 = _rms(x, n2_ref[...]).astype(_BF16)
    hid = jnp.maximum(_dot(h2, w1_ref[...]), 0.0)
    y = x + _dot((hid * hid).astype(_BF16), w2_ref[...])
    o_ref[...] = _rms(y, nf_ref[...]) if final_norm else y


def _mlp(x1, n2, w1, w2, nf, tn, final_norm):
    n, d = x1.shape
    row = pl.BlockSpec((tn, d), lambda i: (i, 0))
    const = lambda a: pl.BlockSpec(a.shape, lambda i: (0, 0))
    return pl.pallas_call(
        functools.partial(_mlp_kernel, final_norm=final_norm),
        grid=(n // tn,),
        in_specs=[row, const(n2), const(w1), const(w2), const(nf)],
        out_specs=row,
        out_shape=jax.ShapeDtypeStruct((n, d), _F32),
        compiler_params=pltpu.CompilerParams(
            dimension_semantics=("arbitrary",), vmem_limit_bytes=VMEM_LIMIT),
        name="mlp",
    )(x1, n2, w1, w2, nf)


def _t5_bucket(rel):
    half = N_BUCKETS // 2
    max_exact = half // 2
    base = jnp.where(rel > 0, half, 0)
    n = jnp.abs(rel)
    n_f = jnp.maximum(n, 1).astype(jnp.float32)
    large = max_exact + (jnp.log(n_f / max_exact) / math.log(MAX_DISTANCE / max_exact)
                         * (half - max_exact)).astype(jnp.int32)
    large = jnp.minimum(large, half - 1)
    return base + jnp.where(n < max_exact, n, large)


def _pick_tile(n, prefs):
    for t in prefs:
        if n % t == 0:
            return t
    raise ValueError(f"no tile in {prefs} divides {n}")


def kernel(x, norm1_w, w_in, conv_a_w, a_log, dt_bias, norm_a_w, rel_bias_table, w_gate, b_gate,
           w_proj_a, w_proj_b, w_out, norm2_w, w_ff1, w_ff2, norm_final_w):
    b_, t_, d = x.shape
    depth = norm1_w.shape[0]
    n = b_ * t_
    ha, hb = H_A * DK_A, H_B * D_HEAD_B
    assert t_ % KEY_TILE == 0 and t_ % GDN_CHUNK == 0 and d % LANE == 0
    assert DK_A == DV_A == D_HEAD_B == LANE and 2 * D_IDX == LANE
    assert Q_BLOCK >= MAX_DISTANCE

    o_za = 3 * ha
    o_ba = 4 * ha
    o_aa = o_ba + H_A
    o_qb = o_aa + H_A
    o_iq = o_qb + 3 * hb
    o_ik = o_iq + H_IDX * D_IDX
    o_iw = o_ik + D_IDX

    rel = (jnp.arange(2 * Q_BLOCK, dtype=jnp.int32)[None, :] - Q_BLOCK) \
        - jnp.arange(Q_BLOCK, dtype=jnp.int32)[:, None]
    far = jnp.full((Q_BLOCK, LANE), -(Q_BLOCK + 1), jnp.int32)
    bucket = _t5_bucket(jnp.concatenate([rel, far], axis=1))
    bias = _bias_tiles(rel_bias_table.astype(_F32), bucket)

    x2d = x.reshape(n, d)
    tn = _pick_tile(n, (512, 256, 128))
    for layer in range(depth):
        wi = w_in[layer].astype(_BF16)
        zpad = lambda w: jnp.zeros((d, w), _BF16)
        ws = jnp.concatenate([wi[:, o_ba:o_qb], wi[:, o_iw:o_iw + H_IDX], zpad(LANE - 3 * H_A)], axis=1)
        ik = wi[:, o_ik:o_iw]
        wb16 = jnp.concatenate([wi[:, 0:o_ba], w_gate[layer].astype(_BF16), wi[:, o_qb:o_ik],
                                ik, zpad(D_IDX), zpad(D_IDX), ik], axis=1)
        nw = norm1_w[layer].reshape(1, d).astype(_F32)
        pb, ps = _norm_proj(x2d, nw, wb16, ws, tn)

        col_a = 0
        col_g = 4 * ha
        col_b = col_g + 2 * d
        assert col_g % (2 * d) == 0
        lanes = lambda v, off: jnp.zeros((1, LANE), _F32).at[0, off:off + H_A].set(v.astype(_F32))
        oa = _gdn(pb, ps, conv_a_w[layer].astype(_F32), lanes(a_log[layer], H_A), lanes(dt_bias[layer], H_A),
                  norm_a_w[layer].reshape(1, DV_A).astype(_F32), b_, t_, col_a)
        ob = _dsa(pb, ps, bias, b_, t_, col_b)

        x2d = _merge(oa, ob, pb, x2d, b_gate[layer].reshape(1, 2 * d).astype(_F32),
                     w_proj_a[layer].astype(_BF16), w_proj_b[layer].astype(_BF16),
                     w_out[layer].astype(_BF16), tn, col_g)
        x2d = _mlp(x2d, norm2_w[layer].reshape(1, d).astype(_F32), w_ff1[layer].astype(_BF16),
                   w_ff2[layer].astype(_BF16), norm_final_w.reshape(1, d).astype(_F32),
                   _pick_tile(n, (256, 128)), final_norm=layer == depth - 1)
    return x2d.reshape(b_, t_, d)
```

```python
import functools
import math

import jax
import jax.numpy as jnp
import numpy as np
from jax import lax
from jax.experimental import pallas as pl
from jax.experimental.pallas import tpu as pltpu

EPS = 1e-6
H_A = 8
DK_A = 128
DV_A = 128
CONV_K = 4
GDN_CHUNK = 128
GDN_STEP_CHUNKS = 4
H_B = 8
D_HEAD_B = 128
H_IDX = 8
D_IDX = 64
TOPK_MAX = 256
Q_BLOCK = 128
CHUNK = 64
KEY_TILE = 512
N_BUCKETS = 32
MAX_DISTANCE = 128

LANE = 128
VMEM_LIMIT = 56 * 1024 * 1024

INT_MIN = -(2**31)
INT_MAX = 2**31 - 1
KEY_NEG_INF = -2139095041
MASK_NEG = -1e30
CODE_KEEP = -1
CODE_DROP = INT_MAX
TIE_BOUND_ALL = 2**30
LOG2E = math.log2(math.e)

_F32 = jnp.float32
_BF16 = jnp.bfloat16


def _dot(a, b):
    return jnp.dot(a, b, preferred_element_type=_F32)


def _dot_nt(a, b):
    return lax.dot_general(a, b, (((1,), (1,)), ((), ())), preferred_element_type=_F32)


def _sigmoid(x):
    return 1.0 / (1.0 + jnp.exp(-x))


def _rms(x, w):
    return x * lax.rsqrt(jnp.mean(x * x, axis=-1, keepdims=True) + EPS) * w


def _norm_proj_kernel(x_ref, nw_ref, w_ref, ws_ref, o_ref, os_ref, *, chunk):
    h = _rms(x_ref[...], nw_ref[...]).astype(_BF16)
    os_ref[...] = _dot(h, ws_ref[...])
    for c0 in range(0, w_ref.shape[1], chunk):
        o_ref[:, c0:c0 + chunk] = _dot(h, w_ref[:, c0:c0 + chunk]).astype(o_ref.dtype)


def _norm_proj(x2d, nw, w, ws, tn):
    n, d = x2d.shape
    c = w.shape[1]
    const = functools.partial(pl.BlockSpec, pipeline_mode=pl.Buffered(1))
    return pl.pallas_call(
        functools.partial(_norm_proj_kernel, chunk=_pick_tile(c, (768, 512, 256, 128))),
        grid=(n // tn,),
        in_specs=[
            pl.BlockSpec((tn, d), lambda i: (i, 0)),
            const((1, d), lambda i: (0, 0)),
            const((d, c), lambda i: (0, 0)),
            const((d, LANE), lambda i: (0, 0)),
        ],
        out_specs=[pl.BlockSpec((tn, c), lambda i: (i, 0)), pl.BlockSpec((tn, LANE), lambda i: (i, 0))],
        out_shape=[jax.ShapeDtypeStruct((n, c), _BF16), jax.ShapeDtypeStruct((n, LANE), _F32)],
        compiler_params=pltpu.CompilerParams(
            dimension_semantics=("arbitrary",), vmem_limit_bytes=VMEM_LIMIT),
        name="proj",
    )(x2d, nw, w, ws)


def _gdn_kernel(qkvz_ref, sm_ref, cw_ref, arow_ref, dtrow_ref, naw_ref, o_ref,
                xbuf, s_ref, kn_s, kb_s, qn_s, rhs_s, dec_s, qdec_s, ktt_s, eg_s, m_s, x_s):
    tt = qkvz_ref.shape[0]
    ct = GDN_CHUNK
    nc = tt // ct
    hd = H_A * DK_A

    @pl.when(pl.program_id(1) == 0)
    def _():
        xbuf[:, 0:8, :] = jnp.zeros((3, 8, hd), _F32)
        s_ref[...] = jnp.zeros_like(s_ref)

    for idx in range(3):
        xbuf[idx, 8:8 + tt, :] = qkvz_ref[:, idx * hd:(idx + 1) * hd].astype(_F32)

    row = lax.broadcasted_iota(jnp.int32, (ct, ct), 0)
    col = lax.broadcasted_iota(jnp.int32, (ct, ct), 1)
    tri = (col <= row).astype(_F32)
    strict = col < row
    eye = (col == row).astype(_F32)
    n_sq = int(math.log2(ct))
    heads = range(H_A)

    def prepare(c):
        r0 = c * ct
        sm = sm_ref[r0:r0 + ct, :]
        beta_full = _sigmoid(sm)
        xg = sm + dtrow_ref[...]
        softplus = jnp.maximum(xg, 0.0) + jnp.log(1.0 + jnp.exp(-jnp.abs(xg)))
        g_full = -jnp.exp(arow_ref[...]) * softplus
        gcum = jnp.dot(tri, g_full, preferred_element_type=_F32, precision=lax.Precision.HIGHEST)
        gcum_t = gcum.T
        for h in heads:
            sl = slice(h * DK_A, (h + 1) * DK_A)
            conv = []
            for idx in range(3):
                acc = None
                for j in range(CONV_K):
                    lo = r0 + 8 - (CONV_K - 1) + j
                    term = cw_ref[j:j + 1, idx * hd + h * DK_A: idx * hd + (h + 1) * DK_A] * xbuf[idx, lo:lo + ct, sl]
                    acc = term if acc is None else acc + term
                conv.append(acc * _sigmoid(acc))
            qh, kh, vh = conv
            qn = qh * lax.rsqrt(jnp.sum(qh * qh, axis=-1, keepdims=True) + EPS) * (DK_A ** -0.5)
            kn = kh * lax.rsqrt(jnp.sum(kh * kh, axis=-1, keepdims=True) + EPS)
            bcol = beta_full[:, h:h + 1]
            gcol = gcum[:, H_A + h:H_A + h + 1]
            grow = gcum_t[H_A + h:H_A + h + 1, :]
            glast = gcum[ct - 1:ct, H_A + h:H_A + h + 1]
            kb = kn * bcol
            eg = jnp.exp(gcol)
            rhs_s[c, h, :, 0:DV_A] = vh * bcol
            rhs_s[c, h, :, DV_A:DV_A + DK_A] = kb * eg
            dec_s[c, h] = jnp.where(strict, jnp.exp(jnp.where(strict, gcol - grow, 0.0)), 0.0)
            qdec_s[c, h] = (qn * eg).astype(_BF16)
            ktt_s[c, h] = (kn * jnp.exp(glast - gcol)).T.astype(_BF16)
            kn_s[c, h] = kn.astype(_BF16)
            kb_s[c, h] = kb.astype(_BF16)
            qn_s[c, h] = qn.astype(_BF16)
            eg_s[c, h] = jnp.broadcast_to(jnp.exp(glast), (DK_A, DV_A))

    def solve(c):
        for h in heads:
            m = -(_dot_nt(kb_s[c, h], kn_s[c, h]) * dec_s[c, h])
            m_s[c, h] = m
            x_s[c, h] = m
        for _ in range(n_sq - 1):
            for h in heads:
                mb = m_s[c, h].astype(_BF16)
                m = _dot(mb, mb)
                m_s[c, h] = m
                x_s[c, h] = x_s[c, h] + m + _dot(x_s[c, h].astype(_BF16), m.astype(_BF16))
        for h in heads:
            rhs = rhs_s[c, h]
            rhs_s[c, h] = rhs + _dot(x_s[c, h].astype(_BF16), rhs.astype(_BF16))
        for h in heads:
            m_s[c, h] = _dot_nt(qn_s[c, h], kn_s[c, h]) * (dec_s[c, h] + eye)

    def recur(c):
        for h in heads:
            sb = s_ref[h].astype(_BF16)
            vnew = rhs_s[c, h, :, 0:DV_A] - _dot(rhs_s[c, h, :, DV_A:DV_A + DK_A].astype(_BF16), sb)
            vb = vnew.astype(_BF16)
            o = _dot(qdec_s[c, h], sb) + _dot(m_s[c, h].astype(_BF16), vb)
            s_ref[h] = s_ref[h] * eg_s[c, h] + _dot(ktt_s[c, h], vb)
            rhs_s[c, h, :, 0:DV_A] = o

    def finish(c):
        r0 = c * ct
        naw = naw_ref[...]
        for h in heads:
            sl = slice(h * DV_A, (h + 1) * DV_A)
            o = rhs_s[c, h, :, 0:DV_A]
            z = qkvz_ref[r0:r0 + ct, 3 * hd + h * DV_A:3 * hd + (h + 1) * DV_A].astype(_F32)
            on = o * lax.rsqrt(jnp.mean(o * o, axis=-1, keepdims=True) + EPS) * naw
            o_ref[r0:r0 + ct, sl] = (on * (z * _sigmoid(z))).astype(o_ref.dtype)

    for c in range(nc):
        prepare(c)
    for idx in range(3):
        xbuf[idx, 0:8, :] = xbuf[idx, tt:tt + 8, :]
    for c in range(nc):
        solve(c)
    for c in range(nc):
        recur(c)
    for c in range(nc):
        finish(c)


def _gdn(pb, ps, cw, arow, dtrow, naw, b_, t_, col0):
    ct = GDN_CHUNK
    nc = GDN_STEP_CHUNKS if t_ % (GDN_STEP_CHUNKS * ct) == 0 else 1
    tt = nc * ct
    hd = H_A * DK_A
    nt = t_ // tt
    assert col0 % (4 * hd) == 0
    cb = col0 // (4 * hd)
    const = lambda shape: pl.BlockSpec(shape, lambda b, t: (0, 0))
    return pl.pallas_call(
        _gdn_kernel,
        grid=(b_, nt),
        in_specs=[pl.BlockSpec((tt, 4 * hd), lambda b, t: (b * nt + t, cb)),
                  pl.BlockSpec((tt, LANE), lambda b, t: (b * nt + t, 0)),
                  const((CONV_K, 3 * hd)), const((1, LANE)), const((1, LANE)), const((1, DV_A))],
        out_specs=pl.BlockSpec((tt, hd), lambda b, t: (b * nt + t, 0)),
        out_shape=jax.ShapeDtypeStruct((b_ * t_, hd), _BF16),
        scratch_shapes=[
            pltpu.VMEM((3, tt + 8, hd), _F32),
            pltpu.VMEM((H_A, DK_A, DV_A), _F32),
            pltpu.VMEM((nc, H_A, ct, DK_A), _BF16),
            pltpu.VMEM((nc, H_A, ct, DK_A), _BF16),
            pltpu.VMEM((nc, H_A, ct, DK_A), _BF16),
            pltpu.VMEM((nc, H_A, ct, DV_A + DK_A), _F32),
            pltpu.VMEM((nc, H_A, ct, ct), _F32),
            pltpu.VMEM((nc, H_A, ct, DK_A), _BF16),
            pltpu.VMEM((nc, H_A, DK_A, ct), _BF16),
            pltpu.VMEM((nc, H_A, DK_A, DV_A), _F32),
            pltpu.VMEM((nc, H_A, ct, ct), _F32),
            pltpu.VMEM((nc, H_A, ct, ct), _F32),
        ],
        compiler_params=pltpu.CompilerParams(
            dimension_semantics=("arbitrary", "arbitrary"), vmem_limit_bytes=VMEM_LIMIT),
        name="gdn",
    )(pb, ps, cw, arow, dtrow, naw)


def _bias_kernel(tab_ref, bucket_ref, o_ref):
    bucket = bucket_ref[...]
    nq, w = bucket.shape
    for h in range(H_B):
        acc = jnp.zeros((nq, w), _F32)
        for b in range(N_BUCKETS):
            acc = acc + jnp.where(bucket == b, tab_ref[b, h], 0.0)
        far = acc[:, 2 * LANE:3 * LANE]
        o_ref[h] = (acc[:, 0:2 * LANE] - jnp.concatenate([far, far], axis=1)) * LOG2E


def _bias_tiles(rel_table, bucket):
    return pl.pallas_call(
        _bias_kernel,
        in_specs=[pl.BlockSpec(memory_space=pltpu.SMEM),
                  pl.BlockSpec(bucket.shape, lambda: (0, 0))],
        out_specs=pl.BlockSpec((H_B, Q_BLOCK, 2 * LANE), lambda: (0, 0, 0)),
        out_shape=jax.ShapeDtypeStruct((H_B, Q_BLOCK, 2 * LANE), _F32),
        name="rel_bias_tiles",
    )(rel_table, bucket)


def _dsa_kernel(q_ref, k_ref, v_ref, iq_ref, ik_ref, sm_ref, bias_ref, o_ref,
                key_s, qs_s, iwb_s, m_s, l_s, acc_s, s_s, si_s, sf_s, *, topk):
    nq = q_ref.shape[0]
    st_w = KEY_TILE
    assert nq == st_w
    ng = st_w // LANE
    qb = pl.program_id(1)
    n_st = qb + 1
    hd = D_HEAD_B
    scale = D_HEAD_B ** -0.5 * LOG2E
    f_topk = float(topk)

    lane_j = lax.broadcasted_iota(jnp.int32, (nq, LANE), 1)
    row_i = lax.broadcasted_iota(jnp.int32, (nq, LANE), 0)

    def grp(g):
        return slice(g * LANE, (g + 1) * LANE)

    iw = sm_ref[...]
    for h in range(H_B):
        sl = slice(h * hd, (h + 1) * hd)
        qs_s[h] = (q_ref[:, sl].astype(_F32) * scale).astype(_BF16)
        iwb_s[h] = jnp.broadcast_to(iw[:, 2 * H_A + h:2 * H_A + h + 1], (nq, LANE))
        m_s[h] = jnp.full((nq, LANE), MASK_NEG, _F32)
        l_s[h] = jnp.zeros((nq, LANE), _F32)
        acc_s[h] = jnp.zeros((nq, hd), _F32)

    def scores(st, diag):
        ikt = ik_ref[pl.ds(pl.multiple_of(st * st_w, st_w), st_w), :]
        acc = [jnp.zeros((nq, LANE), _F32) for _ in range(ng)]
        for p in range(H_IDX // 2):
            xq = iq_ref[:, p * LANE:(p + 1) * LANE]
            for half in range(2):
                h = 2 * p + half
                s = jnp.maximum(_dot_nt(xq, ikt[:, half * LANE:(half + 1) * LANE]), 0.0)
                w = iwb_s[h]
                for g in range(ng):
                    acc[g] = acc[g] + w * s[:, grp(g)]
        for g in range(ng):
            a = acc[g]
            if diag:
                a = jnp.where((g * LANE + lane_j) // CHUNK <= row_i // CHUNK, a, -jnp.inf)
            bits = pltpu.bitcast(a, jnp.int32)
            key_s[st, g] = bits ^ ((bits >> 31) & INT_MAX)

    def p1(st, carry):
        scores(st, False)
        return carry

    lax.fori_loop(0, qb, p1, 0)
    scores(qb, True)

    ones_b = jnp.ones((LANE, LANE), _BF16)
    assert key_s.shape[0] * ng <= 256

    T_, CAND_, MIDX_ = 0, 1, 2
    ACC_, CNT_ = 0, 1
    rb = 64
    lane_rb = lax.broadcasted_iota(jnp.int32, (rb, LANE), 1)

    def count(pred_fn):
        def tile(st, first):
            for r0 in range(0, nq, rb):
                rs = slice(r0, r0 + rb)
                acc = None if first else sf_s[ACC_, rs, :]
                for g in range(ng):
                    one = jnp.where(pred_fn(key_s[st, g, rs, :], rs), 1.0, 0.0)
                    acc = one if acc is None else acc + one
                sf_s[ACC_, rs, :] = acc

        def body(st, carry):
            tile(st, False)
            return carry

        tile(0, True)
        lax.fori_loop(1, n_st, body, 0)
        return _dot(sf_s[ACC_].astype(_BF16), ones_b)

    si_s[CAND_] = jnp.zeros((nq, LANE), jnp.int32)

    def bit_body(i, carry):
        half = jnp.left_shift(jnp.int32(1), 30 - i)
        c = count(lambda k, rs: k >= si_s[CAND_, rs, :])
        si_s[CAND_] = si_s[CAND_] + jnp.where(c >= f_topk, half, -half)
        return carry

    lax.fori_loop(0, 31, bit_body, 0)
    c_last = count(lambda k, rs: k >= si_s[CAND_, rs, :])
    si_s[T_] = jnp.where(c_last >= f_topk, si_s[CAND_], si_s[CAND_] - 1)
    sf_s[CNT_] = count(lambda k, rs: k >= si_s[T_, rs, :])

    def encode(st, carry):
        for r0 in range(0, nq, rb):
            rs = slice(r0, r0 + rb)
            t = si_s[T_, rs, :]
            for g in range(ng):
                k = key_s[st, g, rs, :]
                pos = st * st_w + g * LANE + lane_rb
                key_s[st, g, rs, :] = jnp.where(k > t, CODE_KEEP, jnp.where(k == t, pos, CODE_DROP))
        return carry

    lax.fori_loop(0, n_st, encode, 0)
    finite_thr = si_s[T_] != KEY_NEG_INF
    si_s[MIDX_] = jnp.where(finite_thr, TIE_BOUND_ALL, 0)

    excess = jnp.max(jnp.where(finite_thr, sf_s[CNT_], 0.0)) > f_topk

    @pl.when(excess)
    def _():
        r_keep = f_topk - count(lambda code, rs: code < 0)
        sf_s[CNT_] = jnp.where(finite_thr, r_keep, 0.0)
        sf_s[ACC_] = jnp.zeros((nq, LANE), _F32)
        upper = (lax.broadcasted_iota(jnp.int32, (LANE, LANE), 0)
                 <= lax.broadcasted_iota(jnp.int32, (LANE, LANE), 1)).astype(_BF16)

        def rank_ties(st, carry):
            for r0 in range(0, nq, rb):
                rs = slice(r0, r0 + rb)
                r_blk = sf_s[CNT_, rs, :]
                before = sf_s[ACC_, rs, :]
                for g in range(ng):
                    code = key_s[st, g, rs, :]
                    tie = jnp.where(code >= 0, jnp.where(code < CODE_DROP, 1.0, 0.0), 0.0)
                    tie_b = tie.astype(_BF16)
                    rank = before + _dot(tie_b, upper)
                    kept_tie = jnp.where(rank <= r_blk, tie, 0.0)
                    mb = jnp.where(code < 0, 0.0, jnp.where(kept_tie > 0.0, 0.0, MASK_NEG))
                    key_s[st, g, rs, :] = pltpu.bitcast(mb, jnp.int32)
                    before = before + _dot(tie_b, ones_b)
                sf_s[ACC_, rs, :] = before
            return carry

        lax.fori_loop(0, n_st, rank_ties, 0)

    @pl.when(jnp.logical_not(excess))
    def _():
        def p2(st, carry):
            for r0 in range(0, nq, rb):
                rs = slice(r0, r0 + rb)
                bound = si_s[MIDX_, rs, :]
                for g in range(ng):
                    mb = jnp.where(key_s[st, g, rs, :] < bound, 0.0, MASK_NEG)
                    key_s[st, g, rs, :] = pltpu.bitcast(mb, jnp.int32)
            return carry

        lax.fori_loop(0, n_st, p2, 0)

    def attend(st, mode):
        off = pl.multiple_of(st * st_w, st_w)
        for h in range(H_B):
            sl = slice(h * hd, (h + 1) * hd)
            mask = jnp.concatenate([pltpu.bitcast(key_s[st, g], _F32) for g in range(ng)], axis=1)
            s = _dot_nt(qs_s[h], k_ref[pl.ds(off, st_w), sl]) + mask
            if mode:
                s_s[...] = s
                b_prev = bias_ref[h, :, 0:LANE]
                if mode == 1:
                    s_s[0:LANE, (ng - 1) * LANE:ng * LANE] += b_prev
                else:
                    b_diag = bias_ref[h, :, LANE:2 * LANE]
                    for rt in range(ng):
                        s_s[grp(rt), grp(rt)] += b_diag
                        if rt:
                            s_s[grp(rt), grp(rt - 1)] += b_prev
                s = s_s[...]
            m_old = m_s[h]
            gmax = s[:, grp(0)]
            for g in range(1, ng):
                gmax = jnp.maximum(gmax, s[:, grp(g)])
            m_new = jnp.maximum(m_old, jnp.max(gmax, axis=1, keepdims=True))
            alpha = jnp.exp2(m_old - m_new)
            pieces = [jnp.exp2(s[:, grp(g)] - m_new) for g in range(ng)]
            psum = pieces[0]
            for g in range(1, ng):
                psum = psum + pieces[g]
            p = jnp.concatenate(pieces, axis=1).astype(_BF16)
            m_s[h] = m_new
            l_s[h] = alpha * l_s[h] + psum
            acc_s[h] = alpha * acc_s[h] + _dot(p, v_ref[pl.ds(off, st_w), sl])

    def p3(st, carry):
        attend(st, 0)
        return carry

    lax.fori_loop(0, qb - 1, p3, 0)

    @pl.when(qb > 0)
    def _():
        attend(qb - 1, 1)

    attend(qb, 2)

    for h in range(H_B):
        sl = slice(h * hd, (h + 1) * hd)
        l_fin = jnp.sum(l_s[h], axis=1, keepdims=True)
        o_ref[:, sl] = (acc_s[h] / l_fin).astype(o_ref.dtype)


def _dsa(pb, ps, bias, b_, t_, col0):
    nq = KEY_TILE
    nb = t_ // nq
    hd = H_B * D_HEAD_B
    topk = min(TOPK_MAX, t_ // 4)
    iq_w = H_IDX * D_IDX
    ik_w = 2 * LANE
    assert col0 % hd == 0 and (col0 + 3 * hd) % iq_w == 0 and (col0 + 3 * hd + iq_w) % ik_w == 0
    cq = col0 // hd
    resident = functools.partial(pl.BlockSpec, pipeline_mode=pl.Buffered(1))
    return pl.pallas_call(
        functools.partial(_dsa_kernel, topk=topk),
        grid=(b_, nb),
        in_specs=[
            pl.BlockSpec((nq, hd), lambda b, i: (b * nb + i, cq)),
            resident((t_, hd), lambda b, i: (b, cq + 1)),
            resident((t_, hd), lambda b, i: (b, cq + 2)),
            pl.BlockSpec((nq, iq_w), lambda b, i: (b * nb + i, (col0 + 3 * hd) // iq_w)),
            resident((t_, ik_w), lambda b, i: (b, (col0 + 3 * hd + iq_w) // ik_w)),
            pl.BlockSpec((nq, LANE), lambda b, i: (b * nb + i, 0)),
            resident((H_B, Q_BLOCK, 2 * LANE), lambda b, i: (0, 0, 0)),
        ],
        out_specs=pl.BlockSpec((nq, hd), lambda b, i: (b * nb + i, 0)),
        out_shape=jax.ShapeDtypeStruct((b_ * t_, hd), _BF16),
        scratch_shapes=[
            pltpu.VMEM((nb, KEY_TILE // LANE, nq, LANE), jnp.int32),
            pltpu.VMEM((H_B, nq, D_HEAD_B), _BF16),
            pltpu.VMEM((H_B, nq, LANE), _F32),
            pltpu.VMEM((H_B, nq, LANE), _F32),
            pltpu.VMEM((H_B, nq, LANE), _F32),
            pltpu.VMEM((H_B, nq, D_HEAD_B), _F32),
            pltpu.VMEM((nq, KEY_TILE), _F32),
            pltpu.VMEM((3, nq, LANE), jnp.int32),
            pltpu.VMEM((2, nq, LANE), _F32),
        ],
        compiler_params=pltpu.CompilerParams(
            dimension_semantics=("arbitrary", "arbitrary"), vmem_limit_bytes=VMEM_LIMIT),
        name="dsa",
    )(pb, pb, pb, pb, pb, ps, bias)


def _merge_kernel(oa_ref, ob_ref, g_ref, x_ref, bg_ref, wa_ref, wb_ref, wo_ref, o_ref):
    d = x_ref.shape[1]
    a = _dot(oa_ref[...], wa_ref[...])
    b = _dot(ob_ref[...], wb_ref[...])
    g = _sigmoid(g_ref[...].astype(_F32) + bg_ref[...])
    merged = g[:, 0:d] * a + g[:, d:2 * d] * b
    o_ref[...] = x_ref[...] + _dot(merged.astype(_BF16), wo_ref[...])


def _merge(oa, ob, pb, x2d, bg, wa, wb, wo, tn, col_g):
    n, d = x2d.shape
    row = lambda w: pl.BlockSpec((tn, w), lambda i: (i, 0))
    const = lambda a: pl.BlockSpec(a.shape, lambda i: (0, 0))
    return pl.pallas_call(
        _merge_kernel,
        grid=(n // tn,),
        in_specs=[row(oa.shape[1]), row(ob.shape[1]),
                  pl.BlockSpec((tn, 2 * d), lambda i: (i, col_g // (2 * d))), row(d),
                  const(bg), const(wa), const(wb), const(wo)],
        out_specs=row(d),
        out_shape=jax.ShapeDtypeStruct((n, d), _F32),
        compiler_params=pltpu.CompilerParams(
            dimension_semantics=("arbitrary",), vmem_limit_bytes=VMEM_LIMIT),
        name="merge",
    )(oa, ob, pb, x2d, bg, wa, wb, wo)


def _mlp_kernel(x_ref, n2_ref, w1_ref, w2_ref, nf_ref, o_ref, *, final_norm):
    x = x_ref[...]
    h2 = _rms(x, n2_ref[...]).astype(_BF16)
    hid = jnp.maximum(_dot(h2, w1_ref[...]), 0.0)
    y = x + _dot((hid * hid).astype(_BF16), w2_ref[...])
    o_ref[...] = _rms(y, nf_ref[...]) if final_norm else y


def _mlp(x1, n2, w1, w2, nf, tn, final_norm):
    n, d = x1.shape
    row = pl.BlockSpec((tn, d), lambda i: (i, 0))
    const = lambda a: pl.BlockSpec(a.shape, lambda i: (0, 0), pipeline_mode=pl.Buffered(1))
    return pl.pallas_call(
        functools.partial(_mlp_kernel, final_norm=final_norm),
        grid=(n // tn,),
        in_specs=[row, const(n2), const(w1), const(w2), const(nf)],
        out_specs=row,
        out_shape=jax.ShapeDtypeStruct((n, d), _F32),
        compiler_params=pltpu.CompilerParams(
            dimension_semantics=("arbitrary",), vmem_limit_bytes=VMEM_LIMIT),
        name="mlp",
    )(x1, n2, w1, w2, nf)


def _t5_bucket(rel):
    half = N_BUCKETS // 2
    max_exact = half // 2
    base = jnp.where(rel > 0, half, 0)
    n = jnp.abs(rel)
    n_f = jnp.maximum(n, 1).astype(jnp.float32)
    large = max_exact + (jnp.log(n_f / max_exact) / math.log(MAX_DISTANCE / max_exact)
                         * (half - max_exact)).astype(jnp.int32)
    large = jnp.minimum(large, half - 1)
    return base + jnp.where(n < max_exact, n, large)


def _pick_tile(n, prefs):
    for t in prefs:
        if n % t == 0:
            return t
    raise ValueError(f"no tile in {prefs} divides {n}")


def kernel(x, norm1_w, w_in, conv_a_w, a_log, dt_bias, norm_a_w, rel_bias_table, w_gate, b_gate,
           w_proj_a, w_proj_b, w_out, norm2_w, w_ff1, w_ff2, norm_final_w):
    b_, t_, d = x.shape
    depth = norm1_w.shape[0]
    n = b_ * t_
    ha, hb = H_A * DK_A, H_B * D_HEAD_B
    assert t_ % KEY_TILE == 0 and t_ % GDN_CHUNK == 0 and d % LANE == 0
    assert DK_A == DV_A == D_HEAD_B == LANE and 2 * D_IDX == LANE
    assert Q_BLOCK >= MAX_DISTANCE

    o_za = 3 * ha
    o_ba = 4 * ha
    o_aa = o_ba + H_A
    o_qb = o_aa + H_A
    o_iq = o_qb + 3 * hb
    o_ik = o_iq + H_IDX * D_IDX
    o_iw = o_ik + D_IDX

    rel = (jnp.arange(2 * Q_BLOCK, dtype=jnp.int32)[None, :] - Q_BLOCK) \
        - jnp.arange(Q_BLOCK, dtype=jnp.int32)[:, None]
    far = jnp.full((Q_BLOCK, LANE), -(Q_BLOCK + 1), jnp.int32)
    bucket = _t5_bucket(jnp.concatenate([rel, far], axis=1))
    bias = _bias_tiles(rel_bias_table.astype(_F32), bucket)

    x2d = x.reshape(n, d)
    tn = _pick_tile(n, (512, 256, 128))
    for layer in range(depth):
        wi = w_in[layer].astype(_BF16)
        zpad = lambda w: jnp.zeros((d, w), _BF16)
        ws = jnp.concatenate([wi[:, o_ba:o_qb], wi[:, o_iw:o_iw + H_IDX], zpad(LANE - 3 * H_A)], axis=1)
        ik = wi[:, o_ik:o_iw]
        wb16 = jnp.concatenate([wi[:, 0:o_ba], w_gate[layer].astype(_BF16), wi[:, o_qb:o_ik],
                                ik, zpad(D_IDX), zpad(D_IDX), ik], axis=1)
        nw = norm1_w[layer].reshape(1, d).astype(_F32)
        pb, ps = _norm_proj(x2d, nw, wb16, ws, tn)

        col_a = 0
        col_g = 4 * ha
        col_b = col_g + 2 * d
        assert col_g % (2 * d) == 0
        lanes = lambda v, off: jnp.zeros((1, LANE), _F32).at[0, off:off + H_A].set(v.astype(_F32))
        oa = _gdn(pb, ps, conv_a_w[layer].astype(_F32), lanes(a_log[layer], H_A), lanes(dt_bias[layer], H_A),
                  norm_a_w[layer].reshape(1, DV_A).astype(_F32), b_, t_, col_a)
        ob = _dsa(pb, ps, bias, b_, t_, col_b)

        x2d = _merge(oa, ob, pb, x2d, b_gate[layer].reshape(1, 2 * d).astype(_F32),
                     w_proj_a[layer].astype(_BF16), w_proj_b[layer].astype(_BF16),
                     w_out[layer].astype(_BF16), tn, col_g)
        x2d = _mlp(x2d, norm2_w[layer].reshape(1, d).astype(_F32), w_ff1[layer].astype(_BF16),
                   w_ff2[layer].astype(_BF16), norm_final_w.reshape(1, d).astype(_F32),
                   _pick_tile(n, (512, 256, 128)), final_norm=layer == depth - 1)
    return x2d.reshape(b_, t_, d)
```

```python
import functools
import math

import jax
import jax.numpy as jnp
import numpy as np
from jax import lax
from jax.experimental import pallas as pl
from jax.experimental.pallas import tpu as pltpu

EPS = 1e-6
H_A = 8
DK_A = 128
DV_A = 128
CONV_K = 4
GDN_CHUNK = 128
GDN_STEP_CHUNKS = 4
H_B = 8
D_HEAD_B = 128
H_IDX = 8
D_IDX = 64
TOPK_MAX = 256
Q_BLOCK = 128
CHUNK = 64
KEY_TILE = 512
N_BUCKETS = 32
MAX_DISTANCE = 128

LANE = 128
VMEM_LIMIT = 56 * 1024 * 1024

INT_MIN = -(2**31)
INT_MAX = 2**31 - 1
KEY_NEG_INF = -2139095041
MASK_NEG = -1e30
CODE_KEEP = -1
CODE_DROP = INT_MAX
TIE_BOUND_ALL = 2**30
LOG2E = math.log2(math.e)

_F32 = jnp.float32
_BF16 = jnp.bfloat16


def _dot(a, b):
    return jnp.dot(a, b, preferred_element_type=_F32)


def _dot_nt(a, b):
    return lax.dot_general(a, b, (((1,), (1,)), ((), ())), preferred_element_type=_F32)


def _sigmoid(x):
    return 1.0 / (1.0 + jnp.exp(-x))


def _rms(x, w):
    return x * lax.rsqrt(jnp.mean(x * x, axis=-1, keepdims=True) + EPS) * w


def _norm_proj_kernel(x_ref, nw_ref, w_ref, ws_ref, cw_ref, o_ref, os_ref, halo_s, work_s, *,
                      chunk, conv_cols, tiles_per_seq):
    tn = x_ref.shape[0]
    h = _rms(x_ref[...], nw_ref[...]).astype(_BF16)
    os_ref[...] = _dot(h, ws_ref[...])
    seq_start = pl.program_id(0) % tiles_per_seq == 0
    for ci, c0 in enumerate(range(0, w_ref.shape[1], chunk)):
        y = _dot(h, w_ref[:, c0:c0 + chunk])
        if c0 < conv_cols:
            work_s[0:8, :] = jnp.where(seq_start, 0.0, halo_s[ci])
            work_s[8:8 + tn, :] = y
            halo_s[ci] = y[tn - 8:tn, :]
            acc = cw_ref[CONV_K - 1:CONV_K, c0:c0 + chunk] * y
            for j in range(CONV_K - 1):
                lo = 8 - (CONV_K - 1) + j
                acc = acc + cw_ref[j:j + 1, c0:c0 + chunk] * work_s[lo:lo + tn, :]
            y = acc * _sigmoid(acc)
        o_ref[:, c0:c0 + chunk] = y.astype(o_ref.dtype)


def _norm_proj(x2d, nw, w, ws, cw, tn, tiles_per_seq):
    n, d = x2d.shape
    c = w.shape[1]
    chunk = _pick_tile(c, (768, 512, 256, 128))
    conv_cols = cw.shape[1]
    assert conv_cols % chunk == 0 and tn >= 8
    const = functools.partial(pl.BlockSpec, pipeline_mode=pl.Buffered(1))
    return pl.pallas_call(
        functools.partial(_norm_proj_kernel, chunk=chunk, conv_cols=conv_cols, tiles_per_seq=tiles_per_seq),
        grid=(n // tn,),
        in_specs=[
            pl.BlockSpec((tn, d), lambda i: (i, 0)),
            const((1, d), lambda i: (0, 0)),
            const((d, c), lambda i: (0, 0)),
            const((d, LANE), lambda i: (0, 0)),
            const(cw.shape, lambda i: (0, 0)),
        ],
        out_specs=[pl.BlockSpec((tn, c), lambda i: (i, 0)), pl.BlockSpec((tn, LANE), lambda i: (i, 0))],
        out_shape=[jax.ShapeDtypeStruct((n, c), _BF16), jax.ShapeDtypeStruct((n, LANE), _F32)],
        scratch_shapes=[
            pltpu.VMEM((conv_cols // chunk, 8, chunk), _F32),
            pltpu.VMEM((8 + tn, chunk), _F32),
        ],
        compiler_params=pltpu.CompilerParams(
            dimension_semantics=("arbitrary",), vmem_limit_bytes=VMEM_LIMIT),
        name="proj",
    )(x2d, nw, w, ws, cw)


def _gdn_kernel(qkvz_ref, sm_ref, arow_ref, dtrow_ref, naw_ref, o_ref,
                s_ref, kn_s, kb_s, qn_s, rhs_s, dec_s, qdec_s, ktt_s, eg_s, m_s, x_s):
    tt = qkvz_ref.shape[0]
    ct = GDN_CHUNK
    nc = tt // ct
    hd = H_A * DK_A

    @pl.when(pl.program_id(1) == 0)
    def _():
        s_ref[...] = jnp.zeros_like(s_ref)

    row = lax.broadcasted_iota(jnp.int32, (ct, ct), 0)
    col = lax.broadcasted_iota(jnp.int32, (ct, ct), 1)
    tri = (col <= row).astype(_F32)
    strict = col < row
    eye = (col == row).astype(_F32)
    n_sq = int(math.log2(ct))
    heads = range(H_A)

    def prepare(c):
        r0 = c * ct
        sm = sm_ref[r0:r0 + ct, :]
        beta_full = _sigmoid(sm)
        xg = sm + dtrow_ref[...]
        softplus = jnp.maximum(xg, 0.0) + jnp.log(1.0 + jnp.exp(-jnp.abs(xg)))
        g_full = -jnp.exp(arow_ref[...]) * softplus
        gcum = jnp.dot(tri, g_full, preferred_element_type=_F32, precision=lax.Precision.HIGHEST)
        gcum_t = gcum.T
        for h in heads:
            sl = slice(h * DK_A, (h + 1) * DK_A)
            qh, kh, vh = (qkvz_ref[r0:r0 + ct, idx * hd + h * DK_A:idx * hd + (h + 1) * DK_A].astype(_F32)
                          for idx in range(3))
            qn = qh * lax.rsqrt(jnp.sum(qh * qh, axis=-1, keepdims=True) + EPS) * (DK_A ** -0.5)
            kn = kh * lax.rsqrt(jnp.sum(kh * kh, axis=-1, keepdims=True) + EPS)
            bcol = beta_full[:, h:h + 1]
            gcol = gcum[:, H_A + h:H_A + h + 1]
            grow = gcum_t[H_A + h:H_A + h + 1, :]
            glast = gcum[ct - 1:ct, H_A + h:H_A + h + 1]
            kb = kn * bcol
            eg = jnp.exp(gcol)
            rhs_s[c, h, :, 0:DV_A] = vh * bcol
            rhs_s[c, h, :, DV_A:DV_A + DK_A] = kb * eg
            dec_s[c, h] = jnp.where(strict, jnp.exp(jnp.where(strict, gcol - grow, 0.0)), 0.0)
            qdec_s[c, h] = (qn * eg).astype(_BF16)
            ktt_s[c, h] = (kn * jnp.exp(glast - gcol)).T.astype(_BF16)
            kn_s[c, h] = kn.astype(_BF16)
            kb_s[c, h] = kb.astype(_BF16)
            qn_s[c, h] = qn.astype(_BF16)
            eg_s[c, h] = jnp.broadcast_to(jnp.exp(glast), (DK_A, DV_A))

    def solve(c):
        for h in heads:
            m = -(_dot_nt(kb_s[c, h], kn_s[c, h]) * dec_s[c, h])
            m_s[c, h] = m
            x_s[c, h] = m
        for _ in range(n_sq - 1):
            for h in heads:
                mb = m_s[c, h].astype(_BF16)
                m = _dot(mb, mb)
                m_s[c, h] = m
                x_s[c, h] = x_s[c, h] + m + _dot(x_s[c, h].astype(_BF16), m.astype(_BF16))
        for h in heads:
            rhs = rhs_s[c, h]
            rhs_s[c, h] = rhs + _dot(x_s[c, h].astype(_BF16), rhs.astype(_BF16))
        for h in heads:
            m_s[c, h] = _dot_nt(qn_s[c, h], kn_s[c, h]) * (dec_s[c, h] + eye)

    def recur(c):
        for h in heads:
            sb = s_ref[h].astype(_BF16)
            vnew = rhs_s[c, h, :, 0:DV_A] - _dot(rhs_s[c, h, :, DV_A:DV_A + DK_A].astype(_BF16), sb)
            vb = vnew.astype(_BF16)
            o = _dot(qdec_s[c, h], sb) + _dot(m_s[c, h].astype(_BF16), vb)
            s_ref[h] = s_ref[h] * eg_s[c, h] + _dot(ktt_s[c, h], vb)
            rhs_s[c, h, :, 0:DV_A] = o

    def finish(c):
        r0 = c * ct
        naw = naw_ref[...]
        for h in heads:
            sl = slice(h * DV_A, (h + 1) * DV_A)
            o = rhs_s[c, h, :, 0:DV_A]
            z = qkvz_ref[r0:r0 + ct, 3 * hd + h * DV_A:3 * hd + (h + 1) * DV_A].astype(_F32)
            on = o * lax.rsqrt(jnp.mean(o * o, axis=-1, keepdims=True) + EPS) * naw
            o_ref[r0:r0 + ct, sl] = (on * (z * _sigmoid(z))).astype(o_ref.dtype)

    for c in range(nc):
        prepare(c)
    for c in range(nc):
        solve(c)
    for c in range(nc):
        recur(c)
    for c in range(nc):
        finish(c)


def _gdn(pb, ps, arow, dtrow, naw, b_, t_, col0):
    ct = GDN_CHUNK
    nc = GDN_STEP_CHUNKS if t_ % (GDN_STEP_CHUNKS * ct) == 0 else 1
    tt = nc * ct
    hd = H_A * DK_A
    nt = t_ // tt
    assert col0 % (4 * hd) == 0
    cb = col0 // (4 * hd)
    const = lambda shape: pl.BlockSpec(shape, lambda b, t: (0, 0))
    return pl.pallas_call(
        _gdn_kernel,
        grid=(b_, nt),
        in_specs=[pl.BlockSpec((tt, 4 * hd), lambda b, t: (b * nt + t, cb)),
                  pl.BlockSpec((tt, LANE), lambda b, t: (b * nt + t, 0)),
                  const((1, LANE)), const((1, LANE)), const((1, DV_A))],
        out_specs=pl.BlockSpec((tt, hd), lambda b, t: (b * nt + t, 0)),
        out_shape=jax.ShapeDtypeStruct((b_ * t_, hd), _BF16),
        scratch_shapes=[
            pltpu.VMEM((H_A, DK_A, DV_A), _F32),
            pltpu.VMEM((nc, H_A, ct, DK_A), _BF16),
            pltpu.VMEM((nc, H_A, ct, DK_A), _BF16),
            pltpu.VMEM((nc, H_A, ct, DK_A), _BF16),
            pltpu.VMEM((nc, H_A, ct, DV_A + DK_A), _F32),
            pltpu.VMEM((nc, H_A, ct, ct), _F32),
            pltpu.VMEM((nc, H_A, ct, DK_A), _BF16),
            pltpu.VMEM((nc, H_A, DK_A, ct), _BF16),
            pltpu.VMEM((nc, H_A, DK_A, DV_A), _F32),
            pltpu.VMEM((nc, H_A, ct, ct), _F32),
            pltpu.VMEM((nc, H_A, ct, ct), _F32),
        ],
        compiler_params=pltpu.CompilerParams(
            dimension_semantics=("arbitrary", "arbitrary"), vmem_limit_bytes=VMEM_LIMIT),
        name="gdn",
    )(pb, ps, arow, dtrow, naw)


def _bias_kernel(tab_ref, bucket_ref, o_ref):
    bucket = bucket_ref[...]
    nq, w = bucket.shape
    for h in range(H_B):
        acc = jnp.zeros((nq, w), _F32)
        for b in range(N_BUCKETS):
            acc = acc + jnp.where(bucket == b, tab_ref[b, h], 0.0)
        far = acc[:, 2 * LANE:3 * LANE]
        o_ref[h] = (acc[:, 0:2 * LANE] - jnp.concatenate([far, far], axis=1)) * LOG2E


def _bias_tiles(rel_table, bucket):
    return pl.pallas_call(
        _bias_kernel,
        in_specs=[pl.BlockSpec(memory_space=pltpu.SMEM),
                  pl.BlockSpec(bucket.shape, lambda: (0, 0))],
        out_specs=pl.BlockSpec((H_B, Q_BLOCK, 2 * LANE), lambda: (0, 0, 0)),
        out_shape=jax.ShapeDtypeStruct((H_B, Q_BLOCK, 2 * LANE), _F32),
        name="rel_bias_tiles",
    )(rel_table, bucket)


def _dsa_kernel(q_ref, k_ref, v_ref, iq_ref, ik_ref, sm_ref, bias_ref, o_ref,
                key_s, qs_s, iwb_s, m_s, l_s, acc_s, s_s, si_s, sf_s, *, topk):
    nq = q_ref.shape[0]
    st_w = KEY_TILE
    assert nq == st_w
    ng = st_w // LANE
    qb = pl.program_id(1)
    n_st = qb + 1
    hd = D_HEAD_B
    scale = D_HEAD_B ** -0.5 * LOG2E
    f_topk = float(topk)

    lane_j = lax.broadcasted_iota(jnp.int32, (nq, LANE), 1)
    row_i = lax.broadcasted_iota(jnp.int32, (nq, LANE), 0)

    def grp(g):
        return slice(g * LANE, (g + 1) * LANE)

    iw = sm_ref[...]
    for h in range(H_B):
        sl = slice(h * hd, (h + 1) * hd)
        qs_s[h] = (q_ref[:, sl].astype(_F32) * scale).astype(_BF16)
        iwb_s[h] = jnp.broadcast_to(iw[:, 2 * H_A + h:2 * H_A + h + 1], (nq, LANE))
        m_s[h] = jnp.full((nq, LANE), MASK_NEG, _F32)
        l_s[h] = jnp.zeros((nq, LANE), _F32)
        acc_s[h] = jnp.zeros((nq, hd), _F32)

    def scores(st, diag):
        ikt = ik_ref[pl.ds(pl.multiple_of(st * st_w, st_w), st_w), :]
        acc = [jnp.zeros((nq, LANE), _F32) for _ in range(ng)]
        for p in range(H_IDX // 2):
            xq = iq_ref[:, p * LANE:(p + 1) * LANE]
            for half in range(2):
                h = 2 * p + half
                s = jnp.maximum(_dot_nt(xq, ikt[:, half * LANE:(half + 1) * LANE]), 0.0)
                w = iwb_s[h]
                for g in range(ng):
                    acc[g] = acc[g] + w * s[:, grp(g)]
        for g in range(ng):
            a = acc[g]
            if diag:
                a = jnp.where((g * LANE + lane_j) // CHUNK <= row_i // CHUNK, a, -jnp.inf)
            bits = pltpu.bitcast(a, jnp.int32)
            key_s[st, g] = bits ^ ((bits >> 31) & INT_MAX)

    def p1(st, carry):
        scores(st, False)
        return carry

    lax.fori_loop(0, qb, p1, 0)
    scores(qb, True)

    ones_b = jnp.ones((LANE, LANE), _BF16)
    assert key_s.shape[0] * ng <= 256

    T_, CAND_, MIDX_ = 0, 1, 2
    ACC_, CNT_ = 0, 1
    rb = 64
    lane_rb = lax.broadcasted_iota(jnp.int32, (rb, LANE), 1)

    def count(pred_fn):
        def tile(st, first):
            for r0 in range(0, nq, rb):
                rs = slice(r0, r0 + rb)
                acc = None if first else sf_s[ACC_, rs, :]
                for g in range(ng):
                    one = jnp.where(pred_fn(key_s[st, g, rs, :], rs), 1.0, 0.0)
                    acc = one if acc is None else acc + one
                sf_s[ACC_, rs, :] = acc

        def body(st, carry):
            tile(st, False)
            return carry

        tile(0, True)
        lax.fori_loop(1, n_st, body, 0)
        return _dot(sf_s[ACC_].astype(_BF16), ones_b)

    si_s[CAND_] = jnp.zeros((nq, LANE), jnp.int32)

    def bit_body(i, carry):
        half = jnp.left_shift(jnp.int32(1), 30 - i)
        c = count(lambda k, rs: k >= si_s[CAND_, rs, :])
        si_s[CAND_] = si_s[CAND_] + jnp.where(c >= f_topk, half, -half)
        return carry

    lax.fori_loop(0, 31, bit_body, 0)
    c_last = count(lambda k, rs: k >= si_s[CAND_, rs, :])
    si_s[T_] = jnp.where(c_last >= f_topk, si_s[CAND_], si_s[CAND_] - 1)
    sf_s[CNT_] = count(lambda k, rs: k >= si_s[T_, rs, :])

    def encode(st, carry):
        for r0 in range(0, nq, rb):
            rs = slice(r0, r0 + rb)
            t = si_s[T_, rs, :]
            for g in range(ng):
                k = key_s[st, g, rs, :]
                pos = st * st_w + g * LANE + lane_rb
                key_s[st, g, rs, :] = jnp.where(k > t, CODE_KEEP, jnp.where(k == t, pos, CODE_DROP))
        return carry

    lax.fori_loop(0, n_st, encode, 0)
    finite_thr = si_s[T_] != KEY_NEG_INF
    si_s[MIDX_] = jnp.where(finite_thr, TIE_BOUND_ALL, 0)

    excess = jnp.max(jnp.where(finite_thr, sf_s[CNT_], 0.0)) > f_topk

    @pl.when(excess)
    def _():
        r_keep = f_topk - count(lambda code, rs: code < 0)
        sf_s[CNT_] = jnp.where(finite_thr, r_keep, 0.0)
        sf_s[ACC_] = jnp.zeros((nq, LANE), _F32)
        upper = (lax.broadcasted_iota(jnp.int32, (LANE, LANE), 0)
                 <= lax.broadcasted_iota(jnp.int32, (LANE, LANE), 1)).astype(_BF16)

        def rank_ties(st, carry):
            for r0 in range(0, nq, rb):
                rs = slice(r0, r0 + rb)
                r_blk = sf_s[CNT_, rs, :]
                before = sf_s[ACC_, rs, :]
                for g in range(ng):
                    code = key_s[st, g, rs, :]
                    tie = jnp.where(code >= 0, jnp.where(code < CODE_DROP, 1.0, 0.0), 0.0)
                    tie_b = tie.astype(_BF16)
                    rank = before + _dot(tie_b, upper)
                    kept_tie = jnp.where(rank <= r_blk, tie, 0.0)
                    mb = jnp.where(code < 0, 0.0, jnp.where(kept_tie > 0.0, 0.0, MASK_NEG))
                    key_s[st, g, rs, :] = pltpu.bitcast(mb, jnp.int32)
                    before = before + _dot(tie_b, ones_b)
                sf_s[ACC_, rs, :] = before
            return carry

        lax.fori_loop(0, n_st, rank_ties, 0)

    @pl.when(jnp.logical_not(excess))
    def _():
        def p2(st, carry):
            for r0 in range(0, nq, rb):
                rs = slice(r0, r0 + rb)
                bound = si_s[MIDX_, rs, :]
                for g in range(ng):
                    mb = jnp.where(key_s[st, g, rs, :] < bound, 0.0, MASK_NEG)
                    key_s[st, g, rs, :] = pltpu.bitcast(mb, jnp.int32)
            return carry

        lax.fori_loop(0, n_st, p2, 0)

    def attend(st, mode):
        off = pl.multiple_of(st * st_w, st_w)
        for h in range(H_B):
            sl = slice(h * hd, (h + 1) * hd)
            mask = jnp.concatenate([pltpu.bitcast(key_s[st, g], _F32) for g in range(ng)], axis=1)
            s = _dot_nt(qs_s[h], k_ref[pl.ds(off, st_w), sl]) + mask
            if mode:
                s_s[...] = s
                b_prev = bias_ref[h, :, 0:LANE]
                if mode == 1:
                    s_s[0:LANE, (ng - 1) * LANE:ng * LANE] += b_prev
                else:
                    b_diag = bias_ref[h, :, LANE:2 * LANE]
                    for rt in range(ng):
                        s_s[grp(rt), grp(rt)] += b_diag
                        if rt:
                            s_s[grp(rt), grp(rt - 1)] += b_prev
                s = s_s[...]
            m_old = m_s[h]
            gmax = s[:, grp(0)]
            for g in range(1, ng):
                gmax = jnp.maximum(gmax, s[:, grp(g)])
            m_new = jnp.maximum(m_old, jnp.max(gmax, axis=1, keepdims=True))
            alpha = jnp.exp2(m_old - m_new)
            pieces = [jnp.exp2(s[:, grp(g)] - m_new) for g in range(ng)]
            psum = pieces[0]
            for g in range(1, ng):
                psum = psum + pieces[g]
            p = jnp.concatenate(pieces, axis=1).astype(_BF16)
            m_s[h] = m_new
            l_s[h] = alpha * l_s[h] + psum
            acc_s[h] = alpha * acc_s[h] + _dot(p, v_ref[pl.ds(off, st_w), sl])

    def p3(st, carry):
        attend(st, 0)
        return carry

    lax.fori_loop(0, qb - 1, p3, 0)

    @pl.when(qb > 0)
    def _():
        attend(qb - 1, 1)

    attend(qb, 2)

    for h in range(H_B):
        sl = slice(h * hd, (h + 1) * hd)
        l_fin = jnp.sum(l_s[h], axis=1, keepdims=True)
        o_ref[:, sl] = (acc_s[h] / l_fin).astype(o_ref.dtype)


def _dsa(pb, ps, bias, b_, t_, col0):
    nq = KEY_TILE
    nb = t_ // nq
    hd = H_B * D_HEAD_B
    topk = min(TOPK_MAX, t_ // 4)
    iq_w = H_IDX * D_IDX
    ik_w = 2 * LANE
    assert col0 % hd == 0 and (col0 + 3 * hd) % iq_w == 0 and (col0 + 3 * hd + iq_w) % ik_w == 0
    cq = col0 // hd
    resident = functools.partial(pl.BlockSpec, pipeline_mode=pl.Buffered(1))
    return pl.pallas_call(
        functools.partial(_dsa_kernel, topk=topk),
        grid=(b_, nb),
        in_specs=[
            pl.BlockSpec((nq, hd), lambda b, i: (b * nb + i, cq)),
            resident((t_, hd), lambda b, i: (b, cq + 1)),
            resident((t_, hd), lambda b, i: (b, cq + 2)),
            pl.BlockSpec((nq, iq_w), lambda b, i: (b * nb + i, (col0 + 3 * hd) // iq_w)),
            resident((t_, ik_w), lambda b, i: (b, (col0 + 3 * hd + iq_w) // ik_w)),
            pl.BlockSpec((nq, LANE), lambda b, i: (b * nb + i, 0)),
            resident((H_B, Q_BLOCK, 2 * LANE), lambda b, i: (0, 0, 0)),
        ],
        out_specs=pl.BlockSpec((nq, hd), lambda b, i: (b * nb + i, 0)),
        out_shape=jax.ShapeDtypeStruct((b_ * t_, hd), _BF16),
        scratch_shapes=[
            pltpu.VMEM((nb, KEY_TILE // LANE, nq, LANE), jnp.int32),
            pltpu.VMEM((H_B, nq, D_HEAD_B), _BF16),
            pltpu.VMEM((H_B, nq, LANE), _F32),
            pltpu.VMEM((H_B, nq, LANE), _F32),
            pltpu.VMEM((H_B, nq, LANE), _F32),
            pltpu.VMEM((H_B, nq, D_HEAD_B), _F32),
            pltpu.VMEM((nq, KEY_TILE), _F32),
            pltpu.VMEM((3, nq, LANE), jnp.int32),
            pltpu.VMEM((2, nq, LANE), _F32),
        ],
        compiler_params=pltpu.CompilerParams(
            dimension_semantics=("arbitrary", "arbitrary"), vmem_limit_bytes=VMEM_LIMIT),
        name="dsa",
    )(pb, pb, pb, pb, pb, ps, bias)


def _merge_kernel(oa_ref, ob_ref, g_ref, x_ref, bg_ref, wa_ref, wb_ref, wo_ref, o_ref):
    d = x_ref.shape[1]
    a = _dot(oa_ref[...], wa_ref[...])
    b = _dot(ob_ref[...], wb_ref[...])
    g = _sigmoid(g_ref[...].astype(_F32) + bg_ref[...])
    merged = g[:, 0:d] * a + g[:, d:2 * d] * b
    o_ref[...] = x_ref[...] + _dot(merged.astype(_BF16), wo_ref[...])


def _merge(oa, ob, pb, x2d, bg, wa, wb, wo, tn, col_g):
    n, d = x2d.shape
    row = lambda w: pl.BlockSpec((tn, w), lambda i: (i, 0))
    const = lambda a: pl.BlockSpec(a.shape, lambda i: (0, 0))
    return pl.pallas_call(
        _merge_kernel,
        grid=(n // tn,),
        in_specs=[row(oa.shape[1]), row(ob.shape[1]),
                  pl.BlockSpec((tn, 2 * d), lambda i: (i, col_g // (2 * d))), row(d),
                  const(bg), const(wa), const(wb), const(wo)],
        out_specs=row(d),
        out_shape=jax.ShapeDtypeStruct((n, d), _F32),
        compiler_params=pltpu.CompilerParams(
            dimension_semantics=("arbitrary",), vmem_limit_bytes=VMEM_LIMIT),
        name="merge",
    )(oa, ob, pb, x2d, bg, wa, wb, wo)


def _mlp_kernel(x_ref, n2_ref, w1_ref, w2_ref, nf_ref, o_ref, *, final_norm):
    x = x_ref[...]
    h2 = _rms(x, n2_ref[...]).astype(_BF16)
    hid = jnp.maximum(_dot(h2, w1_ref[...]), 0.0)
    y = x + _dot((hid * hid).astype(_BF16), w2_ref[...])
    o_ref[...] = _rms(y, nf_ref[...]) if final_norm else y


def _mlp(x1, n2, w1, w2, nf, tn, final_norm):
    n, d = x1.shape
    row = pl.BlockSpec((tn, d), lambda i: (i, 0))
    const = lambda a: pl.BlockSpec(a.shape, lambda i: (0, 0), pipeline_mode=pl.Buffered(1))
    return pl.pallas_call(
        functools.partial(_mlp_kernel, final_norm=final_norm),
        grid=(n // tn,),
        in_specs=[row, const(n2), const(w1), const(w2), const(nf)],
        out_specs=row,
        out_shape=jax.ShapeDtypeStruct((n, d), _F32),
        compiler_params=pltpu.CompilerParams(
            dimension_semantics=("arbitrary",), vmem_limit_bytes=VMEM_LIMIT),
        name="mlp",
    )(x1, n2, w1, w2, nf)


def _t5_bucket(rel):
    half = N_BUCKETS // 2
    max_exact = half // 2
    base = jnp.where(rel > 0, half, 0)
    n = jnp.abs(rel)
    n_f = jnp.maximum(n, 1).astype(jnp.float32)
    large = max_exact + (jnp.log(n_f / max_exact) / math.log(MAX_DISTANCE / max_exact)
                         * (half - max_exact)).astype(jnp.int32)
    large = jnp.minimum(large, half - 1)
    return base + jnp.where(n < max_exact, n, large)


def _pick_tile(n, prefs):
    for t in prefs:
        if n % t == 0:
            return t
    raise ValueError(f"no tile in {prefs} divides {n}")


def kernel(x, norm1_w, w_in, conv_a_w, a_log, dt_bias, norm_a_w, rel_bias_table, w_gate, b_gate,
           w_proj_a, w_proj_b, w_out, norm2_w, w_ff1, w_ff2, norm_final_w):
    b_, t_, d = x.shape
    depth = norm1_w.shape[0]
    n = b_ * t_
    ha, hb = H_A * DK_A, H_B * D_HEAD_B
    assert t_ % KEY_TILE == 0 and t_ % GDN_CHUNK == 0 and d % LANE == 0
    assert DK_A == DV_A == D_HEAD_B == LANE and 2 * D_IDX == LANE
    assert Q_BLOCK >= MAX_DISTANCE

    o_za = 3 * ha
    o_ba = 4 * ha
    o_aa = o_ba + H_A
    o_qb = o_aa + H_A
    o_iq = o_qb + 3 * hb
    o_ik = o_iq + H_IDX * D_IDX
    o_iw = o_ik + D_IDX

    rel = (jnp.arange(2 * Q_BLOCK, dtype=jnp.int32)[None, :] - Q_BLOCK) \
        - jnp.arange(Q_BLOCK, dtype=jnp.int32)[:, None]
    far = jnp.full((Q_BLOCK, LANE), -(Q_BLOCK + 1), jnp.int32)
    bucket = _t5_bucket(jnp.concatenate([rel, far], axis=1))
    bias = _bias_tiles(rel_bias_table.astype(_F32), bucket)

    x2d = x.reshape(n, d)
    tn = _pick_tile(t_, (512, 256, 128))
    for layer in range(depth):
        wi = w_in[layer].astype(_BF16)
        zpad = lambda w: jnp.zeros((d, w), _BF16)
        ws = jnp.concatenate([wi[:, o_ba:o_qb], wi[:, o_iw:o_iw + H_IDX], zpad(LANE - 3 * H_A)], axis=1)
        ik = wi[:, o_ik:o_iw]
        wb16 = jnp.concatenate([wi[:, 0:o_ba], w_gate[layer].astype(_BF16), wi[:, o_qb:o_ik],
                                ik, zpad(D_IDX), zpad(D_IDX), ik], axis=1)
        nw = norm1_w[layer].reshape(1, d).astype(_F32)
        pb, ps = _norm_proj(x2d, nw, wb16, ws, conv_a_w[layer].astype(_F32), tn, t_ // tn)

        col_a = 0
        col_g = 4 * ha
        col_b = col_g + 2 * d
        assert col_g % (2 * d) == 0
        lanes = lambda v, off: jnp.zeros((1, LANE), _F32).at[0, off:off + H_A].set(v.astype(_F32))
        oa = _gdn(pb, ps, lanes(a_log[layer], H_A), lanes(dt_bias[layer], H_A),
                  norm_a_w[layer].reshape(1, DV_A).astype(_F32), b_, t_, col_a)
        ob = _dsa(pb, ps, bias, b_, t_, col_b)

        x2d = _merge(oa, ob, pb, x2d, b_gate[layer].reshape(1, 2 * d).astype(_F32),
                     w_proj_a[layer].astype(_BF16), w_proj_b[layer].astype(_BF16),
                     w_out[layer].astype(_BF16), tn, col_g)
        x2d = _mlp(x2d, norm2_w[layer].reshape(1, d).astype(_F32), w_ff1[layer].astype(_BF16),
                   w_ff2[layer].astype(_BF16), norm_final_w.reshape(1, d).astype(_F32),
                   _pick_tile(n, (512, 256, 128)), final_norm=layer == depth - 1)
    return x2d.reshape(b_, t_, d)
```

```python
import functools
import math

import jax
import jax.numpy as jnp
import numpy as np
from jax import lax
from jax.experimental import pallas as pl
from jax.experimental.pallas import tpu as pltpu

EPS = 1e-6
H_A = 8
DK_A = 128
DV_A = 128
CONV_K = 4
GDN_CHUNK = 128
GDN_STEP_CHUNKS = 4
H_B = 8
D_HEAD_B = 128
H_IDX = 8
D_IDX = 64
TOPK_MAX = 256
Q_BLOCK = 128
CHUNK = 64
KEY_TILE = 512
N_BUCKETS = 32
MAX_DISTANCE = 128

LANE = 128
VMEM_LIMIT = 56 * 1024 * 1024

INT_MIN = -(2**31)
INT_MAX = 2**31 - 1
KEY_NEG_INF = -2139095041
MASK_NEG = -1e30
CODE_KEEP = -1
CODE_DROP = INT_MAX
TIE_BOUND_ALL = 2**30
LOG2E = math.log2(math.e)

_F32 = jnp.float32
_BF16 = jnp.bfloat16


def _dot(a, b):
    return jnp.dot(a, b, preferred_element_type=_F32)


def _dot_nt(a, b):
    return lax.dot_general(a, b, (((1,), (1,)), ((), ())), preferred_element_type=_F32)


def _sigmoid(x):
    return 1.0 / (1.0 + jnp.exp(-x))


def _rms(x, w):
    return x * lax.rsqrt(jnp.mean(x * x, axis=-1, keepdims=True) + EPS) * w


def _norm_proj_kernel(x_ref, nw_ref, w_ref, ws_ref, cw_ref, o_ref, os_ref, halo_s, work_s, *,
                      chunk, conv_cols, norm_cols, tiles_per_seq):
    tn = x_ref.shape[0]
    h = _rms(x_ref[...], nw_ref[...]).astype(_BF16)
    os_ref[...] = _dot(h, ws_ref[...])
    seq_start = pl.program_id(0) % tiles_per_seq == 0
    for ci, c0 in enumerate(range(0, w_ref.shape[1], chunk)):
        y = _dot(h, w_ref[:, c0:c0 + chunk])
        if c0 < conv_cols:
            work_s[0:8, :] = jnp.where(seq_start, 0.0, halo_s[ci])
            work_s[8:8 + tn, :] = y
            halo_s[ci] = y[tn - 8:tn, :]
            acc = cw_ref[CONV_K - 1:CONV_K, c0:c0 + chunk] * y
            for j in range(CONV_K - 1):
                lo = 8 - (CONV_K - 1) + j
                acc = acc + cw_ref[j:j + 1, c0:c0 + chunk] * work_s[lo:lo + tn, :]
            y = acc * _sigmoid(acc)
            if c0 < norm_cols:
                parts = []
                for g0 in range(0, chunk, DK_A):
                    yh = y[:, g0:g0 + DK_A]
                    if c0 + g0 < norm_cols:
                        inv = lax.rsqrt(jnp.sum(yh * yh, axis=-1, keepdims=True) + EPS)
                        if c0 + g0 < norm_cols // 2:
                            inv = inv * (DK_A ** -0.5)
                        yh = yh * inv
                    parts.append(yh)
                y = jnp.concatenate(parts, axis=1)
        o_ref[:, c0:c0 + chunk] = y.astype(o_ref.dtype)


def _norm_proj(x2d, nw, w, ws, cw, tn, tiles_per_seq, norm_cols):
    n, d = x2d.shape
    c = w.shape[1]
    chunk = _pick_tile(c, (768, 512, 256, 128))
    conv_cols = cw.shape[1]
    assert conv_cols % chunk == 0 and tn >= 8
    assert norm_cols <= conv_cols and chunk % DK_A == 0 and (norm_cols // 2) % DK_A == 0
    const = functools.partial(pl.BlockSpec, pipeline_mode=pl.Buffered(1))
    return pl.pallas_call(
        functools.partial(_norm_proj_kernel, chunk=chunk, conv_cols=conv_cols, norm_cols=norm_cols,
                          tiles_per_seq=tiles_per_seq),
        grid=(n // tn,),
        in_specs=[
            pl.BlockSpec((tn, d), lambda i: (i, 0)),
            const((1, d), lambda i: (0, 0)),
            const((d, c), lambda i: (0, 0)),
            const((d, LANE), lambda i: (0, 0)),
            const(cw.shape, lambda i: (0, 0)),
        ],
        out_specs=[pl.BlockSpec((tn, c), lambda i: (i, 0)), pl.BlockSpec((tn, LANE), lambda i: (i, 0))],
        out_shape=[jax.ShapeDtypeStruct((n, c), _BF16), jax.ShapeDtypeStruct((n, LANE), _F32)],
        scratch_shapes=[
            pltpu.VMEM((conv_cols // chunk, 8, chunk), _F32),
            pltpu.VMEM((8 + tn, chunk), _F32),
        ],
        compiler_params=pltpu.CompilerParams(
            dimension_semantics=("arbitrary",), vmem_limit_bytes=VMEM_LIMIT),
        name="proj",
    )(x2d, nw, w, ws, cw)


def _gdn_kernel(qkvz_ref, sm_ref, arow_ref, dtrow_ref, naw_ref, o_ref,
                s_ref, kn_s, kb_s, qn_s, rhs_s, dec_s, qdec_s, ktt_s, eg_s, m_s, x_s):
    tt = qkvz_ref.shape[0]
    ct = GDN_CHUNK
    nc = tt // ct
    hd = H_A * DK_A

    @pl.when(pl.program_id(1) == 0)
    def _():
        s_ref[...] = jnp.zeros_like(s_ref)

    row = lax.broadcasted_iota(jnp.int32, (ct, ct), 0)
    col = lax.broadcasted_iota(jnp.int32, (ct, ct), 1)
    tri = (col <= row).astype(_F32)
    strict = col < row
    eye = (col == row).astype(_F32)
    n_sq = int(math.log2(ct))
    heads = range(H_A)

    def prepare(c):
        r0 = c * ct
        sm = sm_ref[r0:r0 + ct, :]
        beta_full = _sigmoid(sm)
        xg = sm + dtrow_ref[...]
        softplus = jnp.maximum(xg, 0.0) + jnp.log(1.0 + jnp.exp(-jnp.abs(xg)))
        g_full = -jnp.exp(arow_ref[...]) * softplus
        gcum = jnp.dot(tri, g_full, preferred_element_type=_F32, precision=lax.Precision.HIGHEST)
        gcum_t = gcum.T
        for h in heads:
            sl = slice(h * DK_A, (h + 1) * DK_A)
            qh, kh, vh = (qkvz_ref[r0:r0 + ct, idx * hd + h * DK_A:idx * hd + (h + 1) * DK_A].astype(_F32)
                          for idx in range(3))
            qn, kn = qh, kh
            bcol = beta_full[:, h:h + 1]
            gcol = gcum[:, H_A + h:H_A + h + 1]
            grow = gcum_t[H_A + h:H_A + h + 1, :]
            glast = gcum[ct - 1:ct, H_A + h:H_A + h + 1]
            kb = kn * bcol
            eg = jnp.exp(gcol)
            rhs_s[c, h, :, 0:DV_A] = vh * bcol
            rhs_s[c, h, :, DV_A:DV_A + DK_A] = kb * eg
            dec_s[c, h] = jnp.where(strict, jnp.exp(jnp.where(strict, gcol - grow, 0.0)), 0.0)
            qdec_s[c, h] = (qn * eg).astype(_BF16)
            ktt_s[c, h] = (kn * jnp.exp(glast - gcol)).T.astype(_BF16)
            kn_s[c, h] = kn.astype(_BF16)
            kb_s[c, h] = kb.astype(_BF16)
            qn_s[c, h] = qn.astype(_BF16)
            eg_s[c, h] = jnp.broadcast_to(jnp.exp(glast), (DK_A, DV_A))

    def solve(c):
        for h in heads:
            m = -(_dot_nt(kb_s[c, h], kn_s[c, h]) * dec_s[c, h])
            m_s[c, h] = m
            x_s[c, h] = m
        for _ in range(n_sq - 1):
            for h in heads:
                mb = m_s[c, h].astype(_BF16)
                m = _dot(mb, mb)
                m_s[c, h] = m
                x_s[c, h] = x_s[c, h] + m + _dot(x_s[c, h].astype(_BF16), m.astype(_BF16))
        for h in heads:
            rhs = rhs_s[c, h]
            rhs_s[c, h] = rhs + _dot(x_s[c, h].astype(_BF16), rhs.astype(_BF16))
        for h in heads:
            m_s[c, h] = _dot_nt(qn_s[c, h], kn_s[c, h]) * (dec_s[c, h] + eye)

    def recur(c):
        for h in heads:
            sb = s_ref[h].astype(_BF16)
            vnew = rhs_s[c, h, :, 0:DV_A] - _dot(rhs_s[c, h, :, DV_A:DV_A + DK_A].astype(_BF16), sb)
            vb = vnew.astype(_BF16)
            o = _dot(qdec_s[c, h], sb) + _dot(m_s[c, h].astype(_BF16), vb)
            s_ref[h] = s_ref[h] * eg_s[c, h] + _dot(ktt_s[c, h], vb)
            rhs_s[c, h, :, 0:DV_A] = o

    def finish(c):
        r0 = c * ct
        naw = naw_ref[...]
        for h in heads:
            sl = slice(h * DV_A, (h + 1) * DV_A)
            o = rhs_s[c, h, :, 0:DV_A]
            z = qkvz_ref[r0:r0 + ct, 3 * hd + h * DV_A:3 * hd + (h + 1) * DV_A].astype(_F32)
            on = o * lax.rsqrt(jnp.mean(o * o, axis=-1, keepdims=True) + EPS) * naw
            o_ref[r0:r0 + ct, sl] = (on * (z * _sigmoid(z))).astype(o_ref.dtype)

    for c in range(nc):
        prepare(c)
    for c in range(nc):
        solve(c)
    for c in range(nc):
        recur(c)
    for c in range(nc):
        finish(c)


def _gdn(pb, ps, arow, dtrow, naw, b_, t_, col0):
    ct = GDN_CHUNK
    nc = GDN_STEP_CHUNKS if t_ % (GDN_STEP_CHUNKS * ct) == 0 else 1
    tt = nc * ct
    hd = H_A * DK_A
    nt = t_ // tt
    assert col0 % (4 * hd) == 0
    cb = col0 // (4 * hd)
    const = lambda shape: pl.BlockSpec(shape, lambda b, t: (0, 0))
    return pl.pallas_call(
        _gdn_kernel,
        grid=(b_, nt),
        in_specs=[pl.BlockSpec((tt, 4 * hd), lambda b, t: (b * nt + t, cb)),
                  pl.BlockSpec((tt, LANE), lambda b, t: (b * nt + t, 0)),
                  const((1, LANE)), const((1, LANE)), const((1, DV_A))],
        out_specs=pl.BlockSpec((tt, hd), lambda b, t: (b * nt + t, 0)),
        out_shape=jax.ShapeDtypeStruct((b_ * t_, hd), _BF16),
        scratch_shapes=[
            pltpu.VMEM((H_A, DK_A, DV_A), _F32),
            pltpu.VMEM((nc, H_A, ct, DK_A), _BF16),
            pltpu.VMEM((nc, H_A, ct, DK_A), _BF16),
            pltpu.VMEM((nc, H_A, ct, DK_A), _BF16),
            pltpu.VMEM((nc, H_A, ct, DV_A + DK_A), _F32),
            pltpu.VMEM((nc, H_A, ct, ct), _F32),
            pltpu.VMEM((nc, H_A, ct, DK_A), _BF16),
            pltpu.VMEM((nc, H_A, DK_A, ct), _BF16),
            pltpu.VMEM((nc, H_A, DK_A, DV_A), _F32),
            pltpu.VMEM((nc, H_A, ct, ct), _F32),
            pltpu.VMEM((nc, H_A, ct, ct), _F32),
        ],
        compiler_params=pltpu.CompilerParams(
            dimension_semantics=("arbitrary", "arbitrary"), vmem_limit_bytes=VMEM_LIMIT),
        name="gdn",
    )(pb, ps, arow, dtrow, naw)


def _bias_kernel(tab_ref, bucket_ref, o_ref):
    bucket = bucket_ref[...]
    nq, w = bucket.shape
    for h in range(H_B):
        acc = jnp.zeros((nq, w), _F32)
        for b in range(N_BUCKETS):
            acc = acc + jnp.where(bucket == b, tab_ref[b, h], 0.0)
        far = acc[:, 2 * LANE:3 * LANE]
        o_ref[h] = (acc[:, 0:2 * LANE] - jnp.concatenate([far, far], axis=1)) * LOG2E


def _bias_tiles(rel_table, bucket):
    return pl.pallas_call(
        _bias_kernel,
        in_specs=[pl.BlockSpec(memory_space=pltpu.SMEM),
                  pl.BlockSpec(bucket.shape, lambda: (0, 0))],
        out_specs=pl.BlockSpec((H_B, Q_BLOCK, 2 * LANE), lambda: (0, 0, 0)),
        out_shape=jax.ShapeDtypeStruct((H_B, Q_BLOCK, 2 * LANE), _F32),
        name="rel_bias_tiles",
    )(rel_table, bucket)


def _dsa_kernel(q_ref, k_ref, v_ref, iq_ref, ik_ref, sm_ref, bias_ref, o_ref,
                key_s, qs_s, iwb_s, m_s, l_s, acc_s, s_s, si_s, sf_s, *, topk):
    nq = q_ref.shape[0]
    st_w = KEY_TILE
    assert nq == st_w
    ng = st_w // LANE
    qb = pl.program_id(1)
    n_st = qb + 1
    hd = D_HEAD_B
    scale = D_HEAD_B ** -0.5 * LOG2E
    f_topk = float(topk)

    lane_j = lax.broadcasted_iota(jnp.int32, (nq, LANE), 1)
    row_i = lax.broadcasted_iota(jnp.int32, (nq, LANE), 0)

    def grp(g):
        return slice(g * LANE, (g + 1) * LANE)

    iw = sm_ref[...]
    for h in range(H_B):
        sl = slice(h * hd, (h + 1) * hd)
        qs_s[h] = (q_ref[:, sl].astype(_F32) * scale).astype(_BF16)
        iwb_s[h] = jnp.broadcast_to(iw[:, 2 * H_A + h:2 * H_A + h + 1], (nq, LANE))
        m_s[h] = jnp.full((nq, LANE), MASK_NEG, _F32)
        l_s[h] = jnp.zeros((nq, LANE), _F32)
        acc_s[h] = jnp.zeros((nq, hd), _F32)

    def scores(st, diag):
        ikt = ik_ref[pl.ds(pl.multiple_of(st * st_w, st_w), st_w), :]
        acc = [jnp.zeros((nq, LANE), _F32) for _ in range(ng)]
        for p in range(H_IDX // 2):
            xq = iq_ref[:, p * LANE:(p + 1) * LANE]
            for half in range(2):
                h = 2 * p + half
                s = jnp.maximum(_dot_nt(xq, ikt[:, half * LANE:(half + 1) * LANE]), 0.0)
                w = iwb_s[h]
                for g in range(ng):
                    acc[g] = acc[g] + w * s[:, grp(g)]
        for g in range(ng):
            a = acc[g]
            if diag:
                a = jnp.where((g * LANE + lane_j) // CHUNK <= row_i // CHUNK, a, -jnp.inf)
            bits = pltpu.bitcast(a, jnp.int32)
            key_s[st, g] = bits ^ ((bits >> 31) & INT_MAX)

    def p1(st, carry):
        scores(st, False)
        return carry

    lax.fori_loop(0, qb, p1, 0)
    scores(qb, True)

    ones_b = jnp.ones((LANE, LANE), _BF16)
    assert key_s.shape[0] * ng <= 256

    T_, CAND_, MIDX_ = 0, 1, 2
    ACC_, CNT_ = 0, 1
    rb = 64
    lane_rb = lax.broadcasted_iota(jnp.int32, (rb, LANE), 1)

    def count(pred_fn):
        def tile(st, first):
            for r0 in range(0, nq, rb):
                rs = slice(r0, r0 + rb)
                acc = None if first else sf_s[ACC_, rs, :]
                for g in range(ng):
                    one = jnp.where(pred_fn(key_s[st, g, rs, :], rs), 1.0, 0.0)
                    acc = one if acc is None else acc + one
                sf_s[ACC_, rs, :] = acc

        def body(st, carry):
            tile(st, False)
            return carry

        tile(0, True)
        lax.fori_loop(1, n_st, body, 0)
        return _dot(sf_s[ACC_].astype(_BF16), ones_b)

    si_s[CAND_] = jnp.zeros((nq, LANE), jnp.int32)

    def bit_body(i, carry):
        half = jnp.left_shift(jnp.int32(1), 30 - i)
        c = count(lambda k, rs: k >= si_s[CAND_, rs, :])
        si_s[CAND_] = si_s[CAND_] + jnp.where(c >= f_topk, half, -half)
        return carry

    lax.fori_loop(0, 31, bit_body, 0)
    c_last = count(lambda k, rs: k >= si_s[CAND_, rs, :])
    si_s[T_] = jnp.where(c_last >= f_topk, si_s[CAND_], si_s[CAND_] - 1)
    sf_s[CNT_] = count(lambda k, rs: k >= si_s[T_, rs, :])

    def encode(st, carry):
        for r0 in range(0, nq, rb):
            rs = slice(r0, r0 + rb)
            t = si_s[T_, rs, :]
            for g in range(ng):
                k = key_s[st, g, rs, :]
                pos = st * st_w + g * LANE + lane_rb
                key_s[st, g, rs, :] = jnp.where(k > t, CODE_KEEP, jnp.where(k == t, pos, CODE_DROP))
        return carry

    lax.fori_loop(0, n_st, encode, 0)
    finite_thr = si_s[T_] != KEY_NEG_INF
    si_s[MIDX_] = jnp.where(finite_thr, TIE_BOUND_ALL, 0)

    excess = jnp.max(jnp.where(finite_thr, sf_s[CNT_], 0.0)) > f_topk

    @pl.when(excess)
    def _():
        r_keep = f_topk - count(lambda code, rs: code < 0)
        sf_s[CNT_] = jnp.where(finite_thr, r_keep, 0.0)
        sf_s[ACC_] = jnp.zeros((nq, LANE), _F32)
        upper = (lax.broadcasted_iota(jnp.int32, (LANE, LANE), 0)
                 <= lax.broadcasted_iota(jnp.int32, (LANE, LANE), 1)).astype(_BF16)

        def rank_ties(st, carry):
            for r0 in range(0, nq, rb):
                rs = slice(r0, r0 + rb)
                r_blk = sf_s[CNT_, rs, :]
                before = sf_s[ACC_, rs, :]
                for g in range(ng):
                    code = key_s[st, g, rs, :]
                    tie = jnp.where(code >= 0, jnp.where(code < CODE_DROP, 1.0, 0.0), 0.0)
                    tie_b = tie.astype(_BF16)
                    rank = before + _dot(tie_b, upper)
                    kept_tie = jnp.where(rank <= r_blk, tie, 0.0)
                    mb = jnp.where(code < 0, 0.0, jnp.where(kept_tie > 0.0, 0.0, MASK_NEG))
                    key_s[st, g, rs, :] = pltpu.bitcast(mb, jnp.int32)
                    before = before + _dot(tie_b, ones_b)
                sf_s[ACC_, rs, :] = before
            return carry

        lax.fori_loop(0, n_st, rank_ties, 0)

    @pl.when(jnp.logical_not(excess))
    def _():
        def p2(st, carry):
            for r0 in range(0, nq, rb):
                rs = slice(r0, r0 + rb)
                bound = si_s[MIDX_, rs, :]
                for g in range(ng):
                    mb = jnp.where(key_s[st, g, rs, :] < bound, 0.0, MASK_NEG)
                    key_s[st, g, rs, :] = pltpu.bitcast(mb, jnp.int32)
            return carry

        lax.fori_loop(0, n_st, p2, 0)

    def attend(st, mode):
        off = pl.multiple_of(st * st_w, st_w)
        for h in range(H_B):
            sl = slice(h * hd, (h + 1) * hd)
            mask = jnp.concatenate([pltpu.bitcast(key_s[st, g], _F32) for g in range(ng)], axis=1)
            s = _dot_nt(qs_s[h], k_ref[pl.ds(off, st_w), sl]) + mask
            if mode:
                s_s[...] = s
                b_prev = bias_ref[h, :, 0:LANE]
                if mode == 1:
                    s_s[0:LANE, (ng - 1) * LANE:ng * LANE] += b_prev
                else:
                    b_diag = bias_ref[h, :, LANE:2 * LANE]
                    for rt in range(ng):
                        s_s[grp(rt), grp(rt)] += b_diag
                        if rt:
                            s_s[grp(rt), grp(rt - 1)] += b_prev
                s = s_s[...]
            m_old = m_s[h]
            gmax = s[:, grp(0)]
            for g in range(1, ng):
                gmax = jnp.maximum(gmax, s[:, grp(g)])
            m_new = jnp.maximum(m_old, jnp.max(gmax, axis=1, keepdims=True))
            alpha = jnp.exp2(m_old - m_new)
            pieces = [jnp.exp2(s[:, grp(g)] - m_new) for g in range(ng)]
            psum = pieces[0]
            for g in range(1, ng):
                psum = psum + pieces[g]
            p = jnp.concatenate(pieces, axis=1).astype(_BF16)
            m_s[h] = m_new
            l_s[h] = alpha * l_s[h] + psum
            acc_s[h] = alpha * acc_s[h] + _dot(p, v_ref[pl.ds(off, st_w), sl])

    def p3(st, carry):
        attend(st, 0)
        return carry

    lax.fori_loop(0, qb - 1, p3, 0)

    @pl.when(qb > 0)
    def _():
        attend(qb - 1, 1)

    attend(qb, 2)

    for h in range(H_B):
        sl = slice(h * hd, (h + 1) * hd)
        l_fin = jnp.sum(l_s[h], axis=1, keepdims=True)
        o_ref[:, sl] = (acc_s[h] / l_fin).astype(o_ref.dtype)


def _dsa(pb, ps, bias, b_, t_, col0):
    nq = KEY_TILE
    nb = t_ // nq
    hd = H_B * D_HEAD_B
    topk = min(TOPK_MAX, t_ // 4)
    iq_w = H_IDX * D_IDX
    ik_w = 2 * LANE
    assert col0 % hd == 0 and (col0 + 3 * hd) % iq_w == 0 and (col0 + 3 * hd + iq_w) % ik_w == 0
    cq = col0 // hd
    resident = functools.partial(pl.BlockSpec, pipeline_mode=pl.Buffered(1))
    return pl.pallas_call(
        functools.partial(_dsa_kernel, topk=topk),
        grid=(b_, nb),
        in_specs=[
            pl.BlockSpec((nq, hd), lambda b, i: (b * nb + i, cq)),
            resident((t_, hd), lambda b, i: (b, cq + 1)),
            resident((t_, hd), lambda b, i: (b, cq + 2)),
            pl.BlockSpec((nq, iq_w), lambda b, i: (b * nb + i, (col0 + 3 * hd) // iq_w)),
            resident((t_, ik_w), lambda b, i: (b, (col0 + 3 * hd + iq_w) // ik_w)),
            pl.BlockSpec((nq, LANE), lambda b, i: (b * nb + i, 0)),
            resident((H_B, Q_BLOCK, 2 * LANE), lambda b, i: (0, 0, 0)),
        ],
        out_specs=pl.BlockSpec((nq, hd), lambda b, i: (b * nb + i, 0)),
        out_shape=jax.ShapeDtypeStruct((b_ * t_, hd), _BF16),
        scratch_shapes=[
            pltpu.VMEM((nb, KEY_TILE // LANE, nq, LANE), jnp.int32),
            pltpu.VMEM((H_B, nq, D_HEAD_B), _BF16),
            pltpu.VMEM((H_B, nq, LANE), _F32),
            pltpu.VMEM((H_B, nq, LANE), _F32),
            pltpu.VMEM((H_B, nq, LANE), _F32),
            pltpu.VMEM((H_B, nq, D_HEAD_B), _F32),
            pltpu.VMEM((nq, KEY_TILE), _F32),
            pltpu.VMEM((3, nq, LANE), jnp.int32),
            pltpu.VMEM((2, nq, LANE), _F32),
        ],
        compiler_params=pltpu.CompilerParams(
            dimension_semantics=("arbitrary", "arbitrary"), vmem_limit_bytes=VMEM_LIMIT),
        name="dsa",
    )(pb, pb, pb, pb, pb, ps, bias)


def _merge_kernel(oa_ref, ob_ref, g_ref, x_ref, bg_ref, wa_ref, wb_ref, wo_ref, o_ref):
    d = x_ref.shape[1]
    a = _dot(oa_ref[...], wa_ref[...])
    b = _dot(ob_ref[...], wb_ref[...])
    g = _sigmoid(g_ref[...].astype(_F32) + bg_ref[...])
    merged = g[:, 0:d] * a + g[:, d:2 * d] * b
    o_ref[...] = x_ref[...] + _dot(merged.astype(_BF16), wo_ref[...])


def _merge(oa, ob, pb, x2d, bg, wa, wb, wo, tn, col_g):
    n, d = x2d.shape
    row = lambda w: pl.BlockSpec((tn, w), lambda i: (i, 0))
    const = lambda a: pl.BlockSpec(a.shape, lambda i: (0, 0))
    return pl.pallas_call(
        _merge_kernel,
        grid=(n // tn,),
        in_specs=[row(oa.shape[1]), row(ob.shape[1]),
                  pl.BlockSpec((tn, 2 * d), lambda i: (i, col_g // (2 * d))), row(d),
                  const(bg), const(wa), const(wb), const(wo)],
        out_specs=row(d),
        out_shape=jax.ShapeDtypeStruct((n, d), _F32),
        compiler_params=pltpu.CompilerParams(
            dimension_semantics=("arbitrary",), vmem_limit_bytes=VMEM_LIMIT),
        name="merge",
    )(oa, ob, pb, x2d, bg, wa, wb, wo)


def _mlp_kernel(x_ref, n2_ref, w1_ref, w2_ref, nf_ref, o_ref, *, final_norm):
    x = x_ref[...]
    h2 = _rms(x, n2_ref[...]).astype(_BF16)
    hid = jnp.maximum(_dot(h2, w1_ref[...]), 0.0)
    y = x + _dot((hid * hid).astype(_BF16), w2_ref[...])
    o_ref[...] = _rms(y, nf_ref[...]) if final_norm else y


def _mlp(x1, n2, w1, w2, nf, tn, final_norm):
    n, d = x1.shape
    row = pl.BlockSpec((tn, d), lambda i: (i, 0))
    const = lambda a: pl.BlockSpec(a.shape, lambda i: (0, 0), pipeline_mode=pl.Buffered(1))
    return pl.pallas_call(
        functools.partial(_mlp_kernel, final_norm=final_norm),
        grid=(n // tn,),
        in_specs=[row, const(n2), const(w1), const(w2), const(nf)],
        out_specs=row,
        out_shape=jax.ShapeDtypeStruct((n, d), _F32),
        compiler_params=pltpu.CompilerParams(
            dimension_semantics=("arbitrary",), vmem_limit_bytes=VMEM_LIMIT),
        name="mlp",
    )(x1, n2, w1, w2, nf)


def _t5_bucket(rel):
    half = N_BUCKETS // 2
    max_exact = half // 2
    base = jnp.where(rel > 0, half, 0)
    n = jnp.abs(rel)
    n_f = jnp.maximum(n, 1).astype(jnp.float32)
    large = max_exact + (jnp.log(n_f / max_exact) / math.log(MAX_DISTANCE / max_exact)
                         * (half - max_exact)).astype(jnp.int32)
    large = jnp.minimum(large, half - 1)
    return base + jnp.where(n < max_exact, n, large)


def _pick_tile(n, prefs):
    for t in prefs:
        if n % t == 0:
            return t
    raise ValueError(f"no tile in {prefs} divides {n}")


def kernel(x, norm1_w, w_in, conv_a_w, a_log, dt_bias, norm_a_w, rel_bias_table, w_gate, b_gate,
           w_proj_a, w_proj_b, w_out, norm2_w, w_ff1, w_ff2, norm_final_w):
    b_, t_, d = x.shape
    depth = norm1_w.shape[0]
    n = b_ * t_
    ha, hb = H_A * DK_A, H_B * D_HEAD_B
    assert t_ % KEY_TILE == 0 and t_ % GDN_CHUNK == 0 and d % LANE == 0
    assert DK_A == DV_A == D_HEAD_B == LANE and 2 * D_IDX == LANE
    assert Q_BLOCK >= MAX_DISTANCE

    o_za = 3 * ha
    o_ba = 4 * ha
    o_aa = o_ba + H_A
    o_qb = o_aa + H_A
    o_iq = o_qb + 3 * hb
    o_ik = o_iq + H_IDX * D_IDX
    o_iw = o_ik + D_IDX

    rel = (jnp.arange(2 * Q_BLOCK, dtype=jnp.int32)[None, :] - Q_BLOCK) \
        - jnp.arange(Q_BLOCK, dtype=jnp.int32)[:, None]
    far = jnp.full((Q_BLOCK, LANE), -(Q_BLOCK + 1), jnp.int32)
    bucket = _t5_bucket(jnp.concatenate([rel, far], axis=1))
    bias = _bias_tiles(rel_bias_table.astype(_F32), bucket)

    x2d = x.reshape(n, d)
    tn = _pick_tile(t_, (512, 256, 128))
    for layer in range(depth):
        wi = w_in[layer].astype(_BF16)
        zpad = lambda w: jnp.zeros((d, w), _BF16)
        ws = jnp.concatenate([wi[:, o_ba:o_qb], wi[:, o_iw:o_iw + H_IDX], zpad(LANE - 3 * H_A)], axis=1)
        ik = wi[:, o_ik:o_iw]
        wb16 = jnp.concatenate([wi[:, 0:o_ba], w_gate[layer].astype(_BF16), wi[:, o_qb:o_ik],
                                ik, zpad(D_IDX), zpad(D_IDX), ik], axis=1)
        nw = norm1_w[layer].reshape(1, d).astype(_F32)
        pb, ps = _norm_proj(x2d, nw, wb16, ws, conv_a_w[layer].astype(_F32), tn, t_ // tn, norm_cols=2 * ha)

        col_a = 0
        col_g = 4 * ha
        col_b = col_g + 2 * d
        assert col_g % (2 * d) == 0
        lanes = lambda v, off: jnp.zeros((1, LANE), _F32).at[0, off:off + H_A].set(v.astype(_F32))
        oa = _gdn(pb, ps, lanes(a_log[layer], H_A), lanes(dt_bias[layer], H_A),
                  norm_a_w[layer].reshape(1, DV_A).astype(_F32), b_, t_, col_a)
        ob = _dsa(pb, ps, bias, b_, t_, col_b)

        x2d = _merge(oa, ob, pb, x2d, b_gate[layer].reshape(1, 2 * d).astype(_F32),
                     w_proj_a[layer].astype(_BF16), w_proj_b[layer].astype(_BF16),
                     w_out[layer].astype(_BF16), tn, col_g)
        x2d = _mlp(x2d, norm2_w[layer].reshape(1, d).astype(_F32), w_ff1[layer].astype(_BF16),
                   w_ff2[layer].astype(_BF16), norm_final_w.reshape(1, d).astype(_F32),
                   _pick_tile(n, (512, 256, 128)), final_norm=layer == depth - 1)
    return x2d.reshape(b_, t_, d)
```

```python
import functools
import math

import jax
import jax.numpy as jnp
from jax import lax
from jax.experimental import pallas as pl
from jax.experimental.pallas import tpu as pltpu

EPS = 1e-6
H_A = 8
DK_A = 128
DV_A = 128
CONV_K = 4
GDN_CHUNK = 128
GDN_STEP_CHUNKS = 4
H_B = 8
D_HEAD_B = 128
H_IDX = 8
D_IDX = 64
TOPK_MAX = 256
Q_BLOCK = 128
CHUNK = 64
KEY_TILE = 512
N_BUCKETS = 32
MAX_DISTANCE = 128

LANE = 128
VMEM_LIMIT = 56 * 1024 * 1024

INT_MAX = 2**31 - 1
KEY_NEG_INF = -2139095041
MASK_NEG = -1e30
CODE_KEEP = -1
CODE_DROP = INT_MAX
TIE_BOUND_ALL = 2**30
LOG2E = math.log2(math.e)

_F32 = jnp.float32
_BF16 = jnp.bfloat16


def _dot(a, b):
    return jnp.dot(a, b, preferred_element_type=_F32)


def _dot_nt(a, b):
    return lax.dot_general(a, b, (((1,), (1,)), ((), ())), preferred_element_type=_F32)


def _sigmoid(x):
    return 1.0 / (1.0 + jnp.exp(-x))


def _rms(x, w):
    return x * lax.rsqrt(jnp.mean(x * x, axis=-1, keepdims=True) + EPS) * w


def _norm_proj_kernel(x_ref, nw_ref, w_ref, ws_ref, cw_ref, o_ref, os_ref, halo_s, work_s, *,
                      chunk, conv_cols, norm_cols, tiles_per_seq):
    tn = x_ref.shape[0]
    h = _rms(x_ref[...], nw_ref[...]).astype(_BF16)
    os_ref[...] = _dot(h, ws_ref[...])
    seq_start = pl.program_id(0) % tiles_per_seq == 0
    for ci, c0 in enumerate(range(0, w_ref.shape[1], chunk)):
        y = _dot(h, w_ref[:, c0:c0 + chunk])
        if c0 < conv_cols:
            work_s[0:8, :] = jnp.where(seq_start, 0.0, halo_s[ci])
            work_s[8:8 + tn, :] = y
            halo_s[ci] = y[tn - 8:tn, :]
            acc = cw_ref[CONV_K - 1:CONV_K, c0:c0 + chunk] * y
            for j in range(CONV_K - 1):
                lo = 8 - (CONV_K - 1) + j
                acc = acc + cw_ref[j:j + 1, c0:c0 + chunk] * work_s[lo:lo + tn, :]
            y = acc * _sigmoid(acc)
            if c0 < norm_cols:
                parts = []
                for g0 in range(0, chunk, DK_A):
                    yh = y[:, g0:g0 + DK_A]
                    if c0 + g0 < norm_cols:
                        inv = lax.rsqrt(jnp.sum(yh * yh, axis=-1, keepdims=True) + EPS)
                        if c0 + g0 < norm_cols // 2:
                            inv = inv * (DK_A ** -0.5)
                        yh = yh * inv
                    parts.append(yh)
                y = jnp.concatenate(parts, axis=1)
        o_ref[:, c0:c0 + chunk] = y.astype(o_ref.dtype)


def _norm_proj(x2d, nw, w, ws, cw, tn, tiles_per_seq, norm_cols):
    n, d = x2d.shape
    c = w.shape[1]
    chunk = _pick_tile(c, (768, 512, 256, 128))
    conv_cols = cw.shape[1]
    assert conv_cols % chunk == 0 and tn >= 8
    assert norm_cols <= conv_cols and chunk % DK_A == 0 and (norm_cols // 2) % DK_A == 0
    const = functools.partial(pl.BlockSpec, pipeline_mode=pl.Buffered(1))
    return pl.pallas_call(
        functools.partial(_norm_proj_kernel, chunk=chunk, conv_cols=conv_cols, norm_cols=norm_cols,
                          tiles_per_seq=tiles_per_seq),
        grid=(n // tn,),
        in_specs=[
            pl.BlockSpec((tn, d), lambda i: (i, 0)),
            const((1, d), lambda i: (0, 0)),
            const((d, c), lambda i: (0, 0)),
            const((d, LANE), lambda i: (0, 0)),
            const(cw.shape, lambda i: (0, 0)),
        ],
        out_specs=[pl.BlockSpec((tn, c), lambda i: (i, 0)), pl.BlockSpec((tn, LANE), lambda i: (i, 0))],
        out_shape=[jax.ShapeDtypeStruct((n, c), _BF16), jax.ShapeDtypeStruct((n, LANE), _F32)],
        scratch_shapes=[
            pltpu.VMEM((conv_cols // chunk, 8, chunk), _F32),
            pltpu.VMEM((8 + tn, chunk), _F32),
        ],
        compiler_params=pltpu.CompilerParams(
            dimension_semantics=("arbitrary",), vmem_limit_bytes=VMEM_LIMIT),
        name="proj",
    )(x2d, nw, w, ws, cw)


def _gdn_kernel(qkvz_ref, sm_ref, arow_ref, dtrow_ref, naw_ref, o_ref,
                s_ref, kn_s, kb_s, qn_s, rhs_s, dec_s, qdec_s, ktt_s, eg_s, m_s, x_s):
    tt = qkvz_ref.shape[0]
    ct = GDN_CHUNK
    nc = tt // ct
    hd = H_A * DK_A

    @pl.when(pl.program_id(1) == 0)
    def _():
        s_ref[...] = jnp.zeros_like(s_ref)

    row = lax.broadcasted_iota(jnp.int32, (ct, ct), 0)
    col = lax.broadcasted_iota(jnp.int32, (ct, ct), 1)
    tri = (col <= row).astype(_F32)
    strict = col < row
    eye = (col == row).astype(_F32)
    n_sq = int(math.log2(ct))
    heads = range(H_A)

    def prepare(c):
        r0 = c * ct
        sm = sm_ref[r0:r0 + ct, :]
        beta_full = _sigmoid(sm)
        xg = sm + dtrow_ref[...]
        softplus = jnp.maximum(xg, 0.0) + jnp.log(1.0 + jnp.exp(-jnp.abs(xg)))
        g_full = -jnp.exp(arow_ref[...]) * softplus
        gcum = jnp.dot(tri, g_full, preferred_element_type=_F32, precision=lax.Precision.HIGHEST)
        gcum_t = gcum.T
        for h in heads:
            sl = slice(h * DK_A, (h + 1) * DK_A)
            qh, kh, vh = (qkvz_ref[r0:r0 + ct, idx * hd + h * DK_A:idx * hd + (h + 1) * DK_A].astype(_F32)
                          for idx in range(3))
            qn, kn = qh, kh
            bcol = beta_full[:, h:h + 1]
            gcol = gcum[:, H_A + h:H_A + h + 1]
            grow = gcum_t[H_A + h:H_A + h + 1, :]
            glast = gcum[ct - 1:ct, H_A + h:H_A + h + 1]
            kb = kn * bcol
            eg = jnp.exp(gcol)
            rhs_s[c, h, :, 0:DV_A] = vh * bcol
            rhs_s[c, h, :, DV_A:DV_A + DK_A] = kb * eg
            dec_s[c, h] = jnp.where(strict, jnp.exp(jnp.where(strict, gcol - grow, 0.0)), 0.0)
            qdec_s[c, h] = (qn * eg).astype(_BF16)
            ktt_s[c, h] = (kn * jnp.exp(glast - gcol)).T.astype(_BF16)
            kn_s[c, h] = kn.astype(_BF16)
            kb_s[c, h] = kb.astype(_BF16)
            qn_s[c, h] = qn.astype(_BF16)
            eg_s[c, h] = jnp.broadcast_to(jnp.exp(glast), (DK_A, DV_A))

    def solve(c):
        for h in heads:
            m = -(_dot_nt(kb_s[c, h], kn_s[c, h]) * dec_s[c, h])
            m_s[c, h] = m
            x_s[c, h] = m
        for _ in range(n_sq - 1):
            for h in heads:
                mb = m_s[c, h].astype(_BF16)
                m = _dot(mb, mb)
                m_s[c, h] = m
                x_s[c, h] = x_s[c, h] + m + _dot(x_s[c, h].astype(_BF16), m.astype(_BF16))
        for h in heads:
            rhs = rhs_s[c, h]
            rhs_s[c, h] = rhs + _dot(x_s[c, h].astype(_BF16), rhs.astype(_BF16))
        for h in heads:
            m_s[c, h] = _dot_nt(qn_s[c, h], kn_s[c, h]) * (dec_s[c, h] + eye)

    def recur(c):
        for h in heads:
            sb = s_ref[h].astype(_BF16)
            vnew = rhs_s[c, h, :, 0:DV_A] - _dot(rhs_s[c, h, :, DV_A:DV_A + DK_A].astype(_BF16), sb)
            vb = vnew.astype(_BF16)
            o = _dot(qdec_s[c, h], sb) + _dot(m_s[c, h].astype(_BF16), vb)
            s_ref[h] = s_ref[h] * eg_s[c, h] + _dot(ktt_s[c, h], vb)
            rhs_s[c, h, :, 0:DV_A] = o

    def finish(c):
        r0 = c * ct
        naw = naw_ref[...]
        for h in heads:
            sl = slice(h * DV_A, (h + 1) * DV_A)
            o = rhs_s[c, h, :, 0:DV_A]
            z = qkvz_ref[r0:r0 + ct, 3 * hd + h * DV_A:3 * hd + (h + 1) * DV_A].astype(_F32)
            on = o * lax.rsqrt(jnp.mean(o * o, axis=-1, keepdims=True) + EPS) * naw
            o_ref[r0:r0 + ct, sl] = (on * (z * _sigmoid(z))).astype(o_ref.dtype)

    for c in range(nc):
        prepare(c)
    for c in range(nc):
        solve(c)
    for c in range(nc):
        recur(c)
    for c in range(nc):
        finish(c)


def _gdn(pb, ps, arow, dtrow, naw, b_, t_, col0):
    ct = GDN_CHUNK
    nc = GDN_STEP_CHUNKS if t_ % (GDN_STEP_CHUNKS * ct) == 0 else 1
    tt = nc * ct
    hd = H_A * DK_A
    nt = t_ // tt
    assert col0 % (4 * hd) == 0
    cb = col0 // (4 * hd)
    const = lambda shape: pl.BlockSpec(shape, lambda b, t: (0, 0))
    return pl.pallas_call(
        _gdn_kernel,
        grid=(b_, nt),
        in_specs=[pl.BlockSpec((tt, 4 * hd), lambda b, t: (b * nt + t, cb)),
                  pl.BlockSpec((tt, LANE), lambda b, t: (b * nt + t, 0)),
                  const((1, LANE)), const((1, LANE)), const((1, DV_A))],
        out_specs=pl.BlockSpec((tt, hd), lambda b, t: (b * nt + t, 0)),
        out_shape=jax.ShapeDtypeStruct((b_ * t_, hd), _BF16),
        scratch_shapes=[
            pltpu.VMEM((H_A, DK_A, DV_A), _F32),
            pltpu.VMEM((nc, H_A, ct, DK_A), _BF16),
            pltpu.VMEM((nc, H_A, ct, DK_A), _BF16),
            pltpu.VMEM((nc, H_A, ct, DK_A), _BF16),
            pltpu.VMEM((nc, H_A, ct, DV_A + DK_A), _F32),
            pltpu.VMEM((nc, H_A, ct, ct), _F32),
            pltpu.VMEM((nc, H_A, ct, DK_A), _BF16),
            pltpu.VMEM((nc, H_A, DK_A, ct), _BF16),
            pltpu.VMEM((nc, H_A, DK_A, DV_A), _F32),
            pltpu.VMEM((nc, H_A, ct, ct), _F32),
            pltpu.VMEM((nc, H_A, ct, ct), _F32),
        ],
        compiler_params=pltpu.CompilerParams(
            dimension_semantics=("arbitrary", "arbitrary"), vmem_limit_bytes=VMEM_LIMIT),
        name="gdn",
    )(pb, ps, arow, dtrow, naw)


def _bias_kernel(tab_ref, bucket_ref, o_ref):
    bucket = bucket_ref[...]
    nq, w = bucket.shape
    for h in range(H_B):
        acc = jnp.zeros((nq, w), _F32)
        for b in range(N_BUCKETS):
            acc = acc + jnp.where(bucket == b, tab_ref[b, h], 0.0)
        far = acc[:, 2 * LANE:3 * LANE]
        o_ref[h] = (acc[:, 0:2 * LANE] - jnp.concatenate([far, far], axis=1)) * LOG2E


def _bias_tiles(rel_table, bucket):
    return pl.pallas_call(
        _bias_kernel,
        in_specs=[pl.BlockSpec(memory_space=pltpu.SMEM),
                  pl.BlockSpec(bucket.shape, lambda: (0, 0))],
        out_specs=pl.BlockSpec((H_B, Q_BLOCK, 2 * LANE), lambda: (0, 0, 0)),
        out_shape=jax.ShapeDtypeStruct((H_B, Q_BLOCK, 2 * LANE), _F32),
        name="rel_bias_tiles",
    )(rel_table, bucket)


def _dsa_kernel(q_ref, k_ref, v_ref, iq_ref, ik_ref, sm_ref, bias_ref, o_ref,
                key_s, qs_s, iwb_s, m_s, l_s, acc_s, s_s, si_s, sf_s, *, topk):
    nq = q_ref.shape[0]
    st_w = KEY_TILE
    assert nq == st_w
    ng = st_w // LANE
    qb = pl.program_id(1)
    n_st = qb + 1
    hd = D_HEAD_B
    scale = D_HEAD_B ** -0.5 * LOG2E
    f_topk = float(topk)

    lane_j = lax.broadcasted_iota(jnp.int32, (nq, LANE), 1)
    row_i = lax.broadcasted_iota(jnp.int32, (nq, LANE), 0)

    def grp(g):
        return slice(g * LANE, (g + 1) * LANE)

    iw = sm_ref[...]
    for h in range(H_B):
        sl = slice(h * hd, (h + 1) * hd)
        qs_s[h] = (q_ref[:, sl].astype(_F32) * scale).astype(_BF16)
        iwb_s[h] = jnp.broadcast_to(iw[:, 2 * H_A + h:2 * H_A + h + 1], (nq, LANE))
        m_s[h] = jnp.full((nq, LANE), MASK_NEG, _F32)
        l_s[h] = jnp.zeros((nq, LANE), _F32)
        acc_s[h] = jnp.zeros((nq, hd), _F32)

    def scores(st, diag):
        ikt = ik_ref[pl.ds(pl.multiple_of(st * st_w, st_w), st_w), :]
        acc = [jnp.zeros((nq, LANE), _F32) for _ in range(ng)]
        for p in range(H_IDX // 2):
            xq = iq_ref[:, p * LANE:(p + 1) * LANE]
            for half in range(2):
                h = 2 * p + half
                s = jnp.maximum(_dot_nt(xq, ikt[:, half * LANE:(half + 1) * LANE]), 0.0)
                w = iwb_s[h]
                for g in range(ng):
                    acc[g] = acc[g] + w * s[:, grp(g)]
        for g in range(ng):
            a = acc[g]
            if diag:
                a = jnp.where((g * LANE + lane_j) // CHUNK <= row_i // CHUNK, a, -jnp.inf)
            bits = pltpu.bitcast(a, jnp.int32)
            key_s[st, g] = bits ^ ((bits >> 31) & INT_MAX)

    def p1(st, carry):
        scores(st, False)
        return carry

    lax.fori_loop(0, qb, p1, 0)
    scores(qb, True)

    ones_b = jnp.ones((LANE, LANE), _BF16)
    assert key_s.shape[0] * ng <= 256

    T_, CAND_, MIDX_ = 0, 1, 2
    ACC_, CNT_ = 0, 1
    rb = 64
    lane_rb = lax.broadcasted_iota(jnp.int32, (rb, LANE), 1)

    def count(pred_fn):
        def tile(st, first):
            for r0 in range(0, nq, rb):
                rs = slice(r0, r0 + rb)
                acc = None if first else sf_s[ACC_, rs, :]
                for g in range(ng):
                    one = jnp.where(pred_fn(key_s[st, g, rs, :], rs), 1.0, 0.0)
                    acc = one if acc is None else acc + one
                sf_s[ACC_, rs, :] = acc

        def body(st, carry):
            tile(st, False)
            return carry

        tile(0, True)
        lax.fori_loop(1, n_st, body, 0)
        return _dot(sf_s[ACC_].astype(_BF16), ones_b)

    si_s[CAND_] = jnp.zeros((nq, LANE), jnp.int32)

    def bit_body(i, carry):
        half = jnp.left_shift(jnp.int32(1), 30 - i)
        c = count(lambda k, rs: k >= si_s[CAND_, rs, :])
        si_s[CAND_] = si_s[CAND_] + jnp.where(c >= f_topk, half, -half)
        return carry

    lax.fori_loop(0, 31, bit_body, 0)
    c_last = count(lambda k, rs: k >= si_s[CAND_, rs, :])
    si_s[T_] = jnp.where(c_last >= f_topk, si_s[CAND_], si_s[CAND_] - 1)
    sf_s[CNT_] = count(lambda k, rs: k >= si_s[T_, rs, :])

    def encode(st, carry):
        for r0 in range(0, nq, rb):
            rs = slice(r0, r0 + rb)
            t = si_s[T_, rs, :]
            for g in range(ng):
                k = key_s[st, g, rs, :]
                pos = st * st_w + g * LANE + lane_rb
                key_s[st, g, rs, :] = jnp.where(k > t, CODE_KEEP, jnp.where(k == t, pos, CODE_DROP))
        return carry

    lax.fori_loop(0, n_st, encode, 0)
    finite_thr = si_s[T_] != KEY_NEG_INF
    si_s[MIDX_] = jnp.where(finite_thr, TIE_BOUND_ALL, 0)

    excess = jnp.max(jnp.where(finite_thr, sf_s[CNT_], 0.0)) > f_topk

    @pl.when(excess)
    def _():
        r_keep = f_topk - count(lambda code, rs: code < 0)
        sf_s[CNT_] = jnp.where(finite_thr, r_keep, 0.0)
        sf_s[ACC_] = jnp.zeros((nq, LANE), _F32)
        upper = (lax.broadcasted_iota(jnp.int32, (LANE, LANE), 0)
                 <= lax.broadcasted_iota(jnp.int32, (LANE, LANE), 1)).astype(_BF16)

        def rank_ties(st, carry):
            for r0 in range(0, nq, rb):
                rs = slice(r0, r0 + rb)
                r_blk = sf_s[CNT_, rs, :]
                before = sf_s[ACC_, rs, :]
                for g in range(ng):
                    code = key_s[st, g, rs, :]
                    tie = jnp.where(code >= 0, jnp.where(code < CODE_DROP, 1.0, 0.0), 0.0)
                    tie_b = tie.astype(_BF16)
                    rank = before + _dot(tie_b, upper)
                    kept_tie = jnp.where(rank <= r_blk, tie, 0.0)
                    mb = jnp.where(code < 0, 0.0, jnp.where(kept_tie > 0.0, 0.0, MASK_NEG))
                    key_s[st, g, rs, :] = pltpu.bitcast(mb, jnp.int32)
                    before = before + _dot(tie_b, ones_b)
                sf_s[ACC_, rs, :] = before
            return carry

        lax.fori_loop(0, n_st, rank_ties, 0)

    @pl.when(jnp.logical_not(excess))
    def _():
        def p2(st, carry):
            for r0 in range(0, nq, rb):
                rs = slice(r0, r0 + rb)
                bound = si_s[MIDX_, rs, :]
                for g in range(ng):
                    mb = jnp.where(key_s[st, g, rs, :] < bound, 0.0, MASK_NEG)
                    key_s[st, g, rs, :] = pltpu.bitcast(mb, jnp.int32)
            return carry

        lax.fori_loop(0, n_st, p2, 0)

    def attend(st, mode):
        off = pl.multiple_of(st * st_w, st_w)
        for h in range(H_B):
            sl = slice(h * hd, (h + 1) * hd)
            mask = jnp.concatenate([pltpu.bitcast(key_s[st, g], _F32) for g in range(ng)], axis=1)
            s = _dot_nt(qs_s[h], k_ref[pl.ds(off, st_w), sl]) + mask
            if mode:
                s_s[...] = s
                b_prev = bias_ref[h, :, 0:LANE]
                if mode == 1:
                    s_s[0:LANE, (ng - 1) * LANE:ng * LANE] += b_prev
                else:
                    b_diag = bias_ref[h, :, LANE:2 * LANE]
                    for rt in range(ng):
                        s_s[grp(rt), grp(rt)] += b_diag
                        if rt:
                            s_s[grp(rt), grp(rt - 1)] += b_prev
                s = s_s[...]
            m_old = m_s[h]
            gmax = s[:, grp(0)]
            for g in range(1, ng):
                gmax = jnp.maximum(gmax, s[:, grp(g)])
            m_new = jnp.maximum(m_old, jnp.max(gmax, axis=1, keepdims=True))
            alpha = jnp.exp2(m_old - m_new)
            pieces = [jnp.exp2(s[:, grp(g)] - m_new) for g in range(ng)]
            psum = pieces[0]
            for g in range(1, ng):
                psum = psum + pieces[g]
            p = jnp.concatenate(pieces, axis=1).astype(_BF16)
            m_s[h] = m_new
            l_s[h] = alpha * l_s[h] + psum
            acc_s[h] = alpha * acc_s[h] + _dot(p, v_ref[pl.ds(off, st_w), sl])

    def p3(st, carry):
        attend(st, 0)
        return carry

    lax.fori_loop(0, qb - 1, p3, 0)

    @pl.when(qb > 0)
    def _():
        attend(qb - 1, 1)

    attend(qb, 2)

    for h in range(H_B):
        sl = slice(h * hd, (h + 1) * hd)
        l_fin = jnp.sum(l_s[h], axis=1, keepdims=True)
        o_ref[:, sl] = (acc_s[h] / l_fin).astype(o_ref.dtype)


def _dsa(pb, ps, bias, b_, t_, col0):
    nq = KEY_TILE
    nb = t_ // nq
    hd = H_B * D_HEAD_B
    topk = min(TOPK_MAX, t_ // 4)
    iq_w = H_IDX * D_IDX
    ik_w = 2 * LANE
    assert col0 % hd == 0 and (col0 + 3 * hd) % iq_w == 0 and (col0 + 3 * hd + iq_w) % ik_w == 0
    cq = col0 // hd
    resident = functools.partial(pl.BlockSpec, pipeline_mode=pl.Buffered(1))
    return pl.pallas_call(
        functools.partial(_dsa_kernel, topk=topk),
        grid=(b_, nb),
        in_specs=[
            pl.BlockSpec((nq, hd), lambda b, i: (b * nb + i, cq)),
            resident((t_, hd), lambda b, i: (b, cq + 1)),
            resident((t_, hd), lambda b, i: (b, cq + 2)),
            pl.BlockSpec((nq, iq_w), lambda b, i: (b * nb + i, (col0 + 3 * hd) // iq_w)),
            resident((t_, ik_w), lambda b, i: (b, (col0 + 3 * hd + iq_w) // ik_w)),
            pl.BlockSpec((nq, LANE), lambda b, i: (b * nb + i, 0)),
            resident((H_B, Q_BLOCK, 2 * LANE), lambda b, i: (0, 0, 0)),
        ],
        out_specs=pl.BlockSpec((nq, hd), lambda b, i: (b * nb + i, 0)),
        out_shape=jax.ShapeDtypeStruct((b_ * t_, hd), _BF16),
        scratch_shapes=[
            pltpu.VMEM((nb, KEY_TILE // LANE, nq, LANE), jnp.int32),
            pltpu.VMEM((H_B, nq, D_HEAD_B), _BF16),
            pltpu.VMEM((H_B, nq, LANE), _F32),
            pltpu.VMEM((H_B, nq, LANE), _F32),
            pltpu.VMEM((H_B, nq, LANE), _F32),
            pltpu.VMEM((H_B, nq, D_HEAD_B), _F32),
            pltpu.VMEM((nq, KEY_TILE), _F32),
            pltpu.VMEM((3, nq, LANE), jnp.int32),
            pltpu.VMEM((2, nq, LANE), _F32),
        ],
        compiler_params=pltpu.CompilerParams(
            dimension_semantics=("arbitrary", "arbitrary"), vmem_limit_bytes=VMEM_LIMIT),
        name="dsa",
    )(pb, pb, pb, pb, pb, ps, bias)


def _merge_kernel(oa_ref, ob_ref, g_ref, x_ref, bg_ref, wa_ref, wb_ref, wo_ref, o_ref):
    d = x_ref.shape[1]
    a = _dot(oa_ref[...], wa_ref[...])
    b = _dot(ob_ref[...], wb_ref[...])
    g = _sigmoid(g_ref[...].astype(_F32) + bg_ref[...])
    merged = g[:, 0:d] * a + g[:, d:2 * d] * b
    o_ref[...] = x_ref[...] + _dot(merged.astype(_BF16), wo_ref[...])


def _merge(oa, ob, pb, x2d, bg, wa, wb, wo, tn, col_g):
    n, d = x2d.shape
    row = lambda w: pl.BlockSpec((tn, w), lambda i: (i, 0))
    const = lambda a: pl.BlockSpec(a.shape, lambda i: (0, 0))
    return pl.pallas_call(
        _merge_kernel,
        grid=(n // tn,),
        in_specs=[row(oa.shape[1]), row(ob.shape[1]),
                  pl.BlockSpec((tn, 2 * d), lambda i: (i, col_g // (2 * d))), row(d),
                  const(bg), const(wa), const(wb), const(wo)],
        out_specs=row(d),
        out_shape=jax.ShapeDtypeStruct((n, d), _F32),
        compiler_params=pltpu.CompilerParams(
            dimension_semantics=("arbitrary",), vmem_limit_bytes=VMEM_LIMIT),
        name="merge",
    )(oa, ob, pb, x2d, bg, wa, wb, wo)


def _mlp_kernel(x_ref, n2_ref, w1_ref, w2_ref, nf_ref, o_ref, *, final_norm):
    x = x_ref[...]
    h2 = _rms(x, n2_ref[...]).astype(_BF16)
    hid = jnp.maximum(_dot(h2, w1_ref[...]), 0.0)
    y = x + _dot((hid * hid).astype(_BF16), w2_ref[...])
    o_ref[...] = _rms(y, nf_ref[...]) if final_norm else y


def _mlp(x1, n2, w1, w2, nf, tn, final_norm):
    n, d = x1.shape
    row = pl.BlockSpec((tn, d), lambda i: (i, 0))
    const = lambda a: pl.BlockSpec(a.shape, lambda i: (0, 0), pipeline_mode=pl.Buffered(1))
    return pl.pallas_call(
        functools.partial(_mlp_kernel, final_norm=final_norm),
        grid=(n // tn,),
        in_specs=[row, const(n2), const(w1), const(w2), const(nf)],
        out_specs=row,
        out_shape=jax.ShapeDtypeStruct((n, d), _F32),
        compiler_params=pltpu.CompilerParams(
            dimension_semantics=("arbitrary",), vmem_limit_bytes=VMEM_LIMIT),
        name="mlp",
    )(x1, n2, w1, w2, nf)


def _t5_bucket(rel):
    half = N_BUCKETS // 2
    max_exact = half // 2
    base = jnp.where(rel > 0, half, 0)
    n = jnp.abs(rel)
    n_f = jnp.maximum(n, 1).astype(jnp.float32)
    large = max_exact + (jnp.log(n_f / max_exact) / math.log(MAX_DISTANCE / max_exact)
                         * (half - max_exact)).astype(jnp.int32)
    large = jnp.minimum(large, half - 1)
    return base + jnp.where(n < max_exact, n, large)


def _pick_tile(n, prefs):
    for t in prefs:
        if n % t == 0:
            return t
    raise ValueError(f"no tile in {prefs} divides {n}")


def kernel(x, norm1_w, w_in, conv_a_w, a_log, dt_bias, norm_a_w, rel_bias_table, w_gate, b_gate,
           w_proj_a, w_proj_b, w_out, norm2_w, w_ff1, w_ff2, norm_final_w):
    b_, t_, d = x.shape
    depth = norm1_w.shape[0]
    n = b_ * t_
    ha, hb = H_A * DK_A, H_B * D_HEAD_B
    assert t_ % KEY_TILE == 0 and t_ % GDN_CHUNK == 0 and d % LANE == 0
    assert DK_A == DV_A == D_HEAD_B == LANE and 2 * D_IDX == LANE
    assert Q_BLOCK >= MAX_DISTANCE

    o_ba = 4 * ha
    o_aa = o_ba + H_A
    o_qb = o_aa + H_A
    o_iq = o_qb + 3 * hb
    o_ik = o_iq + H_IDX * D_IDX
    o_iw = o_ik + D_IDX

    rel = (jnp.arange(2 * Q_BLOCK, dtype=jnp.int32)[None, :] - Q_BLOCK) \
        - jnp.arange(Q_BLOCK, dtype=jnp.int32)[:, None]
    far = jnp.full((Q_BLOCK, LANE), -(Q_BLOCK + 1), jnp.int32)
    bucket = _t5_bucket(jnp.concatenate([rel, far], axis=1))
    bias = _bias_tiles(rel_bias_table.astype(_F32), bucket)

    x2d = x.reshape(n, d)
    tn = _pick_tile(t_, (512, 256, 128))
    for layer in range(depth):
        wi = w_in[layer].astype(_BF16)
        zpad = lambda w: jnp.zeros((d, w), _BF16)
        ws = jnp.concatenate([wi[:, o_ba:o_qb], wi[:, o_iw:o_iw + H_IDX], zpad(LANE - 3 * H_A)], axis=1)
        ik = wi[:, o_ik:o_iw]
        wb16 = jnp.concatenate([wi[:, 0:o_ba], w_gate[layer].astype(_BF16), wi[:, o_qb:o_ik],
                                ik, zpad(D_IDX), zpad(D_IDX), ik], axis=1)
        nw = norm1_w[layer].reshape(1, d).astype(_F32)
        pb, ps = _norm_proj(x2d, nw, wb16, ws, conv_a_w[layer].astype(_F32), tn, t_ // tn, norm_cols=2 * ha)

        col_a = 0
        col_g = 4 * ha
        col_b = col_g + 2 * d
        assert col_g % (2 * d) == 0
        lanes = lambda v, off: jnp.zeros((1, LANE), _F32).at[0, off:off + H_A].set(v.astype(_F32))
        oa = _gdn(pb, ps, lanes(a_log[layer], H_A), lanes(dt_bias[layer], H_A),
                  norm_a_w[layer].reshape(1, DV_A).astype(_F32), b_, t_, col_a)
        ob = _dsa(pb, ps, bias, b_, t_, col_b)

        x2d = _merge(oa, ob, pb, x2d, b_gate[layer].reshape(1, 2 * d).astype(_F32),
                     w_proj_a[layer].astype(_BF16), w_proj_b[layer].astype(_BF16),
                     w_out[layer].astype(_BF16), tn, col_g)
        x2d = _mlp(x2d, norm2_w[layer].reshape(1, d).astype(_F32), w_ff1[layer].astype(_BF16),
                   w_ff2[layer].astype(_BF16), norm_final_w.reshape(1, d).astype(_F32),
                   _pick_tile(n, (512, 256, 128)), final_norm=layer == depth - 1)
    return x2d.reshape(b_, t_, d)
```

```python
import functools
import math

import jax
import jax.numpy as jnp
from jax import lax
from jax.experimental import pallas as pl
from jax.experimental.pallas import tpu as pltpu

EPS = 1e-6
H_A = 8
DK_A = 128
DV_A = 128
CONV_K = 4
GDN_CHUNK = 128
GDN_STEP_CHUNKS = 4
H_B = 8
D_HEAD_B = 128
H_IDX = 8
D_IDX = 64
TOPK_MAX = 256
Q_BLOCK = 128
CHUNK = 64
KEY_TILE = 512
N_BUCKETS = 32
MAX_DISTANCE = 128

LANE = 128
VMEM_LIMIT = 56 * 1024 * 1024

INT_MAX = 2**31 - 1
KEY_NEG_INF = -2139095041
MASK_NEG = -1e30
CODE_KEEP = -1
CODE_DROP = INT_MAX
TIE_BOUND_ALL = 2**30
LOG2E = math.log2(math.e)

_F32 = jnp.float32
_BF16 = jnp.bfloat16


def _dot(a, b):
    return jnp.dot(a, b, preferred_element_type=_F32)


def _dot_nt(a, b):
    return lax.dot_general(a, b, (((1,), (1,)), ((), ())), preferred_element_type=_F32)


def _sigmoid(x):
    return 1.0 / (1.0 + jnp.exp(-x))


def _rms(x, w):
    return x * lax.rsqrt(jnp.mean(x * x, axis=-1, keepdims=True) + EPS) * w


def _norm_proj_kernel(x_ref, nw_ref, w_ref, ws_ref, cw_ref, o_ref, os_ref, halo_s, work_s, *,
                      chunk, conv_cols, norm_cols, tiles_per_seq):
    tn = x_ref.shape[0]
    h = _rms(x_ref[...], nw_ref[...]).astype(_BF16)
    os_ref[...] = _dot(h, ws_ref[...])
    seq_start = pl.program_id(0) % tiles_per_seq == 0

    @pl.when(pl.program_id(0) == 0)
    def _():
        halo_s[...] = jnp.zeros_like(halo_s)

    heavy = [(ci, c0) for ci, c0 in enumerate(range(0, conv_cols, chunk))]
    plain = [(None, c0) for c0 in range(conv_cols, w_ref.shape[1], chunk)]
    per_heavy = -(-len(plain) // max(1, len(heavy)))
    order = []
    for i, hv in enumerate(heavy):
        order.append(hv)
        order.extend(plain[i * per_heavy:(i + 1) * per_heavy])
    order.extend(plain[len(heavy) * per_heavy:])
    for ci, c0 in order:
        y = _dot(h, w_ref[:, c0:c0 + chunk])
        if c0 < conv_cols:
            work_s[0:8, :] = jnp.where(seq_start, 0.0, halo_s[ci])
            work_s[8:8 + tn, :] = y
            halo_s[ci] = y[tn - 8:tn, :]
            acc = cw_ref[CONV_K - 1:CONV_K, c0:c0 + chunk] * y
            for j in range(CONV_K - 1):
                lo = 8 - (CONV_K - 1) + j
                acc = acc + cw_ref[j:j + 1, c0:c0 + chunk] * work_s[lo:lo + tn, :]
            y = acc * _sigmoid(acc)
            if c0 < norm_cols:
                parts = []
                for g0 in range(0, chunk, DK_A):
                    yh = y[:, g0:g0 + DK_A]
                    if c0 + g0 < norm_cols:
                        inv = lax.rsqrt(jnp.sum(yh * yh, axis=-1, keepdims=True) + EPS)
                        if c0 + g0 < norm_cols // 2:
                            inv = inv * (DK_A ** -0.5)
                        yh = yh * inv
                    parts.append(yh)
                y = jnp.concatenate(parts, axis=1)
        o_ref[:, c0:c0 + chunk] = y.astype(o_ref.dtype)


def _norm_proj(x2d, nw, w, ws, cw, tn, tiles_per_seq, norm_cols):
    n, d = x2d.shape
    c = w.shape[1]
    chunk = _pick_tile(c, (768, 512, 256, 128))
    conv_cols = cw.shape[1]
    assert conv_cols % chunk == 0 and tn >= 8
    assert norm_cols <= conv_cols and chunk % DK_A == 0 and (norm_cols // 2) % DK_A == 0
    const = functools.partial(pl.BlockSpec, pipeline_mode=pl.Buffered(1))
    return pl.pallas_call(
        functools.partial(_norm_proj_kernel, chunk=chunk, conv_cols=conv_cols, norm_cols=norm_cols,
                          tiles_per_seq=tiles_per_seq),
        grid=(n // tn,),
        in_specs=[
            pl.BlockSpec((tn, d), lambda i: (i, 0)),
            const((1, d), lambda i: (0, 0)),
            const((d, c), lambda i: (0, 0)),
            const((d, LANE), lambda i: (0, 0)),
            const(cw.shape, lambda i: (0, 0)),
        ],
        out_specs=[pl.BlockSpec((tn, c), lambda i: (i, 0)), pl.BlockSpec((tn, LANE), lambda i: (i, 0))],
        out_shape=[jax.ShapeDtypeStruct((n, c), _BF16), jax.ShapeDtypeStruct((n, LANE), _F32)],
        scratch_shapes=[
            pltpu.VMEM((conv_cols // chunk, 8, chunk), _F32),
            pltpu.VMEM((8 + tn, chunk), _F32),
        ],
        compiler_params=pltpu.CompilerParams(
            dimension_semantics=("arbitrary",), vmem_limit_bytes=VMEM_LIMIT),
        name="proj",
    )(x2d, nw, w, ws, cw)


def _gdn_kernel(qkvz_ref, sm_ref, arow_ref, dtrow_ref, naw_ref, o_ref,
                s_ref, kn_s, kb_s, qn_s, rhs_s, dec_s, qdec_s, ktt_s, eg_s, m_s, x_s):
    tt = qkvz_ref.shape[0]
    ct = GDN_CHUNK
    nc = tt // ct
    hd = H_A * DK_A

    @pl.when(pl.program_id(1) == 0)
    def _():
        s_ref[...] = jnp.zeros_like(s_ref)

    row = lax.broadcasted_iota(jnp.int32, (ct, ct), 0)
    col = lax.broadcasted_iota(jnp.int32, (ct, ct), 1)
    tri = (col <= row).astype(_F32)
    strict = col < row
    eye = (col == row).astype(_F32)
    n_sq = int(math.log2(ct))
    heads = range(H_A)

    def prepare(c):
        r0 = c * ct
        sm = sm_ref[r0:r0 + ct, :]
        beta_full = _sigmoid(sm)
        xg = sm + dtrow_ref[...]
        softplus = jnp.maximum(xg, 0.0) + jnp.log(1.0 + jnp.exp(-jnp.abs(xg)))
        g_full = -jnp.exp(arow_ref[...]) * softplus
        gcum = jnp.dot(tri, g_full, preferred_element_type=_F32, precision=lax.Precision.HIGHEST)
        gcum_t = gcum.T
        for h in heads:
            sl = slice(h * DK_A, (h + 1) * DK_A)
            qh, kh, vh = (qkvz_ref[r0:r0 + ct, idx * hd + h * DK_A:idx * hd + (h + 1) * DK_A].astype(_F32)
                          for idx in range(3))
            qn, kn = qh, kh
            bcol = beta_full[:, h:h + 1]
            gcol = gcum[:, H_A + h:H_A + h + 1]
            grow = gcum_t[H_A + h:H_A + h + 1, :]
            glast = gcum[ct - 1:ct, H_A + h:H_A + h + 1]
            kb = kn * bcol
            eg = jnp.exp(gcol)
            rhs_s[c, h, :, 0:DV_A] = vh * bcol
            rhs_s[c, h, :, DV_A:DV_A + DK_A] = kb * eg
            dec_s[c, h] = jnp.where(strict, jnp.exp(jnp.where(strict, gcol - grow, 0.0)), 0.0)
            qdec_s[c, h] = (qn * eg).astype(_BF16)
            ktt_s[c, h] = (kn * jnp.exp(glast - gcol)).T.astype(_BF16)
            kn_s[c, h] = kn.astype(_BF16)
            kb_s[c, h] = kb.astype(_BF16)
            qn_s[c, h] = qn.astype(_BF16)
            eg_s[c, h] = jnp.broadcast_to(jnp.exp(glast), (DK_A, DV_A))

    def solve(c):
        for h in heads:
            m = -(_dot_nt(kb_s[c, h], kn_s[c, h]) * dec_s[c, h])
            m_s[c, h] = m
            x_s[c, h] = m
        for _ in range(n_sq - 1):
            for h in heads:
                mb = m_s[c, h].astype(_BF16)
                m = _dot(mb, mb)
                m_s[c, h] = m
                x_s[c, h] = x_s[c, h] + m + _dot(x_s[c, h].astype(_BF16), m.astype(_BF16))
        for h in heads:
            rhs = rhs_s[c, h]
            rhs_s[c, h] = rhs + _dot(x_s[c, h].astype(_BF16), rhs.astype(_BF16))
        for h in heads:
            m_s[c, h] = _dot_nt(qn_s[c, h], kn_s[c, h]) * (dec_s[c, h] + eye)

    def recur(c):
        for h in heads:
            sb = s_ref[h].astype(_BF16)
            vnew = rhs_s[c, h, :, 0:DV_A] - _dot(rhs_s[c, h, :, DV_A:DV_A + DK_A].astype(_BF16), sb)
            vb = vnew.astype(_BF16)
            o = _dot(qdec_s[c, h], sb) + _dot(m_s[c, h].astype(_BF16), vb)
            s_ref[h] = s_ref[h] * eg_s[c, h] + _dot(ktt_s[c, h], vb)
            rhs_s[c, h, :, 0:DV_A] = o

    def finish(c):
        r0 = c * ct
        naw = naw_ref[...]
        for h in heads:
            sl = slice(h * DV_A, (h + 1) * DV_A)
            o = rhs_s[c, h, :, 0:DV_A]
            z = qkvz_ref[r0:r0 + ct, 3 * hd + h * DV_A:3 * hd + (h + 1) * DV_A].astype(_F32)
            on = o * lax.rsqrt(jnp.mean(o * o, axis=-1, keepdims=True) + EPS) * naw
            o_ref[r0:r0 + ct, sl] = (on * (z * _sigmoid(z))).astype(o_ref.dtype)

    for c in range(nc):
        prepare(c)
    for c in range(nc):
        solve(c)
    for c in range(nc):
        recur(c)
    for c in range(nc):
        finish(c)


def _gdn(pb, ps, arow, dtrow, naw, b_, t_, col0):
    ct = GDN_CHUNK
    nc = GDN_STEP_CHUNKS if t_ % (GDN_STEP_CHUNKS * ct) == 0 else 1
    tt = nc * ct
    hd = H_A * DK_A
    nt = t_ // tt
    assert col0 % (4 * hd) == 0
    cb = col0 // (4 * hd)
    const = lambda shape: pl.BlockSpec(shape, lambda b, t: (0, 0))
    return pl.pallas_call(
        _gdn_kernel,
        grid=(b_, nt),
        in_specs=[pl.BlockSpec((tt, 4 * hd), lambda b, t: (b * nt + t, cb)),
                  pl.BlockSpec((tt, LANE), lambda b, t: (b * nt + t, 0)),
                  const((1, LANE)), const((1, LANE)), const((1, DV_A))],
        out_specs=pl.BlockSpec((tt, hd), lambda b, t: (b * nt + t, 0)),
        out_shape=jax.ShapeDtypeStruct((b_ * t_, hd), _BF16),
        scratch_shapes=[
            pltpu.VMEM((H_A, DK_A, DV_A), _F32),
            pltpu.VMEM((nc, H_A, ct, DK_A), _BF16),
            pltpu.VMEM((nc, H_A, ct, DK_A), _BF16),
            pltpu.VMEM((nc, H_A, ct, DK_A), _BF16),
            pltpu.VMEM((nc, H_A, ct, DV_A + DK_A), _F32),
            pltpu.VMEM((nc, H_A, ct, ct), _F32),
            pltpu.VMEM((nc, H_A, ct, DK_A), _BF16),
            pltpu.VMEM((nc, H_A, DK_A, ct), _BF16),
            pltpu.VMEM((nc, H_A, DK_A, DV_A), _F32),
            pltpu.VMEM((nc, H_A, ct, ct), _F32),
            pltpu.VMEM((nc, H_A, ct, ct), _F32),
        ],
        compiler_params=pltpu.CompilerParams(
            dimension_semantics=("arbitrary", "arbitrary"), vmem_limit_bytes=VMEM_LIMIT),
        name="gdn",
    )(pb, ps, arow, dtrow, naw)


def _bias_kernel(tab_ref, bucket_ref, o_ref):
    bucket = bucket_ref[...]
    nq, w = bucket.shape
    for h in range(H_B):
        acc = jnp.zeros((nq, w), _F32)
        for b in range(N_BUCKETS):
            acc = acc + jnp.where(bucket == b, tab_ref[b, h], 0.0)
        far = acc[:, 2 * LANE:3 * LANE]
        o_ref[h] = (acc[:, 0:2 * LANE] - jnp.concatenate([far, far], axis=1)) * LOG2E


def _bias_tiles(rel_table, bucket):
    return pl.pallas_call(
        _bias_kernel,
        in_specs=[pl.BlockSpec(memory_space=pltpu.SMEM),
                  pl.BlockSpec(bucket.shape, lambda: (0, 0))],
        out_specs=pl.BlockSpec((H_B, Q_BLOCK, 2 * LANE), lambda: (0, 0, 0)),
        out_shape=jax.ShapeDtypeStruct((H_B, Q_BLOCK, 2 * LANE), _F32),
        name="rel_bias_tiles",
    )(rel_table, bucket)


def _dsa_kernel(q_ref, k_ref, v_ref, iq_ref, ik_ref, sm_ref, bias_ref, o_ref,
                key_s, qs_s, iwb_s, m_s, l_s, acc_s, s_s, si_s, sf_s, *, topk):
    nq = q_ref.shape[0]
    st_w = KEY_TILE
    assert nq == st_w
    ng = st_w // LANE
    qb = pl.program_id(1)
    n_st = qb + 1
    hd = D_HEAD_B
    scale = D_HEAD_B ** -0.5 * LOG2E
    f_topk = float(topk)

    lane_j = lax.broadcasted_iota(jnp.int32, (nq, LANE), 1)
    row_i = lax.broadcasted_iota(jnp.int32, (nq, LANE), 0)

    def grp(g):
        return slice(g * LANE, (g + 1) * LANE)

    iw = sm_ref[...]
    for h in range(H_B):
        sl = slice(h * hd, (h + 1) * hd)
        qs_s[h] = (q_ref[:, sl].astype(_F32) * scale).astype(_BF16)
        iwb_s[h] = jnp.broadcast_to(iw[:, 2 * H_A + h:2 * H_A + h + 1], (nq, LANE))
        m_s[h] = jnp.full((nq, LANE), MASK_NEG, _F32)
        l_s[h] = jnp.zeros((nq, LANE), _F32)
        acc_s[h] = jnp.zeros((nq, hd), _F32)

    def scores(st, diag):
        ikt = ik_ref[pl.ds(pl.multiple_of(st * st_w, st_w), st_w), :]
        acc = [jnp.zeros((nq, LANE), _F32) for _ in range(ng)]
        for p in range(H_IDX // 2):
            xq = iq_ref[:, p * LANE:(p + 1) * LANE]
            for half in range(2):
                h = 2 * p + half
                s = jnp.maximum(_dot_nt(xq, ikt[:, half * LANE:(half + 1) * LANE]), 0.0)
                w = iwb_s[h]
                for g in range(ng):
                    acc[g] = acc[g] + w * s[:, grp(g)]
        for g in range(ng):
            a = acc[g]
            if diag:
                a = jnp.where((g * LANE + lane_j) // CHUNK <= row_i // CHUNK, a, -jnp.inf)
            bits = pltpu.bitcast(a, jnp.int32)
            key_s[st, g] = bits ^ ((bits >> 31) & INT_MAX)

    def p1(st, carry):
        scores(st, False)
        return carry

    lax.fori_loop(0, qb, p1, 0)
    scores(qb, True)

    ones_b = jnp.ones((LANE, LANE), _BF16)
    assert key_s.shape[0] * ng <= 256

    T_, CAND_, MIDX_ = 0, 1, 2
    ACC_, CNT_ = 0, 1
    rb = 64
    lane_rb = lax.broadcasted_iota(jnp.int32, (rb, LANE), 1)

    def count(pred_fn):
        def tile(st, first):
            for r0 in range(0, nq, rb):
                rs = slice(r0, r0 + rb)
                acc = None if first else sf_s[ACC_, rs, :]
                for g in range(ng):
                    one = jnp.where(pred_fn(key_s[st, g, rs, :], rs), 1.0, 0.0)
                    acc = one if acc is None else acc + one
                sf_s[ACC_, rs, :] = acc

        def body(st, carry):
            tile(st, False)
            return carry

        tile(0, True)
        lax.fori_loop(1, n_st, body, 0)
        return _dot(sf_s[ACC_].astype(_BF16), ones_b)

    si_s[CAND_] = jnp.zeros((nq, LANE), jnp.int32)

    def bit_body(i, carry):
        half = jnp.left_shift(jnp.int32(1), 30 - i)
        c = count(lambda k, rs: k >= si_s[CAND_, rs, :])
        si_s[CAND_] = si_s[CAND_] + jnp.where(c >= f_topk, half, -half)
        return carry

    lax.fori_loop(0, 31, bit_body, 0)
    c_last = count(lambda k, rs: k >= si_s[CAND_, rs, :])
    si_s[T_] = jnp.where(c_last >= f_topk, si_s[CAND_], si_s[CAND_] - 1)
    sf_s[CNT_] = count(lambda k, rs: k >= si_s[T_, rs, :])

    def encode(st, carry):
        for r0 in range(0, nq, rb):
            rs = slice(r0, r0 + rb)
            t = si_s[T_, rs, :]
            for g in range(ng):
                k = key_s[st, g, rs, :]
                pos = st * st_w + g * LANE + lane_rb
                key_s[st, g, rs, :] = jnp.where(k > t, CODE_KEEP, jnp.where(k == t, pos, CODE_DROP))
        return carry

    lax.fori_loop(0, n_st, encode, 0)
    finite_thr = si_s[T_] != KEY_NEG_INF
    si_s[MIDX_] = jnp.where(finite_thr, TIE_BOUND_ALL, 0)

    excess = jnp.max(jnp.where(finite_thr, sf_s[CNT_], 0.0)) > f_topk

    @pl.when(excess)
    def _():
        r_keep = f_topk - count(lambda code, rs: code < 0)
        sf_s[CNT_] = jnp.where(finite_thr, r_keep, 0.0)
        sf_s[ACC_] = jnp.zeros((nq, LANE), _F32)
        upper = (lax.broadcasted_iota(jnp.int32, (LANE, LANE), 0)
                 <= lax.broadcasted_iota(jnp.int32, (LANE, LANE), 1)).astype(_BF16)

        def rank_ties(st, carry):
            for r0 in range(0, nq, rb):
                rs = slice(r0, r0 + rb)
                r_blk = sf_s[CNT_, rs, :]
                before = sf_s[ACC_, rs, :]
                for g in range(ng):
                    code = key_s[st, g, rs, :]
                    tie = jnp.where(code >= 0, jnp.where(code < CODE_DROP, 1.0, 0.0), 0.0)
                    tie_b = tie.astype(_BF16)
                    rank = before + _dot(tie_b, upper)
                    kept_tie = jnp.where(rank <= r_blk, tie, 0.0)
                    mb = jnp.where(code < 0, 0.0, jnp.where(kept_tie > 0.0, 0.0, MASK_NEG))
                    key_s[st, g, rs, :] = pltpu.bitcast(mb, jnp.int32)
                    before = before + _dot(tie_b, ones_b)
                sf_s[ACC_, rs, :] = before
            return carry

        lax.fori_loop(0, n_st, rank_ties, 0)

    @pl.when(jnp.logical_not(excess))
    def _():
        def p2(st, carry):
            for r0 in range(0, nq, rb):
                rs = slice(r0, r0 + rb)
                bound = si_s[MIDX_, rs, :]
                for g in range(ng):
                    mb = jnp.where(key_s[st, g, rs, :] < bound, 0.0, MASK_NEG)
                    key_s[st, g, rs, :] = pltpu.bitcast(mb, jnp.int32)
            return carry

        lax.fori_loop(0, n_st, p2, 0)

    def attend(st, mode):
        off = pl.multiple_of(st * st_w, st_w)
        for h in range(H_B):
            sl = slice(h * hd, (h + 1) * hd)
            mask = jnp.concatenate([pltpu.bitcast(key_s[st, g], _F32) for g in range(ng)], axis=1)
            s = _dot_nt(qs_s[h], k_ref[pl.ds(off, st_w), sl]) + mask
            if mode:
                s_s[...] = s
                b_prev = bias_ref[h, :, 0:LANE]
                if mode == 1:
                    s_s[0:LANE, (ng - 1) * LANE:ng * LANE] += b_prev
                else:
                    b_diag = bias_ref[h, :, LANE:2 * LANE]
                    for rt in range(ng):
                        s_s[grp(rt), grp(rt)] += b_diag
                        if rt:
                            s_s[grp(rt), grp(rt - 1)] += b_prev
                s = s_s[...]
            m_old = m_s[h]
            gmax = s[:, grp(0)]
            for g in range(1, ng):
                gmax = jnp.maximum(gmax, s[:, grp(g)])
            m_new = jnp.maximum(m_old, jnp.max(gmax, axis=1, keepdims=True))
            alpha = jnp.exp2(m_old - m_new)
            pieces = [jnp.exp2(s[:, grp(g)] - m_new) for g in range(ng)]
            psum = pieces[0]
            for g in range(1, ng):
                psum = psum + pieces[g]
            p = jnp.concatenate(pieces, axis=1).astype(_BF16)
            m_s[h] = m_new
            l_s[h] = alpha * l_s[h] + psum
            acc_s[h] = alpha * acc_s[h] + _dot(p, v_ref[pl.ds(off, st_w), sl])

    def p3(st, carry):
        attend(st, 0)
        return carry

    lax.fori_loop(0, qb - 1, p3, 0)

    @pl.when(qb > 0)
    def _():
        attend(qb - 1, 1)

    attend(qb, 2)

    for h in range(H_B):
        sl = slice(h * hd, (h + 1) * hd)
        l_fin = jnp.sum(l_s[h], axis=1, keepdims=True)
        o_ref[:, sl] = (acc_s[h] / l_fin).astype(o_ref.dtype)


def _dsa(pb, ps, bias, b_, t_, col0):
    nq = KEY_TILE
    nb = t_ // nq
    hd = H_B * D_HEAD_B
    topk = min(TOPK_MAX, t_ // 4)
    iq_w = H_IDX * D_IDX
    ik_w = 2 * LANE
    assert col0 % hd == 0 and (col0 + 3 * hd) % iq_w == 0 and (col0 + 3 * hd + iq_w) % ik_w == 0
    cq = col0 // hd
    resident = functools.partial(pl.BlockSpec, pipeline_mode=pl.Buffered(1))
    return pl.pallas_call(
        functools.partial(_dsa_kernel, topk=topk),
        grid=(b_, nb),
        in_specs=[
            pl.BlockSpec((nq, hd), lambda b, i: (b * nb + i, cq)),
            resident((t_, hd), lambda b, i: (b, cq + 1)),
            resident((t_, hd), lambda b, i: (b, cq + 2)),
            pl.BlockSpec((nq, iq_w), lambda b, i: (b * nb + i, (col0 + 3 * hd) // iq_w)),
            resident((t_, ik_w), lambda b, i: (b, (col0 + 3 * hd + iq_w) // ik_w)),
            pl.BlockSpec((nq, LANE), lambda b, i: (b * nb + i, 0)),
            resident((H_B, Q_BLOCK, 2 * LANE), lambda b, i: (0, 0, 0)),
        ],
        out_specs=pl.BlockSpec((nq, hd), lambda b, i: (b * nb + i, 0)),
        out_shape=jax.ShapeDtypeStruct((b_ * t_, hd), _BF16),
        scratch_shapes=[
            pltpu.VMEM((nb, KEY_TILE // LANE, nq, LANE), jnp.int32),
            pltpu.VMEM((H_B, nq, D_HEAD_B), _BF16),
            pltpu.VMEM((H_B, nq, LANE), _F32),
            pltpu.VMEM((H_B, nq, LANE), _F32),
            pltpu.VMEM((H_B, nq, LANE), _F32),
            pltpu.VMEM((H_B, nq, D_HEAD_B), _F32),
            pltpu.VMEM((nq, KEY_TILE), _F32),
            pltpu.VMEM((3, nq, LANE), jnp.int32),
            pltpu.VMEM((2, nq, LANE), _F32),
        ],
        compiler_params=pltpu.CompilerParams(
            dimension_semantics=("arbitrary", "arbitrary"), vmem_limit_bytes=VMEM_LIMIT),
        name="dsa",
    )(pb, pb, pb, pb, pb, ps, bias)


def _merge_kernel(oa_ref, ob_ref, g_ref, x_ref, bg_ref, wa_ref, wb_ref, wo_ref, o_ref):
    d = x_ref.shape[1]
    a = _dot(oa_ref[...], wa_ref[...])
    b = _dot(ob_ref[...], wb_ref[...])
    g = _sigmoid(g_ref[...].astype(_F32) + bg_ref[...])
    merged = g[:, 0:d] * a + g[:, d:2 * d] * b
    o_ref[...] = x_ref[...] + _dot(merged.astype(_BF16), wo_ref[...])


def _merge(oa, ob, pb, x2d, bg, wa, wb, wo, tn, col_g):
    n, d = x2d.shape
    row = lambda w: pl.BlockSpec((tn, w), lambda i: (i, 0))
    const = lambda a: pl.BlockSpec(a.shape, lambda i: (0, 0))
    return pl.pallas_call(
        _merge_kernel,
        grid=(n // tn,),
        in_specs=[row(oa.shape[1]), row(ob.shape[1]),
                  pl.BlockSpec((tn, 2 * d), lambda i: (i, col_g // (2 * d))), row(d),
                  const(bg), const(wa), const(wb), const(wo)],
        out_specs=row(d),
        out_shape=jax.ShapeDtypeStruct((n, d), _F32),
        compiler_params=pltpu.CompilerParams(
            dimension_semantics=("arbitrary",), vmem_limit_bytes=VMEM_LIMIT),
        name="merge",
    )(oa, ob, pb, x2d, bg, wa, wb, wo)


def _mlp_kernel(x_ref, n2_ref, w1_ref, w2_ref, nf_ref, o_ref, *, final_norm):
    x = x_ref[...]
    h2 = _rms(x, n2_ref[...]).astype(_BF16)
    hid = jnp.maximum(_dot(h2, w1_ref[...]), 0.0)
    y = x + _dot((hid * hid).astype(_BF16), w2_ref[...])
    o_ref[...] = _rms(y, nf_ref[...]) if final_norm else y


def _mlp(x1, n2, w1, w2, nf, tn, final_norm):
    n, d = x1.shape
    row = pl.BlockSpec((tn, d), lambda i: (i, 0))
    const = lambda a: pl.BlockSpec(a.shape, lambda i: (0, 0), pipeline_mode=pl.Buffered(1))
    return pl.pallas_call(
        functools.partial(_mlp_kernel, final_norm=final_norm),
        grid=(n // tn,),
        in_specs=[row, const(n2), const(w1), const(w2), const(nf)],
        out_specs=row,
        out_shape=jax.ShapeDtypeStruct((n, d), _F32),
        compiler_params=pltpu.CompilerParams(
            dimension_semantics=("arbitrary",), vmem_limit_bytes=VMEM_LIMIT),
        name="mlp",
    )(x1, n2, w1, w2, nf)


def _t5_bucket(rel):
    half = N_BUCKETS // 2
    max_exact = half // 2
    base = jnp.where(rel > 0, half, 0)
    n = jnp.abs(rel)
    n_f = jnp.maximum(n, 1).astype(jnp.float32)
    large = max_exact + (jnp.log(n_f / max_exact) / math.log(MAX_DISTANCE / max_exact)
                         * (half - max_exact)).astype(jnp.int32)
    large = jnp.minimum(large, half - 1)
    return base + jnp.where(n < max_exact, n, large)


def _pick_tile(n, prefs):
    for t in prefs:
        if n % t == 0:
            return t
    raise ValueError(f"no tile in {prefs} divides {n}")


def kernel(x, norm1_w, w_in, conv_a_w, a_log, dt_bias, norm_a_w, rel_bias_table, w_gate, b_gate,
           w_proj_a, w_proj_b, w_out, norm2_w, w_ff1, w_ff2, norm_final_w):
    b_, t_, d = x.shape
    depth = norm1_w.shape[0]
    n = b_ * t_
    ha, hb = H_A * DK_A, H_B * D_HEAD_B
    assert t_ % KEY_TILE == 0 and t_ % GDN_CHUNK == 0 and d % LANE == 0
    assert DK_A == DV_A == D_HEAD_B == LANE and 2 * D_IDX == LANE
    assert Q_BLOCK >= MAX_DISTANCE

    o_ba = 4 * ha
    o_aa = o_ba + H_A
    o_qb = o_aa + H_A
    o_iq = o_qb + 3 * hb
    o_ik = o_iq + H_IDX * D_IDX
    o_iw = o_ik + D_IDX

    rel = (jnp.arange(2 * Q_BLOCK, dtype=jnp.int32)[None, :] - Q_BLOCK) \
        - jnp.arange(Q_BLOCK, dtype=jnp.int32)[:, None]
    far = jnp.full((Q_BLOCK, LANE), -(Q_BLOCK + 1), jnp.int32)
    bucket = _t5_bucket(jnp.concatenate([rel, far], axis=1))
    bias = _bias_tiles(rel_bias_table.astype(_F32), bucket)

    x2d = x.reshape(n, d)
    tn = _pick_tile(t_, (512, 256, 128))
    for layer in range(depth):
        wi = w_in[layer].astype(_BF16)
        zpad = lambda w: jnp.zeros((d, w), _BF16)
        ws = jnp.concatenate([wi[:, o_ba:o_qb], wi[:, o_iw:o_iw + H_IDX], zpad(LANE - 3 * H_A)], axis=1)
        ik = wi[:, o_ik:o_iw]
        wb16 = jnp.concatenate([wi[:, 0:o_ba], w_gate[layer].astype(_BF16), wi[:, o_qb:o_ik],
                                ik, zpad(D_IDX), zpad(D_IDX), ik], axis=1)
        nw = norm1_w[layer].reshape(1, d).astype(_F32)
        pb, ps = _norm_proj(x2d, nw, wb16, ws, conv_a_w[layer].astype(_F32), tn, t_ // tn, norm_cols=2 * ha)

        col_a = 0
        col_g = 4 * ha
        col_b = col_g + 2 * d
        assert col_g % (2 * d) == 0
        lanes = lambda v, off: jnp.zeros((1, LANE), _F32).at[0, off:off + H_A].set(v.astype(_F32))
        oa = _gdn(pb, ps, lanes(a_log[layer], H_A), lanes(dt_bias[layer], H_A),
                  norm_a_w[layer].reshape(1, DV_A).astype(_F32), b_, t_, col_a)
        ob = _dsa(pb, ps, bias, b_, t_, col_b)

        x2d = _merge(oa, ob, pb, x2d, b_gate[layer].reshape(1, 2 * d).astype(_F32),
                     w_proj_a[layer].astype(_BF16), w_proj_b[layer].astype(_BF16),
                     w_out[layer].astype(_BF16), tn, col_g)
        x2d = _mlp(x2d, norm2_w[layer].reshape(1, d).astype(_F32), w_ff1[layer].astype(_BF16),
                   w_ff2[layer].astype(_BF16), norm_final_w.reshape(1, d).astype(_F32),
                   _pick_tile(n, (512, 256, 128)), final_norm=layer == depth - 1)
    return x2d.reshape(b_, t_, d)
```

```python
import functools
import math

import jax
import jax.numpy as jnp
from jax import lax
from jax.experimental import pallas as pl
from jax.experimental.pallas import tpu as pltpu

EPS = 1e-6
H_A = 8
DK_A = 128
DV_A = 128
CONV_K = 4
GDN_CHUNK = 128
GDN_STEP_CHUNKS = 4
H_B = 8
D_HEAD_B = 128
H_IDX = 8
D_IDX = 64
TOPK_MAX = 256
Q_BLOCK = 128
CHUNK = 64
KEY_TILE = 512
N_BUCKETS = 32
MAX_DISTANCE = 128

LANE = 128
VMEM_LIMIT = 56 * 1024 * 1024

INT_MAX = 2**31 - 1
KEY_NEG_INF = -2139095041
MASK_NEG = -1e30
CODE_KEEP = -1
CODE_DROP = INT_MAX
TIE_BOUND_ALL = 2**30
LOG2E = math.log2(math.e)

_F32 = jnp.float32
_BF16 = jnp.bfloat16


def _dot(a, b):
    return jnp.dot(a, b, preferred_element_type=_F32)


def _dot_nt(a, b):
    return lax.dot_general(a, b, (((1,), (1,)), ((), ())), preferred_element_type=_F32)


def _sigmoid(x):
    return 1.0 / (1.0 + jnp.exp(-x))


def _rms(x, w):
    return x * lax.rsqrt(jnp.mean(x * x, axis=-1, keepdims=True) + EPS) * w


def _norm_proj_kernel(x_ref, nw_ref, w_ref, ws_ref, cw_ref, o_ref, os_ref, halo_s, work_s, *,
                      chunk, conv_cols, norm_cols, tiles_per_seq):
    tn = x_ref.shape[0]

    @pl.when(pl.program_id(0) == 0)
    def _():
        halo_s[...] = jnp.zeros_like(halo_s)

    h = _rms(x_ref[...], nw_ref[...]).astype(_BF16)
    os_ref[...] = _dot(h, ws_ref[...])
    seq_start = pl.program_id(0) % tiles_per_seq == 0
    heavy = [(ci, c0) for ci, c0 in enumerate(range(0, conv_cols, chunk))]
    plain = [(None, c0) for c0 in range(conv_cols, w_ref.shape[1], chunk)]
    per_heavy = -(-len(plain) // max(1, len(heavy)))
    order = []
    for i, hv in enumerate(heavy):
        order.append(hv)
        order.extend(plain[i * per_heavy:(i + 1) * per_heavy])
    order.extend(plain[len(heavy) * per_heavy:])
    for ci, c0 in order:
        y = _dot(h, w_ref[:, c0:c0 + chunk])
        if c0 < conv_cols:
            work_s[0:8, :] = jnp.where(seq_start, 0.0, halo_s[ci])
            work_s[8:8 + tn, :] = y
            halo_s[ci] = y[tn - 8:tn, :]
            acc = cw_ref[CONV_K - 1:CONV_K, c0:c0 + chunk] * y
            for j in range(CONV_K - 1):
                lo = 8 - (CONV_K - 1) + j
                acc = acc + cw_ref[j:j + 1, c0:c0 + chunk] * work_s[lo:lo + tn, :]
            y = acc * _sigmoid(acc)
            if c0 < norm_cols:
                parts = []
                for g0 in range(0, chunk, DK_A):
                    yh = y[:, g0:g0 + DK_A]
                    if c0 + g0 < norm_cols:
                        inv = lax.rsqrt(jnp.sum(yh * yh, axis=-1, keepdims=True) + EPS)
                        if c0 + g0 < norm_cols // 2:
                            inv = inv * (DK_A ** -0.5)
                        yh = yh * inv
                    parts.append(yh)
                y = jnp.concatenate(parts, axis=1)
        o_ref[:, c0:c0 + chunk] = y.astype(o_ref.dtype)


def _norm_proj(x2d, nw, w, ws, cw, tn, tiles_per_seq, norm_cols):
    n, d = x2d.shape
    c = w.shape[1]
    chunk = _pick_tile(c, (768, 512, 256, 128))
    conv_cols = cw.shape[1]
    assert conv_cols % chunk == 0 and tn >= 8
    assert norm_cols <= conv_cols and chunk % DK_A == 0 and (norm_cols // 2) % DK_A == 0
    const = functools.partial(pl.BlockSpec, pipeline_mode=pl.Buffered(1))
    return pl.pallas_call(
        functools.partial(_norm_proj_kernel, chunk=chunk, conv_cols=conv_cols, norm_cols=norm_cols,
                          tiles_per_seq=tiles_per_seq),
        grid=(n // tn,),
        in_specs=[
            pl.BlockSpec((tn, d), lambda i: (i, 0)),
            const((1, d), lambda i: (0, 0)),
            const((d, c), lambda i: (0, 0)),
            const((d, LANE), lambda i: (0, 0)),
            const(cw.shape, lambda i: (0, 0)),
        ],
        out_specs=[pl.BlockSpec((tn, c), lambda i: (i, 0)), pl.BlockSpec((tn, LANE), lambda i: (i, 0))],
        out_shape=[jax.ShapeDtypeStruct((n, c), _BF16), jax.ShapeDtypeStruct((n, LANE), _F32)],
        scratch_shapes=[
            pltpu.VMEM((conv_cols // chunk, 8, chunk), _F32),
            pltpu.VMEM((8 + tn, chunk), _F32),
        ],
        compiler_params=pltpu.CompilerParams(
            dimension_semantics=("arbitrary",), vmem_limit_bytes=VMEM_LIMIT),
        name="proj",
    )(x2d, nw, w, ws, cw)


def _gdn_kernel(qkvz_ref, sm_ref, arow_ref, dtrow_ref, naw_ref, o_ref,
                s_ref, kn_s, kb_s, qn_s, rhs_s, dec_s, qdec_s, ktt_s, eg_s, m_s, x_s):
    tt = qkvz_ref.shape[0]
    ct = GDN_CHUNK
    nc = tt // ct
    hd = H_A * DK_A

    @pl.when(pl.program_id(1) == 0)
    def _():
        s_ref[...] = jnp.zeros_like(s_ref)

    row = lax.broadcasted_iota(jnp.int32, (ct, ct), 0)
    col = lax.broadcasted_iota(jnp.int32, (ct, ct), 1)
    tri = (col <= row).astype(_F32)
    strict = col < row
    eye = (col == row).astype(_F32)
    n_sq = int(math.log2(ct))
    heads = range(H_A)

    def prepare(c):
        r0 = c * ct
        sm = sm_ref[r0:r0 + ct, :]
        beta_full = _sigmoid(sm)
        xg = sm + dtrow_ref[...]
        softplus = jnp.maximum(xg, 0.0) + jnp.log(1.0 + jnp.exp(-jnp.abs(xg)))
        g_full = -jnp.exp(arow_ref[...]) * softplus
        gcum = jnp.dot(tri, g_full, preferred_element_type=_F32, precision=lax.Precision.HIGHEST)
        gcum_t = gcum.T
        for h in heads:
            sl = slice(h * DK_A, (h + 1) * DK_A)
            qh, kh, vh = (qkvz_ref[r0:r0 + ct, idx * hd + h * DK_A:idx * hd + (h + 1) * DK_A].astype(_F32)
                          for idx in range(3))
            qn, kn = qh, kh
            bcol = beta_full[:, h:h + 1]
            gcol = gcum[:, H_A + h:H_A + h + 1]
            grow = gcum_t[H_A + h:H_A + h + 1, :]
            glast = gcum[ct - 1:ct, H_A + h:H_A + h + 1]
            kb = kn * bcol
            eg = jnp.exp(gcol)
            rhs_s[c, h, :, 0:DV_A] = vh * bcol
            rhs_s[c, h, :, DV_A:DV_A + DK_A] = kb * eg
            dec_s[c, h] = jnp.where(strict, jnp.exp(jnp.where(strict, gcol - grow, 0.0)), 0.0)
            qdec_s[c, h] = (qn * eg).astype(_BF16)
            ktt_s[c, h] = (kn * jnp.exp(glast - gcol)).T.astype(_BF16)
            kn_s[c, h] = kn.astype(_BF16)
            kb_s[c, h] = kb.astype(_BF16)
            qn_s[c, h] = qn.astype(_BF16)
            eg_s[c, h] = jnp.broadcast_to(jnp.exp(glast), (DK_A, DV_A))

    def solve(c):
        for h in heads:
            m = -(_dot_nt(kb_s[c, h], kn_s[c, h]) * dec_s[c, h])
            m_s[c, h] = m
            x_s[c, h] = m
        for _ in range(n_sq - 1):
            for h in heads:
                mb = m_s[c, h].astype(_BF16)
                m = _dot(mb, mb)
                m_s[c, h] = m
                x_s[c, h] = x_s[c, h] + m + _dot(x_s[c, h].astype(_BF16), m.astype(_BF16))
        for h in heads:
            rhs = rhs_s[c, h]
            rhs_s[c, h] = rhs + _dot(x_s[c, h].astype(_BF16), rhs.astype(_BF16))
        for h in heads:
            m_s[c, h] = _dot_nt(qn_s[c, h], kn_s[c, h]) * (dec_s[c, h] + eye)

    def recur(c):
        for h in heads:
            sb = s_ref[h].astype(_BF16)
            vnew = rhs_s[c, h, :, 0:DV_A] - _dot(rhs_s[c, h, :, DV_A:DV_A + DK_A].astype(_BF16), sb)
            vb = vnew.astype(_BF16)
            o = _dot(qdec_s[c, h], sb) + _dot(m_s[c, h].astype(_BF16), vb)
            s_ref[h] = s_ref[h] * eg_s[c, h] + _dot(ktt_s[c, h], vb)
            rhs_s[c, h, :, 0:DV_A] = o

    def finish(c):
        r0 = c * ct
        naw = naw_ref[...]
        for h in heads:
            sl = slice(h * DV_A, (h + 1) * DV_A)
            o = rhs_s[c, h, :, 0:DV_A]
            z = qkvz_ref[r0:r0 + ct, 3 * hd + h * DV_A:3 * hd + (h + 1) * DV_A].astype(_F32)
            on = o * lax.rsqrt(jnp.mean(o * o, axis=-1, keepdims=True) + EPS) * naw
            o_ref[r0:r0 + ct, sl] = (on * (z * _sigmoid(z))).astype(o_ref.dtype)

    for c in range(nc):
        prepare(c)
    for c in range(nc):
        solve(c)
    for c in range(nc):
        recur(c)
    for c in range(nc):
        finish(c)


def _gdn(pb, ps, arow, dtrow, naw, b_, t_, col0):
    ct = GDN_CHUNK
    nc = GDN_STEP_CHUNKS if t_ % (GDN_STEP_CHUNKS * ct) == 0 else 1
    tt = nc * ct
    hd = H_A * DK_A
    nt = t_ // tt
    assert col0 % (4 * hd) == 0
    cb = col0 // (4 * hd)
    const = lambda shape: pl.BlockSpec(shape, lambda b, t: (0, 0))
    return pl.pallas_call(
        _gdn_kernel,
        grid=(b_, nt),
        in_specs=[pl.BlockSpec((tt, 4 * hd), lambda b, t: (b * nt + t, cb)),
                  pl.BlockSpec((tt, LANE), lambda b, t: (b * nt + t, 0)),
                  const((1, LANE)), const((1, LANE)), const((1, DV_A))],
        out_specs=pl.BlockSpec((tt, hd), lambda b, t: (b * nt + t, 0)),
        out_shape=jax.ShapeDtypeStruct((b_ * t_, hd), _BF16),
        scratch_shapes=[
            pltpu.VMEM((H_A, DK_A, DV_A), _F32),
            pltpu.VMEM((nc, H_A, ct, DK_A), _BF16),
            pltpu.VMEM((nc, H_A, ct, DK_A), _BF16),
            pltpu.VMEM((nc, H_A, ct, DK_A), _BF16),
            pltpu.VMEM((nc, H_A, ct, DV_A + DK_A), _F32),
            pltpu.VMEM((nc, H_A, ct, ct), _F32),
            pltpu.VMEM((nc, H_A, ct, DK_A), _BF16),
            pltpu.VMEM((nc, H_A, DK_A, ct), _BF16),
            pltpu.VMEM((nc, H_A, DK_A, DV_A), _F32),
            pltpu.VMEM((nc, H_A, ct, ct), _F32),
            pltpu.VMEM((nc, H_A, ct, ct), _F32),
        ],
        compiler_params=pltpu.CompilerParams(
            dimension_semantics=("arbitrary", "arbitrary"), vmem_limit_bytes=VMEM_LIMIT),
        name="gdn",
    )(pb, ps, arow, dtrow, naw)


def _bias_kernel(tab_ref, bucket_ref, o_ref):
    bucket = bucket_ref[...]
    nq, w = bucket.shape
    for h in range(H_B):
        acc = jnp.zeros((nq, w), _F32)
        for b in range(N_BUCKETS):
            acc = acc + jnp.where(bucket == b, tab_ref[b, h], 0.0)
        far = acc[:, 2 * LANE:3 * LANE]
        o_ref[h] = (acc[:, 0:2 * LANE] - jnp.concatenate([far, far], axis=1)) * LOG2E


def _bias_tiles(rel_table, bucket):
    return pl.pallas_call(
        _bias_kernel,
        in_specs=[pl.BlockSpec(memory_space=pltpu.SMEM),
                  pl.BlockSpec(bucket.shape, lambda: (0, 0))],
        out_specs=pl.BlockSpec((H_B, Q_BLOCK, 2 * LANE), lambda: (0, 0, 0)),
        out_shape=jax.ShapeDtypeStruct((H_B, Q_BLOCK, 2 * LANE), _F32),
        name="rel_bias_tiles",
    )(rel_table, bucket)


def _dsa_kernel(q_ref, k_ref, v_ref, iq_ref, ik_ref, sm_ref, bias_ref, o_ref,
                key_s, qs_s, iwb_s, m_s, l_s, acc_s, s_s, si_s, sf_s, *, topk):
    nq = q_ref.shape[0]
    st_w = KEY_TILE
    assert nq == st_w
    ng = st_w // LANE
    qb = pl.program_id(1)
    n_st = qb + 1
    hd = D_HEAD_B
    scale = D_HEAD_B ** -0.5 * LOG2E
    f_topk = float(topk)

    lane_j = lax.broadcasted_iota(jnp.int32, (nq, LANE), 1)
    row_i = lax.broadcasted_iota(jnp.int32, (nq, LANE), 0)

    def grp(g):
        return slice(g * LANE, (g + 1) * LANE)

    iw = sm_ref[...]
    for h in range(H_B):
        sl = slice(h * hd, (h + 1) * hd)
        qs_s[h] = (q_ref[:, sl].astype(_F32) * scale).astype(_BF16)
        iwb_s[h] = jnp.broadcast_to(iw[:, 2 * H_A + h:2 * H_A + h + 1], (nq, LANE))
        m_s[h] = jnp.full((nq, LANE), MASK_NEG, _F32)
        l_s[h] = jnp.zeros((nq, LANE), _F32)
        acc_s[h] = jnp.zeros((nq, hd), _F32)

    def scores(st, diag):
        ikt = ik_ref[pl.ds(pl.multiple_of(st * st_w, st_w), st_w), :]
        acc = [jnp.zeros((nq, LANE), _F32) for _ in range(ng)]
        for p in range(H_IDX // 2):
            xq = iq_ref[:, p * LANE:(p + 1) * LANE]
            for half in range(2):
                h = 2 * p + half
                s = jnp.maximum(_dot_nt(xq, ikt[:, half * LANE:(half + 1) * LANE]), 0.0)
                w = iwb_s[h]
                for g in range(ng):
                    acc[g] = acc[g] + w * s[:, grp(g)]
        for g in range(ng):
            a = acc[g]
            if diag:
                a = jnp.where((g * LANE + lane_j) // CHUNK <= row_i // CHUNK, a, -jnp.inf)
            bits = pltpu.bitcast(a, jnp.int32)
            key_s[st, g] = bits ^ ((bits >> 31) & INT_MAX)

    def p1(st, carry):
        scores(st, False)
        return carry

    lax.fori_loop(0, qb, p1, 0)
    scores(qb, True)

    ones_b = jnp.ones((LANE, LANE), _BF16)
    assert key_s.shape[0] * ng <= 256

    T_, CAND_, MIDX_ = 0, 1, 2
    ACC_, CNT_ = 0, 1
    rb = 64
    lane_rb = lax.broadcasted_iota(jnp.int32, (rb, LANE), 1)

    def count(pred_fn):
        def tile(st, first):
            for r0 in range(0, nq, rb):
                rs = slice(r0, r0 + rb)
                acc = None if first else sf_s[ACC_, rs, :]
                for g in range(ng):
                    one = jnp.where(pred_fn(key_s[st, g, rs, :], rs), 1.0, 0.0)
                    acc = one if acc is None else acc + one
                sf_s[ACC_, rs, :] = acc

        def body(st, carry):
            tile(st, False)
            return carry

        tile(0, True)
        lax.fori_loop(1, n_st, body, 0)
        return _dot(sf_s[ACC_].astype(_BF16), ones_b)

    si_s[CAND_] = jnp.zeros((nq, LANE), jnp.int32)

    def bit_body(i, carry):
        half = jnp.left_shift(jnp.int32(1), 30 - i)
        c = count(lambda k, rs: k >= si_s[CAND_, rs, :])
        si_s[CAND_] = si_s[CAND_] + jnp.where(c >= f_topk, half, -half)
        return carry

    lax.fori_loop(0, 31, bit_body, 0)
    c_last = count(lambda k, rs: k >= si_s[CAND_, rs, :])
    si_s[T_] = jnp.where(c_last >= f_topk, si_s[CAND_], si_s[CAND_] - 1)
    sf_s[CNT_] = count(lambda k, rs: k >= si_s[T_, rs, :])

    def encode(st, carry):
        for r0 in range(0, nq, rb):
            rs = slice(r0, r0 + rb)
            t = si_s[T_, rs, :]
            for g in range(ng):
                k = key_s[st, g, rs, :]
                pos = st * st_w + g * LANE + lane_rb
                key_s[st, g, rs, :] = jnp.where(k > t, CODE_KEEP, jnp.where(k == t, pos, CODE_DROP))
        return carry

    lax.fori_loop(0, n_st, encode, 0)
    finite_thr = si_s[T_] != KEY_NEG_INF
    si_s[MIDX_] = jnp.where(finite_thr, TIE_BOUND_ALL, 0)

    excess = jnp.max(jnp.where(finite_thr, sf_s[CNT_], 0.0)) > f_topk

    @pl.when(excess)
    def _():
        r_keep = f_topk - count(lambda code, rs: code < 0)
        sf_s[CNT_] = jnp.where(finite_thr, r_keep, 0.0)
        sf_s[ACC_] = jnp.zeros((nq, LANE), _F32)
        upper = (lax.broadcasted_iota(jnp.int32, (LANE, LANE), 0)
                 <= lax.broadcasted_iota(jnp.int32, (LANE, LANE), 1)).astype(_BF16)

        def rank_ties(st, carry):
            for r0 in range(0, nq, rb):
                rs = slice(r0, r0 + rb)
                r_blk = sf_s[CNT_, rs, :]
                before = sf_s[ACC_, rs, :]
                for g in range(ng):
                    code = key_s[st, g, rs, :]
                    tie = jnp.where(code >= 0, jnp.where(code < CODE_DROP, 1.0, 0.0), 0.0)
                    tie_b = tie.astype(_BF16)
                    rank = before + _dot(tie_b, upper)
                    kept_tie = jnp.where(rank <= r_blk, tie, 0.0)
                    mb = jnp.where(code < 0, 0.0, jnp.where(kept_tie > 0.0, 0.0, MASK_NEG))
                    key_s[st, g, rs, :] = pltpu.bitcast(mb, jnp.int32)
                    before = before + _dot(tie_b, ones_b)
                sf_s[ACC_, rs, :] = before
            return carry

        lax.fori_loop(0, n_st, rank_ties, 0)

    @pl.when(jnp.logical_not(excess))
    def _():
        def p2(st, carry):
            for r0 in range(0, nq, rb):
                rs = slice(r0, r0 + rb)
                bound = si_s[MIDX_, rs, :]
                for g in range(ng):
                    mb = jnp.where(key_s[st, g, rs, :] < bound, 0.0, MASK_NEG)
                    key_s[st, g, rs, :] = pltpu.bitcast(mb, jnp.int32)
            return carry

        lax.fori_loop(0, n_st, p2, 0)

    def attend(st, mode):
        off = pl.multiple_of(st * st_w, st_w)
        for h in range(H_B):
            sl = slice(h * hd, (h + 1) * hd)
            mask = jnp.concatenate([pltpu.bitcast(key_s[st, g], _F32) for g in range(ng)], axis=1)
            s = _dot_nt(qs_s[h], k_ref[pl.ds(off, st_w), sl]) + mask
            if mode:
                s_s[...] = s
                b_prev = bias_ref[h, :, 0:LANE]
                if mode == 1:
                    s_s[0:LANE, (ng - 1) * LANE:ng * LANE] += b_prev
                else:
                    b_diag = bias_ref[h, :, LANE:2 * LANE]
                    for rt in range(ng):
                        s_s[grp(rt), grp(rt)] += b_diag
                        if rt:
                            s_s[grp(rt), grp(rt - 1)] += b_prev
                s = s_s[...]
            m_old = m_s[h]
            gmax = s[:, grp(0)]
            for g in range(1, ng):
                gmax = jnp.maximum(gmax, s[:, grp(g)])
            m_new = jnp.maximum(m_old, jnp.max(gmax, axis=1, keepdims=True))
            alpha = jnp.exp2(m_old - m_new)
            pieces = [jnp.exp2(s[:, grp(g)] - m_new) for g in range(ng)]
            psum = pieces[0]
            for g in range(1, ng):
                psum = psum + pieces[g]
            p = jnp.concatenate(pieces, axis=1).astype(_BF16)
            m_s[h] = m_new
            l_s[h] = alpha * l_s[h] + psum
            acc_s[h] = alpha * acc_s[h] + _dot(p, v_ref[pl.ds(off, st_w), sl])

    def p3(st, carry):
        attend(st, 0)
        return carry

    lax.fori_loop(0, qb - 1, p3, 0)

    @pl.when(qb > 0)
    def _():
        attend(qb - 1, 1)

    attend(qb, 2)

    for h in range(H_B):
        sl = slice(h * hd, (h + 1) * hd)
        l_fin = jnp.sum(l_s[h], axis=1, keepdims=True)
        o_ref[:, sl] = (acc_s[h] / l_fin).astype(o_ref.dtype)


def _dsa(pb, ps, bias, b_, t_, col0):
    nq = KEY_TILE
    nb = t_ // nq
    hd = H_B * D_HEAD_B
    topk = min(TOPK_MAX, t_ // 4)
    iq_w = H_IDX * D_IDX
    ik_w = 2 * LANE
    assert col0 % hd == 0 and (col0 + 3 * hd) % iq_w == 0 and (col0 + 3 * hd + iq_w) % ik_w == 0
    cq = col0 // hd
    resident = functools.partial(pl.BlockSpec, pipeline_mode=pl.Buffered(1))
    return pl.pallas_call(
        functools.partial(_dsa_kernel, topk=topk),
        grid=(b_, nb),
        in_specs=[
            pl.BlockSpec((nq, hd), lambda b, i: (b * nb + i, cq)),
            resident((t_, hd), lambda b, i: (b, cq + 1)),
            resident((t_, hd), lambda b, i: (b, cq + 2)),
            pl.BlockSpec((nq, iq_w), lambda b, i: (b * nb + i, (col0 + 3 * hd) // iq_w)),
            resident((t_, ik_w), lambda b, i: (b, (col0 + 3 * hd + iq_w) // ik_w)),
            pl.BlockSpec((nq, LANE), lambda b, i: (b * nb + i, 0)),
            resident((H_B, Q_BLOCK, 2 * LANE), lambda b, i: (0, 0, 0)),
        ],
        out_specs=pl.BlockSpec((nq, hd), lambda b, i: (b * nb + i, 0)),
        out_shape=jax.ShapeDtypeStruct((b_ * t_, hd), _BF16),
        scratch_shapes=[
            pltpu.VMEM((nb, KEY_TILE // LANE, nq, LANE), jnp.int32),
            pltpu.VMEM((H_B, nq, D_HEAD_B), _BF16),
            pltpu.VMEM((H_B, nq, LANE), _F32),
            pltpu.VMEM((H_B, nq, LANE), _F32),
            pltpu.VMEM((H_B, nq, LANE), _F32),
            pltpu.VMEM((H_B, nq, D_HEAD_B), _F32),
            pltpu.VMEM((nq, KEY_TILE), _F32),
            pltpu.VMEM((3, nq, LANE), jnp.int32),
            pltpu.VMEM((2, nq, LANE), _F32),
        ],
        compiler_params=pltpu.CompilerParams(
            dimension_semantics=("arbitrary", "arbitrary"), vmem_limit_bytes=VMEM_LIMIT),
        name="dsa",
    )(pb, pb, pb, pb, pb, ps, bias)


def _merge_kernel(oa_ref, ob_ref, g_ref, x_ref, bg_ref, wa_ref, wb_ref, wo_ref, o_ref):
    d = x_ref.shape[1]
    a = _dot(oa_ref[...], wa_ref[...])
    b = _dot(ob_ref[...], wb_ref[...])
    g = _sigmoid(g_ref[...].astype(_F32) + bg_ref[...])
    merged = g[:, 0:d] * a + g[:, d:2 * d] * b
    o_ref[...] = x_ref[...] + _dot(merged.astype(_BF16), wo_ref[...])


def _merge(oa, ob, pb, x2d, bg, wa, wb, wo, tn, col_g):
    n, d = x2d.shape
    row = lambda w: pl.BlockSpec((tn, w), lambda i: (i, 0))
    const = lambda a: pl.BlockSpec(a.shape, lambda i: (0, 0))
    return pl.pallas_call(
        _merge_kernel,
        grid=(n // tn,),
        in_specs=[row(oa.shape[1]), row(ob.shape[1]),
                  pl.BlockSpec((tn, 2 * d), lambda i: (i, col_g // (2 * d))), row(d),
                  const(bg), const(wa), const(wb), const(wo)],
        out_specs=row(d),
        out_shape=jax.ShapeDtypeStruct((n, d), _F32),
        compiler_params=pltpu.CompilerParams(
            dimension_semantics=("arbitrary",), vmem_limit_bytes=VMEM_LIMIT),
        name="merge",
    )(oa, ob, pb, x2d, bg, wa, wb, wo)


def _mlp_kernel(x_ref, n2_ref, w1_ref, w2_ref, nf_ref, o_ref, *, final_norm):
    x = x_ref[...]
    h2 = _rms(x, n2_ref[...]).astype(_BF16)
    hid = jnp.maximum(_dot(h2, w1_ref[...]), 0.0)
    y = x + _dot((hid * hid).astype(_BF16), w2_ref[...])
    o_ref[...] = _rms(y, nf_ref[...]) if final_norm else y


def _mlp(x1, n2, w1, w2, nf, tn, final_norm):
    n, d = x1.shape
    row = pl.BlockSpec((tn, d), lambda i: (i, 0))
    const = lambda a: pl.BlockSpec(a.shape, lambda i: (0, 0), pipeline_mode=pl.Buffered(1))
    return pl.pallas_call(
        functools.partial(_mlp_kernel, final_norm=final_norm),
        grid=(n // tn,),
        in_specs=[row, const(n2), const(w1), const(w2), const(nf)],
        out_specs=row,
        out_shape=jax.ShapeDtypeStruct((n, d), _F32),
        compiler_params=pltpu.CompilerParams(
            dimension_semantics=("arbitrary",), vmem_limit_bytes=VMEM_LIMIT),
        name="mlp",
    )(x1, n2, w1, w2, nf)


def _t5_bucket(rel):
    half = N_BUCKETS // 2
    max_exact = half // 2
    base = jnp.where(rel > 0, half, 0)
    n = jnp.abs(rel)
    n_f = jnp.maximum(n, 1).astype(jnp.float32)
    large = max_exact + (jnp.log(n_f / max_exact) / math.log(MAX_DISTANCE / max_exact)
                         * (half - max_exact)).astype(jnp.int32)
    large = jnp.minimum(large, half - 1)
    return base + jnp.where(n < max_exact, n, large)


def _pick_tile(n, prefs):
    for t in prefs:
        if n % t == 0:
            return t
    raise ValueError(f"no tile in {prefs} divides {n}")


def kernel(x, norm1_w, w_in, conv_a_w, a_log, dt_bias, norm_a_w, rel_bias_table, w_gate, b_gate,
           w_proj_a, w_proj_b, w_out, norm2_w, w_ff1, w_ff2, norm_final_w):
    b_, t_, d = x.shape
    depth = norm1_w.shape[0]
    n = b_ * t_
    ha, hb = H_A * DK_A, H_B * D_HEAD_B
    assert t_ % KEY_TILE == 0 and t_ % GDN_CHUNK == 0 and d % LANE == 0
    assert DK_A == DV_A == D_HEAD_B == LANE and 2 * D_IDX == LANE
    assert Q_BLOCK >= MAX_DISTANCE

    o_ba = 4 * ha
    o_aa = o_ba + H_A
    o_qb = o_aa + H_A
    o_iq = o_qb + 3 * hb
    o_ik = o_iq + H_IDX * D_IDX
    o_iw = o_ik + D_IDX

    rel = (jnp.arange(2 * Q_BLOCK, dtype=jnp.int32)[None, :] - Q_BLOCK) \
        - jnp.arange(Q_BLOCK, dtype=jnp.int32)[:, None]
    far = jnp.full((Q_BLOCK, LANE), -(Q_BLOCK + 1), jnp.int32)
    bucket = _t5_bucket(jnp.concatenate([rel, far], axis=1))
    bias = _bias_tiles(rel_bias_table.astype(_F32), bucket)

    x2d = x.reshape(n, d)
    tn = _pick_tile(t_, (512, 256, 128))
    for layer in range(depth):
        wi = w_in[layer].astype(_BF16)
        zpad = lambda w: jnp.zeros((d, w), _BF16)
        ws = jnp.concatenate([wi[:, o_ba:o_qb], wi[:, o_iw:o_iw + H_IDX], zpad(LANE - 3 * H_A)], axis=1)
        ik = wi[:, o_ik:o_iw]
        wb16 = jnp.concatenate([wi[:, 0:o_ba], w_gate[layer].astype(_BF16), wi[:, o_qb:o_ik],
                                ik, zpad(D_IDX), zpad(D_IDX), ik], axis=1)
        nw = norm1_w[layer].reshape(1, d).astype(_F32)
        pb, ps = _norm_proj(x2d, nw, wb16, ws, conv_a_w[layer].astype(_F32), tn, t_ // tn, norm_cols=2 * ha)

        col_a = 0
        col_g = 4 * ha
        col_b = col_g + 2 * d
        assert col_g % (2 * d) == 0
        lanes = lambda v, off: jnp.zeros((1, LANE), _F32).at[0, off:off + H_A].set(v.astype(_F32))
        oa = _gdn(pb, ps, lanes(a_log[layer], H_A), lanes(dt_bias[layer], H_A),
                  norm_a_w[layer].reshape(1, DV_A).astype(_F32), b_, t_, col_a)
        ob = _dsa(pb, ps, bias, b_, t_, col_b)

        x2d = _merge(oa, ob, pb, x2d, b_gate[layer].reshape(1, 2 * d).astype(_F32),
                     w_proj_a[layer].astype(_BF16), w_proj_b[layer].astype(_BF16),
                     w_out[layer].astype(_BF16), tn, col_g)
        x2d = _mlp(x2d, norm2_w[layer].reshape(1, d).astype(_F32), w_ff1[layer].astype(_BF16),
                   w_ff2[layer].astype(_BF16), norm_final_w.reshape(1, d).astype(_F32),
                   _pick_tile(n, (512, 256, 128)), final_norm=layer == depth - 1)
    return x2d.reshape(b_, t_, d)
```

```python
import functools
import math

import jax
import jax.numpy as jnp
from jax import lax
from jax.experimental import pallas as pl
from jax.experimental.pallas import tpu as pltpu

EPS = 1e-6
H_A = 8
DK_A = 128
DV_A = 128
CONV_K = 4
GDN_CHUNK = 128
GDN_STEP_CHUNKS = 4
H_B = 8
D_HEAD_B = 128
H_IDX = 8
D_IDX = 64
TOPK_MAX = 256
Q_BLOCK = 128
CHUNK = 64
KEY_TILE = 512
N_BUCKETS = 32
MAX_DISTANCE = 128

LANE = 128
VMEM_LIMIT = 56 * 1024 * 1024

INT_MAX = 2**31 - 1
KEY_NEG_INF = -2139095041
MASK_NEG = -1e30
CODE_KEEP = -1
CODE_DROP = INT_MAX
TIE_BOUND_ALL = 2**30
LOG2E = math.log2(math.e)

_F32 = jnp.float32
_BF16 = jnp.bfloat16


def _dot(a, b):
    return jnp.dot(a, b, preferred_element_type=_F32)


def _dot_nt(a, b):
    return lax.dot_general(a, b, (((1,), (1,)), ((), ())), preferred_element_type=_F32)


def _sigmoid(x):
    return 1.0 / (1.0 + jnp.exp(-x))


def _rms(x, w):
    return x * lax.rsqrt(jnp.mean(x * x, axis=-1, keepdims=True) + EPS) * w


def _norm_proj_kernel(x_ref, nw_ref, w_ref, ws_ref, cw_ref, o_ref, os_ref, halo_s, work_s, *,
                      chunk, conv_cols, norm_cols, tiles_per_seq):
    tn = x_ref.shape[0]

    @pl.when(pl.program_id(0) == 0)
    def _():
        halo_s[...] = jnp.zeros_like(halo_s)

    h = _rms(x_ref[...], nw_ref[...]).astype(_BF16)
    os_ref[...] = _dot(h, ws_ref[...])
    seq_start = pl.program_id(0) % tiles_per_seq == 0
    heavy = [(ci, c0) for ci, c0 in enumerate(range(0, conv_cols, chunk))]
    plain = [(None, c0) for c0 in range(conv_cols, w_ref.shape[1], chunk)]
    per_heavy = -(-len(plain) // max(1, len(heavy)))
    order = []
    for i, hv in enumerate(heavy):
        order.append(hv)
        order.extend(plain[i * per_heavy:(i + 1) * per_heavy])
    order.extend(plain[len(heavy) * per_heavy:])
    for ci, c0 in order:
        y = _dot(h, w_ref[:, c0:c0 + chunk])
        if c0 < conv_cols:
            work_s[0:8, :] = jnp.where(seq_start, 0.0, halo_s[ci])
            work_s[8:8 + tn, :] = y
            halo_s[ci] = y[tn - 8:tn, :]
            acc = cw_ref[CONV_K - 1:CONV_K, c0:c0 + chunk] * y
            for j in range(CONV_K - 1):
                lo = 8 - (CONV_K - 1) + j
                acc = acc + cw_ref[j:j + 1, c0:c0 + chunk] * work_s[lo:lo + tn, :]
            y = acc * _sigmoid(acc)
            if c0 < norm_cols:
                parts = []
                for g0 in range(0, chunk, DK_A):
                    yh = y[:, g0:g0 + DK_A]
                    if c0 + g0 < norm_cols:
                        inv = lax.rsqrt(jnp.sum(yh * yh, axis=-1, keepdims=True) + EPS)
                        if c0 + g0 < norm_cols // 2:
                            inv = inv * (DK_A ** -0.5)
                        yh = yh * inv
                    parts.append(yh)
                y = jnp.concatenate(parts, axis=1)
        o_ref[:, c0:c0 + chunk] = y.astype(o_ref.dtype)


def _norm_proj(x2d, nw, w, ws, cw, tn, tiles_per_seq, norm_cols):
    n, d = x2d.shape
    c = w.shape[1]
    chunk = _pick_tile(c, (768, 512, 256, 128))
    conv_cols = cw.shape[1]
    assert conv_cols % chunk == 0 and tn >= 8
    assert norm_cols <= conv_cols and chunk % DK_A == 0 and (norm_cols // 2) % DK_A == 0
    const = functools.partial(pl.BlockSpec, pipeline_mode=pl.Buffered(1))
    return pl.pallas_call(
        functools.partial(_norm_proj_kernel, chunk=chunk, conv_cols=conv_cols, norm_cols=norm_cols,
                          tiles_per_seq=tiles_per_seq),
        grid=(n // tn,),
        in_specs=[
            pl.BlockSpec((tn, d), lambda i: (i, 0)),
            const((1, d), lambda i: (0, 0)),
            const((d, c), lambda i: (0, 0)),
            const((d, LANE), lambda i: (0, 0)),
            const(cw.shape, lambda i: (0, 0)),
        ],
        out_specs=[pl.BlockSpec((tn, c), lambda i: (i, 0)), pl.BlockSpec((tn, LANE), lambda i: (i, 0))],
        out_shape=[jax.ShapeDtypeStruct((n, c), _BF16), jax.ShapeDtypeStruct((n, LANE), _F32)],
        scratch_shapes=[
            pltpu.VMEM((conv_cols // chunk, 8, chunk), _F32),
            pltpu.VMEM((8 + tn, chunk), _F32),
        ],
        compiler_params=pltpu.CompilerParams(
            dimension_semantics=("arbitrary",), vmem_limit_bytes=VMEM_LIMIT),
        name="proj",
    )(x2d, nw, w, ws, cw)


def _gdn_kernel(qkvz_ref, sm_ref, arow_ref, dtrow_ref, naw_ref, o_ref,
                s_ref, kn_s, kb_s, qn_s, rhs_s, dec_s, qdec_s, ktt_s, eg_s, m_s, x_s):
    tt = qkvz_ref.shape[0]
    ct = GDN_CHUNK
    nc = tt // ct
    hd = H_A * DK_A

    @pl.when(pl.program_id(1) == 0)
    def _():
        s_ref[...] = jnp.zeros_like(s_ref)

    row = lax.broadcasted_iota(jnp.int32, (ct, ct), 0)
    col = lax.broadcasted_iota(jnp.int32, (ct, ct), 1)
    tri = (col <= row).astype(_F32)
    strict = col < row
    eye = (col == row).astype(_F32)
    n_sq = int(math.log2(ct))
    heads = range(H_A)

    def prepare(c):
        r0 = c * ct
        sm = sm_ref[r0:r0 + ct, :]
        beta_full = _sigmoid(sm)
        xg = sm + dtrow_ref[...]
        softplus = jnp.maximum(xg, 0.0) + jnp.log(1.0 + jnp.exp(-jnp.abs(xg)))
        g_full = -jnp.exp(arow_ref[...]) * softplus
        gcum = jnp.dot(tri, g_full, preferred_element_type=_F32, precision=lax.Precision.HIGHEST)
        gcum_t = gcum.T
        for h in heads:
            sl = slice(h * DK_A, (h + 1) * DK_A)
            qh, kh, vh = (qkvz_ref[r0:r0 + ct, idx * hd + h * DK_A:idx * hd + (h + 1) * DK_A].astype(_F32)
                          for idx in range(3))
            qn, kn = qh, kh
            bcol = beta_full[:, h:h + 1]
            gcol = gcum[:, H_A + h:H_A + h + 1]
            grow = gcum_t[H_A + h:H_A + h + 1, :]
            glast = gcum[ct - 1:ct, H_A + h:H_A + h + 1]
            kb = kn * bcol
            eg = jnp.exp(gcol)
            rhs_s[c, h, :, 0:DV_A] = vh * bcol
            rhs_s[c, h, :, DV_A:DV_A + DK_A] = kb * eg
            dec_s[c, h] = jnp.where(strict, jnp.exp(jnp.where(strict, gcol - grow, 0.0)), 0.0)
            qdec_s[c, h] = (qn * eg).astype(_BF16)
            ktt_s[c, h] = (kn * jnp.exp(glast - gcol)).T.astype(_BF16)
            kn_s[c, h] = kn.astype(_BF16)
            kb_s[c, h] = kb.astype(_BF16)
            qn_s[c, h] = qn.astype(_BF16)
            eg_s[c, h] = jnp.broadcast_to(jnp.exp(glast), (DK_A, DV_A))

    def solve(c):
        for h in heads:
            m = -(_dot_nt(kb_s[c, h], kn_s[c, h]) * dec_s[c, h])
            m_s[c, h] = m
            x_s[c, h] = m
        for _ in range(n_sq - 1):
            for h in heads:
                mb = m_s[c, h].astype(_BF16)
                m = _dot(mb, mb)
                m_s[c, h] = m
                x_s[c, h] = x_s[c, h] + m + _dot(x_s[c, h].astype(_BF16), m.astype(_BF16))
        for h in heads:
            rhs = rhs_s[c, h]
            rhs_s[c, h] = rhs + _dot(x_s[c, h].astype(_BF16), rhs.astype(_BF16))
        for h in heads:
            m_s[c, h] = _dot_nt(qn_s[c, h], kn_s[c, h]) * (dec_s[c, h] + eye)

    def recur(c):
        for h in heads:
            sb = s_ref[h].astype(_BF16)
            vnew = rhs_s[c, h, :, 0:DV_A] - _dot(rhs_s[c, h, :, DV_A:DV_A + DK_A].astype(_BF16), sb)
            vb = vnew.astype(_BF16)
            o = _dot(qdec_s[c, h], sb) + _dot(m_s[c, h].astype(_BF16), vb)
            s_ref[h] = s_ref[h] * eg_s[c, h] + _dot(ktt_s[c, h], vb)
            rhs_s[c, h, :, 0:DV_A] = o

    def finish(c):
        r0 = c * ct
        naw = naw_ref[...]
        for h in heads:
            sl = slice(h * DV_A, (h + 1) * DV_A)
            o = rhs_s[c, h, :, 0:DV_A]
            z = qkvz_ref[r0:r0 + ct, 3 * hd + h * DV_A:3 * hd + (h + 1) * DV_A].astype(_F32)
            on = o * lax.rsqrt(jnp.mean(o * o, axis=-1, keepdims=True) + EPS) * naw
            o_ref[r0:r0 + ct, sl] = (on * (z * _sigmoid(z))).astype(o_ref.dtype)

    for c in range(nc):
        prepare(c)
    for c in range(nc):
        solve(c)
    for c in range(nc):
        recur(c)
    for c in range(nc):
        finish(c)


def _gdn(pb, ps, arow, dtrow, naw, b_, t_, col0):
    ct = GDN_CHUNK
    nc = GDN_STEP_CHUNKS if t_ % (GDN_STEP_CHUNKS * ct) == 0 else 1
    tt = nc * ct
    hd = H_A * DK_A
    nt = t_ // tt
    assert col0 % (4 * hd) == 0
    cb = col0 // (4 * hd)
    const = lambda shape: pl.BlockSpec(shape, lambda b, t: (0, 0))
    return pl.pallas_call(
        _gdn_kernel,
        grid=(b_, nt),
        in_specs=[pl.BlockSpec((tt, 4 * hd), lambda b, t: (b * nt + t, cb)),
                  pl.BlockSpec((tt, LANE), lambda b, t: (b * nt + t, 0)),
                  const((1, LANE)), const((1, LANE)), const((1, DV_A))],
        out_specs=pl.BlockSpec((tt, hd), lambda b, t: (b * nt + t, 0)),
        out_shape=jax.ShapeDtypeStruct((b_ * t_, hd), _BF16),
        scratch_shapes=[
            pltpu.VMEM((H_A, DK_A, DV_A), _F32),
            pltpu.VMEM((nc, H_A, ct, DK_A), _BF16),
            pltpu.VMEM((nc, H_A, ct, DK_A), _BF16),
            pltpu.VMEM((nc, H_A, ct, DK_A), _BF16),
            pltpu.VMEM((nc, H_A, ct, DV_A + DK_A), _F32),
            pltpu.VMEM((nc, H_A, ct, ct), _F32),
            pltpu.VMEM((nc, H_A, ct, DK_A), _BF16),
            pltpu.VMEM((nc, H_A, DK_A, ct), _BF16),
            pltpu.VMEM((nc, H_A, DK_A, DV_A), _F32),
            pltpu.VMEM((nc, H_A, ct, ct), _F32),
            pltpu.VMEM((nc, H_A, ct, ct), _F32),
        ],
        compiler_params=pltpu.CompilerParams(
            dimension_semantics=("arbitrary", "arbitrary"), vmem_limit_bytes=VMEM_LIMIT),
        name="gdn",
    )(pb, ps, arow, dtrow, naw)


def _bias_kernel(tab_ref, bucket_ref, o_ref):
    bucket = bucket_ref[...]
    nq, w = bucket.shape
    for h in range(H_B):
        acc = jnp.zeros((nq, w), _F32)
        for b in range(N_BUCKETS):
            acc = acc + jnp.where(bucket == b, tab_ref[b, h], 0.0)
        far = acc[:, 2 * LANE:3 * LANE]
        o_ref[h] = (acc[:, 0:2 * LANE] - jnp.concatenate([far, far], axis=1)) * LOG2E


def _bias_tiles(rel_table, bucket):
    return pl.pallas_call(
        _bias_kernel,
        in_specs=[pl.BlockSpec(memory_space=pltpu.SMEM),
                  pl.BlockSpec(bucket.shape, lambda: (0, 0))],
        out_specs=pl.BlockSpec((H_B, Q_BLOCK, 2 * LANE), lambda: (0, 0, 0)),
        out_shape=jax.ShapeDtypeStruct((H_B, Q_BLOCK, 2 * LANE), _F32),
        name="rel_bias_tiles",
    )(rel_table, bucket)


def _dsa_kernel(q_ref, k_ref, v_ref, iq_ref, ik_ref, sm_ref, bias_ref, o_ref,
                key_s, qs_s, iwb_s, m_s, l_s, acc_s, s_s, si_s, sf_s, *, topk):
    nq = q_ref.shape[0]
    st_w = KEY_TILE
    assert nq == st_w
    ng = st_w // LANE
    qb = pl.program_id(1)
    n_st = qb + 1
    hd = D_HEAD_B
    scale = D_HEAD_B ** -0.5 * LOG2E
    f_topk = float(topk)

    lane_j = lax.broadcasted_iota(jnp.int32, (nq, LANE), 1)
    row_i = lax.broadcasted_iota(jnp.int32, (nq, LANE), 0)

    def grp(g):
        return slice(g * LANE, (g + 1) * LANE)

    iw = sm_ref[...]
    for h in range(H_B):
        sl = slice(h * hd, (h + 1) * hd)
        qs_s[h] = (q_ref[:, sl].astype(_F32) * scale).astype(_BF16)
        iwb_s[h] = jnp.broadcast_to(iw[:, 2 * H_A + h:2 * H_A + h + 1], (nq, LANE))
        m_s[h] = jnp.full((nq, LANE), MASK_NEG, _F32)
        l_s[h] = jnp.zeros((nq, LANE), _F32)
        acc_s[h] = jnp.zeros((nq, hd), _F32)

    def scores(st, diag):
        ikt = ik_ref[pl.ds(pl.multiple_of(st * st_w, st_w), st_w), :]
        acc = [jnp.zeros((nq, LANE), _F32) for _ in range(ng)]
        for p in range(H_IDX // 2):
            xq = iq_ref[:, p * LANE:(p + 1) * LANE]
            for half in range(2):
                h = 2 * p + half
                s = jnp.maximum(_dot_nt(xq, ikt[:, half * LANE:(half + 1) * LANE]), 0.0)
                w = iwb_s[h]
                for g in range(ng):
                    acc[g] = acc[g] + w * s[:, grp(g)]
        for g in range(ng):
            a = acc[g]
            if diag:
                a = jnp.where((g * LANE + lane_j) // CHUNK <= row_i // CHUNK, a, -jnp.inf)
            bits = pltpu.bitcast(a, jnp.int32)
            key_s[st, g] = bits ^ ((bits >> 31) & INT_MAX)

    def p1(st, carry):
        scores(st, False)
        return carry

    lax.fori_loop(0, qb, p1, 0)
    scores(qb, True)

    ones_b = jnp.ones((LANE, LANE), _BF16)
    assert key_s.shape[0] * ng <= 256

    T_, CAND_, MIDX_ = 0, 1, 2
    ACC_, CNT_ = 0, 1
    rb = 64
    lane_rb = lax.broadcasted_iota(jnp.int32, (rb, LANE), 1)

    def count(pred_fn):
        def tile(st, first):
            for r0 in range(0, nq, rb):
                rs = slice(r0, r0 + rb)
                acc = None if first else sf_s[ACC_, rs, :]
                for g in range(ng):
                    one = jnp.where(pred_fn(key_s[st, g, rs, :], rs), 1.0, 0.0)
                    acc = one if acc is None else acc + one
                sf_s[ACC_, rs, :] = acc

        def body(st, carry):
            tile(st, False)
            return carry

        tile(0, True)
        lax.fori_loop(1, n_st, body, 0)
        return _dot(sf_s[ACC_].astype(_BF16), ones_b)

    si_s[CAND_] = jnp.zeros((nq, LANE), jnp.int32)

    def bit_body(i, carry):
        half = jnp.left_shift(jnp.int32(1), 30 - i)
        c = count(lambda k, rs: k >= si_s[CAND_, rs, :])
        si_s[CAND_] = si_s[CAND_] + jnp.where(c >= f_topk, half, -half)
        return carry

    lax.fori_loop(0, 31, bit_body, 0)
    c_last = count(lambda k, rs: k >= si_s[CAND_, rs, :])
    si_s[T_] = jnp.where(c_last >= f_topk, si_s[CAND_], si_s[CAND_] - 1)
    sf_s[CNT_] = count(lambda k, rs: k >= si_s[T_, rs, :])

    def encode(st, carry):
        for r0 in range(0, nq, rb):
            rs = slice(r0, r0 + rb)
            t = si_s[T_, rs, :]
            for g in range(ng):
                k = key_s[st, g, rs, :]
                pos = st * st_w + g * LANE + lane_rb
                key_s[st, g, rs, :] = jnp.where(k > t, CODE_KEEP, jnp.where(k == t, pos, CODE_DROP))
        return carry

    lax.fori_loop(0, n_st, encode, 0)
    finite_thr = si_s[T_] != KEY_NEG_INF
    si_s[MIDX_] = jnp.where(finite_thr, TIE_BOUND_ALL, 0)

    excess = jnp.max(jnp.where(finite_thr, sf_s[CNT_], 0.0)) > f_topk

    @pl.when(excess)
    def _():
        r_keep = f_topk - count(lambda code, rs: code < 0)
        sf_s[CNT_] = jnp.where(finite_thr, r_keep, 0.0)
        sf_s[ACC_] = jnp.zeros((nq, LANE), _F32)
        upper = (lax.broadcasted_iota(jnp.int32, (LANE, LANE), 0)
                 <= lax.broadcasted_iota(jnp.int32, (LANE, LANE), 1)).astype(_BF16)

        def rank_ties(st, carry):
            for r0 in range(0, nq, rb):
                rs = slice(r0, r0 + rb)
                r_blk = sf_s[CNT_, rs, :]
                before = sf_s[ACC_, rs, :]
                for g in range(ng):
                    code = key_s[st, g, rs, :]
                    tie = jnp.where(code >= 0, jnp.where(code < CODE_DROP, 1.0, 0.0), 0.0)
                    tie_b = tie.astype(_BF16)
                    rank = before + _dot(tie_b, upper)
                    kept_tie = jnp.where(rank <= r_blk, tie, 0.0)
                    mb = jnp.where(code < 0, 0.0, jnp.where(kept_tie > 0.0, 0.0, MASK_NEG))
                    key_s[st, g, rs, :] = pltpu.bitcast(mb, jnp.int32)
                    before = before + _dot(tie_b, ones_b)
                sf_s[ACC_, rs, :] = before
            return carry

        lax.fori_loop(0, n_st, rank_ties, 0)

    @pl.when(jnp.logical_not(excess))
    def _():
        def p2(st, carry):
            for r0 in range(0, nq, rb):
                rs = slice(r0, r0 + rb)
                bound = si_s[MIDX_, rs, :]
                for g in range(ng):
                    mb = jnp.where(key_s[st, g, rs, :] < bound, 0.0, MASK_NEG)
                    key_s[st, g, rs, :] = pltpu.bitcast(mb, jnp.int32)
            return carry

        lax.fori_loop(0, n_st, p2, 0)

    def attend(st, mode):
        off = pl.multiple_of(st * st_w, st_w)
        for h in range(H_B):
            sl = slice(h * hd, (h + 1) * hd)
            mask = jnp.concatenate([pltpu.bitcast(key_s[st, g], _F32) for g in range(ng)], axis=1)
            s = _dot_nt(qs_s[h], k_ref[pl.ds(off, st_w), sl]) + mask
            if mode:
                s_s[...] = s
                b_prev = bias_ref[h, :, 0:LANE]
                if mode == 1:
                    s_s[0:LANE, (ng - 1) * LANE:ng * LANE] += b_prev
                else:
                    b_diag = bias_ref[h, :, LANE:2 * LANE]
                    for rt in range(ng):
                        s_s[grp(rt), grp(rt)] += b_diag
                        if rt:
                            s_s[grp(rt), grp(rt - 1)] += b_prev
                s = s_s[...]
            m_old = m_s[h]
            gmax = s[:, grp(0)]
            for g in range(1, ng):
                gmax = jnp.maximum(gmax, s[:, grp(g)])
            m_new = jnp.maximum(m_old, jnp.max(gmax, axis=1, keepdims=True))
            alpha = jnp.exp2(m_old - m_new)
            pieces = [jnp.exp2(s[:, grp(g)] - m_new) for g in range(ng)]
            psum = pieces[0]
            for g in range(1, ng):
                psum = psum + pieces[g]
            p = jnp.concatenate(pieces, axis=1).astype(_BF16)
            m_s[h] = m_new
            l_s[h] = alpha * l_s[h] + psum
            acc_s[h] = alpha * acc_s[h] + _dot(p, v_ref[pl.ds(off, st_w), sl])

    def p3(st, carry):
        attend(st, 0)
        return carry

    lax.fori_loop(0, qb - 1, p3, 0)

    @pl.when(qb > 0)
    def _():
        attend(qb - 1, 1)

    attend(qb, 2)

    for h in range(H_B):
        sl = slice(h * hd, (h + 1) * hd)
        l_fin = jnp.sum(l_s[h], axis=1, keepdims=True)
        o_ref[:, sl] = (acc_s[h] / l_fin).astype(o_ref.dtype)


def _dsa(pb, ps, bias, b_, t_, col0):
    nq = KEY_TILE
    nb = t_ // nq
    hd = H_B * D_HEAD_B
    topk = min(TOPK_MAX, t_ // 4)
    iq_w = H_IDX * D_IDX
    ik_w = 2 * LANE
    assert col0 % hd == 0 and (col0 + 3 * hd) % iq_w == 0 and (col0 + 3 * hd + iq_w) % ik_w == 0
    cq = col0 // hd
    resident = functools.partial(pl.BlockSpec, pipeline_mode=pl.Buffered(1))
    return pl.pallas_call(
        functools.partial(_dsa_kernel, topk=topk),
        grid=(b_, nb),
        in_specs=[
            pl.BlockSpec((nq, hd), lambda b, i: (b * nb + i, cq)),
            resident((t_, hd), lambda b, i: (b, cq + 1)),
            resident((t_, hd), lambda b, i: (b, cq + 2)),
            pl.BlockSpec((nq, iq_w), lambda b, i: (b * nb + i, (col0 + 3 * hd) // iq_w)),
            resident((t_, ik_w), lambda b, i: (b, (col0 + 3 * hd + iq_w) // ik_w)),
            pl.BlockSpec((nq, LANE), lambda b, i: (b * nb + i, 0)),
            resident((H_B, Q_BLOCK, 2 * LANE), lambda b, i: (0, 0, 0)),
        ],
        out_specs=pl.BlockSpec((nq, hd), lambda b, i: (b * nb + i, 0)),
        out_shape=jax.ShapeDtypeStruct((b_ * t_, hd), _BF16),
        scratch_shapes=[
            pltpu.VMEM((nb, KEY_TILE // LANE, nq, LANE), jnp.int32),
            pltpu.VMEM((H_B, nq, D_HEAD_B), _BF16),
            pltpu.VMEM((H_B, nq, LANE), _F32),
            pltpu.VMEM((H_B, nq, LANE), _F32),
            pltpu.VMEM((H_B, nq, LANE), _F32),
            pltpu.VMEM((H_B, nq, D_HEAD_B), _F32),
            pltpu.VMEM((nq, KEY_TILE), _F32),
            pltpu.VMEM((3, nq, LANE), jnp.int32),
            pltpu.VMEM((2, nq, LANE), _F32),
        ],
        compiler_params=pltpu.CompilerParams(
            dimension_semantics=("arbitrary", "arbitrary"), vmem_limit_bytes=VMEM_LIMIT),
        name="dsa",
    )(pb, pb, pb, pb, pb, ps, bias)


def _merge_kernel(oa_ref, ob_ref, g_ref, x_ref, bg_ref, wa_ref, wb_ref, wo_ref, o_ref):
    d = x_ref.shape[1]
    a = _dot(oa_ref[...], wa_ref[...])
    b = _dot(ob_ref[...], wb_ref[...])
    g = _sigmoid(g_ref[...].astype(_F32) + bg_ref[...])
    merged = g[:, 0:d] * a + g[:, d:2 * d] * b
    o_ref[...] = x_ref[...] + _dot(merged.astype(_BF16), wo_ref[...])


def _merge(oa, ob, pb, x2d, bg, wa, wb, wo, tn, col_g):
    n, d = x2d.shape
    row = lambda w: pl.BlockSpec((tn, w), lambda i: (i, 0))
    const = lambda a: pl.BlockSpec(a.shape, lambda i: (0, 0))
    return pl.pallas_call(
        _merge_kernel,
        grid=(n // tn,),
        in_specs=[row(oa.shape[1]), row(ob.shape[1]),
                  pl.BlockSpec((tn, 2 * d), lambda i: (i, col_g // (2 * d))), row(d),
                  const(bg), const(wa), const(wb), const(wo)],
        out_specs=row(d),
        out_shape=jax.ShapeDtypeStruct((n, d), _F32),
        compiler_params=pltpu.CompilerParams(
            dimension_semantics=("arbitrary",), vmem_limit_bytes=VMEM_LIMIT),
        name="merge",
    )(oa, ob, pb, x2d, bg, wa, wb, wo)


def _mlp_kernel(x_ref, n2_ref, w1_ref, w2_ref, nf_ref, o_ref, *, final_norm):
    x = x_ref[...]
    h2 = _rms(x, n2_ref[...]).astype(_BF16)
    y = x
    dff = w1_ref.shape[1]
    for f0 in range(0, dff, dff // 2):
        hid = jnp.maximum(_dot(h2, w1_ref[:, f0:f0 + dff // 2]), 0.0)
        y = y + _dot((hid * hid).astype(_BF16), w2_ref[f0:f0 + dff // 2, :])
    o_ref[...] = _rms(y, nf_ref[...]) if final_norm else y


def _mlp(x1, n2, w1, w2, nf, tn, final_norm):
    n, d = x1.shape
    row = pl.BlockSpec((tn, d), lambda i: (i, 0))
    const = lambda a: pl.BlockSpec(a.shape, lambda i: (0, 0), pipeline_mode=pl.Buffered(1))
    return pl.pallas_call(
        functools.partial(_mlp_kernel, final_norm=final_norm),
        grid=(n // tn,),
        in_specs=[row, const(n2), const(w1), const(w2), const(nf)],
        out_specs=row,
        out_shape=jax.ShapeDtypeStruct((n, d), _F32),
        compiler_params=pltpu.CompilerParams(
            dimension_semantics=("arbitrary",), vmem_limit_bytes=VMEM_LIMIT),
        name="mlp",
    )(x1, n2, w1, w2, nf)


def _t5_bucket(rel):
    half = N_BUCKETS // 2
    max_exact = half // 2
    base = jnp.where(rel > 0, half, 0)
    n = jnp.abs(rel)
    n_f = jnp.maximum(n, 1).astype(jnp.float32)
    large = max_exact + (jnp.log(n_f / max_exact) / math.log(MAX_DISTANCE / max_exact)
                         * (half - max_exact)).astype(jnp.int32)
    large = jnp.minimum(large, half - 1)
    return base + jnp.where(n < max_exact, n, large)


def _pick_tile(n, prefs):
    for t in prefs:
        if n % t == 0:
            return t
    raise ValueError(f"no tile in {prefs} divides {n}")


def kernel(x, norm1_w, w_in, conv_a_w, a_log, dt_bias, norm_a_w, rel_bias_table, w_gate, b_gate,
           w_proj_a, w_proj_b, w_out, norm2_w, w_ff1, w_ff2, norm_final_w):
    b_, t_, d = x.shape
    depth = norm1_w.shape[0]
    n = b_ * t_
    ha, hb = H_A * DK_A, H_B * D_HEAD_B
    assert t_ % KEY_TILE == 0 and t_ % GDN_CHUNK == 0 and d % LANE == 0
    assert DK_A == DV_A == D_HEAD_B == LANE and 2 * D_IDX == LANE
    assert Q_BLOCK >= MAX_DISTANCE

    o_ba = 4 * ha
    o_aa = o_ba + H_A
    o_qb = o_aa + H_A
    o_iq = o_qb + 3 * hb
    o_ik = o_iq + H_IDX * D_IDX
    o_iw = o_ik + D_IDX

    rel = (jnp.arange(2 * Q_BLOCK, dtype=jnp.int32)[None, :] - Q_BLOCK) \
        - jnp.arange(Q_BLOCK, dtype=jnp.int32)[:, None]
    far = jnp.full((Q_BLOCK, LANE), -(Q_BLOCK + 1), jnp.int32)
    bucket = _t5_bucket(jnp.concatenate([rel, far], axis=1))
    bias = _bias_tiles(rel_bias_table.astype(_F32), bucket)

    x2d = x.reshape(n, d)
    tn = _pick_tile(t_, (512, 256, 128))
    for layer in range(depth):
        wi = w_in[layer].astype(_BF16)
        zpad = lambda w: jnp.zeros((d, w), _BF16)
        ws = jnp.concatenate([wi[:, o_ba:o_qb], wi[:, o_iw:o_iw + H_IDX], zpad(LANE - 3 * H_A)], axis=1)
        ik = wi[:, o_ik:o_iw]
        wb16 = jnp.concatenate([wi[:, 0:o_ba], w_gate[layer].astype(_BF16), wi[:, o_qb:o_ik],
                                ik, zpad(D_IDX), zpad(D_IDX), ik], axis=1)
        nw = norm1_w[layer].reshape(1, d).astype(_F32)
        pb, ps = _norm_proj(x2d, nw, wb16, ws, conv_a_w[layer].astype(_F32), tn, t_ // tn, norm_cols=2 * ha)

        col_a = 0
        col_g = 4 * ha
        col_b = col_g + 2 * d
        assert col_g % (2 * d) == 0
        lanes = lambda v, off: jnp.zeros((1, LANE), _F32).at[0, off:off + H_A].set(v.astype(_F32))
        oa = _gdn(pb, ps, lanes(a_log[layer], H_A), lanes(dt_bias[layer], H_A),
                  norm_a_w[layer].reshape(1, DV_A).astype(_F32), b_, t_, col_a)
        ob = _dsa(pb, ps, bias, b_, t_, col_b)

        x2d = _merge(oa, ob, pb, x2d, b_gate[layer].reshape(1, 2 * d).astype(_F32),
                     w_proj_a[layer].astype(_BF16), w_proj_b[layer].astype(_BF16),
                     w_out[layer].astype(_BF16), tn, col_g)
        x2d = _mlp(x2d, norm2_w[layer].reshape(1, d).astype(_F32), w_ff1[layer].astype(_BF16),
                   w_ff2[layer].astype(_BF16), norm_final_w.reshape(1, d).astype(_F32),
                   _pick_tile(n, (1024, 512, 256, 128)), final_norm=layer == depth - 1)
    return x2d.reshape(b_, t_, d)
```

```python
import functools
import math

import jax
import jax.numpy as jnp
from jax import lax
from jax.experimental import pallas as pl
from jax.experimental.pallas import tpu as pltpu

EPS = 1e-6
H_A = 8
DK_A = 128
DV_A = 128
CONV_K = 4
GDN_CHUNK = 128
GDN_STEP_CHUNKS = 4
H_B = 8
D_HEAD_B = 128
H_IDX = 8
D_IDX = 64
TOPK_MAX = 256
Q_BLOCK = 128
CHUNK = 64
KEY_TILE = 512
N_BUCKETS = 32
MAX_DISTANCE = 128

LANE = 128
VMEM_LIMIT = 56 * 1024 * 1024

INT_MAX = 2**31 - 1
KEY_NEG_INF = -2139095041
MASK_NEG = -1e30
CODE_KEEP = -1
CODE_DROP = INT_MAX
TIE_BOUND_ALL = 2**30
LOG2E = math.log2(math.e)

_F32 = jnp.float32
_BF16 = jnp.bfloat16


def _dot(a, b):
    return jnp.dot(a, b, preferred_element_type=_F32)


def _dot_nt(a, b):
    return lax.dot_general(a, b, (((1,), (1,)), ((), ())), preferred_element_type=_F32)


def _sigmoid(x):
    return 1.0 / (1.0 + jnp.exp(-x))


def _rms(x, w):
    return x * lax.rsqrt(jnp.mean(x * x, axis=-1, keepdims=True) + EPS) * w


def _norm_proj_kernel(x_ref, nw_ref, w_ref, ws_ref, cw_ref, o_ref, os_ref, halo_s, work_s, *,
                      chunk, conv_cols, norm_cols, tiles_per_seq):
    tn = x_ref.shape[0]

    @pl.when(pl.program_id(0) == 0)
    def _():
        halo_s[...] = jnp.zeros_like(halo_s)

    h = _rms(x_ref[...], nw_ref[...]).astype(_BF16)
    os_ref[...] = _dot(h, ws_ref[...])
    seq_start = pl.program_id(0) % tiles_per_seq == 0
    heavy = [(ci, c0) for ci, c0 in enumerate(range(0, conv_cols, chunk))]
    plain = [(None, c0) for c0 in range(conv_cols, w_ref.shape[1], chunk)]
    per_heavy = -(-len(plain) // max(1, len(heavy)))
    order = []
    for i, hv in enumerate(heavy):
        order.append(hv)
        order.extend(plain[i * per_heavy:(i + 1) * per_heavy])
    order.extend(plain[len(heavy) * per_heavy:])
    for ci, c0 in order:
        y = _dot(h, w_ref[:, c0:c0 + chunk])
        if c0 < conv_cols:
            work_s[0:8, :] = jnp.where(seq_start, 0.0, halo_s[ci])
            work_s[8:8 + tn, :] = y
            halo_s[ci] = y[tn - 8:tn, :]
            acc = cw_ref[CONV_K - 1:CONV_K, c0:c0 + chunk] * y
            for j in range(CONV_K - 1):
                lo = 8 - (CONV_K - 1) + j
                acc = acc + cw_ref[j:j + 1, c0:c0 + chunk] * work_s[lo:lo + tn, :]
            y = acc * _sigmoid(acc)
            if c0 < norm_cols:
                parts = []
                for g0 in range(0, chunk, DK_A):
                    yh = y[:, g0:g0 + DK_A]
                    if c0 + g0 < norm_cols:
                        inv = lax.rsqrt(jnp.sum(yh * yh, axis=-1, keepdims=True) + EPS)
                        if c0 + g0 < norm_cols // 2:
                            inv = inv * (DK_A ** -0.5)
                        yh = yh * inv
                    parts.append(yh)
                y = jnp.concatenate(parts, axis=1)
        o_ref[:, c0:c0 + chunk] = y.astype(o_ref.dtype)


def _norm_proj(x2d, nw, w, ws, cw, tn, tiles_per_seq, norm_cols):
    n, d = x2d.shape
    c = w.shape[1]
    chunk = _pick_tile(c, (768, 512, 256, 128))
    conv_cols = cw.shape[1]
    assert conv_cols % chunk == 0 and tn >= 8
    assert norm_cols <= conv_cols and chunk % DK_A == 0 and (norm_cols // 2) % DK_A == 0
    const = functools.partial(pl.BlockSpec, pipeline_mode=pl.Buffered(1))
    return pl.pallas_call(
        functools.partial(_norm_proj_kernel, chunk=chunk, conv_cols=conv_cols, norm_cols=norm_cols,
                          tiles_per_seq=tiles_per_seq),
        grid=(n // tn,),
        in_specs=[
            pl.BlockSpec((tn, d), lambda i: (i, 0)),
            const((1, d), lambda i: (0, 0)),
            const((d, c), lambda i: (0, 0)),
            const((d, LANE), lambda i: (0, 0)),
            const(cw.shape, lambda i: (0, 0)),
        ],
        out_specs=[pl.BlockSpec((tn, c), lambda i: (i, 0)), pl.BlockSpec((tn, LANE), lambda i: (i, 0))],
        out_shape=[jax.ShapeDtypeStruct((n, c), _BF16), jax.ShapeDtypeStruct((n, LANE), _F32)],
        scratch_shapes=[
            pltpu.VMEM((conv_cols // chunk, 8, chunk), _F32),
            pltpu.VMEM((8 + tn, chunk), _F32),
        ],
        compiler_params=pltpu.CompilerParams(
            dimension_semantics=("arbitrary",), vmem_limit_bytes=VMEM_LIMIT),
        name="proj",
    )(x2d, nw, w, ws, cw)


def _gdn_kernel(qkvz_ref, sm_ref, arow_ref, dtrow_ref, naw_ref, o_ref,
                s_ref, kn_s, kb_s, qn_s, rhs_s, dec_s, qdec_s, ktt_s, eg_s, m_s, x_s):
    tt = qkvz_ref.shape[0]
    ct = GDN_CHUNK
    nc = tt // ct
    hd = H_A * DK_A

    @pl.when(pl.program_id(1) == 0)
    def _():
        s_ref[...] = jnp.zeros_like(s_ref)

    row = lax.broadcasted_iota(jnp.int32, (ct, ct), 0)
    col = lax.broadcasted_iota(jnp.int32, (ct, ct), 1)
    tri = (col <= row).astype(_F32)
    strict = col < row
    eye = (col == row).astype(_F32)
    n_sq = int(math.log2(ct))
    heads = range(H_A)

    def prepare(c):
        r0 = c * ct
        sm = sm_ref[r0:r0 + ct, :]
        beta_full = _sigmoid(sm)
        xg = sm + dtrow_ref[...]
        softplus = jnp.maximum(xg, 0.0) + jnp.log(1.0 + jnp.exp(-jnp.abs(xg)))
        g_full = -jnp.exp(arow_ref[...]) * softplus
        gcum = jnp.dot(tri, g_full, preferred_element_type=_F32, precision=lax.Precision.HIGHEST)
        gcum_t = gcum.T
        for h in heads:
            sl = slice(h * DK_A, (h + 1) * DK_A)
            qh, kh, vh = (qkvz_ref[r0:r0 + ct, idx * hd + h * DK_A:idx * hd + (h + 1) * DK_A].astype(_F32)
                          for idx in range(3))
            qn, kn = qh, kh
            bcol = beta_full[:, h:h + 1]
            gcol = gcum[:, H_A + h:H_A + h + 1]
            grow = gcum_t[H_A + h:H_A + h + 1, :]
            glast = gcum[ct - 1:ct, H_A + h:H_A + h + 1]
            kb = kn * bcol
            eg = jnp.exp(gcol)
            rhs_s[c, h, :, 0:DV_A] = vh * bcol
            rhs_s[c, h, :, DV_A:DV_A + DK_A] = kb * eg
            dec_s[c, h] = jnp.where(strict, jnp.exp(jnp.where(strict, gcol - grow, 0.0)), 0.0)
            qdec_s[c, h] = (qn * eg).astype(_BF16)
            ktt_s[c, h] = (kn * jnp.exp(glast - gcol)).T.astype(_BF16)
            kn_s[c, h] = kn.astype(_BF16)
            kb_s[c, h] = kb.astype(_BF16)
            qn_s[c, h] = qn.astype(_BF16)
            eg_s[c, h] = jnp.broadcast_to(jnp.exp(glast), (DK_A, DV_A))

    def solve(c):
        for h in heads:
            m = -(_dot_nt(kb_s[c, h], kn_s[c, h]) * dec_s[c, h])
            m_s[c, h] = m
            x_s[c, h] = m
        for _ in range(n_sq - 1):
            for h in heads:
                mb = m_s[c, h].astype(_BF16)
                m = _dot(mb, mb)
                m_s[c, h] = m
                x_s[c, h] = x_s[c, h] + m + _dot(x_s[c, h].astype(_BF16), m.astype(_BF16))
        for h in heads:
            rhs = rhs_s[c, h]
            rhs_s[c, h] = rhs + _dot(x_s[c, h].astype(_BF16), rhs.astype(_BF16))
        for h in heads:
            m_s[c, h] = _dot_nt(qn_s[c, h], kn_s[c, h]) * (dec_s[c, h] + eye)

    def recur(c):
        for h in heads:
            sb = s_ref[h].astype(_BF16)
            vnew = rhs_s[c, h, :, 0:DV_A] - _dot(rhs_s[c, h, :, DV_A:DV_A + DK_A].astype(_BF16), sb)
            vb = vnew.astype(_BF16)
            o = _dot(qdec_s[c, h], sb) + _dot(m_s[c, h].astype(_BF16), vb)
            s_ref[h] = s_ref[h] * eg_s[c, h] + _dot(ktt_s[c, h], vb)
            rhs_s[c, h, :, 0:DV_A] = o

    def finish(c):
        r0 = c * ct
        naw = naw_ref[...]
        for h in heads:
            sl = slice(h * DV_A, (h + 1) * DV_A)
            o = rhs_s[c, h, :, 0:DV_A]
            z = qkvz_ref[r0:r0 + ct, 3 * hd + h * DV_A:3 * hd + (h + 1) * DV_A].astype(_F32)
            on = o * lax.rsqrt(jnp.mean(o * o, axis=-1, keepdims=True) + EPS) * naw
            o_ref[r0:r0 + ct, sl] = (on * (z * _sigmoid(z))).astype(o_ref.dtype)

    for c in range(nc):
        prepare(c)
    for c in range(nc):
        solve(c)
    for c in range(nc):
        recur(c)
    for c in range(nc):
        finish(c)


def _gdn(pb, ps, arow, dtrow, naw, b_, t_, col0):
    ct = GDN_CHUNK
    nc = GDN_STEP_CHUNKS if t_ % (GDN_STEP_CHUNKS * ct) == 0 else 1
    tt = nc * ct
    hd = H_A * DK_A
    nt = t_ // tt
    assert col0 % (4 * hd) == 0
    cb = col0 // (4 * hd)
    const = lambda shape: pl.BlockSpec(shape, lambda b, t: (0, 0))
    return pl.pallas_call(
        _gdn_kernel,
        grid=(b_, nt),
        in_specs=[pl.BlockSpec((tt, 4 * hd), lambda b, t: (b * nt + t, cb)),
                  pl.BlockSpec((tt, LANE), lambda b, t: (b * nt + t, 0)),
                  const((1, LANE)), const((1, LANE)), const((1, DV_A))],
        out_specs=pl.BlockSpec((tt, hd), lambda b, t: (b * nt + t, 0)),
        out_shape=jax.ShapeDtypeStruct((b_ * t_, hd), _BF16),
        scratch_shapes=[
            pltpu.VMEM((H_A, DK_A, DV_A), _F32),
            pltpu.VMEM((nc, H_A, ct, DK_A), _BF16),
            pltpu.VMEM((nc, H_A, ct, DK_A), _BF16),
            pltpu.VMEM((nc, H_A, ct, DK_A), _BF16),
            pltpu.VMEM((nc, H_A, ct, DV_A + DK_A), _F32),
            pltpu.VMEM((nc, H_A, ct, ct), _F32),
            pltpu.VMEM((nc, H_A, ct, DK_A), _BF16),
            pltpu.VMEM((nc, H_A, DK_A, ct), _BF16),
            pltpu.VMEM((nc, H_A, DK_A, DV_A), _F32),
            pltpu.VMEM((nc, H_A, ct, ct), _F32),
            pltpu.VMEM((nc, H_A, ct, ct), _F32),
        ],
        compiler_params=pltpu.CompilerParams(
            dimension_semantics=("arbitrary", "arbitrary"), vmem_limit_bytes=VMEM_LIMIT),
        name="gdn",
    )(pb, ps, arow, dtrow, naw)


def _bias_kernel(tab_ref, bucket_ref, o_ref):
    bucket = bucket_ref[...]
    nq, w = bucket.shape
    for h in range(H_B):
        acc = jnp.zeros((nq, w), _F32)
        for b in range(N_BUCKETS):
            acc = acc + jnp.where(bucket == b, tab_ref[b, h], 0.0)
        far = acc[:, 2 * LANE:3 * LANE]
        o_ref[h] = (acc[:, 0:2 * LANE] - jnp.concatenate([far, far], axis=1)) * LOG2E


def _bias_tiles(rel_table, bucket):
    return pl.pallas_call(
        _bias_kernel,
        in_specs=[pl.BlockSpec(memory_space=pltpu.SMEM),
                  pl.BlockSpec(bucket.shape, lambda: (0, 0))],
        out_specs=pl.BlockSpec((H_B, Q_BLOCK, 2 * LANE), lambda: (0, 0, 0)),
        out_shape=jax.ShapeDtypeStruct((H_B, Q_BLOCK, 2 * LANE), _F32),
        name="rel_bias_tiles",
    )(rel_table, bucket)


def _dsa_kernel(q_ref, k_ref, v_ref, iq_ref, ik_ref, sm_ref, bias_ref, o_ref,
                key_s, qs_s, iwb_s, m_s, l_s, acc_s, s_s, si_s, sf_s, *, topk):
    nq = q_ref.shape[0]
    st_w = KEY_TILE
    assert nq == st_w
    ng = st_w // LANE
    qb = pl.program_id(1)
    n_st = qb + 1
    hd = D_HEAD_B
    scale = D_HEAD_B ** -0.5 * LOG2E
    f_topk = float(topk)

    lane_j = lax.broadcasted_iota(jnp.int32, (nq, LANE), 1)
    row_i = lax.broadcasted_iota(jnp.int32, (nq, LANE), 0)

    def grp(g):
        return slice(g * LANE, (g + 1) * LANE)

    iw = sm_ref[...]
    for h in range(H_B):
        sl = slice(h * hd, (h + 1) * hd)
        qs_s[h] = (q_ref[:, sl].astype(_F32) * scale).astype(_BF16)
        iwb_s[h] = jnp.broadcast_to(iw[:, 2 * H_A + h:2 * H_A + h + 1], (nq, LANE))
        m_s[h] = jnp.full((nq, LANE), MASK_NEG, _F32)
        l_s[h] = jnp.zeros((nq, LANE), _F32)
        acc_s[h] = jnp.zeros((nq, hd), _F32)

    def scores(st, diag):
        ikt = ik_ref[pl.ds(pl.multiple_of(st * st_w, st_w), st_w), :]
        acc = [jnp.zeros((nq, LANE), _F32) for _ in range(ng)]
        for p in range(H_IDX // 2):
            xq = iq_ref[:, p * LANE:(p + 1) * LANE]
            for half in range(2):
                h = 2 * p + half
                s = jnp.maximum(_dot_nt(xq, ikt[:, half * LANE:(half + 1) * LANE]), 0.0)
                w = iwb_s[h]
                for g in range(ng):
                    acc[g] = acc[g] + w * s[:, grp(g)]
        for g in range(ng):
            a = acc[g]
            if diag:
                a = jnp.where((g * LANE + lane_j) // CHUNK <= row_i // CHUNK, a, -jnp.inf)
            bits = pltpu.bitcast(a, jnp.int32)
            key_s[st, g] = bits ^ ((bits >> 31) & INT_MAX)

    def p1(st, carry):
        scores(st, False)
        return carry

    lax.fori_loop(0, qb, p1, 0)
    scores(qb, True)

    ones_b = jnp.ones((LANE, LANE), _BF16)
    assert key_s.shape[0] * ng <= 256

    T_, CAND_, MIDX_ = 0, 1, 2
    ACC_, CNT_ = 0, 1
    rb = 64
    lane_rb = lax.broadcasted_iota(jnp.int32, (rb, LANE), 1)

    def count(pred_fn):
        def tile(st, first):
            for r0 in range(0, nq, rb):
                rs = slice(r0, r0 + rb)
                acc = None if first else sf_s[ACC_, rs, :]
                for g in range(ng):
                    one = jnp.where(pred_fn(key_s[st, g, rs, :], rs), 1.0, 0.0)
                    acc = one if acc is None else acc + one
                sf_s[ACC_, rs, :] = acc

        def body(st, carry):
            tile(st, False)
            return carry

        tile(0, True)
        lax.fori_loop(1, n_st, body, 0)
        return _dot(sf_s[ACC_].astype(_BF16), ones_b)

    si_s[CAND_] = jnp.zeros((nq, LANE), jnp.int32)

    def bit_body(i, carry):
        half = jnp.left_shift(jnp.int32(1), 30 - i)
        c = count(lambda k, rs: k >= si_s[CAND_, rs, :])
        si_s[CAND_] = si_s[CAND_] + jnp.where(c >= f_topk, half, -half)
        return carry

    lax.fori_loop(0, 31, bit_body, 0)
    c_last = count(lambda k, rs: k >= si_s[CAND_, rs, :])
    si_s[T_] = jnp.where(c_last >= f_topk, si_s[CAND_], si_s[CAND_] - 1)
    sf_s[CNT_] = count(lambda k, rs: k >= si_s[T_, rs, :])

    def encode(st, carry):
        for r0 in range(0, nq, rb):
            rs = slice(r0, r0 + rb)
            t = si_s[T_, rs, :]
            for g in range(ng):
                k = key_s[st, g, rs, :]
                pos = st * st_w + g * LANE + lane_rb
                key_s[st, g, rs, :] = jnp.where(k > t, CODE_KEEP, jnp.where(k == t, pos, CODE_DROP))
        return carry

    lax.fori_loop(0, n_st, encode, 0)
    finite_thr = si_s[T_] != KEY_NEG_INF
    si_s[MIDX_] = jnp.where(finite_thr, TIE_BOUND_ALL, 0)

    excess = jnp.max(jnp.where(finite_thr, sf_s[CNT_], 0.0)) > f_topk

    @pl.when(excess)
    def _():
        r_keep = f_topk - count(lambda code, rs: code < 0)
        sf_s[CNT_] = jnp.where(finite_thr, r_keep, 0.0)
        sf_s[ACC_] = jnp.zeros((nq, LANE), _F32)
        upper = (lax.broadcasted_iota(jnp.int32, (LANE, LANE), 0)
                 <= lax.broadcasted_iota(jnp.int32, (LANE, LANE), 1)).astype(_BF16)

        def rank_ties(st, carry):
            for r0 in range(0, nq, rb):
                rs = slice(r0, r0 + rb)
                r_blk = sf_s[CNT_, rs, :]
                before = sf_s[ACC_, rs, :]
                for g in range(ng):
                    code = key_s[st, g, rs, :]
                    tie = jnp.where(code >= 0, jnp.where(code < CODE_DROP, 1.0, 0.0), 0.0)
                    tie_b = tie.astype(_BF16)
                    rank = before + _dot(tie_b, upper)
                    kept_tie = jnp.where(rank <= r_blk, tie, 0.0)
                    mb = jnp.where(code < 0, 0.0, jnp.where(kept_tie > 0.0, 0.0, MASK_NEG))
                    key_s[st, g, rs, :] = pltpu.bitcast(mb, jnp.int32)
                    before = before + _dot(tie_b, ones_b)
                sf_s[ACC_, rs, :] = before
            return carry

        lax.fori_loop(0, n_st, rank_ties, 0)

    @pl.when(jnp.logical_not(excess))
    def _():
        def p2(st, carry):
            for r0 in range(0, nq, rb):
                rs = slice(r0, r0 + rb)
                bound = si_s[MIDX_, rs, :]
                for g in range(ng):
                    mb = jnp.where(key_s[st, g, rs, :] < bound, 0.0, MASK_NEG)
                    key_s[st, g, rs, :] = pltpu.bitcast(mb, jnp.int32)
            return carry

        lax.fori_loop(0, n_st, p2, 0)

    def attend(st, mode):
        off = pl.multiple_of(st * st_w, st_w)
        for h in range(H_B):
            sl = slice(h * hd, (h + 1) * hd)
            mask = jnp.concatenate([pltpu.bitcast(key_s[st, g], _F32) for g in range(ng)], axis=1)
            s = _dot_nt(qs_s[h], k_ref[pl.ds(off, st_w), sl]) + mask
            if mode:
                s_s[...] = s
                b_prev = bias_ref[h, :, 0:LANE]
                if mode == 1:
                    s_s[0:LANE, (ng - 1) * LANE:ng * LANE] += b_prev
                else:
                    b_diag = bias_ref[h, :, LANE:2 * LANE]
                    for rt in range(ng):
                        s_s[grp(rt), grp(rt)] += b_diag
                        if rt:
                            s_s[grp(rt), grp(rt - 1)] += b_prev
                s = s_s[...]
            m_old = m_s[h]
            gmax = s[:, grp(0)]
            for g in range(1, ng):
                gmax = jnp.maximum(gmax, s[:, grp(g)])
            m_new = jnp.maximum(m_old, jnp.max(gmax, axis=1, keepdims=True))
            alpha = jnp.exp2(m_old - m_new)
            pieces = [jnp.exp2(s[:, grp(g)] - m_new) for g in range(ng)]
            psum = pieces[0]
            for g in range(1, ng):
                psum = psum + pieces[g]
            p = jnp.concatenate(pieces, axis=1).astype(_BF16)
            m_s[h] = m_new
            l_s[h] = alpha * l_s[h] + psum
            acc_s[h] = alpha * acc_s[h] + _dot(p, v_ref[pl.ds(off, st_w), sl])

    def p3(st, carry):
        attend(st, 0)
        return carry

    lax.fori_loop(0, qb - 1, p3, 0)

    @pl.when(qb > 0)
    def _():
        attend(qb - 1, 1)

    attend(qb, 2)

    for h in range(H_B):
        sl = slice(h * hd, (h + 1) * hd)
        l_fin = jnp.sum(l_s[h], axis=1, keepdims=True)
        o_ref[:, sl] = (acc_s[h] / l_fin).astype(o_ref.dtype)


def _dsa(pb, ps, bias, b_, t_, col0):
    nq = KEY_TILE
    nb = t_ // nq
    hd = H_B * D_HEAD_B
    topk = min(TOPK_MAX, t_ // 4)
    iq_w = H_IDX * D_IDX
    ik_w = 2 * LANE
    assert col0 % hd == 0 and (col0 + 3 * hd) % iq_w == 0 and (col0 + 3 * hd + iq_w) % ik_w == 0
    cq = col0 // hd
    resident = functools.partial(pl.BlockSpec, pipeline_mode=pl.Buffered(1))
    return pl.pallas_call(
        functools.partial(_dsa_kernel, topk=topk),
        grid=(b_, nb),
        in_specs=[
            pl.BlockSpec((nq, hd), lambda b, i: (b * nb + i, cq)),
            pl.BlockSpec((t_, hd), lambda b, i: (b, cq + 1)),
            resident((t_, hd), lambda b, i: (b, cq + 2)),
            pl.BlockSpec((nq, iq_w), lambda b, i: (b * nb + i, (col0 + 3 * hd) // iq_w)),
            resident((t_, ik_w), lambda b, i: (b, (col0 + 3 * hd + iq_w) // ik_w)),
            pl.BlockSpec((nq, LANE), lambda b, i: (b * nb + i, 0)),
            resident((H_B, Q_BLOCK, 2 * LANE), lambda b, i: (0, 0, 0)),
        ],
        out_specs=pl.BlockSpec((nq, hd), lambda b, i: (b * nb + i, 0)),
        out_shape=jax.ShapeDtypeStruct((b_ * t_, hd), _BF16),
        scratch_shapes=[
            pltpu.VMEM((nb, KEY_TILE // LANE, nq, LANE), jnp.int32),
            pltpu.VMEM((H_B, nq, D_HEAD_B), _BF16),
            pltpu.VMEM((H_B, nq, LANE), _F32),
            pltpu.VMEM((H_B, nq, LANE), _F32),
            pltpu.VMEM((H_B, nq, LANE), _F32),
            pltpu.VMEM((H_B, nq, D_HEAD_B), _F32),
            pltpu.VMEM((nq, KEY_TILE), _F32),
            pltpu.VMEM((3, nq, LANE), jnp.int32),
            pltpu.VMEM((2, nq, LANE), _F32),
        ],
        compiler_params=pltpu.CompilerParams(
            dimension_semantics=("arbitrary", "arbitrary"), vmem_limit_bytes=VMEM_LIMIT),
        name="dsa",
    )(pb, pb, pb, pb, pb, ps, bias)


def _merge_kernel(oa_ref, ob_ref, g_ref, x_ref, bg_ref, wa_ref, wb_ref, wo_ref, o_ref):
    d = x_ref.shape[1]
    a = _dot(oa_ref[...], wa_ref[...])
    b = _dot(ob_ref[...], wb_ref[...])
    g = _sigmoid(g_ref[...].astype(_F32) + bg_ref[...])
    merged = g[:, 0:d] * a + g[:, d:2 * d] * b
    o_ref[...] = x_ref[...] + _dot(merged.astype(_BF16), wo_ref[...])


def _merge(oa, ob, pb, x2d, bg, wa, wb, wo, tn, col_g):
    n, d = x2d.shape
    row = lambda w: pl.BlockSpec((tn, w), lambda i: (i, 0))
    const = lambda a: pl.BlockSpec(a.shape, lambda i: (0, 0))
    return pl.pallas_call(
        _merge_kernel,
        grid=(n // tn,),
        in_specs=[row(oa.shape[1]), row(ob.shape[1]),
                  pl.BlockSpec((tn, 2 * d), lambda i: (i, col_g // (2 * d))), row(d),
                  const(bg), const(wa), const(wb), const(wo)],
        out_specs=row(d),
        out_shape=jax.ShapeDtypeStruct((n, d), _F32),
        compiler_params=pltpu.CompilerParams(
            dimension_semantics=("arbitrary",), vmem_limit_bytes=VMEM_LIMIT),
        name="merge",
    )(oa, ob, pb, x2d, bg, wa, wb, wo)


def _mlp_kernel(x_ref, n2_ref, w1_ref, w2_ref, nf_ref, o_ref, *, final_norm):
    x = x_ref[...]
    h2 = _rms(x, n2_ref[...]).astype(_BF16)
    hid = jnp.maximum(_dot(h2, w1_ref[...]), 0.0)
    y = x + _dot((hid * hid).astype(_BF16), w2_ref[...])
    o_ref[...] = _rms(y, nf_ref[...]) if final_norm else y


def _mlp(x1, n2, w1, w2, nf, tn, final_norm):
    n, d = x1.shape
    row = pl.BlockSpec((tn, d), lambda i: (i, 0))
    const = lambda a: pl.BlockSpec(a.shape, lambda i: (0, 0), pipeline_mode=pl.Buffered(1))
    return pl.pallas_call(
        functools.partial(_mlp_kernel, final_norm=final_norm),
        grid=(n // tn,),
        in_specs=[row, const(n2), const(w1), const(w2), const(nf)],
        out_specs=row,
        out_shape=jax.ShapeDtypeStruct((n, d), _F32),
        compiler_params=pltpu.CompilerParams(
            dimension_semantics=("arbitrary",), vmem_limit_bytes=VMEM_LIMIT),
        name="mlp",
    )(x1, n2, w1, w2, nf)


def _t5_bucket(rel):
    half = N_BUCKETS // 2
    max_exact = half // 2
    base = jnp.where(rel > 0, half, 0)
    n = jnp.abs(rel)
    n_f = jnp.maximum(n, 1).astype(jnp.float32)
    large = max_exact + (jnp.log(n_f / max_exact) / math.log(MAX_DISTANCE / max_exact)
                         * (half - max_exact)).astype(jnp.int32)
    large = jnp.minimum(large, half - 1)
    return base + jnp.where(n < max_exact, n, large)


def _pick_tile(n, prefs):
    for t in prefs:
        if n % t == 0:
            return t
    raise ValueError(f"no tile in {prefs} divides {n}")


def kernel(x, norm1_w, w_in, conv_a_w, a_log, dt_bias, norm_a_w, rel_bias_table, w_gate, b_gate,
           w_proj_a, w_proj_b, w_out, norm2_w, w_ff1, w_ff2, norm_final_w):
    b_, t_, d = x.shape
    depth = norm1_w.shape[0]
    n = b_ * t_
    ha, hb = H_A * DK_A, H_B * D_HEAD_B
    assert t_ % KEY_TILE == 0 and t_ % GDN_CHUNK == 0 and d % LANE == 0
    assert DK_A == DV_A == D_HEAD_B == LANE and 2 * D_IDX == LANE
    assert Q_BLOCK >= MAX_DISTANCE

    o_ba = 4 * ha
    o_aa = o_ba + H_A
    o_qb = o_aa + H_A
    o_iq = o_qb + 3 * hb
    o_ik = o_iq + H_IDX * D_IDX
    o_iw = o_ik + D_IDX

    rel = (jnp.arange(2 * Q_BLOCK, dtype=jnp.int32)[None, :] - Q_BLOCK) \
        - jnp.arange(Q_BLOCK, dtype=jnp.int32)[:, None]
    far = jnp.full((Q_BLOCK, LANE), -(Q_BLOCK + 1), jnp.int32)
    bucket = _t5_bucket(jnp.concatenate([rel, far], axis=1))
    bias = _bias_tiles(rel_bias_table.astype(_F32), bucket)

    x2d = x.reshape(n, d)
    tn = _pick_tile(t_, (512, 256, 128))
    for layer in range(depth):
        wi = w_in[layer].astype(_BF16)
        zpad = lambda w: jnp.zeros((d, w), _BF16)
        ws = jnp.concatenate([wi[:, o_ba:o_qb], wi[:, o_iw:o_iw + H_IDX], zpad(LANE - 3 * H_A)], axis=1)
        ik = wi[:, o_ik:o_iw]
        wb16 = jnp.concatenate([wi[:, 0:o_ba], w_gate[layer].astype(_BF16), wi[:, o_qb:o_ik],
                                ik, zpad(D_IDX), zpad(D_IDX), ik], axis=1)
        nw = norm1_w[layer].reshape(1, d).astype(_F32)
        pb, ps = _norm_proj(x2d, nw, wb16, ws, conv_a_w[layer].astype(_F32), tn, t_ // tn, norm_cols=2 * ha)

        col_a = 0
        col_g = 4 * ha
        col_b = col_g + 2 * d
        assert col_g % (2 * d) == 0
        lanes = lambda v, off: jnp.zeros((1, LANE), _F32).at[0, off:off + H_A].set(v.astype(_F32))
        oa = _gdn(pb, ps, lanes(a_log[layer], H_A), lanes(dt_bias[layer], H_A),
                  norm_a_w[layer].reshape(1, DV_A).astype(_F32), b_, t_, col_a)
        ob = _dsa(pb, ps, bias, b_, t_, col_b)

        x2d = _merge(oa, ob, pb, x2d, b_gate[layer].reshape(1, 2 * d).astype(_F32),
                     w_proj_a[layer].astype(_BF16), w_proj_b[layer].astype(_BF16),
                     w_out[layer].astype(_BF16), tn, col_g)
        x2d = _mlp(x2d, norm2_w[layer].reshape(1, d).astype(_F32), w_ff1[layer].astype(_BF16),
                   w_ff2[layer].astype(_BF16), norm_final_w.reshape(1, d).astype(_F32),
                   _pick_tile(n, (512, 256, 128)), final_norm=layer == depth - 1)
    return x2d.reshape(b_, t_, d)
```

```python
import functools
import math

import jax
import jax.numpy as jnp
from jax import lax
from jax.experimental import pallas as pl
from jax.experimental.pallas import tpu as pltpu

EPS = 1e-6
H_A = 8
DK_A = 128
DV_A = 128
CONV_K = 4
GDN_CHUNK = 128
GDN_STEP_CHUNKS = 4
H_B = 8
D_HEAD_B = 128
H_IDX = 8
D_IDX = 64
TOPK_MAX = 256
Q_BLOCK = 128
CHUNK = 64
KEY_TILE = 512
N_BUCKETS = 32
MAX_DISTANCE = 128

LANE = 128
VMEM_LIMIT = 56 * 1024 * 1024

INT_MAX = 2**31 - 1
KEY_NEG_INF = -2139095041
MASK_NEG = -1e30
CODE_KEEP = -1
CODE_DROP = INT_MAX
TIE_BOUND_ALL = 2**30
LOG2E = math.log2(math.e)

_F32 = jnp.float32
_BF16 = jnp.bfloat16


def _dot(a, b):
    return jnp.dot(a, b, preferred_element_type=_F32)


def _dot_nt(a, b):
    return lax.dot_general(a, b, (((1,), (1,)), ((), ())), preferred_element_type=_F32)


def _sigmoid(x):
    return 1.0 / (1.0 + jnp.exp(-x))


def _rms(x, w):
    return x * lax.rsqrt(jnp.mean(x * x, axis=-1, keepdims=True) + EPS) * w


def _norm_proj_kernel(x_ref, nw_ref, w_ref, ws_ref, cw_ref, o_ref, os_ref, halo_s, work_s, *,
                      chunk, conv_cols, norm_cols, tiles_per_seq):
    tn = x_ref.shape[0]

    @pl.when(pl.program_id(0) == 0)
    def _():
        halo_s[...] = jnp.zeros_like(halo_s)

    h = _rms(x_ref[...], nw_ref[...]).astype(_BF16)
    os_ref[...] = _dot(h, ws_ref[...])
    seq_start = pl.program_id(0) % tiles_per_seq == 0
    heavy = [(ci, c0) for ci, c0 in enumerate(range(0, conv_cols, chunk))]
    plain = [(None, c0) for c0 in range(conv_cols, w_ref.shape[1], chunk)]
    per_heavy = -(-len(plain) // max(1, len(heavy)))
    order = []
    for i, hv in enumerate(heavy):
        order.append(hv)
        order.extend(plain[i * per_heavy:(i + 1) * per_heavy])
    order.extend(plain[len(heavy) * per_heavy:])
    if len(order) > 1 and order[-1][0] is not None:
        order[-1], order[-2] = order[-2], order[-1]
    for ci, c0 in order:
        y = _dot(h, w_ref[:, c0:c0 + chunk])
        if c0 < conv_cols:
            work_s[0:8, :] = jnp.where(seq_start, 0.0, halo_s[ci])
            work_s[8:8 + tn, :] = y
            halo_s[ci] = y[tn - 8:tn, :]
            acc = cw_ref[CONV_K - 1:CONV_K, c0:c0 + chunk] * y
            for j in range(CONV_K - 1):
                lo = 8 - (CONV_K - 1) + j
                acc = acc + cw_ref[j:j + 1, c0:c0 + chunk] * work_s[lo:lo + tn, :]
            y = acc * _sigmoid(acc)
            if c0 < norm_cols:
                parts = []
                for g0 in range(0, chunk, DK_A):
                    yh = y[:, g0:g0 + DK_A]
                    if c0 + g0 < norm_cols:
                        inv = lax.rsqrt(jnp.sum(yh * yh, axis=-1, keepdims=True) + EPS)
                        if c0 + g0 < norm_cols // 2:
                            inv = inv * (DK_A ** -0.5)
                        yh = yh * inv
                    parts.append(yh)
                y = jnp.concatenate(parts, axis=1)
        o_ref[:, c0:c0 + chunk] = y.astype(o_ref.dtype)


def _norm_proj(x2d, nw, w, ws, cw, tn, tiles_per_seq, norm_cols):
    n, d = x2d.shape
    c = w.shape[1]
    chunk = _pick_tile(c, (768, 512, 256, 128))
    conv_cols = cw.shape[1]
    assert conv_cols % chunk == 0 and tn >= 8
    assert norm_cols <= conv_cols and chunk % DK_A == 0 and (norm_cols // 2) % DK_A == 0
    const = functools.partial(pl.BlockSpec, pipeline_mode=pl.Buffered(1))
    return pl.pallas_call(
        functools.partial(_norm_proj_kernel, chunk=chunk, conv_cols=conv_cols, norm_cols=norm_cols,
                          tiles_per_seq=tiles_per_seq),
        grid=(n // tn,),
        in_specs=[
            pl.BlockSpec((tn, d), lambda i: (i, 0)),
            const((1, d), lambda i: (0, 0)),
            const((d, c), lambda i: (0, 0)),
            const((d, LANE), lambda i: (0, 0)),
            const(cw.shape, lambda i: (0, 0)),
        ],
        out_specs=[pl.BlockSpec((tn, c), lambda i: (i, 0)), pl.BlockSpec((tn, LANE), lambda i: (i, 0))],
        out_shape=[jax.ShapeDtypeStruct((n, c), _BF16), jax.ShapeDtypeStruct((n, LANE), _F32)],
        scratch_shapes=[
            pltpu.VMEM((conv_cols // chunk, 8, chunk), _F32),
            pltpu.VMEM((8 + tn, chunk), _F32),
        ],
        compiler_params=pltpu.CompilerParams(
            dimension_semantics=("arbitrary",), vmem_limit_bytes=VMEM_LIMIT),
        name="proj",
    )(x2d, nw, w, ws, cw)


def _gdn_kernel(qkvz_ref, sm_ref, arow_ref, dtrow_ref, naw_ref, o_ref,
                s_ref, kn_s, kb_s, qn_s, rhs_s, dec_s, qdec_s, ktt_s, eg_s, m_s, x_s):
    tt = qkvz_ref.shape[0]
    ct = GDN_CHUNK
    nc = tt // ct
    hd = H_A * DK_A

    @pl.when(pl.program_id(1) == 0)
    def _():
        s_ref[...] = jnp.zeros_like(s_ref)

    row = lax.broadcasted_iota(jnp.int32, (ct, ct), 0)
    col = lax.broadcasted_iota(jnp.int32, (ct, ct), 1)
    tri = (col <= row).astype(_F32)
    strict = col < row
    eye = (col == row).astype(_F32)
    n_sq = int(math.log2(ct))
    heads = range(H_A)

    def prepare(c):
        r0 = c * ct
        sm = sm_ref[r0:r0 + ct, :]
        beta_full = _sigmoid(sm)
        xg = sm + dtrow_ref[...]
        softplus = jnp.maximum(xg, 0.0) + jnp.log(1.0 + jnp.exp(-jnp.abs(xg)))
        g_full = -jnp.exp(arow_ref[...]) * softplus
        gcum = jnp.dot(tri, g_full, preferred_element_type=_F32, precision=lax.Precision.HIGHEST)
        gcum_t = gcum.T
        for h in heads:
            sl = slice(h * DK_A, (h + 1) * DK_A)
            qh, kh, vh = (qkvz_ref[r0:r0 + ct, idx * hd + h * DK_A:idx * hd + (h + 1) * DK_A].astype(_F32)
                          for idx in range(3))
            qn, kn = qh, kh
            bcol = beta_full[:, h:h + 1]
            gcol = gcum[:, H_A + h:H_A + h + 1]
            grow = gcum_t[H_A + h:H_A + h + 1, :]
            glast = gcum[ct - 1:ct, H_A + h:H_A + h + 1]
            kb = kn * bcol
            eg = jnp.exp(gcol)
            rhs_s[c, h, :, 0:DV_A] = vh * bcol
            rhs_s[c, h, :, DV_A:DV_A + DK_A] = kb * eg
            dec_s[c, h] = jnp.where(strict, jnp.exp(jnp.where(strict, gcol - grow, 0.0)), 0.0)
            qdec_s[c, h] = (qn * eg).astype(_BF16)
            ktt_s[c, h] = (kn * jnp.exp(glast - gcol)).T.astype(_BF16)
            kn_s[c, h] = kn.astype(_BF16)
            kb_s[c, h] = kb.astype(_BF16)
            qn_s[c, h] = qn.astype(_BF16)
            eg_s[c, h] = jnp.broadcast_to(jnp.exp(glast), (DK_A, DV_A))

    def solve(c):
        for h in heads:
            m = -(_dot_nt(kb_s[c, h], kn_s[c, h]) * dec_s[c, h])
            m_s[c, h] = m
            x_s[c, h] = m
        for _ in range(n_sq - 1):
            for h in heads:
                mb = m_s[c, h].astype(_BF16)
                m = _dot(mb, mb)
                m_s[c, h] = m
                x_s[c, h] = x_s[c, h] + m + _dot(x_s[c, h].astype(_BF16), m.astype(_BF16))
        for h in heads:
            rhs = rhs_s[c, h]
            rhs_s[c, h] = rhs + _dot(x_s[c, h].astype(_BF16), rhs.astype(_BF16))
        for h in heads:
            m_s[c, h] = _dot_nt(qn_s[c, h], kn_s[c, h]) * (dec_s[c, h] + eye)

    def recur(c):
        for h in heads:
            sb = s_ref[h].astype(_BF16)
            vnew = rhs_s[c, h, :, 0:DV_A] - _dot(rhs_s[c, h, :, DV_A:DV_A + DK_A].astype(_BF16), sb)
            vb = vnew.astype(_BF16)
            o = _dot(qdec_s[c, h], sb) + _dot(m_s[c, h].astype(_BF16), vb)
            s_ref[h] = s_ref[h] * eg_s[c, h] + _dot(ktt_s[c, h], vb)
            rhs_s[c, h, :, 0:DV_A] = o

    def finish(c):
        r0 = c * ct
        naw = naw_ref[...]
        for h in heads:
            sl = slice(h * DV_A, (h + 1) * DV_A)
            o = rhs_s[c, h, :, 0:DV_A]
            z = qkvz_ref[r0:r0 + ct, 3 * hd + h * DV_A:3 * hd + (h + 1) * DV_A].astype(_F32)
            on = o * lax.rsqrt(jnp.mean(o * o, axis=-1, keepdims=True) + EPS) * naw
            o_ref[r0:r0 + ct, sl] = (on * (z * _sigmoid(z))).astype(o_ref.dtype)

    for c in range(nc):
        prepare(c)
    for c in range(nc):
        solve(c)
    for c in range(nc):
        recur(c)
    for c in range(nc):
        finish(c)


def _gdn(pb, ps, arow, dtrow, naw, b_, t_, col0):
    ct = GDN_CHUNK
    nc = GDN_STEP_CHUNKS if t_ % (GDN_STEP_CHUNKS * ct) == 0 else 1
    tt = nc * ct
    hd = H_A * DK_A
    nt = t_ // tt
    assert col0 % (4 * hd) == 0
    cb = col0 // (4 * hd)
    const = lambda shape: pl.BlockSpec(shape, lambda b, t: (0, 0))
    return pl.pallas_call(
        _gdn_kernel,
        grid=(b_, nt),
        in_specs=[pl.BlockSpec((tt, 4 * hd), lambda b, t: (b * nt + t, cb)),
                  pl.BlockSpec((tt, LANE), lambda b, t: (b * nt + t, 0)),
                  const((1, LANE)), const((1, LANE)), const((1, DV_A))],
        out_specs=pl.BlockSpec((tt, hd), lambda b, t: (b * nt + t, 0)),
        out_shape=jax.ShapeDtypeStruct((b_ * t_, hd), _BF16),
        scratch_shapes=[
            pltpu.VMEM((H_A, DK_A, DV_A), _F32),
            pltpu.VMEM((nc, H_A, ct, DK_A), _BF16),
            pltpu.VMEM((nc, H_A, ct, DK_A), _BF16),
            pltpu.VMEM((nc, H_A, ct, DK_A), _BF16),
            pltpu.VMEM((nc, H_A, ct, DV_A + DK_A), _F32),
            pltpu.VMEM((nc, H_A, ct, ct), _F32),
            pltpu.VMEM((nc, H_A, ct, DK_A), _BF16),
            pltpu.VMEM((nc, H_A, DK_A, ct), _BF16),
            pltpu.VMEM((nc, H_A, DK_A, DV_A), _F32),
            pltpu.VMEM((nc, H_A, ct, ct), _F32),
            pltpu.VMEM((nc, H_A, ct, ct), _F32),
        ],
        compiler_params=pltpu.CompilerParams(
            dimension_semantics=("arbitrary", "arbitrary"), vmem_limit_bytes=VMEM_LIMIT),
        name="gdn",
    )(pb, ps, arow, dtrow, naw)


def _bias_kernel(tab_ref, bucket_ref, o_ref):
    bucket = bucket_ref[...]
    nq, w = bucket.shape
    for h in range(H_B):
        acc = jnp.zeros((nq, w), _F32)
        for b in range(N_BUCKETS):
            acc = acc + jnp.where(bucket == b, tab_ref[b, h], 0.0)
        far = acc[:, 2 * LANE:3 * LANE]
        o_ref[h] = (acc[:, 0:2 * LANE] - jnp.concatenate([far, far], axis=1)) * LOG2E


def _bias_tiles(rel_table, bucket):
    return pl.pallas_call(
        _bias_kernel,
        in_specs=[pl.BlockSpec(memory_space=pltpu.SMEM),
                  pl.BlockSpec(bucket.shape, lambda: (0, 0))],
        out_specs=pl.BlockSpec((H_B, Q_BLOCK, 2 * LANE), lambda: (0, 0, 0)),
        out_shape=jax.ShapeDtypeStruct((H_B, Q_BLOCK, 2 * LANE), _F32),
        name="rel_bias_tiles",
    )(rel_table, bucket)


def _dsa_kernel(q_ref, k_ref, v_ref, iq_ref, ik_ref, sm_ref, bias_ref, o_ref,
                key_s, qs_s, iwb_s, m_s, l_s, acc_s, s_s, si_s, sf_s, *, topk):
    nq = q_ref.shape[0]
    st_w = KEY_TILE
    assert nq == st_w
    ng = st_w // LANE
    qb = pl.program_id(1)
    n_st = qb + 1
    hd = D_HEAD_B
    scale = D_HEAD_B ** -0.5 * LOG2E
    f_topk = float(topk)

    lane_j = lax.broadcasted_iota(jnp.int32, (nq, LANE), 1)
    row_i = lax.broadcasted_iota(jnp.int32, (nq, LANE), 0)

    def grp(g):
        return slice(g * LANE, (g + 1) * LANE)

    iw = sm_ref[...]
    for h in range(H_B):
        sl = slice(h * hd, (h + 1) * hd)
        qs_s[h] = (q_ref[:, sl].astype(_F32) * scale).astype(_BF16)
        iwb_s[h] = jnp.broadcast_to(iw[:, 2 * H_A + h:2 * H_A + h + 1], (nq, LANE))
        m_s[h] = jnp.full((nq, LANE), MASK_NEG, _F32)
        l_s[h] = jnp.zeros((nq, LANE), _F32)
        acc_s[h] = jnp.zeros((nq, hd), _F32)

    def scores(st, diag):
        ikt = ik_ref[pl.ds(pl.multiple_of(st * st_w, st_w), st_w), :]
        acc = [jnp.zeros((nq, LANE), _F32) for _ in range(ng)]
        for p in range(H_IDX // 2):
            xq = iq_ref[:, p * LANE:(p + 1) * LANE]
            for half in range(2):
                h = 2 * p + half
                s = jnp.maximum(_dot_nt(xq, ikt[:, half * LANE:(half + 1) * LANE]), 0.0)
                w = iwb_s[h]
                for g in range(ng):
                    acc[g] = acc[g] + w * s[:, grp(g)]
        for g in range(ng):
            a = acc[g]
            if diag:
                a = jnp.where((g * LANE + lane_j) // CHUNK <= row_i // CHUNK, a, -jnp.inf)
            bits = pltpu.bitcast(a, jnp.int32)
            key_s[st, g] = bits ^ ((bits >> 31) & INT_MAX)

    def p1(st, carry):
        scores(st, False)
        return carry

    lax.fori_loop(0, qb, p1, 0)
    scores(qb, True)

    ones_b = jnp.ones((LANE, LANE), _BF16)
    assert key_s.shape[0] * ng <= 256

    T_, CAND_, MIDX_ = 0, 1, 2
    ACC_, CNT_ = 0, 1
    rb = 64
    lane_rb = lax.broadcasted_iota(jnp.int32, (rb, LANE), 1)

    def count(pred_fn):
        def tile(st, first):
            for r0 in range(0, nq, rb):
                rs = slice(r0, r0 + rb)
                acc = None if first else sf_s[ACC_, rs, :]
                for g in range(ng):
                    one = jnp.where(pred_fn(key_s[st, g, rs, :], rs), 1.0, 0.0)
                    acc = one if acc is None else acc + one
                sf_s[ACC_, rs, :] = acc

        def body(st, carry):
            tile(st, False)
            return carry

        tile(0, True)
        lax.fori_loop(1, n_st, body, 0)
        return _dot(sf_s[ACC_].astype(_BF16), ones_b)

    si_s[CAND_] = jnp.zeros((nq, LANE), jnp.int32)

    def bit_body(i, carry):
        half = jnp.left_shift(jnp.int32(1), 30 - i)
        c = count(lambda k, rs: k >= si_s[CAND_, rs, :])
        si_s[CAND_] = si_s[CAND_] + jnp.where(c >= f_topk, half, -half)
        return carry

    lax.fori_loop(0, 31, bit_body, 0)
    c_last = count(lambda k, rs: k >= si_s[CAND_, rs, :])
    si_s[T_] = jnp.where(c_last >= f_topk, si_s[CAND_], si_s[CAND_] - 1)
    sf_s[CNT_] = count(lambda k, rs: k >= si_s[T_, rs, :])

    def encode(st, carry):
        for r0 in range(0, nq, rb):
            rs = slice(r0, r0 + rb)
            t = si_s[T_, rs, :]
            for g in range(ng):
                k = key_s[st, g, rs, :]
                pos = st * st_w + g * LANE + lane_rb
                key_s[st, g, rs, :] = jnp.where(k > t, CODE_KEEP, jnp.where(k == t, pos, CODE_DROP))
        return carry

    lax.fori_loop(0, n_st, encode, 0)
    finite_thr = si_s[T_] != KEY_NEG_INF
    si_s[MIDX_] = jnp.where(finite_thr, TIE_BOUND_ALL, 0)

    excess = jnp.max(jnp.where(finite_thr, sf_s[CNT_], 0.0)) > f_topk

    @pl.when(excess)
    def _():
        r_keep = f_topk - count(lambda code, rs: code < 0)
        sf_s[CNT_] = jnp.where(finite_thr, r_keep, 0.0)
        sf_s[ACC_] = jnp.zeros((nq, LANE), _F32)
        upper = (lax.broadcasted_iota(jnp.int32, (LANE, LANE), 0)
                 <= lax.broadcasted_iota(jnp.int32, (LANE, LANE), 1)).astype(_BF16)

        def rank_ties(st, carry):
            for r0 in range(0, nq, rb):
                rs = slice(r0, r0 + rb)
                r_blk = sf_s[CNT_, rs, :]
                before = sf_s[ACC_, rs, :]
                for g in range(ng):
                    code = key_s[st, g, rs, :]
                    tie = jnp.where(code >= 0, jnp.where(code < CODE_DROP, 1.0, 0.0), 0.0)
                    tie_b = tie.astype(_BF16)
                    rank = before + _dot(tie_b, upper)
                    kept_tie = jnp.where(rank <= r_blk, tie, 0.0)
                    mb = jnp.where(code < 0, 0.0, jnp.where(kept_tie > 0.0, 0.0, MASK_NEG))
                    key_s[st, g, rs, :] = pltpu.bitcast(mb, jnp.int32)
                    before = before + _dot(tie_b, ones_b)
                sf_s[ACC_, rs, :] = before
            return carry

        lax.fori_loop(0, n_st, rank_ties, 0)

    @pl.when(jnp.logical_not(excess))
    def _():
        def p2(st, carry):
            for r0 in range(0, nq, rb):
                rs = slice(r0, r0 + rb)
                bound = si_s[MIDX_, rs, :]
                for g in range(ng):
                    mb = jnp.where(key_s[st, g, rs, :] < bound, 0.0, MASK_NEG)
                    key_s[st, g, rs, :] = pltpu.bitcast(mb, jnp.int32)
            return carry

        lax.fori_loop(0, n_st, p2, 0)

    def attend(st, mode):
        off = pl.multiple_of(st * st_w, st_w)
        for h in range(H_B):
            sl = slice(h * hd, (h + 1) * hd)
            mask = jnp.concatenate([pltpu.bitcast(key_s[st, g], _F32) for g in range(ng)], axis=1)
            s = _dot_nt(qs_s[h], k_ref[pl.ds(off, st_w), sl]) + mask
            if mode:
                s_s[...] = s
                b_prev = bias_ref[h, :, 0:LANE]
                if mode == 1:
                    s_s[0:LANE, (ng - 1) * LANE:ng * LANE] += b_prev
                else:
                    b_diag = bias_ref[h, :, LANE:2 * LANE]
                    for rt in range(ng):
                        s_s[grp(rt), grp(rt)] += b_diag
                        if rt:
                            s_s[grp(rt), grp(rt - 1)] += b_prev
                s = s_s[...]
            m_old = m_s[h]
            gmax = s[:, grp(0)]
            for g in range(1, ng):
                gmax = jnp.maximum(gmax, s[:, grp(g)])
            m_new = jnp.maximum(m_old, jnp.max(gmax, axis=1, keepdims=True))
            alpha = jnp.exp2(m_old - m_new)
            pieces = [jnp.exp2(s[:, grp(g)] - m_new) for g in range(ng)]
            psum = pieces[0]
            for g in range(1, ng):
                psum = psum + pieces[g]
            p = jnp.concatenate(pieces, axis=1).astype(_BF16)
            m_s[h] = m_new
            l_s[h] = alpha * l_s[h] + psum
            acc_s[h] = alpha * acc_s[h] + _dot(p, v_ref[pl.ds(off, st_w), sl])

    def p3(st, carry):
        attend(st, 0)
        return carry

    lax.fori_loop(0, qb - 1, p3, 0)

    @pl.when(qb > 0)
    def _():
        attend(qb - 1, 1)

    attend(qb, 2)

    for h in range(H_B):
        sl = slice(h * hd, (h + 1) * hd)
        l_fin = jnp.sum(l_s[h], axis=1, keepdims=True)
        o_ref[:, sl] = (acc_s[h] / l_fin).astype(o_ref.dtype)


def _dsa(pb, ps, bias, b_, t_, col0):
    nq = KEY_TILE
    nb = t_ // nq
    hd = H_B * D_HEAD_B
    topk = min(TOPK_MAX, t_ // 4)
    iq_w = H_IDX * D_IDX
    ik_w = 2 * LANE
    assert col0 % hd == 0 and (col0 + 3 * hd) % iq_w == 0 and (col0 + 3 * hd + iq_w) % ik_w == 0
    cq = col0 // hd
    resident = functools.partial(pl.BlockSpec, pipeline_mode=pl.Buffered(1))
    return pl.pallas_call(
        functools.partial(_dsa_kernel, topk=topk),
        grid=(b_, nb),
        in_specs=[
            pl.BlockSpec((nq, hd), lambda b, i: (b * nb + i, cq)),
            resident((t_, hd), lambda b, i: (b, cq + 1)),
            resident((t_, hd), lambda b, i: (b, cq + 2)),
            pl.BlockSpec((nq, iq_w), lambda b, i: (b * nb + i, (col0 + 3 * hd) // iq_w)),
            resident((t_, ik_w), lambda b, i: (b, (col0 + 3 * hd + iq_w) // ik_w)),
            pl.BlockSpec((nq, LANE), lambda b, i: (b * nb + i, 0)),
            resident((H_B, Q_BLOCK, 2 * LANE), lambda b, i: (0, 0, 0)),
        ],
        out_specs=pl.BlockSpec((nq, hd), lambda b, i: (b * nb + i, 0)),
        out_shape=jax.ShapeDtypeStruct((b_ * t_, hd), _BF16),
        scratch_shapes=[
            pltpu.VMEM((nb, KEY_TILE // LANE, nq, LANE), jnp.int32),
            pltpu.VMEM((H_B, nq, D_HEAD_B), _BF16),
            pltpu.VMEM((H_B, nq, LANE), _F32),
            pltpu.VMEM((H_B, nq, LANE), _F32),
            pltpu.VMEM((H_B, nq, LANE), _F32),
            pltpu.VMEM((H_B, nq, D_HEAD_B), _F32),
            pltpu.VMEM((nq, KEY_TILE), _F32),
            pltpu.VMEM((3, nq, LANE), jnp.int32),
            pltpu.VMEM((2, nq, LANE), _F32),
        ],
        compiler_params=pltpu.CompilerParams(
            dimension_semantics=("arbitrary", "arbitrary"), vmem_limit_bytes=VMEM_LIMIT),
        name="dsa",
    )(pb, pb, pb, pb, pb, ps, bias)


def _merge_kernel(oa_ref, ob_ref, g_ref, x_ref, bg_ref, wa_ref, wb_ref, wo_ref, o_ref):
    d = x_ref.shape[1]
    a = _dot(oa_ref[...], wa_ref[...])
    b = _dot(ob_ref[...], wb_ref[...])
    g = _sigmoid(g_ref[...].astype(_F32) + bg_ref[...])
    merged = g[:, 0:d] * a + g[:, d:2 * d] * b
    o_ref[...] = x_ref[...] + _dot(merged.astype(_BF16), wo_ref[...])


def _merge(oa, ob, pb, x2d, bg, wa, wb, wo, tn, col_g):
    n, d = x2d.shape
    row = lambda w: pl.BlockSpec((tn, w), lambda i: (i, 0))
    const = lambda a: pl.BlockSpec(a.shape, lambda i: (0, 0))
    return pl.pallas_call(
        _merge_kernel,
        grid=(n // tn,),
        in_specs=[row(oa.shape[1]), row(ob.shape[1]),
                  pl.BlockSpec((tn, 2 * d), lambda i: (i, col_g // (2 * d))), row(d),
                  const(bg), const(wa), const(wb), const(wo)],
        out_specs=row(d),
        out_shape=jax.ShapeDtypeStruct((n, d), _F32),
        compiler_params=pltpu.CompilerParams(
            dimension_semantics=("arbitrary",), vmem_limit_bytes=VMEM_LIMIT),
        name="merge",
    )(oa, ob, pb, x2d, bg, wa, wb, wo)


def _mlp_kernel(x_ref, n2_ref, w1_ref, w2_ref, nf_ref, o_ref, *, final_norm):
    x = x_ref[...]
    h2 = _rms(x, n2_ref[...]).astype(_BF16)
    hid = jnp.maximum(_dot(h2, w1_ref[...]), 0.0)
    y = x + _dot((hid * hid).astype(_BF16), w2_ref[...])
    o_ref[...] = _rms(y, nf_ref[...]) if final_norm else y


def _mlp(x1, n2, w1, w2, nf, tn, final_norm):
    n, d = x1.shape
    row = pl.BlockSpec((tn, d), lambda i: (i, 0))
    const = lambda a: pl.BlockSpec(a.shape, lambda i: (0, 0), pipeline_mode=pl.Buffered(1))
    return pl.pallas_call(
        functools.partial(_mlp_kernel, final_norm=final_norm),
        grid=(n // tn,),
        in_specs=[row, const(n2), const(w1), const(w2), const(nf)],
        out_specs=row,
        out_shape=jax.ShapeDtypeStruct((n, d), _F32),
        compiler_params=pltpu.CompilerParams(
            dimension_semantics=("arbitrary",), vmem_limit_bytes=VMEM_LIMIT),
        name="mlp",
    )(x1, n2, w1, w2, nf)


def _t5_bucket(rel):
    half = N_BUCKETS // 2
    max_exact = half // 2
    base = jnp.where(rel > 0, half, 0)
    n = jnp.abs(rel)
    n_f = jnp.maximum(n, 1).astype(jnp.float32)
    large = max_exact + (jnp.log(n_f / max_exact) / math.log(MAX_DISTANCE / max_exact)
                         * (half - max_exact)).astype(jnp.int32)
    large = jnp.minimum(large, half - 1)
    return base + jnp.where(n < max_exact, n, large)


def _pick_tile(n, prefs):
    for t in prefs:
        if n % t == 0:
            return t
    raise ValueError(f"no tile in {prefs} divides {n}")


def kernel(x, norm1_w, w_in, conv_a_w, a_log, dt_bias, norm_a_w, rel_bias_table, w_gate, b_gate,
           w_proj_a, w_proj_b, w_out, norm2_w, w_ff1, w_ff2, norm_final_w):
    b_, t_, d = x.shape
    depth = norm1_w.shape[0]
    n = b_ * t_
    ha, hb = H_A * DK_A, H_B * D_HEAD_B
    assert t_ % KEY_TILE == 0 and t_ % GDN_CHUNK == 0 and d % LANE == 0
    assert DK_A == DV_A == D_HEAD_B == LANE and 2 * D_IDX == LANE
    assert Q_BLOCK >= MAX_DISTANCE

    o_ba = 4 * ha
    o_aa = o_ba + H_A
    o_qb = o_aa + H_A
    o_iq = o_qb + 3 * hb
    o_ik = o_iq + H_IDX * D_IDX
    o_iw = o_ik + D_IDX

    rel = (jnp.arange(2 * Q_BLOCK, dtype=jnp.int32)[None, :] - Q_BLOCK) \
        - jnp.arange(Q_BLOCK, dtype=jnp.int32)[:, None]
    far = jnp.full((Q_BLOCK, LANE), -(Q_BLOCK + 1), jnp.int32)
    bucket = _t5_bucket(jnp.concatenate([rel, far], axis=1))
    bias = _bias_tiles(rel_bias_table.astype(_F32), bucket)

    x2d = x.reshape(n, d)
    tn = _pick_tile(t_, (512, 256, 128))
    for layer in range(depth):
        wi = w_in[layer].astype(_BF16)
        zpad = lambda w: jnp.zeros((d, w), _BF16)
        ws = jnp.concatenate([wi[:, o_ba:o_qb], wi[:, o_iw:o_iw + H_IDX], zpad(LANE - 3 * H_A)], axis=1)
        ik = wi[:, o_ik:o_iw]
        wb16 = jnp.concatenate([wi[:, 0:o_ba], w_gate[layer].astype(_BF16), wi[:, o_qb:o_ik],
                                ik, zpad(D_IDX), zpad(D_IDX), ik], axis=1)
        nw = norm1_w[layer].reshape(1, d).astype(_F32)
        pb, ps = _norm_proj(x2d, nw, wb16, ws, conv_a_w[layer].astype(_F32), tn, t_ // tn, norm_cols=2 * ha)

        col_a = 0
        col_g = 4 * ha
        col_b = col_g + 2 * d
        assert col_g % (2 * d) == 0
        lanes = lambda v, off: jnp.zeros((1, LANE), _F32).at[0, off:off + H_A].set(v.astype(_F32))
        oa = _gdn(pb, ps, lanes(a_log[layer], H_A), lanes(dt_bias[layer], H_A),
                  norm_a_w[layer].reshape(1, DV_A).astype(_F32), b_, t_, col_a)
        ob = _dsa(pb, ps, bias, b_, t_, col_b)

        x2d = _merge(oa, ob, pb, x2d, b_gate[layer].reshape(1, 2 * d).astype(_F32),
                     w_proj_a[layer].astype(_BF16), w_proj_b[layer].astype(_BF16),
                     w_out[layer].astype(_BF16), tn, col_g)
        x2d = _mlp(x2d, norm2_w[layer].reshape(1, d).astype(_F32), w_ff1[layer].astype(_BF16),
                   w_ff2[layer].astype(_BF16), norm_final_w.reshape(1, d).astype(_F32),
                   _pick_tile(n, (512, 256, 128)), final_norm=layer == depth - 1)
    return x2d.reshape(b_, t_, d)
```

```python
import functools
import math

import jax
import jax.numpy as jnp
from jax import lax
from jax.experimental import pallas as pl
from jax.experimental.pallas import tpu as pltpu

EPS = 1e-6
H_A = 8
DK_A = 128
DV_A = 128
CONV_K = 4
GDN_CHUNK = 128
GDN_STEP_CHUNKS = 4
H_B = 8
D_HEAD_B = 128
H_IDX = 8
D_IDX = 64
TOPK_MAX = 256
Q_BLOCK = 128
CHUNK = 64
KEY_TILE = 512
N_BUCKETS = 32
MAX_DISTANCE = 128

LANE = 128
VMEM_LIMIT = 56 * 1024 * 1024

INT_MAX = 2**31 - 1
KEY_NEG_INF = -2139095041
MASK_NEG = -1e30
CODE_KEEP = -1
CODE_DROP = INT_MAX
TIE_BOUND_ALL = 2**30
LOG2E = math.log2(math.e)

_F32 = jnp.float32
_BF16 = jnp.bfloat16


def _dot(a, b):
    return jnp.dot(a, b, preferred_element_type=_F32)


def _dot_nt(a, b):
    return lax.dot_general(a, b, (((1,), (1,)), ((), ())), preferred_element_type=_F32)


def _sigmoid(x):
    return 1.0 / (1.0 + jnp.exp(-x))


def _rms(x, w):
    return x * lax.rsqrt(jnp.mean(x * x, axis=-1, keepdims=True) + EPS) * w


def _norm_proj_kernel(x_ref, nw_ref, w_ref, ws_ref, cw_ref, o_ref, os_ref, halo_s, work_s, *,
                      chunk, conv_cols, norm_cols, tiles_per_seq):
    tn = x_ref.shape[0]

    @pl.when(pl.program_id(0) == 0)
    def _():
        halo_s[...] = jnp.zeros_like(halo_s)

    h = _rms(x_ref[...], nw_ref[...]).astype(_BF16)
    os_ref[...] = _dot(h, ws_ref[...])
    seq_start = pl.program_id(0) % tiles_per_seq == 0
    heavy = [(ci, c0) for ci, c0 in enumerate(range(0, conv_cols, chunk))]
    plain = [(None, c0) for c0 in range(conv_cols, w_ref.shape[1], chunk)]
    per_heavy = -(-len(plain) // max(1, len(heavy)))
    order = []
    for i, hv in enumerate(heavy):
        order.append(hv)
        order.extend(plain[i * per_heavy:(i + 1) * per_heavy])
    order.extend(plain[len(heavy) * per_heavy:])
    for ci, c0 in order:
        y = _dot(h, w_ref[:, c0:c0 + chunk])
        if c0 < conv_cols:
            work_s[0:8, :] = jnp.where(seq_start, 0.0, halo_s[ci])
            work_s[8:8 + tn, :] = y
            halo_s[ci] = y[tn - 8:tn, :]
            acc = cw_ref[CONV_K - 1:CONV_K, c0:c0 + chunk] * y
            for j in range(CONV_K - 1):
                lo = 8 - (CONV_K - 1) + j
                acc = acc + cw_ref[j:j + 1, c0:c0 + chunk] * work_s[lo:lo + tn, :]
            y = acc * _sigmoid(acc)
            if c0 < norm_cols:
                parts = []
                for g0 in range(0, chunk, DK_A):
                    yh = y[:, g0:g0 + DK_A]
                    if c0 + g0 < norm_cols:
                        inv = lax.rsqrt(jnp.sum(yh * yh, axis=-1, keepdims=True) + EPS)
                        if c0 + g0 < norm_cols // 2:
                            inv = inv * (DK_A ** -0.5)
                        yh = yh * inv
                    parts.append(yh)
                y = jnp.concatenate(parts, axis=1)
        o_ref[:, c0:c0 + chunk] = y.astype(o_ref.dtype)


def _norm_proj(x2d, nw, w, ws, cw, tn, tiles_per_seq, norm_cols):
    n, d = x2d.shape
    c = w.shape[1]
    chunk = _pick_tile(c, (768, 512, 256, 128))
    conv_cols = cw.shape[1]
    assert conv_cols % chunk == 0 and tn >= 8
    assert norm_cols <= conv_cols and chunk % DK_A == 0 and (norm_cols // 2) % DK_A == 0
    const = functools.partial(pl.BlockSpec, pipeline_mode=pl.Buffered(1))
    return pl.pallas_call(
        functools.partial(_norm_proj_kernel, chunk=chunk, conv_cols=conv_cols, norm_cols=norm_cols,
                          tiles_per_seq=tiles_per_seq),
        grid=(n // tn,),
        in_specs=[
            pl.BlockSpec((tn, d), lambda i: (i, 0)),
            const((1, d), lambda i: (0, 0)),
            const((d, c), lambda i: (0, 0)),
            const((d, LANE), lambda i: (0, 0)),
            const(cw.shape, lambda i: (0, 0)),
        ],
        out_specs=[pl.BlockSpec((tn, c), lambda i: (i, 0)), pl.BlockSpec((tn, LANE), lambda i: (i, 0))],
        out_shape=[jax.ShapeDtypeStruct((n, c), _BF16), jax.ShapeDtypeStruct((n, LANE), _F32)],
        scratch_shapes=[
            pltpu.VMEM((conv_cols // chunk, 8, chunk), _F32),
            pltpu.VMEM((8 + tn, chunk), _F32),
        ],
        compiler_params=pltpu.CompilerParams(
            dimension_semantics=("arbitrary",), vmem_limit_bytes=VMEM_LIMIT),
        name="proj",
    )(x2d, nw, w, ws, cw)


def _gdn_kernel(qkvz_ref, sm_ref, arow_ref, dtrow_ref, naw_ref, o_ref,
                s_ref, kn_s, kb_s, qn_s, rhs_s, dec_s, qdec_s, ktt_s, eg_s, m_s, x_s):
    tt = qkvz_ref.shape[0]
    ct = GDN_CHUNK
    nc = tt // ct
    hd = H_A * DK_A

    @pl.when(pl.program_id(1) == 0)
    def _():
        s_ref[...] = jnp.zeros_like(s_ref)

    row = lax.broadcasted_iota(jnp.int32, (ct, ct), 0)
    col = lax.broadcasted_iota(jnp.int32, (ct, ct), 1)
    tri = (col <= row).astype(_F32)
    strict = col < row
    eye = (col == row).astype(_F32)
    n_sq = int(math.log2(ct))
    heads = range(H_A)

    def prepare(c):
        r0 = c * ct
        sm = sm_ref[r0:r0 + ct, :]
        beta_full = _sigmoid(sm)
        xg = sm + dtrow_ref[...]
        softplus = jnp.maximum(xg, 0.0) + jnp.log(1.0 + jnp.exp(-jnp.abs(xg)))
        g_full = -jnp.exp(arow_ref[...]) * softplus
        gcum = jnp.dot(tri, g_full, preferred_element_type=_F32, precision=lax.Precision.HIGHEST)
        gcum_t = gcum.T
        for h in heads:
            sl = slice(h * DK_A, (h + 1) * DK_A)
            qh, kh, vh = (qkvz_ref[r0:r0 + ct, idx * hd + h * DK_A:idx * hd + (h + 1) * DK_A].astype(_F32)
                          for idx in range(3))
            qn, kn = qh, kh
            bcol = beta_full[:, h:h + 1]
            gcol = gcum[:, H_A + h:H_A + h + 1]
            grow = gcum_t[H_A + h:H_A + h + 1, :]
            glast = gcum[ct - 1:ct, H_A + h:H_A + h + 1]
            kb = kn * bcol
            eg = jnp.exp(gcol)
            rhs_s[c, h, :, 0:DV_A] = vh * bcol
            rhs_s[c, h, :, DV_A:DV_A + DK_A] = kb * eg
            dec_s[c, h] = jnp.where(strict, jnp.exp(jnp.where(strict, gcol - grow, 0.0)), 0.0)
            qdec_s[c, h] = (qn * eg).astype(_BF16)
            ktt_s[c, h] = (kn * jnp.exp(glast - gcol)).T.astype(_BF16)
            kn_s[c, h] = kn.astype(_BF16)
            kb_s[c, h] = kb.astype(_BF16)
            qn_s[c, h] = qn.astype(_BF16)
            eg_s[c, h] = jnp.broadcast_to(jnp.exp(glast), (DK_A, DV_A))

    def solve(c):
        for h in heads:
            m = -(_dot_nt(kb_s[c, h], kn_s[c, h]) * dec_s[c, h])
            m_s[c, h] = m
            x_s[c, h] = m
        for _ in range(n_sq - 1):
            for h in heads:
                mb = m_s[c, h].astype(_BF16)
                m = _dot(mb, mb)
                m_s[c, h] = m
                x_s[c, h] = x_s[c, h] + m + _dot(x_s[c, h].astype(_BF16), m.astype(_BF16))
        for h in heads:
            rhs = rhs_s[c, h]
            rhs_s[c, h] = rhs + _dot(x_s[c, h].astype(_BF16), rhs.astype(_BF16))
        for h in heads:
            m_s[c, h] = _dot_nt(qn_s[c, h], kn_s[c, h]) * (dec_s[c, h] + eye)

    def recur(c):
        for h in heads:
            sb = s_ref[h].astype(_BF16)
            vnew = rhs_s[c, h, :, 0:DV_A] - _dot(rhs_s[c, h, :, DV_A:DV_A + DK_A].astype(_BF16), sb)
            vb = vnew.astype(_BF16)
            o = _dot(qdec_s[c, h], sb) + _dot(m_s[c, h].astype(_BF16), vb)
            s_ref[h] = s_ref[h] * eg_s[c, h] + _dot(ktt_s[c, h], vb)
            rhs_s[c, h, :, 0:DV_A] = o

    def finish(c):
        r0 = c * ct
        naw = naw_ref[...]
        for h in heads:
            sl = slice(h * DV_A, (h + 1) * DV_A)
            o = rhs_s[c, h, :, 0:DV_A]
            z = qkvz_ref[r0:r0 + ct, 3 * hd + h * DV_A:3 * hd + (h + 1) * DV_A].astype(_F32)
            on = o * lax.rsqrt(jnp.mean(o * o, axis=-1, keepdims=True) + EPS) * naw
            o_ref[r0:r0 + ct, sl] = (on * (z * _sigmoid(z))).astype(o_ref.dtype)

    for c in range(nc):
        prepare(c)
    for c in range(nc):
        solve(c)
    for c in range(nc):
        recur(c)
    for c in range(nc):
        finish(c)


def _gdn(pb, ps, arow, dtrow, naw, b_, t_, col0):
    ct = GDN_CHUNK
    nc = GDN_STEP_CHUNKS if t_ % (GDN_STEP_CHUNKS * ct) == 0 else 1
    tt = nc * ct
    hd = H_A * DK_A
    nt = t_ // tt
    assert col0 % (4 * hd) == 0
    cb = col0 // (4 * hd)
    const = lambda shape: pl.BlockSpec(shape, lambda b, t: (0, 0))
    return pl.pallas_call(
        _gdn_kernel,
        grid=(b_, nt),
        in_specs=[pl.BlockSpec((tt, 4 * hd), lambda b, t: (b * nt + t, cb)),
                  pl.BlockSpec((tt, LANE), lambda b, t: (b * nt + t, 0)),
                  const((1, LANE)), const((1, LANE)), const((1, DV_A))],
        out_specs=pl.BlockSpec((tt, hd), lambda b, t: (b * nt + t, 0)),
        out_shape=jax.ShapeDtypeStruct((b_ * t_, hd), _BF16),
        scratch_shapes=[
            pltpu.VMEM((H_A, DK_A, DV_A), _F32),
            pltpu.VMEM((nc, H_A, ct, DK_A), _BF16),
            pltpu.VMEM((nc, H_A, ct, DK_A), _BF16),
            pltpu.VMEM((nc, H_A, ct, DK_A), _BF16),
            pltpu.VMEM((nc, H_A, ct, DV_A + DK_A), _F32),
            pltpu.VMEM((nc, H_A, ct, ct), _F32),
            pltpu.VMEM((nc, H_A, ct, DK_A), _BF16),
            pltpu.VMEM((nc, H_A, DK_A, ct), _BF16),
            pltpu.VMEM((nc, H_A, DK_A, DV_A), _F32),
            pltpu.VMEM((nc, H_A, ct, ct), _F32),
            pltpu.VMEM((nc, H_A, ct, ct), _F32),
        ],
        compiler_params=pltpu.CompilerParams(
            dimension_semantics=("arbitrary", "arbitrary"), vmem_limit_bytes=VMEM_LIMIT),
        name="gdn",
    )(pb, ps, arow, dtrow, naw)


def _bias_kernel(tab_ref, bucket_ref, o_ref):
    bucket = bucket_ref[...]
    nq, w = bucket.shape
    for h in range(H_B):
        acc = jnp.zeros((nq, w), _F32)
        for b in range(N_BUCKETS):
            acc = acc + jnp.where(bucket == b, tab_ref[b, h], 0.0)
        far = acc[:, 2 * LANE:3 * LANE]
        o_ref[h] = (acc[:, 0:2 * LANE] - jnp.concatenate([far, far], axis=1)) * LOG2E


def _bias_tiles(rel_table, bucket):
    return pl.pallas_call(
        _bias_kernel,
        in_specs=[pl.BlockSpec(memory_space=pltpu.SMEM),
                  pl.BlockSpec(bucket.shape, lambda: (0, 0))],
        out_specs=pl.BlockSpec((H_B, Q_BLOCK, 2 * LANE), lambda: (0, 0, 0)),
        out_shape=jax.ShapeDtypeStruct((H_B, Q_BLOCK, 2 * LANE), _F32),
        name="rel_bias_tiles",
    )(rel_table, bucket)


def _dsa_kernel(q_ref, k_ref, v_ref, iq_ref, ik_ref, sm_ref, bias_ref, o_ref,
                key_s, qs_s, iwb_s, m_s, acc_s, s_s, si_s, sf_s, *, topk):
    nq = q_ref.shape[0]
    st_w = KEY_TILE
    assert nq == st_w
    ng = st_w // LANE
    qb = pl.program_id(1)
    n_st = qb + 1
    hd = D_HEAD_B
    scale = D_HEAD_B ** -0.5 * LOG2E
    f_topk = float(topk)

    lane_j = lax.broadcasted_iota(jnp.int32, (nq, LANE), 1)
    row_i = lax.broadcasted_iota(jnp.int32, (nq, LANE), 0)

    def grp(g):
        return slice(g * LANE, (g + 1) * LANE)

    iw = sm_ref[...]
    for h in range(H_B):
        sl = slice(h * hd, (h + 1) * hd)
        qs_s[h] = (q_ref[:, sl].astype(_F32) * scale).astype(_BF16)
        iwb_s[h] = jnp.broadcast_to(iw[:, 2 * H_A + h:2 * H_A + h + 1], (nq, LANE))
        m_s[h] = jnp.full((nq, LANE), MASK_NEG, _F32)
        acc_s[h] = jnp.zeros((nq, 2 * hd), _F32)

    def scores(st, diag):
        ikt = ik_ref[pl.ds(pl.multiple_of(st * st_w, st_w), st_w), :]
        acc = [jnp.zeros((nq, LANE), _F32) for _ in range(ng)]
        for p in range(H_IDX // 2):
            xq = iq_ref[:, p * LANE:(p + 1) * LANE]
            for half in range(2):
                h = 2 * p + half
                s = jnp.maximum(_dot_nt(xq, ikt[:, half * LANE:(half + 1) * LANE]), 0.0)
                w = iwb_s[h]
                for g in range(ng):
                    acc[g] = acc[g] + w * s[:, grp(g)]
        for g in range(ng):
            a = acc[g]
            if diag:
                a = jnp.where((g * LANE + lane_j) // CHUNK <= row_i // CHUNK, a, -jnp.inf)
            bits = pltpu.bitcast(a, jnp.int32)
            key_s[st, g] = bits ^ ((bits >> 31) & INT_MAX)

    def p1(st, carry):
        scores(st, False)
        return carry

    lax.fori_loop(0, qb, p1, 0)
    scores(qb, True)

    ones_b = jnp.ones((LANE, LANE), _BF16)
    assert key_s.shape[0] * ng <= 256

    T_, CAND_, MIDX_ = 0, 1, 2
    ACC_, CNT_ = 0, 1
    rb = 64
    lane_rb = lax.broadcasted_iota(jnp.int32, (rb, LANE), 1)

    def count(pred_fn):
        def tile(st, first):
            for r0 in range(0, nq, rb):
                rs = slice(r0, r0 + rb)
                acc = None if first else sf_s[ACC_, rs, :]
                for g in range(ng):
                    one = jnp.where(pred_fn(key_s[st, g, rs, :], rs), 1.0, 0.0)
                    acc = one if acc is None else acc + one
                sf_s[ACC_, rs, :] = acc

        def body(st, carry):
            tile(st, False)
            return carry

        tile(0, True)
        lax.fori_loop(1, n_st, body, 0)
        return _dot(sf_s[ACC_].astype(_BF16), ones_b)

    si_s[CAND_] = jnp.zeros((nq, LANE), jnp.int32)

    def bit_body(i, carry):
        half = jnp.left_shift(jnp.int32(1), 30 - i)
        c = count(lambda k, rs: k >= si_s[CAND_, rs, :])
        si_s[CAND_] = si_s[CAND_] + jnp.where(c >= f_topk, half, -half)
        return carry

    lax.fori_loop(0, 31, bit_body, 0)
    c_last = count(lambda k, rs: k >= si_s[CAND_, rs, :])
    si_s[T_] = jnp.where(c_last >= f_topk, si_s[CAND_], si_s[CAND_] - 1)
    sf_s[CNT_] = count(lambda k, rs: k >= si_s[T_, rs, :])

    def encode(st, carry):
        for r0 in range(0, nq, rb):
            rs = slice(r0, r0 + rb)
            t = si_s[T_, rs, :]
            for g in range(ng):
                k = key_s[st, g, rs, :]
                pos = st * st_w + g * LANE + lane_rb
                key_s[st, g, rs, :] = jnp.where(k > t, CODE_KEEP, jnp.where(k == t, pos, CODE_DROP))
        return carry

    lax.fori_loop(0, n_st, encode, 0)
    finite_thr = si_s[T_] != KEY_NEG_INF
    si_s[MIDX_] = jnp.where(finite_thr, TIE_BOUND_ALL, 0)

    excess = jnp.max(jnp.where(finite_thr, sf_s[CNT_], 0.0)) > f_topk

    @pl.when(excess)
    def _():
        r_keep = f_topk - count(lambda code, rs: code < 0)
        sf_s[CNT_] = jnp.where(finite_thr, r_keep, 0.0)
        sf_s[ACC_] = jnp.zeros((nq, LANE), _F32)
        upper = (lax.broadcasted_iota(jnp.int32, (LANE, LANE), 0)
                 <= lax.broadcasted_iota(jnp.int32, (LANE, LANE), 1)).astype(_BF16)

        def rank_ties(st, carry):
            for r0 in range(0, nq, rb):
                rs = slice(r0, r0 + rb)
                r_blk = sf_s[CNT_, rs, :]
                before = sf_s[ACC_, rs, :]
                for g in range(ng):
                    code = key_s[st, g, rs, :]
                    tie = jnp.where(code >= 0, jnp.where(code < CODE_DROP, 1.0, 0.0), 0.0)
                    tie_b = tie.astype(_BF16)
                    rank = before + _dot(tie_b, upper)
                    kept_tie = jnp.where(rank <= r_blk, tie, 0.0)
                    mb = jnp.where(code < 0, 0.0, jnp.where(kept_tie > 0.0, 0.0, MASK_NEG))
                    key_s[st, g, rs, :] = pltpu.bitcast(mb, jnp.int32)
                    before = before + _dot(tie_b, ones_b)
                sf_s[ACC_, rs, :] = before
            return carry

        lax.fori_loop(0, n_st, rank_ties, 0)

    @pl.when(jnp.logical_not(excess))
    def _():
        def p2(st, carry):
            for r0 in range(0, nq, rb):
                rs = slice(r0, r0 + rb)
                bound = si_s[MIDX_, rs, :]
                for g in range(ng):
                    mb = jnp.where(key_s[st, g, rs, :] < bound, 0.0, MASK_NEG)
                    key_s[st, g, rs, :] = pltpu.bitcast(mb, jnp.int32)
            return carry

        lax.fori_loop(0, n_st, p2, 0)

    ones_v = jnp.ones((st_w, hd), _BF16)

    def attend(st, mode):
        off = pl.multiple_of(st * st_w, st_w)
        for h in range(H_B):
            sl = slice(h * hd, (h + 1) * hd)
            mask = jnp.concatenate([pltpu.bitcast(key_s[st, g], _F32) for g in range(ng)], axis=1)
            s = _dot_nt(qs_s[h], k_ref[pl.ds(off, st_w), sl]) + mask
            if mode:
                s_s[...] = s
                b_prev = bias_ref[h, :, 0:LANE]
                if mode == 1:
                    s_s[0:LANE, (ng - 1) * LANE:ng * LANE] += b_prev
                else:
                    b_diag = bias_ref[h, :, LANE:2 * LANE]
                    for rt in range(ng):
                        s_s[grp(rt), grp(rt)] += b_diag
                        if rt:
                            s_s[grp(rt), grp(rt - 1)] += b_prev
                s = s_s[...]
            m_old = m_s[h]
            gmax = s[:, grp(0)]
            for g in range(1, ng):
                gmax = jnp.maximum(gmax, s[:, grp(g)])
            m_new = jnp.maximum(m_old, jnp.max(gmax, axis=1, keepdims=True))
            alpha = jnp.exp2(m_old - m_new)
            p = jnp.concatenate([jnp.exp2(s[:, grp(g)] - m_new) for g in range(ng)], axis=1).astype(_BF16)
            m_s[h] = m_new
            v_ones = jnp.concatenate([v_ref[pl.ds(off, st_w), sl], ones_v], axis=1)
            acc_s[h] = jnp.concatenate([alpha, alpha], axis=1) * acc_s[h] + _dot(p, v_ones)

    def p3(st, carry):
        attend(st, 0)
        return carry

    lax.fori_loop(0, qb - 1, p3, 0)

    @pl.when(qb > 0)
    def _():
        attend(qb - 1, 1)

    attend(qb, 2)

    for h in range(H_B):
        sl = slice(h * hd, (h + 1) * hd)
        o_ref[:, sl] = (acc_s[h, :, 0:hd] / acc_s[h, :, hd:2 * hd]).astype(o_ref.dtype)


def _dsa(pb, ps, bias, b_, t_, col0):
    nq = KEY_TILE
    nb = t_ // nq
    hd = H_B * D_HEAD_B
    topk = min(TOPK_MAX, t_ // 4)
    iq_w = H_IDX * D_IDX
    ik_w = 2 * LANE
    assert col0 % hd == 0 and (col0 + 3 * hd) % iq_w == 0 and (col0 + 3 * hd + iq_w) % ik_w == 0
    cq = col0 // hd
    resident = functools.partial(pl.BlockSpec, pipeline_mode=pl.Buffered(1))
    return pl.pallas_call(
        functools.partial(_dsa_kernel, topk=topk),
        grid=(b_, nb),
        in_specs=[
            pl.BlockSpec((nq, hd), lambda b, i: (b * nb + i, cq)),
            resident((t_, hd), lambda b, i: (b, cq + 1)),
            resident((t_, hd), lambda b, i: (b, cq + 2)),
            pl.BlockSpec((nq, iq_w), lambda b, i: (b * nb + i, (col0 + 3 * hd) // iq_w)),
            resident((t_, ik_w), lambda b, i: (b, (col0 + 3 * hd + iq_w) // ik_w)),
            pl.BlockSpec((nq, LANE), lambda b, i: (b * nb + i, 0)),
            resident((H_B, Q_BLOCK, 2 * LANE), lambda b, i: (0, 0, 0)),
        ],
        out_specs=pl.BlockSpec((nq, hd), lambda b, i: (b * nb + i, 0)),
        out_shape=jax.ShapeDtypeStruct((b_ * t_, hd), _BF16),
        scratch_shapes=[
            pltpu.VMEM((nb, KEY_TILE // LANE, nq, LANE), jnp.int32),
            pltpu.VMEM((H_B, nq, D_HEAD_B), _BF16),
            pltpu.VMEM((H_B, nq, LANE), _F32),
            pltpu.VMEM((H_B, nq, LANE), _F32),
            pltpu.VMEM((H_B, nq, 2 * D_HEAD_B), _F32),
            pltpu.VMEM((nq, KEY_TILE), _F32),
            pltpu.VMEM((3, nq, LANE), jnp.int32),
            pltpu.VMEM((2, nq, LANE), _F32),
        ],
        compiler_params=pltpu.CompilerParams(
            dimension_semantics=("arbitrary", "arbitrary"), vmem_limit_bytes=VMEM_LIMIT),
        name="dsa",
    )(pb, pb, pb, pb, pb, ps, bias)


def _merge_kernel(oa_ref, ob_ref, g_ref, x_ref, bg_ref, wa_ref, wb_ref, wo_ref, o_ref):
    d = x_ref.shape[1]
    a = _dot(oa_ref[...], wa_ref[...])
    b = _dot(ob_ref[...], wb_ref[...])
    g = _sigmoid(g_ref[...].astype(_F32) + bg_ref[...])
    merged = g[:, 0:d] * a + g[:, d:2 * d] * b
    o_ref[...] = x_ref[...] + _dot(merged.astype(_BF16), wo_ref[...])


def _merge(oa, ob, pb, x2d, bg, wa, wb, wo, tn, col_g):
    n, d = x2d.shape
    row = lambda w: pl.BlockSpec((tn, w), lambda i: (i, 0))
    const = lambda a: pl.BlockSpec(a.shape, lambda i: (0, 0))
    return pl.pallas_call(
        _merge_kernel,
        grid=(n // tn,),
        in_specs=[row(oa.shape[1]), row(ob.shape[1]),
                  pl.BlockSpec((tn, 2 * d), lambda i: (i, col_g // (2 * d))), row(d),
                  const(bg), const(wa), const(wb), const(wo)],
        out_specs=row(d),
        out_shape=jax.ShapeDtypeStruct((n, d), _F32),
        compiler_params=pltpu.CompilerParams(
            dimension_semantics=("arbitrary",), vmem_limit_bytes=VMEM_LIMIT),
        name="merge",
    )(oa, ob, pb, x2d, bg, wa, wb, wo)


def _mlp_kernel(x_ref, n2_ref, w1_ref, w2_ref, nf_ref, o_ref, *, final_norm):
    x = x_ref[...]
    h2 = _rms(x, n2_ref[...]).astype(_BF16)
    hid = jnp.maximum(_dot(h2, w1_ref[...]), 0.0)
    y = x + _dot((hid * hid).astype(_BF16), w2_ref[...])
    o_ref[...] = _rms(y, nf_ref[...]) if final_norm else y


def _mlp(x1, n2, w1, w2, nf, tn, final_norm):
    n, d = x1.shape
    row = pl.BlockSpec((tn, d), lambda i: (i, 0))
    const = lambda a: pl.BlockSpec(a.shape, lambda i: (0, 0), pipeline_mode=pl.Buffered(1))
    return pl.pallas_call(
        functools.partial(_mlp_kernel, final_norm=final_norm),
        grid=(n // tn,),
        in_specs=[row, const(n2), const(w1), const(w2), const(nf)],
        out_specs=row,
        out_shape=jax.ShapeDtypeStruct((n, d), _F32),
        compiler_params=pltpu.CompilerParams(
            dimension_semantics=("arbitrary",), vmem_limit_bytes=VMEM_LIMIT),
        name="mlp",
    )(x1, n2, w1, w2, nf)


def _t5_bucket(rel):
    half = N_BUCKETS // 2
    max_exact = half // 2
    base = jnp.where(rel > 0, half, 0)
    n = jnp.abs(rel)
    n_f = jnp.maximum(n, 1).astype(jnp.float32)
    large = max_exact + (jnp.log(n_f / max_exact) / math.log(MAX_DISTANCE / max_exact)
                         * (half - max_exact)).astype(jnp.int32)
    large = jnp.minimum(large, half - 1)
    return base + jnp.where(n < max_exact, n, large)


def _pick_tile(n, prefs):
    for t in prefs:
        if n % t == 0:
            return t
    raise ValueError(f"no tile in {prefs} divides {n}")


def kernel(x, norm1_w, w_in, conv_a_w, a_log, dt_bias, norm_a_w, rel_bias_table, w_gate, b_gate,
           w_proj_a, w_proj_b, w_out, norm2_w, w_ff1, w_ff2, norm_final_w):
    b_, t_, d = x.shape
    depth = norm1_w.shape[0]
    n = b_ * t_
    ha, hb = H_A * DK_A, H_B * D_HEAD_B
    assert t_ % KEY_TILE == 0 and t_ % GDN_CHUNK == 0 and d % LANE == 0
    assert DK_A == DV_A == D_HEAD_B == LANE and 2 * D_IDX == LANE
    assert Q_BLOCK >= MAX_DISTANCE

    o_ba = 4 * ha
    o_aa = o_ba + H_A
    o_qb = o_aa + H_A
    o_iq = o_qb + 3 * hb
    o_ik = o_iq + H_IDX * D_IDX
    o_iw = o_ik + D_IDX

    rel = (jnp.arange(2 * Q_BLOCK, dtype=jnp.int32)[None, :] - Q_BLOCK) \
        - jnp.arange(Q_BLOCK, dtype=jnp.int32)[:, None]
    far = jnp.full((Q_BLOCK, LANE), -(Q_BLOCK + 1), jnp.int32)
    bucket = _t5_bucket(jnp.concatenate([rel, far], axis=1))
    bias = _bias_tiles(rel_bias_table.astype(_F32), bucket)

    x2d = x.reshape(n, d)
    tn = _pick_tile(t_, (512, 256, 128))
    for layer in range(depth):
        wi = w_in[layer].astype(_BF16)
        zpad = lambda w: jnp.zeros((d, w), _BF16)
        ws = jnp.concatenate([wi[:, o_ba:o_qb], wi[:, o_iw:o_iw + H_IDX], zpad(LANE - 3 * H_A)], axis=1)
        ik = wi[:, o_ik:o_iw]
        wb16 = jnp.concatenate([wi[:, 0:o_ba], w_gate[layer].astype(_BF16), wi[:, o_qb:o_ik],
                                ik, zpad(D_IDX), zpad(D_IDX), ik], axis=1)
        nw = norm1_w[layer].reshape(1, d).astype(_F32)
        pb, ps = _norm_proj(x2d, nw, wb16, ws, conv_a_w[layer].astype(_F32), tn, t_ // tn, norm_cols=2 * ha)

        col_a = 0
        col_g = 4 * ha
        col_b = col_g + 2 * d
        assert col_g % (2 * d) == 0
        lanes = lambda v, off: jnp.zeros((1, LANE), _F32).at[0, off:off + H_A].set(v.astype(_F32))
        oa = _gdn(pb, ps, lanes(a_log[layer], H_A), lanes(dt_bias[layer], H_A),
                  norm_a_w[layer].reshape(1, DV_A).astype(_F32), b_, t_, col_a)
        ob = _dsa(pb, ps, bias, b_, t_, col_b)

        x2d = _merge(oa, ob, pb, x2d, b_gate[layer].reshape(1, 2 * d).astype(_F32),
                     w_proj_a[layer].astype(_BF16), w_proj_b[layer].astype(_BF16),
                     w_out[layer].astype(_BF16), tn, col_g)
        x2d = _mlp(x2d, norm2_w[layer].reshape(1, d).astype(_F32), w_ff1[layer].astype(_BF16),
                   w_ff2[layer].astype(_BF16), norm_final_w.reshape(1, d).astype(_F32),
                   _pick_tile(n, (512, 256, 128)), final_norm=layer == depth - 1)
    return x2d.reshape(b_, t_, d)
```

```python
import functools
import math

import jax
import jax.numpy as jnp
from jax import lax
from jax.experimental import pallas as pl
from jax.experimental.pallas import tpu as pltpu

EPS = 1e-6
H_A = 8
DK_A = 128
DV_A = 128
CONV_K = 4
GDN_CHUNK = 128
GDN_STEP_CHUNKS = 4
H_B = 8
D_HEAD_B = 128
H_IDX = 8
D_IDX = 64
TOPK_MAX = 256
Q_BLOCK = 128
CHUNK = 64
KEY_TILE = 512
N_BUCKETS = 32
MAX_DISTANCE = 128

LANE = 128
VMEM_LIMIT = 56 * 1024 * 1024

INT_MAX = 2**31 - 1
KEY_NEG_INF = -2139095041
MASK_NEG = -1e30
CODE_KEEP = -1
CODE_DROP = INT_MAX
TIE_BOUND_ALL = 2**30
LOG2E = math.log2(math.e)

_F32 = jnp.float32
_BF16 = jnp.bfloat16


def _dot(a, b):
    return jnp.dot(a, b, preferred_element_type=_F32)


def _dot_nt(a, b):
    return lax.dot_general(a, b, (((1,), (1,)), ((), ())), preferred_element_type=_F32)


def _sigmoid(x):
    return 1.0 / (1.0 + jnp.exp(-x))


def _rms(x, w):
    return x * lax.rsqrt(jnp.mean(x * x, axis=-1, keepdims=True) + EPS) * w


def _norm_proj_kernel(x_ref, nw_ref, w_ref, ws_ref, cw_ref, o_ref, os_ref, halo_s, work_s, *,
                      chunk, conv_cols, norm_cols, tiles_per_seq):
    tn = x_ref.shape[0]

    @pl.when(pl.program_id(0) == 0)
    def _():
        halo_s[...] = jnp.zeros_like(halo_s)

    h = _rms(x_ref[...], nw_ref[...]).astype(_BF16)
    os_ref[...] = _dot(h, ws_ref[...])
    seq_start = pl.program_id(0) % tiles_per_seq == 0
    heavy = [(ci, c0) for ci, c0 in enumerate(range(0, conv_cols, chunk))]
    plain = [(None, c0) for c0 in range(conv_cols, w_ref.shape[1], chunk)]
    per_heavy = -(-len(plain) // max(1, len(heavy)))
    order = []
    for i, hv in enumerate(heavy):
        order.append(hv)
        order.extend(plain[i * per_heavy:(i + 1) * per_heavy])
    order.extend(plain[len(heavy) * per_heavy:])
    for ci, c0 in order:
        y = _dot(h, w_ref[:, c0:c0 + chunk])
        if c0 < conv_cols:
            work_s[0:8, :] = jnp.where(seq_start, 0.0, halo_s[ci])
            work_s[8:8 + tn, :] = y
            halo_s[ci] = y[tn - 8:tn, :]
            acc = cw_ref[CONV_K - 1:CONV_K, c0:c0 + chunk] * y
            for j in range(CONV_K - 1):
                lo = 8 - (CONV_K - 1) + j
                acc = acc + cw_ref[j:j + 1, c0:c0 + chunk] * work_s[lo:lo + tn, :]
            y = acc * _sigmoid(acc)
            if c0 < norm_cols:
                parts = []
                for g0 in range(0, chunk, DK_A):
                    yh = y[:, g0:g0 + DK_A]
                    if c0 + g0 < norm_cols:
                        inv = lax.rsqrt(jnp.sum(yh * yh, axis=-1, keepdims=True) + EPS)
                        if c0 + g0 < norm_cols // 2:
                            inv = inv * (DK_A ** -0.5)
                        yh = yh * inv
                    parts.append(yh)
                y = jnp.concatenate(parts, axis=1)
        o_ref[:, c0:c0 + chunk] = y.astype(o_ref.dtype)


def _norm_proj(x2d, nw, w, ws, cw, tn, tiles_per_seq, norm_cols):
    n, d = x2d.shape
    c = w.shape[1]
    chunk = _pick_tile(c, (768, 512, 256, 128))
    conv_cols = cw.shape[1]
    assert conv_cols % chunk == 0 and tn >= 8
    assert norm_cols <= conv_cols and chunk % DK_A == 0 and (norm_cols // 2) % DK_A == 0
    const = functools.partial(pl.BlockSpec, pipeline_mode=pl.Buffered(1))
    return pl.pallas_call(
        functools.partial(_norm_proj_kernel, chunk=chunk, conv_cols=conv_cols, norm_cols=norm_cols,
                          tiles_per_seq=tiles_per_seq),
        grid=(n // tn,),
        in_specs=[
            pl.BlockSpec((tn, d), lambda i: (i, 0)),
            const((1, d), lambda i: (0, 0)),
            const((d, c), lambda i: (0, 0)),
            const((d, LANE), lambda i: (0, 0)),
            const(cw.shape, lambda i: (0, 0)),
        ],
        out_specs=[pl.BlockSpec((tn, c), lambda i: (i, 0)), pl.BlockSpec((tn, LANE), lambda i: (i, 0))],
        out_shape=[jax.ShapeDtypeStruct((n, c), _BF16), jax.ShapeDtypeStruct((n, LANE), _F32)],
        scratch_shapes=[
            pltpu.VMEM((conv_cols // chunk, 8, chunk), _F32),
            pltpu.VMEM((8 + tn, chunk), _F32),
        ],
        compiler_params=pltpu.CompilerParams(
            dimension_semantics=("arbitrary",), vmem_limit_bytes=VMEM_LIMIT),
        name="proj",
    )(x2d, nw, w, ws, cw)


def _gdn_kernel(qkvz_ref, sm_ref, arow_ref, dtrow_ref, naw_ref, o_ref,
                s_ref, kn_s, kb_s, qn_s, rhs_s, dec_s, qdec_s, ktt_s, eg_s, m_s, x_s):
    tt = qkvz_ref.shape[0]
    ct = GDN_CHUNK
    nc = tt // ct
    hd = H_A * DK_A

    @pl.when(pl.program_id(1) == 0)
    def _():
        s_ref[...] = jnp.zeros_like(s_ref)

    row = lax.broadcasted_iota(jnp.int32, (ct, ct), 0)
    col = lax.broadcasted_iota(jnp.int32, (ct, ct), 1)
    tri = (col <= row).astype(_F32)
    strict = col < row
    eye = (col == row).astype(_F32)
    n_sq = int(math.log2(ct))
    heads = range(H_A)

    def prepare(c):
        r0 = c * ct
        sm = sm_ref[r0:r0 + ct, :]
        beta_full = _sigmoid(sm)
        xg = sm + dtrow_ref[...]
        softplus = jnp.maximum(xg, 0.0) + jnp.log(1.0 + jnp.exp(-jnp.abs(xg)))
        g_full = -jnp.exp(arow_ref[...]) * softplus
        gcum = jnp.dot(tri, g_full, preferred_element_type=_F32, precision=lax.Precision.HIGHEST)
        gcum_t = gcum.T
        for h in heads:
            sl = slice(h * DK_A, (h + 1) * DK_A)
            qh, kh, vh = (qkvz_ref[r0:r0 + ct, idx * hd + h * DK_A:idx * hd + (h + 1) * DK_A].astype(_F32)
                          for idx in range(3))
            qn, kn = qh, kh
            bcol = beta_full[:, h:h + 1]
            gcol = gcum[:, H_A + h:H_A + h + 1]
            grow = gcum_t[H_A + h:H_A + h + 1, :]
            glast = gcum[ct - 1:ct, H_A + h:H_A + h + 1]
            kb = kn * bcol
            eg = jnp.exp(gcol)
            rhs_s[c, h, :, 0:DV_A] = vh * bcol
            rhs_s[c, h, :, DV_A:DV_A + DK_A] = kb * eg
            dec_s[c, h] = jnp.where(strict, jnp.exp(jnp.where(strict, gcol - grow, 0.0)), 0.0)
            qdec_s[c, h] = (qn * eg).astype(_BF16)
            ktt_s[c, h] = (kn * jnp.exp(glast - gcol)).T.astype(_BF16)
            kn_s[c, h] = kn.astype(_BF16)
            kb_s[c, h] = kb.astype(_BF16)
            qn_s[c, h] = qn.astype(_BF16)
            eg_s[c, h] = jnp.broadcast_to(jnp.exp(glast), (DK_A, DV_A))

    def solve(c):
        for h in heads:
            m = -(_dot_nt(kb_s[c, h], kn_s[c, h]) * dec_s[c, h])
            m_s[c, h] = m
            x_s[c, h] = m
        for _ in range(n_sq - 1):
            for h in heads:
                mb = m_s[c, h].astype(_BF16)
                m = _dot(mb, mb)
                m_s[c, h] = m
                x_s[c, h] = x_s[c, h] + m + _dot(x_s[c, h].astype(_BF16), m.astype(_BF16))
        for h in heads:
            rhs = rhs_s[c, h]
            rhs_s[c, h] = rhs + _dot(x_s[c, h].astype(_BF16), rhs.astype(_BF16))
        for h in heads:
            m_s[c, h] = _dot_nt(qn_s[c, h], kn_s[c, h]) * (dec_s[c, h] + eye)

    def recur(c):
        for h in heads:
            sb = s_ref[h].astype(_BF16)
            vnew = rhs_s[c, h, :, 0:DV_A] - _dot(rhs_s[c, h, :, DV_A:DV_A + DK_A].astype(_BF16), sb)
            vb = vnew.astype(_BF16)
            o = _dot(qdec_s[c, h], sb) + _dot(m_s[c, h].astype(_BF16), vb)
            s_ref[h] = s_ref[h] * eg_s[c, h] + _dot(ktt_s[c, h], vb)
            rhs_s[c, h, :, 0:DV_A] = o

    def finish(c):
        r0 = c * ct
        naw = naw_ref[...]
        for h in heads:
            sl = slice(h * DV_A, (h + 1) * DV_A)
            o = rhs_s[c, h, :, 0:DV_A]
            z = qkvz_ref[r0:r0 + ct, 3 * hd + h * DV_A:3 * hd + (h + 1) * DV_A].astype(_F32)
            on = o * lax.rsqrt(jnp.mean(o * o, axis=-1, keepdims=True) + EPS) * naw
            o_ref[r0:r0 + ct, sl] = (on * (z * _sigmoid(z))).astype(o_ref.dtype)

    for c in range(nc):
        prepare(c)
    for c in range(nc):
        solve(c)
    for c in range(nc):
        recur(c)
    for c in range(nc):
        finish(c)


def _gdn(pb, ps, arow, dtrow, naw, b_, t_, col0):
    ct = GDN_CHUNK
    nc = GDN_STEP_CHUNKS if t_ % (GDN_STEP_CHUNKS * ct) == 0 else 1
    tt = nc * ct
    hd = H_A * DK_A
    nt = t_ // tt
    assert col0 % (4 * hd) == 0
    cb = col0 // (4 * hd)
    const = lambda shape: pl.BlockSpec(shape, lambda b, t: (0, 0))
    return pl.pallas_call(
        _gdn_kernel,
        grid=(b_, nt),
        in_specs=[pl.BlockSpec((tt, 4 * hd), lambda b, t: (b * nt + t, cb)),
                  pl.BlockSpec((tt, LANE), lambda b, t: (b * nt + t, 0)),
                  const((1, LANE)), const((1, LANE)), const((1, DV_A))],
        out_specs=pl.BlockSpec((tt, hd), lambda b, t: (b * nt + t, 0)),
        out_shape=jax.ShapeDtypeStruct((b_ * t_, hd), _BF16),
        scratch_shapes=[
            pltpu.VMEM((H_A, DK_A, DV_A), _F32),
            pltpu.VMEM((nc, H_A, ct, DK_A), _BF16),
            pltpu.VMEM((nc, H_A, ct, DK_A), _BF16),
            pltpu.VMEM((nc, H_A, ct, DK_A), _BF16),
            pltpu.VMEM((nc, H_A, ct, DV_A + DK_A), _F32),
            pltpu.VMEM((nc, H_A, ct, ct), _F32),
            pltpu.VMEM((nc, H_A, ct, DK_A), _BF16),
            pltpu.VMEM((nc, H_A, DK_A, ct), _BF16),
            pltpu.VMEM((nc, H_A, DK_A, DV_A), _F32),
            pltpu.VMEM((nc, H_A, ct, ct), _F32),
            pltpu.VMEM((nc, H_A, ct, ct), _F32),
        ],
        compiler_params=pltpu.CompilerParams(
            dimension_semantics=("arbitrary", "arbitrary"), vmem_limit_bytes=VMEM_LIMIT),
        name="gdn",
    )(pb, ps, arow, dtrow, naw)


def _bias_kernel(tab_ref, bucket_ref, o_ref):
    bucket = bucket_ref[...]
    nq, w = bucket.shape
    for h in range(H_B):
        acc = jnp.zeros((nq, w), _F32)
        for b in range(N_BUCKETS):
            acc = acc + jnp.where(bucket == b, tab_ref[b, h], 0.0)
        far = acc[:, 2 * LANE:3 * LANE]
        o_ref[h] = (acc[:, 0:2 * LANE] - jnp.concatenate([far, far], axis=1)) * LOG2E


def _bias_tiles(rel_table, bucket):
    return pl.pallas_call(
        _bias_kernel,
        in_specs=[pl.BlockSpec(memory_space=pltpu.SMEM),
                  pl.BlockSpec(bucket.shape, lambda: (0, 0))],
        out_specs=pl.BlockSpec((H_B, Q_BLOCK, 2 * LANE), lambda: (0, 0, 0)),
        out_shape=jax.ShapeDtypeStruct((H_B, Q_BLOCK, 2 * LANE), _F32),
        name="rel_bias_tiles",
    )(rel_table, bucket)


def _dsa_kernel(q_ref, k_ref, v_ref, iq_ref, ik_ref, sm_ref, bias_ref, o_ref,
                key_s, qs_s, iwb_s, m_s, acc_s, s_s, si_s, sf_s, *, topk):
    nq = q_ref.shape[0]
    st_w = KEY_TILE
    assert nq == st_w
    ng = st_w // LANE
    qb = pl.program_id(1)
    n_st = qb + 1
    hd = D_HEAD_B
    scale = D_HEAD_B ** -0.5 * LOG2E
    f_topk = float(topk)

    lane_j = lax.broadcasted_iota(jnp.int32, (nq, LANE), 1)
    row_i = lax.broadcasted_iota(jnp.int32, (nq, LANE), 0)

    def grp(g):
        return slice(g * LANE, (g + 1) * LANE)

    iw = sm_ref[...]
    for h in range(H_B):
        sl = slice(h * hd, (h + 1) * hd)
        qs_s[h] = (q_ref[:, sl].astype(_F32) * scale).astype(_BF16)
        iwb_s[h] = jnp.broadcast_to(iw[:, 2 * H_A + h:2 * H_A + h + 1], (nq, LANE))
        m_s[h] = jnp.full((nq, LANE), MASK_NEG, _F32)
        acc_s[h] = jnp.zeros((nq, 2 * hd), _F32)

    def scores(st, diag):
        ikt = ik_ref[pl.ds(pl.multiple_of(st * st_w, st_w), st_w), :]
        acc = [jnp.zeros((nq, LANE), _F32) for _ in range(ng)]
        for p in range(H_IDX // 2):
            xq = iq_ref[:, p * LANE:(p + 1) * LANE]
            for half in range(2):
                h = 2 * p + half
                s = jnp.maximum(_dot_nt(xq, ikt[:, half * LANE:(half + 1) * LANE]), 0.0)
                w = iwb_s[h]
                for g in range(ng):
                    acc[g] = acc[g] + w * s[:, grp(g)]
        for g in range(ng):
            a = acc[g]
            if diag:
                a = jnp.where((g * LANE + lane_j) // CHUNK <= row_i // CHUNK, a, -jnp.inf)
            bits = pltpu.bitcast(a, jnp.int32)
            key_s[st, g] = bits ^ ((bits >> 31) & INT_MAX)

    def p1(st, carry):
        scores(st, False)
        return carry

    lax.fori_loop(0, qb, p1, 0)
    scores(qb, True)

    ones_b = jnp.ones((LANE, LANE), _BF16)
    assert key_s.shape[0] * ng <= 256

    T_, CAND_, MIDX_ = 0, 1, 2
    ACC_, CNT_ = 0, 1
    rb = 64
    lane_rb = lax.broadcasted_iota(jnp.int32, (rb, LANE), 1)

    def count(pred_fn):
        def tile(st, first):
            for r0 in range(0, nq, rb):
                rs = slice(r0, r0 + rb)
                acc = None if first else sf_s[ACC_, rs, :]
                for g in range(ng):
                    one = jnp.where(pred_fn(key_s[st, g, rs, :], rs), 1.0, 0.0)
                    acc = one if acc is None else acc + one
                sf_s[ACC_, rs, :] = acc

        def body(st, carry):
            tile(st, False)
            return carry

        tile(0, True)
        lax.fori_loop(1, n_st, body, 0)
        return _dot(sf_s[ACC_].astype(_BF16), ones_b)

    si_s[CAND_] = jnp.zeros((nq, LANE), jnp.int32)

    def bit_body(i, carry):
        half = jnp.left_shift(jnp.int32(1), 30 - i)
        c = count(lambda k, rs: k >= si_s[CAND_, rs, :])
        si_s[CAND_] = si_s[CAND_] + jnp.where(c >= f_topk, half, -half)
        return carry

    lax.fori_loop(0, 31, bit_body, 0)
    c_last = count(lambda k, rs: k >= si_s[CAND_, rs, :])
    si_s[T_] = jnp.where(c_last >= f_topk, si_s[CAND_], si_s[CAND_] - 1)
    sf_s[CNT_] = count(lambda k, rs: k >= si_s[T_, rs, :])

    def encode(st, carry):
        for r0 in range(0, nq, rb):
            rs = slice(r0, r0 + rb)
            t = si_s[T_, rs, :]
            for g in range(ng):
                k = key_s[st, g, rs, :]
                pos = st * st_w + g * LANE + lane_rb
                key_s[st, g, rs, :] = jnp.where(k > t, CODE_KEEP, jnp.where(k == t, pos, CODE_DROP))
        return carry

    lax.fori_loop(0, n_st, encode, 0)
    finite_thr = si_s[T_] != KEY_NEG_INF
    si_s[MIDX_] = jnp.where(finite_thr, TIE_BOUND_ALL, 0)

    excess = jnp.max(jnp.where(finite_thr, sf_s[CNT_], 0.0)) > f_topk

    @pl.when(excess)
    def _():
        r_keep = f_topk - count(lambda code, rs: code < 0)
        sf_s[CNT_] = jnp.where(finite_thr, r_keep, 0.0)
        sf_s[ACC_] = jnp.zeros((nq, LANE), _F32)
        upper = (lax.broadcasted_iota(jnp.int32, (LANE, LANE), 0)
                 <= lax.broadcasted_iota(jnp.int32, (LANE, LANE), 1)).astype(_BF16)

        def rank_ties(st, carry):
            for r0 in range(0, nq, rb):
                rs = slice(r0, r0 + rb)
                r_blk = sf_s[CNT_, rs, :]
                before = sf_s[ACC_, rs, :]
                for g in range(ng):
                    code = key_s[st, g, rs, :]
                    tie = jnp.where(code >= 0, jnp.where(code < CODE_DROP, 1.0, 0.0), 0.0)
                    tie_b = tie.astype(_BF16)
                    rank = before + _dot(tie_b, upper)
                    kept_tie = jnp.where(rank <= r_blk, tie, 0.0)
                    mb = jnp.where(code < 0, 0.0, jnp.where(kept_tie > 0.0, 0.0, MASK_NEG))
                    key_s[st, g, rs, :] = pltpu.bitcast(mb, jnp.int32)
                    before = before + _dot(tie_b, ones_b)
                sf_s[ACC_, rs, :] = before
            return carry

        lax.fori_loop(0, n_st, rank_ties, 0)

    @pl.when(jnp.logical_not(excess))
    def _():
        def p2(st, carry):
            for r0 in range(0, nq, rb):
                rs = slice(r0, r0 + rb)
                bound = si_s[MIDX_, rs, :]
                for g in range(ng):
                    mb = jnp.where(key_s[st, g, rs, :] < bound, 0.0, MASK_NEG)
                    key_s[st, g, rs, :] = pltpu.bitcast(mb, jnp.int32)
            return carry

        lax.fori_loop(0, n_st, p2, 0)

    ones_v = jnp.ones((st_w, hd), _BF16)

    def attend(st, mode):
        off = pl.multiple_of(st * st_w, st_w)
        for h in range(H_B):
            sl = slice(h * hd, (h + 1) * hd)
            mask = jnp.concatenate([pltpu.bitcast(key_s[st, g], _F32) for g in range(ng)], axis=1)
            s = _dot_nt(qs_s[h], k_ref[pl.ds(off, st_w), sl]) + mask
            if mode:
                s_s[...] = s
                b_prev = bias_ref[h, :, 0:LANE]
                if mode == 1:
                    s_s[0:LANE, (ng - 1) * LANE:ng * LANE] += b_prev
                else:
                    b_diag = bias_ref[h, :, LANE:2 * LANE]
                    for rt in range(ng):
                        s_s[grp(rt), grp(rt)] += b_diag
                        if rt:
                            s_s[grp(rt), grp(rt - 1)] += b_prev
                s = s_s[...]
            m_old = m_s[h]
            gmax = s[:, grp(0)]
            for g in range(1, ng):
                gmax = jnp.maximum(gmax, s[:, grp(g)])
            m_new = jnp.maximum(m_old, jnp.max(gmax, axis=1, keepdims=True))
            alpha = jnp.exp2(m_old - m_new)
            p = jnp.concatenate([jnp.exp2((s[:, grp(g)] - m_new).astype(_BF16)) for g in range(ng)], axis=1)
            m_s[h] = m_new
            v_ones = jnp.concatenate([v_ref[pl.ds(off, st_w), sl], ones_v], axis=1)
            acc_s[h] = jnp.concatenate([alpha, alpha], axis=1) * acc_s[h] + _dot(p, v_ones)

    def p3(st, carry):
        attend(st, 0)
        return carry

    lax.fori_loop(0, qb - 1, p3, 0)

    @pl.when(qb > 0)
    def _():
        attend(qb - 1, 1)

    attend(qb, 2)

    for h in range(H_B):
        sl = slice(h * hd, (h + 1) * hd)
        o_ref[:, sl] = (acc_s[h, :, 0:hd] / acc_s[h, :, hd:2 * hd]).astype(o_ref.dtype)


def _dsa(pb, ps, bias, b_, t_, col0):
    nq = KEY_TILE
    nb = t_ // nq
    hd = H_B * D_HEAD_B
    topk = min(TOPK_MAX, t_ // 4)
    iq_w = H_IDX * D_IDX
    ik_w = 2 * LANE
    assert col0 % hd == 0 and (col0 + 3 * hd) % iq_w == 0 and (col0 + 3 * hd + iq_w) % ik_w == 0
    cq = col0 // hd
    resident = functools.partial(pl.BlockSpec, pipeline_mode=pl.Buffered(1))
    return pl.pallas_call(
        functools.partial(_dsa_kernel, topk=topk),
        grid=(b_, nb),
        in_specs=[
            pl.BlockSpec((nq, hd), lambda b, i: (b * nb + i, cq)),
            resident((t_, hd), lambda b, i: (b, cq + 1)),
            resident((t_, hd), lambda b, i: (b, cq + 2)),
            pl.BlockSpec((nq, iq_w), lambda b, i: (b * nb + i, (col0 + 3 * hd) // iq_w)),
            resident((t_, ik_w), lambda b, i: (b, (col0 + 3 * hd + iq_w) // ik_w)),
            pl.BlockSpec((nq, LANE), lambda b, i: (b * nb + i, 0)),
            resident((H_B, Q_BLOCK, 2 * LANE), lambda b, i: (0, 0, 0)),
        ],
        out_specs=pl.BlockSpec((nq, hd), lambda b, i: (b * nb + i, 0)),
        out_shape=jax.ShapeDtypeStruct((b_ * t_, hd), _BF16),
        scratch_shapes=[
            pltpu.VMEM((nb, KEY_TILE // LANE, nq, LANE), jnp.int32),
            pltpu.VMEM((H_B, nq, D_HEAD_B), _BF16),
            pltpu.VMEM((H_B, nq, LANE), _F32),
            pltpu.VMEM((H_B, nq, LANE), _F32),
            pltpu.VMEM((H_B, nq, 2 * D_HEAD_B), _F32),
            pltpu.VMEM((nq, KEY_TILE), _F32),
            pltpu.VMEM((3, nq, LANE), jnp.int32),
            pltpu.VMEM((2, nq, LANE), _F32),
        ],
        compiler_params=pltpu.CompilerParams(
            dimension_semantics=("arbitrary", "arbitrary"), vmem_limit_bytes=VMEM_LIMIT),
        name="dsa",
    )(pb, pb, pb, pb, pb, ps, bias)


def _merge_kernel(oa_ref, ob_ref, g_ref, x_ref, bg_ref, wa_ref, wb_ref, wo_ref, o_ref):
    d = x_ref.shape[1]
    a = _dot(oa_ref[...], wa_ref[...])
    b = _dot(ob_ref[...], wb_ref[...])
    g = _sigmoid(g_ref[...].astype(_F32) + bg_ref[...])
    merged = g[:, 0:d] * a + g[:, d:2 * d] * b
    o_ref[...] = x_ref[...] + _dot(merged.astype(_BF16), wo_ref[...])


def _merge(oa, ob, pb, x2d, bg, wa, wb, wo, tn, col_g):
    n, d = x2d.shape
    row = lambda w: pl.BlockSpec((tn, w), lambda i: (i, 0))
    const = lambda a: pl.BlockSpec(a.shape, lambda i: (0, 0))
    return pl.pallas_call(
        _merge_kernel,
        grid=(n // tn,),
        in_specs=[row(oa.shape[1]), row(ob.shape[1]),
                  pl.BlockSpec((tn, 2 * d), lambda i: (i, col_g // (2 * d))), row(d),
                  const(bg), const(wa), const(wb), const(wo)],
        out_specs=row(d),
        out_shape=jax.ShapeDtypeStruct((n, d), _F32),
        compiler_params=pltpu.CompilerParams(
            dimension_semantics=("arbitrary",), vmem_limit_bytes=VMEM_LIMIT),
        name="merge",
    )(oa, ob, pb, x2d, bg, wa, wb, wo)


def _mlp_kernel(x_ref, n2_ref, w1_ref, w2_ref, nf_ref, o_ref, *, final_norm):
    x = x_ref[...]
    h2 = _rms(x, n2_ref[...]).astype(_BF16)
    hid = jnp.maximum(_dot(h2, w1_ref[...]), 0.0)
    y = x + _dot((hid * hid).astype(_BF16), w2_ref[...])
    o_ref[...] = _rms(y, nf_ref[...]) if final_norm else y


def _mlp(x1, n2, w1, w2, nf, tn, final_norm):
    n, d = x1.shape
    row = pl.BlockSpec((tn, d), lambda i: (i, 0))
    const = lambda a: pl.BlockSpec(a.shape, lambda i: (0, 0), pipeline_mode=pl.Buffered(1))
    return pl.pallas_call(
        functools.partial(_mlp_kernel, final_norm=final_norm),
        grid=(n // tn,),
        in_specs=[row, const(n2), const(w1), const(w2), const(nf)],
        out_specs=row,
        out_shape=jax.ShapeDtypeStruct((n, d), _F32),
        compiler_params=pltpu.CompilerParams(
            dimension_semantics=("arbitrary",), vmem_limit_bytes=VMEM_LIMIT),
        name="mlp",
    )(x1, n2, w1, w2, nf)


def _t5_bucket(rel):
    half = N_BUCKETS // 2
    max_exact = half // 2
    base = jnp.where(rel > 0, half, 0)
    n = jnp.abs(rel)
    n_f = jnp.maximum(n, 1).astype(jnp.float32)
    large = max_exact + (jnp.log(n_f / max_exact) / math.log(MAX_DISTANCE / max_exact)
                         * (half - max_exact)).astype(jnp.int32)
    large = jnp.minimum(large, half - 1)
    return base + jnp.where(n < max_exact, n, large)


def _pick_tile(n, prefs):
    for t in prefs:
        if n % t == 0:
            return t
    raise ValueError(f"no tile in {prefs} divides {n}")


def kernel(x, norm1_w, w_in, conv_a_w, a_log, dt_bias, norm_a_w, rel_bias_table, w_gate, b_gate,
           w_proj_a, w_proj_b, w_out, norm2_w, w_ff1, w_ff2, norm_final_w):
    b_, t_, d = x.shape
    depth = norm1_w.shape[0]
    n = b_ * t_
    ha, hb = H_A * DK_A, H_B * D_HEAD_B
    assert t_ % KEY_TILE == 0 and t_ % GDN_CHUNK == 0 and d % LANE == 0
    assert DK_A == DV_A == D_HEAD_B == LANE and 2 * D_IDX == LANE
    assert Q_BLOCK >= MAX_DISTANCE

    o_ba = 4 * ha
    o_aa = o_ba + H_A
    o_qb = o_aa + H_A
    o_iq = o_qb + 3 * hb
    o_ik = o_iq + H_IDX * D_IDX
    o_iw = o_ik + D_IDX

    rel = (jnp.arange(2 * Q_BLOCK, dtype=jnp.int32)[None, :] - Q_BLOCK) \
        - jnp.arange(Q_BLOCK, dtype=jnp.int32)[:, None]
    far = jnp.full((Q_BLOCK, LANE), -(Q_BLOCK + 1), jnp.int32)
    bucket = _t5_bucket(jnp.concatenate([rel, far], axis=1))
    bias = _bias_tiles(rel_bias_table.astype(_F32), bucket)

    x2d = x.reshape(n, d)
    tn = _pick_tile(t_, (512, 256, 128))
    for layer in range(depth):
        wi = w_in[layer].astype(_BF16)
        zpad = lambda w: jnp.zeros((d, w), _BF16)
        ws = jnp.concatenate([wi[:, o_ba:o_qb], wi[:, o_iw:o_iw + H_IDX], zpad(LANE - 3 * H_A)], axis=1)
        ik = wi[:, o_ik:o_iw]
        wb16 = jnp.concatenate([wi[:, 0:o_ba], w_gate[layer].astype(_BF16), wi[:, o_qb:o_ik],
                                ik, zpad(D_IDX), zpad(D_IDX), ik], axis=1)
        nw = norm1_w[layer].reshape(1, d).astype(_F32)
        pb, ps = _norm_proj(x2d, nw, wb16, ws, conv_a_w[layer].astype(_F32), tn, t_ // tn, norm_cols=2 * ha)

        col_a = 0
        col_g = 4 * ha
        col_b = col_g + 2 * d
        assert col_g % (2 * d) == 0
        lanes = lambda v, off: jnp.zeros((1, LANE), _F32).at[0, off:off + H_A].set(v.astype(_F32))
        oa = _gdn(pb, ps, lanes(a_log[layer], H_A), lanes(dt_bias[layer], H_A),
                  norm_a_w[layer].reshape(1, DV_A).astype(_F32), b_, t_, col_a)
        ob = _dsa(pb, ps, bias, b_, t_, col_b)

        x2d = _merge(oa, ob, pb, x2d, b_gate[layer].reshape(1, 2 * d).astype(_F32),
                     w_proj_a[layer].astype(_BF16), w_proj_b[layer].astype(_BF16),
                     w_out[layer].astype(_BF16), tn, col_g)
        x2d = _mlp(x2d, norm2_w[layer].reshape(1, d).astype(_F32), w_ff1[layer].astype(_BF16),
                   w_ff2[layer].astype(_BF16), norm_final_w.reshape(1, d).astype(_F32),
                   _pick_tile(n, (512, 256, 128)), final_norm=layer == depth - 1)
    return x2d.reshape(b_, t_, d)
```

```python
import functools
import math

import jax
import jax.numpy as jnp
from jax import lax
from jax.experimental import pallas as pl
from jax.experimental.pallas import tpu as pltpu

EPS = 1e-6
H_A = 8
DK_A = 128
DV_A = 128
CONV_K = 4
GDN_CHUNK = 128
GDN_STEP_CHUNKS = 4
H_B = 8
D_HEAD_B = 128
H_IDX = 8
D_IDX = 64
TOPK_MAX = 256
Q_BLOCK = 128
CHUNK = 64
KEY_TILE = 512
N_BUCKETS = 32
MAX_DISTANCE = 128

LANE = 128
VMEM_LIMIT = 56 * 1024 * 1024

INT_MAX = 2**31 - 1
KEY_NEG_INF = -2139095041
MASK_NEG = -1e30
CODE_KEEP = -1
CODE_DROP = INT_MAX
TIE_BOUND_ALL = 2**30
LOG2E = math.log2(math.e)

_F32 = jnp.float32
_BF16 = jnp.bfloat16


def _dot(a, b):
    return jnp.dot(a, b, preferred_element_type=_F32)


def _dot_nt(a, b):
    return lax.dot_general(a, b, (((1,), (1,)), ((), ())), preferred_element_type=_F32)


def _sigmoid(x):
    return 1.0 / (1.0 + jnp.exp(-x))


def _rms(x, w):
    return x * lax.rsqrt(jnp.mean(x * x, axis=-1, keepdims=True) + EPS) * w


def _norm_proj_kernel(x_ref, nw_ref, w_ref, ws_ref, cw_ref, o_ref, os_ref, halo_s, work_s, *,
                      chunk, conv_cols, norm_cols, tiles_per_seq):
    tn = x_ref.shape[0]

    @pl.when(pl.program_id(0) == 0)
    def _():
        halo_s[...] = jnp.zeros_like(halo_s)

    h = _rms(x_ref[...], nw_ref[...]).astype(_BF16)
    os_ref[...] = _dot(h, ws_ref[...])
    seq_start = pl.program_id(0) % tiles_per_seq == 0
    heavy = [(ci, c0) for ci, c0 in enumerate(range(0, conv_cols, chunk))]
    plain = [(None, c0) for c0 in range(conv_cols, w_ref.shape[1], chunk)]
    per_heavy = -(-len(plain) // max(1, len(heavy)))
    order = []
    for i, hv in enumerate(heavy):
        order.append(hv)
        order.extend(plain[i * per_heavy:(i + 1) * per_heavy])
    order.extend(plain[len(heavy) * per_heavy:])
    for ci, c0 in order:
        y = _dot(h, w_ref[:, c0:c0 + chunk])
        if c0 < conv_cols:
            work_s[0:8, :] = jnp.where(seq_start, 0.0, halo_s[ci])
            work_s[8:8 + tn, :] = y
            halo_s[ci] = y[tn - 8:tn, :]
            acc = cw_ref[CONV_K - 1:CONV_K, c0:c0 + chunk] * y
            for j in range(CONV_K - 1):
                lo = 8 - (CONV_K - 1) + j
                acc = acc + cw_ref[j:j + 1, c0:c0 + chunk] * work_s[lo:lo + tn, :]
            y = acc * _sigmoid(acc)
            if c0 < norm_cols:
                parts = []
                for g0 in range(0, chunk, DK_A):
                    yh = y[:, g0:g0 + DK_A]
                    if c0 + g0 < norm_cols:
                        inv = lax.rsqrt(jnp.sum(yh * yh, axis=-1, keepdims=True) + EPS)
                        if c0 + g0 < norm_cols // 2:
                            inv = inv * (DK_A ** -0.5)
                        yh = yh * inv
                    parts.append(yh)
                y = jnp.concatenate(parts, axis=1)
        o_ref[:, c0:c0 + chunk] = y.astype(o_ref.dtype)


def _norm_proj(x2d, nw, w, ws, cw, tn, tiles_per_seq, norm_cols):
    n, d = x2d.shape
    c = w.shape[1]
    chunk = _pick_tile(c, (768, 512, 256, 128))
    conv_cols = cw.shape[1]
    assert conv_cols % chunk == 0 and tn >= 8
    assert norm_cols <= conv_cols and chunk % DK_A == 0 and (norm_cols // 2) % DK_A == 0
    const = functools.partial(pl.BlockSpec, pipeline_mode=pl.Buffered(1))
    return pl.pallas_call(
        functools.partial(_norm_proj_kernel, chunk=chunk, conv_cols=conv_cols, norm_cols=norm_cols,
                          tiles_per_seq=tiles_per_seq),
        grid=(n // tn,),
        in_specs=[
            pl.BlockSpec((tn, d), lambda i: (i, 0)),
            const((1, d), lambda i: (0, 0)),
            const((d, c), lambda i: (0, 0)),
            const((d, LANE), lambda i: (0, 0)),
            const(cw.shape, lambda i: (0, 0)),
        ],
        out_specs=[pl.BlockSpec((tn, c), lambda i: (i, 0)), pl.BlockSpec((tn, LANE), lambda i: (i, 0))],
        out_shape=[jax.ShapeDtypeStruct((n, c), _BF16), jax.ShapeDtypeStruct((n, LANE), _F32)],
        scratch_shapes=[
            pltpu.VMEM((conv_cols // chunk, 8, chunk), _F32),
            pltpu.VMEM((8 + tn, chunk), _F32),
        ],
        compiler_params=pltpu.CompilerParams(
            dimension_semantics=("arbitrary",), vmem_limit_bytes=VMEM_LIMIT),
        name="proj",
    )(x2d, nw, w, ws, cw)


def _gdn_kernel(qkvz_ref, sm_ref, arow_ref, dtrow_ref, naw_ref, o_ref,
                s_ref, kn_s, kb_s, qn_s, rhs_s, dec_s, qdec_s, ktt_s, eg_s, m_s, x_s):
    tt = qkvz_ref.shape[0]
    ct = GDN_CHUNK
    nc = tt // ct
    hd = H_A * DK_A

    @pl.when(pl.program_id(1) == 0)
    def _():
        s_ref[...] = jnp.zeros_like(s_ref)

    row = lax.broadcasted_iota(jnp.int32, (ct, ct), 0)
    col = lax.broadcasted_iota(jnp.int32, (ct, ct), 1)
    tri = (col <= row).astype(_F32)
    strict = col < row
    eye = (col == row).astype(_F32)
    n_sq = int(math.log2(ct))
    heads = range(H_A)

    def prepare(c):
        r0 = c * ct
        sm = sm_ref[r0:r0 + ct, :]
        beta_full = _sigmoid(sm)
        xg = sm + dtrow_ref[...]
        softplus = jnp.maximum(xg, 0.0) + jnp.log(1.0 + jnp.exp(-jnp.abs(xg)))
        g_full = -jnp.exp(arow_ref[...]) * softplus
        gcum = jnp.dot(tri, g_full, preferred_element_type=_F32, precision=lax.Precision.HIGHEST)
        gcum_t = gcum.T
        for h in heads:
            sl = slice(h * DK_A, (h + 1) * DK_A)
            qh, kh, vh = (qkvz_ref[r0:r0 + ct, idx * hd + h * DK_A:idx * hd + (h + 1) * DK_A].astype(_F32)
                          for idx in range(3))
            qn, kn = qh, kh
            bcol = beta_full[:, h:h + 1]
            gcol = gcum[:, H_A + h:H_A + h + 1]
            grow = gcum_t[H_A + h:H_A + h + 1, :]
            glast = gcum[ct - 1:ct, H_A + h:H_A + h + 1]
            kb = kn * bcol
            eg = jnp.exp(gcol)
            rhs_s[c, h, :, 0:DV_A] = vh * bcol
            rhs_s[c, h, :, DV_A:DV_A + DK_A] = kb * eg
            dec_s[c, h] = jnp.where(strict, jnp.exp(jnp.where(strict, gcol - grow, 0.0)), 0.0)
            qdec_s[c, h] = (qn * eg).astype(_BF16)
            ktt_s[c, h] = (kn * jnp.exp(glast - gcol)).T.astype(_BF16)
            kn_s[c, h] = kn.astype(_BF16)
            kb_s[c, h] = kb.astype(_BF16)
            qn_s[c, h] = qn.astype(_BF16)
            eg_s[c, h] = jnp.broadcast_to(jnp.exp(glast), (DK_A, DV_A))

    def solve(c):
        for h in heads:
            m = -(_dot_nt(kb_s[c, h], kn_s[c, h]) * dec_s[c, h])
            m_s[c, h] = m
            x_s[c, h] = m
        for _ in range(n_sq - 1):
            for h in heads:
                mb = m_s[c, h].astype(_BF16)
                m = _dot(mb, mb)
                m_s[c, h] = m
                x_s[c, h] = x_s[c, h] + m + _dot(x_s[c, h].astype(_BF16), m.astype(_BF16))
        for h in heads:
            rhs = rhs_s[c, h]
            rhs_s[c, h] = rhs + _dot(x_s[c, h].astype(_BF16), rhs.astype(_BF16))
        for h in heads:
            m_s[c, h] = _dot_nt(qn_s[c, h], kn_s[c, h]) * (dec_s[c, h] + eye)

    def recur(c):
        for h in heads:
            sb = s_ref[h].astype(_BF16)
            vnew = rhs_s[c, h, :, 0:DV_A] - _dot(rhs_s[c, h, :, DV_A:DV_A + DK_A].astype(_BF16), sb)
            vb = vnew.astype(_BF16)
            o = _dot(qdec_s[c, h], sb) + _dot(m_s[c, h].astype(_BF16), vb)
            s_ref[h] = s_ref[h] * eg_s[c, h] + _dot(ktt_s[c, h], vb)
            rhs_s[c, h, :, 0:DV_A] = o

    def finish(c):
        r0 = c * ct
        naw = naw_ref[...]
        for h in heads:
            sl = slice(h * DV_A, (h + 1) * DV_A)
            o = rhs_s[c, h, :, 0:DV_A]
            z = qkvz_ref[r0:r0 + ct, 3 * hd + h * DV_A:3 * hd + (h + 1) * DV_A].astype(_F32)
            on = o * lax.rsqrt(jnp.mean(o * o, axis=-1, keepdims=True) + EPS) * naw
            o_ref[r0:r0 + ct, sl] = (on * (z * _sigmoid(z))).astype(o_ref.dtype)

    for c in range(nc):
        prepare(c)
    for c in range(nc):
        solve(c)
    for c in range(nc):
        recur(c)
    for c in range(nc):
        finish(c)


def _gdn(pb, ps, arow, dtrow, naw, b_, t_, col0):
    ct = GDN_CHUNK
    nc = GDN_STEP_CHUNKS if t_ % (GDN_STEP_CHUNKS * ct) == 0 else 1
    tt = nc * ct
    hd = H_A * DK_A
    nt = t_ // tt
    assert col0 % (4 * hd) == 0
    cb = col0 // (4 * hd)
    const = lambda shape: pl.BlockSpec(shape, lambda b, t: (0, 0))
    return pl.pallas_call(
        _gdn_kernel,
        grid=(b_, nt),
        in_specs=[pl.BlockSpec((tt, 4 * hd), lambda b, t: (b * nt + t, cb)),
                  pl.BlockSpec((tt, LANE), lambda b, t: (b * nt + t, 0)),
                  const((1, LANE)), const((1, LANE)), const((1, DV_A))],
        out_specs=pl.BlockSpec((tt, hd), lambda b, t: (b * nt + t, 0)),
        out_shape=jax.ShapeDtypeStruct((b_ * t_, hd), _BF16),
        scratch_shapes=[
            pltpu.VMEM((H_A, DK_A, DV_A), _F32),
            pltpu.VMEM((nc, H_A, ct, DK_A), _BF16),
            pltpu.VMEM((nc, H_A, ct, DK_A), _BF16),
            pltpu.VMEM((nc, H_A, ct, DK_A), _BF16),
            pltpu.VMEM((nc, H_A, ct, DV_A + DK_A), _F32),
            pltpu.VMEM((nc, H_A, ct, ct), _F32),
            pltpu.VMEM((nc, H_A, ct, DK_A), _BF16),
            pltpu.VMEM((nc, H_A, DK_A, ct), _BF16),
            pltpu.VMEM((nc, H_A, DK_A, DV_A), _F32),
            pltpu.VMEM((nc, H_A, ct, ct), _F32),
            pltpu.VMEM((nc, H_A, ct, ct), _F32),
        ],
        compiler_params=pltpu.CompilerParams(
            dimension_semantics=("arbitrary", "arbitrary"), vmem_limit_bytes=VMEM_LIMIT),
        name="gdn",
    )(pb, ps, arow, dtrow, naw)


def _bias_kernel(tab_ref, bucket_ref, o_ref):
    bucket = bucket_ref[...]
    nq, w = bucket.shape
    for h in range(H_B):
        acc = jnp.zeros((nq, w), _F32)
        for b in range(N_BUCKETS):
            acc = acc + jnp.where(bucket == b, tab_ref[b, h], 0.0)
        far = acc[:, 2 * LANE:3 * LANE]
        o_ref[h] = (acc[:, 0:2 * LANE] - jnp.concatenate([far, far], axis=1)) * LOG2E


def _bias_tiles(rel_table, bucket):
    return pl.pallas_call(
        _bias_kernel,
        in_specs=[pl.BlockSpec(memory_space=pltpu.SMEM),
                  pl.BlockSpec(bucket.shape, lambda: (0, 0))],
        out_specs=pl.BlockSpec((H_B, Q_BLOCK, 2 * LANE), lambda: (0, 0, 0)),
        out_shape=jax.ShapeDtypeStruct((H_B, Q_BLOCK, 2 * LANE), _F32),
        name="rel_bias_tiles",
    )(rel_table, bucket)


def _dsa_kernel(q_ref, k_ref, v_ref, iq_ref, ik_ref, sm_ref, bias_ref, o_ref,
                key_s, qs_s, iwb_s, m_s, acc_s, s_s, si_s, sf_s, *, topk):
    nq = q_ref.shape[0]
    st_w = KEY_TILE
    assert nq == st_w
    ng = st_w // LANE
    qb = pl.program_id(1)
    n_st = qb + 1
    hd = D_HEAD_B
    scale = D_HEAD_B ** -0.5 * LOG2E
    f_topk = float(topk)

    lane_j = lax.broadcasted_iota(jnp.int32, (nq, LANE), 1)
    row_i = lax.broadcasted_iota(jnp.int32, (nq, LANE), 0)

    def grp(g):
        return slice(g * LANE, (g + 1) * LANE)

    iw = sm_ref[...]
    for h in range(H_B):
        sl = slice(h * hd, (h + 1) * hd)
        qs_s[h] = (q_ref[:, sl].astype(_F32) * scale).astype(_BF16)
        iwb_s[h] = jnp.broadcast_to(iw[:, 2 * H_A + h:2 * H_A + h + 1], (nq, LANE))
        m_s[h] = jnp.full((nq, LANE), MASK_NEG, _F32)
        acc_s[h] = jnp.zeros((nq, 2 * hd), _F32)

    def scores(st, diag):
        ikt = ik_ref[pl.ds(pl.multiple_of(st * st_w, st_w), st_w), :]
        acc = [jnp.zeros((nq, LANE), _F32) for _ in range(ng)]
        for p in range(H_IDX // 2):
            xq = iq_ref[:, p * LANE:(p + 1) * LANE]
            for half in range(2):
                h = 2 * p + half
                s = jnp.maximum(_dot_nt(xq, ikt[:, half * LANE:(half + 1) * LANE]), 0.0)
                w = iwb_s[h]
                for g in range(ng):
                    acc[g] = acc[g] + w * s[:, grp(g)]
        for g in range(ng):
            a = acc[g]
            if diag:
                a = jnp.where((g * LANE + lane_j) // CHUNK <= row_i // CHUNK, a, -jnp.inf)
            bits = pltpu.bitcast(a, jnp.int32)
            key_s[st, g] = bits ^ ((bits >> 31) & INT_MAX)

    def p1(st, carry):
        scores(st, False)
        return carry

    lax.fori_loop(0, qb, p1, 0)
    scores(qb, True)

    ones_b = jnp.ones((LANE, LANE), _BF16)
    assert key_s.shape[0] * ng <= 256

    T_, CAND_, MIDX_ = 0, 1, 2
    ACC_, CNT_ = 0, 1
    rb = 64
    lane_rb = lax.broadcasted_iota(jnp.int32, (rb, LANE), 1)

    def count(pred_fn):
        def tile(st, first):
            for r0 in range(0, nq, rb):
                rs = slice(r0, r0 + rb)
                acc = None if first else sf_s[ACC_, rs, :]
                for g in range(ng):
                    one = jnp.where(pred_fn(key_s[st, g, rs, :], rs), 1.0, 0.0)
                    acc = one if acc is None else acc + one
                sf_s[ACC_, rs, :] = acc

        def body(st, carry):
            tile(st, False)
            return carry

        tile(0, True)
        lax.fori_loop(1, n_st, body, 0)
        return _dot(sf_s[ACC_].astype(_BF16), ones_b)

    si_s[CAND_] = jnp.zeros((nq, LANE), jnp.int32)

    def bit_body(i, carry):
        half = jnp.left_shift(jnp.int32(1), 30 - i)
        c = count(lambda k, rs: k >= si_s[CAND_, rs, :])
        si_s[CAND_] = si_s[CAND_] + jnp.where(c >= f_topk, half, -half)
        return carry

    lax.fori_loop(0, 31, bit_body, 0)
    c_last = count(lambda k, rs: k >= si_s[CAND_, rs, :])
    si_s[T_] = jnp.where(c_last >= f_topk, si_s[CAND_], si_s[CAND_] - 1)
    sf_s[CNT_] = count(lambda k, rs: k >= si_s[T_, rs, :])

    def encode(st, carry):
        for r0 in range(0, nq, rb):
            rs = slice(r0, r0 + rb)
            t = si_s[T_, rs, :]
            for g in range(ng):
                k = key_s[st, g, rs, :]
                pos = st * st_w + g * LANE + lane_rb
                key_s[st, g, rs, :] = jnp.where(k > t, CODE_KEEP, jnp.where(k == t, pos, CODE_DROP))
        return carry

    lax.fori_loop(0, n_st, encode, 0)
    finite_thr = si_s[T_] != KEY_NEG_INF
    si_s[MIDX_] = jnp.where(finite_thr, TIE_BOUND_ALL, 0)

    excess = jnp.max(jnp.where(finite_thr, sf_s[CNT_], 0.0)) > f_topk

    @pl.when(excess)
    def _():
        r_keep = f_topk - count(lambda code, rs: code < 0)
        sf_s[CNT_] = jnp.where(finite_thr, r_keep, 0.0)
        sf_s[ACC_] = jnp.zeros((nq, LANE), _F32)
        upper = (lax.broadcasted_iota(jnp.int32, (LANE, LANE), 0)
                 <= lax.broadcasted_iota(jnp.int32, (LANE, LANE), 1)).astype(_BF16)

        def rank_ties(st, carry):
            for r0 in range(0, nq, rb):
                rs = slice(r0, r0 + rb)
                r_blk = sf_s[CNT_, rs, :]
                before = sf_s[ACC_, rs, :]
                for g in range(ng):
                    code = key_s[st, g, rs, :]
                    tie = jnp.where(code >= 0, jnp.where(code < CODE_DROP, 1.0, 0.0), 0.0)
                    tie_b = tie.astype(_BF16)
                    rank = before + _dot(tie_b, upper)
                    kept_tie = jnp.where(rank <= r_blk, tie, 0.0)
                    mb = jnp.where(code < 0, 0.0, jnp.where(kept_tie > 0.0, 0.0, MASK_NEG))
                    key_s[st, g, rs, :] = pltpu.bitcast(mb, jnp.int32)
                    before = before + _dot(tie_b, ones_b)
                sf_s[ACC_, rs, :] = before
            return carry

        lax.fori_loop(0, n_st, rank_ties, 0)

    @pl.when(jnp.logical_not(excess))
    def _():
        def p2(st, carry):
            for r0 in range(0, nq, rb):
                rs = slice(r0, r0 + rb)
                bound = si_s[MIDX_, rs, :]
                for g in range(ng):
                    mb = jnp.where(key_s[st, g, rs, :] < bound, 0.0, MASK_NEG)
                    key_s[st, g, rs, :] = pltpu.bitcast(mb, jnp.int32)
            return carry

        lax.fori_loop(0, n_st, p2, 0)

    ones_v = jnp.ones((st_w, hd), _BF16)

    def attend(st, mode):
        off = pl.multiple_of(st * st_w, st_w)
        for h in range(H_B):
            sl = slice(h * hd, (h + 1) * hd)
            mask = jnp.concatenate([pltpu.bitcast(key_s[st, g], _F32) for g in range(ng)], axis=1)
            s = _dot_nt(qs_s[h], k_ref[pl.ds(off, st_w), sl]) + mask
            if mode:
                s_s[...] = s
                b_prev = bias_ref[h, :, 0:LANE]
                if mode == 1:
                    s_s[0:LANE, (ng - 1) * LANE:ng * LANE] += b_prev
                else:
                    b_diag = bias_ref[h, :, LANE:2 * LANE]
                    for rt in range(ng):
                        s_s[grp(rt), grp(rt)] += b_diag
                        if rt:
                            s_s[grp(rt), grp(rt - 1)] += b_prev
                s = s_s[...]
            m_old = m_s[h]
            gmax = s[:, grp(0)]
            for g in range(1, ng):
                gmax = jnp.maximum(gmax, s[:, grp(g)])
            m_new = jnp.maximum(m_old, jnp.max(gmax, axis=1, keepdims=True))
            alpha = jnp.exp2(m_old - m_new)
            p = jnp.concatenate([jnp.exp2(s[:, grp(g)] - m_new) for g in range(ng)], axis=1).astype(_BF16)
            m_s[h] = m_new
            v_ones = jnp.concatenate([v_ref[pl.ds(off, st_w), sl], ones_v], axis=1)
            acc_s[h] = jnp.concatenate([alpha, alpha], axis=1) * acc_s[h] + _dot(p, v_ones)

    def p3(st, carry):
        attend(st, 0)
        return carry

    lax.fori_loop(0, qb - 1, p3, 0)

    @pl.when(qb > 0)
    def _():
        attend(qb - 1, 1)

    attend(qb, 2)

    for h in range(H_B):
        sl = slice(h * hd, (h + 1) * hd)
        o_ref[:, sl] = (acc_s[h, :, 0:hd] / acc_s[h, :, hd:2 * hd]).astype(o_ref.dtype)


def _dsa(pb, ps, bias, b_, t_, col0):
    nq = KEY_TILE
    nb = t_ // nq
    hd = H_B * D_HEAD_B
    topk = min(TOPK_MAX, t_ // 4)
    iq_w = H_IDX * D_IDX
    ik_w = 2 * LANE
    assert col0 % hd == 0 and (col0 + 3 * hd) % iq_w == 0 and (col0 + 3 * hd + iq_w) % ik_w == 0
    cq = col0 // hd
    resident = functools.partial(pl.BlockSpec, pipeline_mode=pl.Buffered(1))
    return pl.pallas_call(
        functools.partial(_dsa_kernel, topk=topk),
        grid=(b_, nb),
        in_specs=[
            pl.BlockSpec((nq, hd), lambda b, i: (b * nb + i, cq)),
            resident((t_, hd), lambda b, i: (b, cq + 1)),
            resident((t_, hd), lambda b, i: (b, cq + 2)),
            pl.BlockSpec((nq, iq_w), lambda b, i: (b * nb + i, (col0 + 3 * hd) // iq_w)),
            resident((t_, ik_w), lambda b, i: (b, (col0 + 3 * hd + iq_w) // ik_w)),
            pl.BlockSpec((nq, LANE), lambda b, i: (b * nb + i, 0)),
            resident((H_B, Q_BLOCK, 2 * LANE), lambda b, i: (0, 0, 0)),
        ],
        out_specs=pl.BlockSpec((nq, hd), lambda b, i: (b * nb + i, 0)),
        out_shape=jax.ShapeDtypeStruct((b_ * t_, hd), _BF16),
        scratch_shapes=[
            pltpu.VMEM((nb, KEY_TILE // LANE, nq, LANE), jnp.int32),
            pltpu.VMEM((H_B, nq, D_HEAD_B), _BF16),
            pltpu.VMEM((H_B, nq, LANE), _F32),
            pltpu.VMEM((H_B, nq, LANE), _F32),
            pltpu.VMEM((H_B, nq, 2 * D_HEAD_B), _F32),
            pltpu.VMEM((nq, KEY_TILE), _F32),
            pltpu.VMEM((3, nq, LANE), jnp.int32),
            pltpu.VMEM((2, nq, LANE), _F32),
        ],
        compiler_params=pltpu.CompilerParams(
            dimension_semantics=("arbitrary", "arbitrary"), vmem_limit_bytes=VMEM_LIMIT),
        name="dsa",
    )(pb, pb, pb, pb, pb, ps, bias)


def _merge_mlp_kernel(oa_ref, ob_ref, g_ref, x_ref, bg_ref, wa_ref, wb_ref, wo_ref,
                      n2_ref, w1_ref, w2_ref, nf_ref, o_ref, *, final_norm):
    d = x_ref.shape[1]
    a = _dot(oa_ref[...], wa_ref[...])
    b = _dot(ob_ref[...], wb_ref[...])
    g = _sigmoid(g_ref[...].astype(_F32) + bg_ref[...])
    merged = g[:, 0:d] * a + g[:, d:2 * d] * b
    x = x_ref[...] + _dot(merged.astype(_BF16), wo_ref[...])
    h2 = _rms(x, n2_ref[...]).astype(_BF16)
    hid = jnp.maximum(_dot(h2, w1_ref[...]), 0.0)
    y = x + _dot((hid * hid).astype(_BF16), w2_ref[...])
    o_ref[...] = _rms(y, nf_ref[...]) if final_norm else y


def _merge_mlp(oa, ob, pb, x2d, bg, wa, wb, wo, n2, w1, w2, nf, tn, col_g, final_norm):
    n, d = x2d.shape
    row = lambda w: pl.BlockSpec((tn, w), lambda i: (i, 0))
    const = lambda a: pl.BlockSpec(a.shape, lambda i: (0, 0), pipeline_mode=pl.Buffered(1))
    return pl.pallas_call(
        functools.partial(_merge_mlp_kernel, final_norm=final_norm),
        grid=(n // tn,),
        in_specs=[row(oa.shape[1]), row(ob.shape[1]),
                  pl.BlockSpec((tn, 2 * d), lambda i: (i, col_g // (2 * d))), row(d),
                  const(bg), const(wa), const(wb), const(wo),
                  const(n2), const(w1), const(w2), const(nf)],
        out_specs=row(d),
        out_shape=jax.ShapeDtypeStruct((n, d), _F32),
        compiler_params=pltpu.CompilerParams(
            dimension_semantics=("arbitrary",), vmem_limit_bytes=VMEM_LIMIT),
        name="merge_mlp",
    )(oa, ob, pb, x2d, bg, wa, wb, wo, n2, w1, w2, nf)


def _t5_bucket(rel):
    half = N_BUCKETS // 2
    max_exact = half // 2
    base = jnp.where(rel > 0, half, 0)
    n = jnp.abs(rel)
    n_f = jnp.maximum(n, 1).astype(jnp.float32)
    large = max_exact + (jnp.log(n_f / max_exact) / math.log(MAX_DISTANCE / max_exact)
                         * (half - max_exact)).astype(jnp.int32)
    large = jnp.minimum(large, half - 1)
    return base + jnp.where(n < max_exact, n, large)


def _pick_tile(n, prefs):
    for t in prefs:
        if n % t == 0:
            return t
    raise ValueError(f"no tile in {prefs} divides {n}")


def kernel(x, norm1_w, w_in, conv_a_w, a_log, dt_bias, norm_a_w, rel_bias_table, w_gate, b_gate,
           w_proj_a, w_proj_b, w_out, norm2_w, w_ff1, w_ff2, norm_final_w):
    b_, t_, d = x.shape
    depth = norm1_w.shape[0]
    n = b_ * t_
    ha, hb = H_A * DK_A, H_B * D_HEAD_B
    assert t_ % KEY_TILE == 0 and t_ % GDN_CHUNK == 0 and d % LANE == 0
    assert DK_A == DV_A == D_HEAD_B == LANE and 2 * D_IDX == LANE
    assert Q_BLOCK >= MAX_DISTANCE

    o_ba = 4 * ha
    o_aa = o_ba + H_A
    o_qb = o_aa + H_A
    o_iq = o_qb + 3 * hb
    o_ik = o_iq + H_IDX * D_IDX
    o_iw = o_ik + D_IDX

    rel = (jnp.arange(2 * Q_BLOCK, dtype=jnp.int32)[None, :] - Q_BLOCK) \
        - jnp.arange(Q_BLOCK, dtype=jnp.int32)[:, None]
    far = jnp.full((Q_BLOCK, LANE), -(Q_BLOCK + 1), jnp.int32)
    bucket = _t5_bucket(jnp.concatenate([rel, far], axis=1))
    bias = _bias_tiles(rel_bias_table.astype(_F32), bucket)

    x2d = x.reshape(n, d)
    tn = _pick_tile(t_, (512, 256, 128))
    for layer in range(depth):
        wi = w_in[layer].astype(_BF16)
        zpad = lambda w: jnp.zeros((d, w), _BF16)
        ws = jnp.concatenate([wi[:, o_ba:o_qb], wi[:, o_iw:o_iw + H_IDX], zpad(LANE - 3 * H_A)], axis=1)
        ik = wi[:, o_ik:o_iw]
        wb16 = jnp.concatenate([wi[:, 0:o_ba], w_gate[layer].astype(_BF16), wi[:, o_qb:o_ik],
                                ik, zpad(D_IDX), zpad(D_IDX), ik], axis=1)
        nw = norm1_w[layer].reshape(1, d).astype(_F32)
        pb, ps = _norm_proj(x2d, nw, wb16, ws, conv_a_w[layer].astype(_F32), tn, t_ // tn, norm_cols=2 * ha)

        col_a = 0
        col_g = 4 * ha
        col_b = col_g + 2 * d
        assert col_g % (2 * d) == 0
        lanes = lambda v, off: jnp.zeros((1, LANE), _F32).at[0, off:off + H_A].set(v.astype(_F32))
        oa = _gdn(pb, ps, lanes(a_log[layer], H_A), lanes(dt_bias[layer], H_A),
                  norm_a_w[layer].reshape(1, DV_A).astype(_F32), b_, t_, col_a)
        ob = _dsa(pb, ps, bias, b_, t_, col_b)

        x2d = _merge_mlp(oa, ob, pb, x2d, b_gate[layer].reshape(1, 2 * d).astype(_F32),
                         w_proj_a[layer].astype(_BF16), w_proj_b[layer].astype(_BF16),
                         w_out[layer].astype(_BF16),
                         norm2_w[layer].reshape(1, d).astype(_F32), w_ff1[layer].astype(_BF16),
                         w_ff2[layer].astype(_BF16), norm_final_w.reshape(1, d).astype(_F32),
                         tn, col_g, final_norm=layer == depth - 1)
    return x2d.reshape(b_, t_, d)
```
